```python
import math
import jax, jax.numpy as jnp
from jax import lax
import numpy as np

D_MODEL = 1024
BATCH = 1
SEQ = 16384
DEPTH = 1

CHUNK = 64
N_MEM = 256
EPS = 1e-6

A_HEADS = 4
A_QK_DIM = 64
A_V_DIM = 2 * A_QK_DIM
A_WIDTH = A_HEADS * A_V_DIM
Q_BLOCK = 128

REL_BUCKETS = 32
REL_MAX_DIST = 128

B_HEADS = 4
B_K_DIM = 128
B_V_DIM = 128
B_WIDTH = B_HEADS * B_V_DIM
CONV_WIDTH = 4

X_HEADS = 4
X_HEAD_DIM = D_MODEL // X_HEADS

N_EXPERTS = 32
TOP_K = 4
D_FF = 1024
SWIGLU_LIMIT = 7.0
SWIGLU_ALPHA = 1.702
MOE_BLOCK = 128

IN_SPLIT_SIZES = (
    A_HEADS * 2 * A_QK_DIM,
    A_HEADS * 2 * A_QK_DIM,
    A_WIDTH,
    B_HEADS * B_K_DIM,
    B_HEADS * B_K_DIM,
    B_WIDTH,
    B_WIDTH,
    B_HEADS,
    B_HEADS,
    D_MODEL,
    D_MODEL,
)
IN_COLS = 512 + 512 + 512 + 512 + 512 + 512 + 512 + 4 + 4 + 1024 + 1024

kernel_name = "hybrid_diffattn_gdn_moe_block"


def rmsnorm(x, g):
    xf = x.astype(jnp.float32)
    y = xf * lax.rsqrt(jnp.mean(xf * xf, axis=-1, keepdims=True) + EPS)
    return (y * g.astype(jnp.float32)).astype(x.dtype)


def l2norm(t):
    return t * lax.rsqrt(jnp.sum(t * t, axis=-1, keepdims=True) + EPS)


def split_in(proj):
    idx = [int(i) for i in np.cumsum(IN_SPLIT_SIZES)[:-1]]
    return jnp.split(proj, idx, axis=-1)


def t5_bucket(rel):
    half = REL_BUCKETS // 2
    max_exact = half // 2
    ret = jnp.where(rel > 0, half, 0)
    n = jnp.abs(rel)
    large = max_exact + (jnp.log(jnp.maximum(n, 1).astype(jnp.float32) / max_exact)
                         / math.log(REL_MAX_DIST / max_exact) * (half - max_exact)).astype(jnp.int32)
    large = jnp.minimum(large, half - 1)
    return ret + jnp.where(n < max_exact, n, large)


def diff_attention(qa, ka, va, rel_bias, lam):
    B_, S_ = qa.shape[0], qa.shape[1]
    q = qa.reshape(B_, S_, A_HEADS, 2, A_QK_DIM)
    k = ka.reshape(B_, S_, A_HEADS, 2, A_QK_DIM)
    v = va.reshape(B_, S_, A_HEADS, A_V_DIM)
    nqb = S_ // Q_BLOCK
    qblocks = q.reshape(B_, nqb, Q_BLOCK, A_HEADS, 2, A_QK_DIM).swapaxes(0, 1)
    k_pos = jnp.arange(S_, dtype=jnp.int32)
    k_chunk = k_pos // CHUNK
    scale = A_QK_DIM ** -0.5

    def block(args):
        qb, bi = args
        q_pos = bi * Q_BLOCK + jnp.arange(Q_BLOCK, dtype=jnp.int32)
        logits = jnp.einsum('bqhmd,bkhmd->bhmqk', qb, k).astype(jnp.float32) * scale
        bias = rel_bias[t5_bucket(k_pos[None, :] - q_pos[:, None])].astype(jnp.float32)
        logits = logits + jnp.transpose(bias, (2, 0, 1))[None, :, None]
        mask = k_chunk[None, :] <= (q_pos // CHUNK)[:, None]
        p = jax.nn.softmax(jnp.where(mask, logits, -jnp.inf), axis=-1)
        pd = p[:, :, 0] - lam * p[:, :, 1]
        return jnp.einsum('bhqk,bkhd->bqhd', pd.astype(v.dtype), v)

    o = lax.map(block, (qblocks, jnp.arange(nqb, dtype=jnp.int32)))
    return o.swapaxes(0, 1).reshape(B_, S_, A_HEADS, A_V_DIM)


def causal_conv(x, w):
    return lax.conv_general_dilated(
        x, w[:, None, :].astype(x.dtype), window_strides=(1,), padding=[(CONV_WIDTH - 1, 0)],
        dimension_numbers=('NWC', 'WIO', 'NWC'), feature_group_count=x.shape[-1])


def chunk_gated_delta_rule(q, k, v, g, beta):
    B_, S_, H, dk = q.shape
    dv = v.shape[-1]
    N = S_ // CHUNK

    def chunks(t):
        return t.reshape(B_, N, CHUNK, H, -1).transpose(0, 3, 1, 2, 4)

    q = chunks(q) * (dk ** -0.5)
    k = chunks(k)
    v = chunks(v)
    g = g.reshape(B_, N, CHUNK, H).transpose(0, 3, 1, 2)
    beta = beta.reshape(B_, N, CHUNK, H).transpose(0, 3, 1, 2)
    gc = jnp.cumsum(g, axis=-1)
    kb = k * beta[..., None]
    vb = v * beta[..., None]
    idx = jnp.arange(CHUNK)
    incl = idx[:, None] >= idx[None, :]
    strict = idx[:, None] > idx[None, :]
    decay = jnp.exp(jnp.where(incl, gc[..., :, None] - gc[..., None, :], -jnp.inf))
    L = jnp.einsum('bhnid,bhnjd->bhnij', kb, k) * jnp.where(strict, decay, 0.0)
    tri = jnp.eye(CHUNK, dtype=jnp.float32) + L
    rhs = jnp.concatenate([vb, kb * jnp.exp(gc)[..., None]], axis=-1)
    sol = lax.linalg.triangular_solve(tri, rhs, left_side=True, lower=True)
    u, w = sol[..., :dv], sol[..., dv:]
    qk = jnp.einsum('bhnid,bhnjd->bhnij', q, k) * decay
    q_dec = q * jnp.exp(gc)[..., None]
    g_last = gc[..., -1]
    k_dec = k * jnp.exp(g_last[..., None] - gc)[..., None]

    def step(S, xs):
        u_i, w_i, qk_i, qd_i, kd_i, gl_i = xs
        v_new = u_i - jnp.einsum('bhcd,bhde->bhce', w_i, S)
        o = jnp.einsum('bhcd,bhde->bhce', qd_i, S) + jnp.einsum('bhij,bhje->bhie', qk_i, v_new)
        S = S * jnp.exp(gl_i)[..., None, None] + jnp.einsum('bhcd,bhce->bhde', kd_i, v_new)
        return S, o

    xs = tuple(jnp.moveaxis(t, 2, 0) for t in (u, w, qk, q_dec, k_dec, g_last))
    S0 = jnp.zeros((B_, H, dk, dv), jnp.float32)
    _, o = lax.scan(step, S0, xs)
    return o.transpose(1, 0, 3, 2, 4).reshape(B_, S_, H, dv)


def cross_attention(h, m, w_q, w_kv, w_o):
    B_, S_ = h.shape[0], h.shape[1]
    q = (h @ w_q).reshape(B_, S_, X_HEADS, X_HEAD_DIM)
    kv = m @ w_kv
    k = kv[..., :D_MODEL].reshape(B_, m.shape[1], X_HEADS, X_HEAD_DIM)
    v = kv[..., D_MODEL:].reshape(B_, m.shape[1], X_HEADS, X_HEAD_DIM)
    logits = jnp.einsum('bqhd,bkhd->bhqk', q, k).astype(jnp.float32) * (X_HEAD_DIM ** -0.5)
    p = jax.nn.softmax(logits, axis=-1).astype(v.dtype)
    o = jnp.einsum('bhqk,bkhd->bqhd', p, v).reshape(B_, S_, D_MODEL)
    return o @ w_o


def moe(h, w_router, b_router, w_mlp1, b_mlp1, w_mlp2, b_mlp2):
    B_, S_, D = h.shape
    T = B_ * S_
    A = T * TOP_K
    hf = h.reshape(T, D)
    logits = (hf @ w_router + b_router).astype(jnp.float32)
    top_val, top_idx = lax.top_k(logits, TOP_K)
    wts = jax.nn.softmax(top_val, axis=-1)
    e_flat = top_idx.reshape(A).astype(jnp.int32)
    tok_flat = jnp.repeat(jnp.arange(T, dtype=jnp.int32), TOP_K)
    w_flat = wts.reshape(A)
    order = jnp.argsort(e_flat, stable=True)
    se, st, sw = e_flat[order], tok_flat[order], w_flat[order]
    counts = jnp.zeros((N_EXPERTS,), jnp.int32).at[e_flat].add(1)
    start = jnp.cumsum(counts) - counts
    nblk = (counts + MOE_BLOCK - 1) // MOE_BLOCK
    blk_end = jnp.cumsum(nblk)
    blk_start = blk_end - nblk
    dest = blk_start[se] * MOE_BLOCK + (jnp.arange(A, dtype=jnp.int32) - start[se])
    NB = A // MOE_BLOCK + N_EXPERTS
    P = NB * MOE_BLOCK
    row_tok = jnp.full((P,), T, jnp.int32).at[dest].set(st)
    block_e = jnp.minimum(jnp.searchsorted(blk_end, jnp.arange(NB, dtype=jnp.int32), side='right'),
                          N_EXPERTS - 1).astype(jnp.int32)
    xs = jnp.concatenate([hf, jnp.zeros((1, D), hf.dtype)], axis=0)[row_tok].reshape(NB, MOE_BLOCK, D)

    def expert_block(args):
        xb, e = args
        hid = xb @ w_mlp1[e] + b_mlp1[e]
        glu = jnp.minimum(hid[:, :D_FF], SWIGLU_LIMIT)
        lin = jnp.clip(hid[:, D_FF:], -SWIGLU_LIMIT, SWIGLU_LIMIT)
        act = glu * jax.nn.sigmoid(SWIGLU_ALPHA * glu) * (lin + 1.0)
        return act @ w_mlp2[e] + b_mlp2[e]

    ys = lax.map(expert_block, (xs, block_e)).reshape(P, D)
    contrib = ys[dest] * sw[:, None].astype(ys.dtype)
    y = jnp.zeros((T, D), ys.dtype).at[st].add(contrib)
    return y.reshape(B_, S_, D).astype(h.dtype)


def setup_inputs(seed: int = 0) -> dict:
    key = jax.random.key(seed)
    ks = jax.random.split(key, 32)
    f32 = jnp.float32

    def nrm(k, shape, scale):
        return jax.random.normal(k, shape, f32) * scale

    def gain(k, shape):
        return 1.0 + 0.02 * jax.random.normal(k, shape, f32)

    L, D, E, F = DEPTH, D_MODEL, N_EXPERTS, D_FF
    dt = jnp.exp(jax.random.uniform(ks[10], (L, B_HEADS), f32, math.log(1e-3), math.log(1e-1)))
    return {
        "x": nrm(ks[0], (BATCH, SEQ, D), 1.0),
        "mem": nrm(ks[1], (BATCH, N_MEM, D), 1.0),
        "g_mix": gain(ks[2], (L, D)),
        "w_in": nrm(ks[3], (L, D, IN_COLS), D ** -0.5),
        "rel_bias": nrm(ks[4], (REL_BUCKETS, A_HEADS), 0.5),
        "lambda_q1": nrm(ks[5], (L, A_QK_DIM), 0.1),
        "lambda_k1": nrm(ks[6], (L, A_QK_DIM), 0.1),
        "lambda_q2": nrm(ks[7], (L, A_QK_DIM), 0.1),
        "lambda_k2": nrm(ks[8], (L, A_QK_DIM), 0.1),
        "g_subln": gain(ks[9], (L, A_V_DIM)),
        "conv_w": nrm(ks[11], (L, CONV_WIDTH, 2 * B_HEADS * B_K_DIM + B_WIDTH), CONV_WIDTH ** -0.5),
        "a_log": jnp.log(jax.random.uniform(ks[12], (L, B_HEADS), f32, 1.0, 16.0)),
        "dt_bias": dt + jnp.log(-jnp.expm1(-dt)),
        "g_out_b": gain(ks[13], (L, B_V_DIM)),
        "w_up_a": nrm(ks[14], (L, A_WIDTH, D), A_WIDTH ** -0.5),
        "w_up_b": nrm(ks[15], (L, B_WIDTH, D), B_WIDTH ** -0.5),
        "w_out": nrm(ks[16], (L, D, D), D ** -0.5),
        "g_xattn": gain(ks[17], (L, D)),
        "g_mem": gain(ks[18], (L, D)),
        "w_q_x": nrm(ks[19], (L, D, D), D ** -0.5),
        "w_kv_x": nrm(ks[20], (L, D, 2 * D), D ** -0.5),
        "w_o_x": nrm(ks[21], (L, D, D), D ** -0.5),
        "g_moe": gain(ks[22], (L, D)),
        "w_router": nrm(ks[23], (L, D, E), D ** -0.5),
        "b_router": nrm(ks[24], (L, E), 0.01),
        "w_mlp1": nrm(ks[25], (L, E, D, 2 * F), D ** -0.5),
        "b_mlp1": nrm(ks[26], (L, E, 2 * F), 0.01),
        "w_mlp2": nrm(ks[27], (L, E, F, D), F ** -0.5),
        "b_mlp2": nrm(ks[28], (L, E, D), 0.01),
        "g_final": gain(ks[29], (D,)),
    }


def reference(x, mem, g_mix, w_in, rel_bias, lambda_q1, lambda_k1, lambda_q2, lambda_k2, g_subln,
              conv_w, a_log, dt_bias, g_out_b, w_up_a, w_up_b, w_out, g_xattn, g_mem, w_q_x, w_kv_x,
              w_o_x, g_moe, w_router, b_router, w_mlp1, b_mlp1, w_mlp2, b_mlp2, g_final):
    B_, S_ = x.shape[0], x.shape[1]
    f32 = jnp.float32
    for l in range(DEPTH):
        lambda_init = 0.8 - 0.6 * math.exp(-0.3 * l)
        h = rmsnorm(x, g_mix[l])
        qa, ka, va, qb, kb, vb, zb, ab, bb, gate_a, gate_b = split_in(h @ w_in[l])

        lam = (jnp.exp(jnp.sum(lambda_q1[l].astype(f32) * lambda_k1[l].astype(f32)))
               - jnp.exp(jnp.sum(lambda_q2[l].astype(f32) * lambda_k2[l].astype(f32))) + lambda_init)
        oa = diff_attention(qa, ka, va, rel_bias, lam)
        oa = (rmsnorm(oa, g_subln[l]) * (1.0 - lambda_init)).reshape(B_, S_, A_WIDTH)

        qkv = jax.nn.silu(causal_conv(jnp.concatenate([qb, kb, vb], axis=-1), conv_w[l]))
        qc = qkv[..., :B_HEADS * B_K_DIM].reshape(B_, S_, B_HEADS, B_K_DIM).astype(f32)
        kc = qkv[..., B_HEADS * B_K_DIM:2 * B_HEADS * B_K_DIM].reshape(B_, S_, B_HEADS, B_K_DIM).astype(f32)
        vc = qkv[..., 2 * B_HEADS * B_K_DIM:].reshape(B_, S_, B_HEADS, B_V_DIM).astype(f32)
        beta = jax.nn.sigmoid(bb.astype(f32))
        g = -jnp.exp(a_log[l].astype(f32)) * jax.nn.softplus(ab.astype(f32) + dt_bias[l].astype(f32))
        ob = chunk_gated_delta_rule(l2norm(qc), l2norm(kc), vc, g, beta).astype(x.dtype)
        ob = (rmsnorm(ob, g_out_b[l]) * jax.nn.silu(zb.reshape(B_, S_, B_HEADS, B_V_DIM))).reshape(B_, S_, B_WIDTH)

        merged = jax.nn.sigmoid(gate_a) * (oa @ w_up_a[l]) + jax.nn.sigmoid(gate_b) * (ob @ w_up_b[l])
        x = x + merged @ w_out[l]

        x = x + cross_attention(rmsnorm(x, g_xattn[l]), rmsnorm(mem, g_mem[l]), w_q_x[l], w_kv_x[l], w_o_x[l])

        x = x + moe(rmsnorm(x, g_moe[l]), w_router[l], b_router[l], w_mlp1[l], b_mlp1[l], w_mlp2[l], b_mlp2[l])
    return rmsnorm(x, g_final)
```

```python
import functools
import math

import jax
import jax.numpy as jnp
from jax import lax
from jax.experimental import pallas as pl
from jax.experimental.pallas import tpu as pltpu

F32 = jnp.float32
BF16 = jnp.bfloat16
I32 = jnp.int32

D_MODEL = 1024
CHUNK = 64
EPS = 1e-6
A_HEADS = 4
A_QK_DIM = 64
A_V_DIM = 128
REL_BUCKETS = 32
REL_MAX_DIST = 128
B_HEADS = 4
B_DIM = 128
CONV_WIDTH = 4
X_HEADS = 4
X_HEAD_DIM = 256
N_EXPERTS = 32
TOP_K = 4
D_FF = 1024
SWIGLU_LIMIT = 7.0
SWIGLU_ALPHA = 1.702
LAMBDA_INIT = 0.8 - 0.6 * math.exp(-0.3 * 0)

LOG2E = 1.4426950408889634
NEG_BIG = -1e30

V7X_LANES = 128
V7X_SUBLANES = 8
V7X_VMEM_BYTES = 64 * 1024 * 1024

TM_PROJ = 512
TQ = 256
TK = 256
TC_GDN = 256
TM_MIX = 512
TT_MOE = 2048
RB_MOE = 128

C_QA, C_KA, C_QKVB, C_ZB, C_GATE, C_AB, C_END = 0, 512, 1024, 2560, 3072, 5120, 5248

_NT = (((1,), (1,)), ((), ()))
_TN = (((0,), (0,)), ((), ()))


def _rms(x, g):
    return x * lax.rsqrt(jnp.mean(x * x, axis=-1, keepdims=True) + EPS) * g


def _sigmoid(x):
    return 1.0 / (1.0 + jnp.exp(-x))


def _vmem_limit(nbytes):
    return int(min(nbytes, V7X_VMEM_BYTES - 4 * 1024 * 1024))


def _inproj_kernel(x_ref, g_ref, w_ref, wvt_ref, wabt_ref,
                   q_ref, k_ref, vt_ref, qkvb_ref, zb_ref, gates_ref, ab_ref, abt_ref):
    h = _rms(x_ref[...], g_ref[...]).astype(BF16)

    def mm(c0, c1):
        return jnp.dot(h, w_ref[:, c0:c1], preferred_element_type=F32)

    nb = TM_PROJ // TK
    qa = mm(C_QA, C_KA) * (A_QK_DIM ** -0.5 * LOG2E)
    ka = mm(C_KA, C_QKVB)
    vt = lax.dot_general(wvt_ref[...], h, _NT, preferred_element_type=F32)
    for hh in range(A_HEADS):
        cs = slice(hh * 128, (hh + 1) * 128)
        q_ref[hh] = qa[:, cs].astype(BF16)
        for b in range(nb):
            rs = slice(b * TK, (b + 1) * TK)
            k_ref[hh, b] = ka[rs, cs].astype(BF16)
            vt_ref[hh, b] = vt[cs, rs].astype(BF16)
    for j in range(3):
        qkvb_ref[:, j * 512:(j + 1) * 512] = mm(C_QKVB + j * 512, C_QKVB + (j + 1) * 512)
    zb_ref[...] = mm(C_ZB, C_GATE)
    for j in range(4):
        gates_ref[:, j * 512:(j + 1) * 512] = mm(C_GATE + j * 512, C_GATE + (j + 1) * 512)
    ab_ref[...] = mm(C_AB, C_END)
    abt_ref[...] = lax.dot_general(wabt_ref[...], h, _NT, preferred_element_type=F32)


def _inproj(x2d, g_mix, w_main, w_vt, w_abt):
    T = x2d.shape[0]
    n = T // TM_PROJ
    nkb = T // TK
    nb = TM_PROJ // TK
    full = lambda shape: pl.BlockSpec(shape, lambda i: (0,) * len(shape))
    out_shape = (
        jax.ShapeDtypeStruct((A_HEADS, T, 128), BF16),
        jax.ShapeDtypeStruct((A_HEADS, nkb, TK, 128), BF16),
        jax.ShapeDtypeStruct((A_HEADS, nkb, 128, TK), BF16),
        jax.ShapeDtypeStruct((T, 1536), F32),
        jax.ShapeDtypeStruct((T, 512), F32),
        jax.ShapeDtypeStruct((T, 2048), F32),
        jax.ShapeDtypeStruct((T, 128), F32),
        jax.ShapeDtypeStruct((16, T), F32),
    )
    out_specs = (
        pl.BlockSpec((A_HEADS, TM_PROJ, 128), lambda i: (0, i, 0)),
        pl.BlockSpec((A_HEADS, nb, TK, 128), lambda i: (0, i, 0, 0)),
        pl.BlockSpec((A_HEADS, nb, 128, TK), lambda i: (0, i, 0, 0)),
        pl.BlockSpec((TM_PROJ, 1536), lambda i: (i, 0)),
        pl.BlockSpec((TM_PROJ, 512), lambda i: (i, 0)),
        pl.BlockSpec((TM_PROJ, 2048), lambda i: (i, 0)),
        pl.BlockSpec((TM_PROJ, 128), lambda i: (i, 0)),
        pl.BlockSpec((16, TM_PROJ), lambda i: (0, i)),
    )
    return pl.pallas_call(
        _inproj_kernel,
        grid=(n,),
        in_specs=[
            pl.BlockSpec((TM_PROJ, D_MODEL), lambda i: (i, 0)),
            full((1, D_MODEL)),
            full(w_main.shape),
            full(w_vt.shape),
            full(w_abt.shape),
        ],
        out_specs=out_specs,
        out_shape=out_shape,
        compiler_params=pltpu.CompilerParams(
            dimension_semantics=("arbitrary",), vmem_limit_bytes=_vmem_limit(56 << 20)),
        name="inproj",
    )(x2d, g_mix, w_main, w_vt, w_abt)


def _attn_kernel(lam_ref, q_ref, k_ref, vt_ref, bd_ref, bp_ref, gs_ref, o_ref, m_sc, l_sc, acc_sc):
    qi = pl.program_id(1)
    q = q_ref[0]
    lane = lax.broadcasted_iota(I32, q.shape, 1)
    zero = jnp.zeros_like(q)
    qm = (jnp.where(lane < A_QK_DIM, q, zero), jnp.where(lane >= A_QK_DIM, q, zero))
    m_sc[...] = jnp.full(m_sc.shape, NEG_BIG, F32)
    l_sc[...] = jnp.zeros(l_sc.shape, F32)
    acc_sc[...] = jnp.zeros(acc_sc.shape, F32)

    def block(kb, bias):
        kblk = k_ref[0, kb]
        vtb = vt_ref[0, kb]
        for m in range(2):
            s = lax.dot_general(kblk, qm[m], _NT, preferred_element_type=F32)
            if bias is not None:
                s = s + bias
            m_old = m_sc[m]
            m_new = jnp.maximum(m_old, jnp.max(s, axis=0, keepdims=True))
            p = jnp.exp2(s - m_new)
            alpha = jnp.exp2(m_old - m_new)
            l_sc[m] = alpha * l_sc[m] + jnp.sum(p, axis=0, keepdims=True)
            acc_sc[m] = alpha * acc_sc[m] + jnp.dot(vtb, p.astype(BF16), preferred_element_type=F32)
            m_sc[m] = m_new

    def far_body(kb, c):
        block(kb, None)
        return c

    lax.fori_loop(0, jnp.maximum(qi - 1, 0), far_body, 0)

    @pl.when(qi >= 1)
    def _():
        block(qi - 1, bp_ref[0])

    block(qi, bd_ref[0])

    o = acc_sc[0] / l_sc[0] - lam_ref[0] * (acc_sc[1] / l_sc[1])
    ot = o.T
    o_ref[...] = (_rms(ot, gs_ref[...]) * (1.0 - LAMBDA_INIT)).astype(BF16)


def _attention(lam, q, k, vt, bias_d, bias_p, g_subln):
    T = q.shape[1]
    nq = T // TQ
    nkb = T // TK
    return pl.pallas_call(
        _attn_kernel,
        grid=(A_HEADS, nq),
        in_specs=[
            pl.BlockSpec(memory_space=pltpu.SMEM),
            pl.BlockSpec((1, TQ, 128), lambda h, i: (h, i, 0)),
            pl.BlockSpec((1, nkb, TK, 128), lambda h, i: (h, 0, 0, 0)),
            pl.BlockSpec((1, nkb, 128, TK), lambda h, i: (h, 0, 0, 0)),
            pl.BlockSpec((1, TK, TQ), lambda h, i: (h, 0, 0)),
            pl.BlockSpec((1, TK, TQ), lambda h, i: (h, 0, 0)),
            pl.BlockSpec((1, 128), lambda h, i: (0, 0)),
        ],
        out_specs=pl.BlockSpec((TQ, 128), lambda h, i: (i, h)),
        out_shape=jax.ShapeDtypeStruct((T, A_HEADS * A_V_DIM), BF16),
        scratch_shapes=[
            pltpu.VMEM((2, 1, TQ), F32),
            pltpu.VMEM((2, 1, TQ), F32),
            pltpu.VMEM((2, 128, TQ), F32),
        ],
        compiler_params=pltpu.CompilerParams(
            dimension_semantics=("arbitrary", "arbitrary"), vmem_limit_bytes=_vmem_limit(48 << 20)),
        name="attn",
    )(lam, q, k, vt, bias_d, bias_p, g_subln)


def _t5_bucket(rel):
    half = REL_BUCKETS // 2
    max_exact = half // 2
    ret = jnp.where(rel > 0, half, 0)
    n = jnp.abs(rel)
    large = max_exact + (jnp.log(jnp.maximum(n, 1).astype(F32) / max_exact)
                         / math.log(REL_MAX_DIST / max_exact) * (half - max_exact)).astype(I32)
    large = jnp.minimum(large, half - 1)
    return ret + jnp.where(n < max_exact, n, large)


def _attn_bias_tables(rel_bias):
    assert TK >= REL_MAX_DIST and TK == TQ and TK % CHUNK == 0
    kk = jnp.arange(TK, dtype=I32)[:, None]
    qq = jnp.arange(TQ, dtype=I32)[None, :]
    rb = rel_bias.astype(F32)
    far = rb[_t5_bucket(jnp.full((1,), -REL_MAX_DIST, I32))[0]]
    bd = jnp.transpose(rb[_t5_bucket(kk - qq)] - far, (2, 0, 1)) * LOG2E
    bd = jnp.where((kk // CHUNK <= qq // CHUNK)[None], bd, NEG_BIG)
    bp = jnp.transpose(rb[_t5_bucket(kk - TK - qq)] - far, (2, 0, 1)) * LOG2E
    return bd, bp


def _gdn_kernel(x_ref, zb_ref, ab_ref, abt_ref, cw_ref, alc_ref, dtc_ref, alr_ref, dtr_ref, gob_ref,
                o_ref, xbuf, s_sc):
    i = pl.program_id(0)
    TC = TC_GDN
    NCH = TC // CHUNK
    HB = B_HEADS * CHUNK

    @pl.when(i == 0)
    def _():
        xbuf[0:8, :] = jnp.zeros((8, 1536), F32)
        s_sc[...] = jnp.zeros(s_sc.shape, F32)

    xbuf[8:8 + TC, :] = x_ref[...]
    y = cw_ref[3:4, :] * xbuf[8:8 + TC, :]
    for d in range(1, CONV_WIDTH):
        y = y + cw_ref[3 - d:4 - d, :] * xbuf[8 - d:8 - d + TC, :]
    xbuf[0:8, :] = x_ref[TC - 8:TC, :]
    a = y * _sigmoid(y)

    def l2n(t):
        return t * lax.rsqrt(jnp.sum(t * t, axis=-1, keepdims=True) + EPS)

    qh = [l2n(a[:, h * 128:(h + 1) * 128]) * (B_DIM ** -0.5) for h in range(B_HEADS)]
    kh = [l2n(a[:, 512 + h * 128:512 + (h + 1) * 128]) for h in range(B_HEADS)]
    vh = [a[:, 1024 + h * 128:1024 + (h + 1) * 128] for h in range(B_HEADS)]

    def softplus(t):
        return jnp.maximum(t, 0.0) + jnp.log(1.0 + jnp.exp(-jnp.abs(t)))

    ab = ab_ref[...]
    g_col = -jnp.exp(alc_ref[...]) * softplus(ab + dtc_ref[...])
    beta_col = _sigmoid(ab)
    g_row = -jnp.exp(alr_ref[...]) * softplus(abt_ref[...] + dtr_ref[...])

    rt = lax.broadcasted_iota(I32, (TC, TC), 0)
    ct = lax.broadcasted_iota(I32, (TC, TC), 1)
    same_chunk = (rt // CHUNK) == (ct // CHUNK)
    tril = jnp.where(same_chunk & (rt >= ct), 1.0, 0.0).astype(F32)
    triu = jnp.where(same_chunk & (rt <= ct), 1.0, 0.0).astype(F32)
    gc_col = jnp.dot(tril, g_col, precision=lax.Precision.HIGHEST, preferred_element_type=F32)
    gc_row = jnp.dot(g_row, triu, precision=lax.Precision.HIGHEST, preferred_element_type=F32)

    ri = lax.broadcasted_iota(I32, (HB, HB), 0)
    ci = lax.broadcasted_iota(I32, (HB, HB), 1)
    same_head = (ri // CHUNK) == (ci // CHUNK)
    incl = same_head & (ri >= ci)
    strict = same_head & (ri > ci)
    eye = jnp.where(ri == ci, 1.0, 0.0).astype(F32)

    def level_mask(s):
        return ((ri // (2 * s)) == (ci // (2 * s))) & (((ri // s) % 2) == 1) & (((ci // s) % 2) == 0)

    for c in range(NCH):
        rs = slice(c * CHUNK, (c + 1) * CHUNK)
        last = slice(c * CHUNK + CHUNK - 1, (c + 1) * CHUNK)
        cat0 = lambda parts: jnp.concatenate(parts, axis=0)
        K = cat0([kh[h][rs] for h in range(B_HEADS)])
        Q = cat0([qh[h][rs] for h in range(B_HEADS)])
        V = cat0([vh[h][rs] for h in range(B_HEADS)])
        beta = cat0([beta_col[rs, 4 + h:5 + h] for h in range(B_HEADS)])
        gcc = cat0([gc_col[rs, h:h + 1] for h in range(B_HEADS)])
        gl = cat0([jnp.broadcast_to(gc_col[last, h:h + 1], (CHUNK, 1)) for h in range(B_HEADS)])
        gcr = jnp.concatenate([gc_row[h:h + 1, rs] for h in range(B_HEADS)], axis=1)

        Kb = K * beta
        Vb = V * beta
        dec = jnp.where(incl, jnp.exp(jnp.where(incl, gcc - gcr, 0.0)), 0.0)
        L = jnp.where(strict, lax.dot_general(Kb, K, _NT, preferred_element_type=F32) * dec, 0.0)
        QK = lax.dot_general(Q, K, _NT, preferred_element_type=F32) * dec

        inv = eye - jnp.where(level_mask(1), L, 0.0)
        for s in (2, 4, 8, 16, 32):
            cs_ = jnp.where(level_mask(s), L, 0.0)
            t_ = jnp.dot(inv, cs_, preferred_element_type=F32)
            inv = inv - jnp.dot(t_, inv, preferred_element_type=F32)

        egc = jnp.exp(gcc)
        rhs = jnp.concatenate([Vb, Kb * egc], axis=1)
        sol = jnp.dot(inv, rhs, preferred_element_type=F32)
        U = sol[:, :B_DIM]
        W = sol[:, B_DIM:]
        Qd = Q * egc
        Kd = K * jnp.exp(gl - gcc)

        vnew = []
        ost = []
        for h in range(B_HEADS):
            hs = slice(h * CHUNK, (h + 1) * CHUNK)
            S = s_sc[h]
            vn = U[hs] - jnp.dot(W[hs], S, preferred_element_type=F32)
            ost.append(jnp.dot(Qd[hs], S, preferred_element_type=F32))
            s_sc[h] = S * jnp.exp(gl[h * CHUNK:h * CHUNK + 1, :]) + lax.dot_general(
                Kd[hs], vn, _TN, preferred_element_type=F32)
            vnew.append(vn)
        O = cat0(ost) + jnp.dot(QK, cat0(vnew), preferred_element_type=F32)
        for h in range(B_HEADS):
            oh = _rms(O[h * CHUNK:(h + 1) * CHUNK], gob_ref[...])
            z = zb_ref[rs, h * 128:(h + 1) * 128]
            o_ref[rs, h * 128:(h + 1) * 128] = (oh * (z * _sigmoid(z))).astype(BF16)


def _gdn(qkvb, zb, ab, abt, conv_w8, alc, dtc, alr, dtr, gob):
    T = qkvb.shape[0]
    TC = TC_GDN
    full = lambda shape: pl.BlockSpec(shape, lambda i: (0,) * len(shape))
    return pl.pallas_call(
        _gdn_kernel,
        grid=(T // TC,),
        in_specs=[
            pl.BlockSpec((TC, 1536), lambda i: (i, 0)),
            pl.BlockSpec((TC, 512), lambda i: (i, 0)),
            pl.BlockSpec((TC, 128), lambda i: (i, 0)),
            pl.BlockSpec((16, TC), lambda i: (0, i)),
            full((8, 1536)), full((1, 128)), full((1, 128)), full((16, TC)), full((16, TC)), full((1, 128)),
        ],
        out_specs=pl.BlockSpec((TC, 512), lambda i: (i, 0)),
        out_shape=jax.ShapeDtypeStruct((T, 512), BF16),
        scratch_shapes=[pltpu.VMEM((TC + 8, 1536), F32), pltpu.VMEM((B_HEADS, B_DIM, B_DIM), F32)],
        compiler_params=pltpu.CompilerParams(
            dimension_semantics=("arbitrary",), vmem_limit_bytes=_vmem_limit(48 << 20)),
        name="gdn",
    )(qkvb, zb, ab, abt, conv_w8, alc, dtc, alr, dtr, gob)


def _memkv_kernel(m_ref, g_ref, w_ref, k_ref, v_ref):
    hm = _rms(m_ref[...], g_ref[...]).astype(BF16)
    kv = jnp.dot(hm, w_ref[...], preferred_element_type=F32)
    k_ref[...] = kv[:, :D_MODEL].astype(BF16)
    v_ref[...] = kv[:, D_MODEL:].astype(BF16)


def _memkv(mem2d, g_mem, w_kv):
    n = mem2d.shape[0]
    return pl.pallas_call(
        _memkv_kernel,
        out_shape=(jax.ShapeDtypeStruct((n, D_MODEL), BF16), jax.ShapeDtypeStruct((n, D_MODEL), BF16)),
        compiler_params=pltpu.CompilerParams(vmem_limit_bytes=_vmem_limit(32 << 20)),
        name="memkv",
    )(mem2d, g_mem, w_kv)


def _mix_kernel(x_ref, oa_ref, ob_ref, gates_ref, wua_ref, wub_ref, wout_ref, gx_ref, wq_ref, kx_ref, vx_ref,
                wo_ref, gm_ref, wr_ref, br_ref, x2t_ref, hmt_ref, idx_ref, wts_ref):
    TM = TM_MIX
    ma = jnp.dot(oa_ref[...], wua_ref[...], preferred_element_type=F32)
    mb = jnp.dot(ob_ref[...], wub_ref[...], preferred_element_type=F32)
    merged = _sigmoid(gates_ref[:, :D_MODEL]) * ma + _sigmoid(gates_ref[:, D_MODEL:]) * mb
    x1 = x_ref[...] + jnp.dot(merged.astype(BF16), wout_ref[...], preferred_element_type=F32)

    hx = _rms(x1, gx_ref[...]).astype(BF16)
    heads = []
    for h in range(X_HEADS):
        cs = slice(h * X_HEAD_DIM, (h + 1) * X_HEAD_DIM)
        qh = jnp.dot(hx, wq_ref[:, cs], preferred_element_type=F32).astype(BF16)
        s = lax.dot_general(qh, kx_ref[:, cs], _NT, preferred_element_type=F32) * (X_HEAD_DIM ** -0.5)
        s = s - jnp.max(s, axis=-1, keepdims=True)
        p = jnp.exp(s)
        p = p / jnp.sum(p, axis=-1, keepdims=True)
        heads.append(jnp.dot(p.astype(BF16), vx_ref[:, cs], preferred_element_type=F32).astype(BF16))
    o = jnp.concatenate(heads, axis=1)
    x2 = x1 + jnp.dot(o, wo_ref[...], preferred_element_type=F32)
    hm = _rms(x2, gm_ref[...])

    for g in range(D_MODEL // 128):
        x2t_ref[pl.ds(g, TM, stride=8), :] = x2[:, g * 128:(g + 1) * 128]
        hmt_ref[pl.ds(g, TM, stride=8), :] = hm[:, g * 128:(g + 1) * 128]

    logits = lax.dot_general(wr_ref[...], hm, _NT, precision=lax.Precision.HIGHEST,
                             preferred_element_type=F32) + br_ref[:, 0:1]
    eidx = lax.broadcasted_iota(I32, logits.shape, 0)
    vals, idxs = [], []
    cur = logits
    for _ in range(TOP_K):
        mx = jnp.max(cur, axis=0, keepdims=True)
        ix = jnp.min(jnp.where(cur == mx, eidx, N_EXPERTS), axis=0, keepdims=True)
        vals.append(mx)
        idxs.append(ix)
        cur = jnp.where(eidx == ix, -jnp.inf, cur)
    ex = [jnp.exp(v - vals[0]) for v in vals]
    den = ex[0] + ex[1] + ex[2] + ex[3]
    idx_ref[...] = jnp.concatenate(idxs, axis=0)
    wts_ref[...] = jnp.concatenate([e / den for e in ex], axis=0)


def _mix(x2d, oa, ob, gates, wua, wub, wout, gx, wq, kx, vx, wo, gm, wr_t, br):
    T = x2d.shape[0]
    TM = TM_MIX
    full = lambda a: pl.BlockSpec(a.shape, lambda i: (0,) * a.ndim)
    out_shape = (
        jax.ShapeDtypeStruct((T * 8, 128), F32),
        jax.ShapeDtypeStruct((T * 8, 128), F32),
        jax.ShapeDtypeStruct((TOP_K, T), I32),
        jax.ShapeDtypeStruct((TOP_K, T), F32),
    )
    return pl.pallas_call(
        _mix_kernel,
        grid=(T // TM,),
        in_specs=[
            pl.BlockSpec((TM, D_MODEL), lambda i: (i, 0)),
            pl.BlockSpec((TM, 512), lambda i: (i, 0)),
            pl.BlockSpec((TM, 512), lambda i: (i, 0)),
            pl.BlockSpec((TM, 2048), lambda i: (i, 0)),
            full(wua), full(wub), full(wout), full(gx), full(wq), full(kx), full(vx), full(wo), full(gm),
            full(wr_t), full(br),
        ],
        out_specs=(
            pl.BlockSpec((TM * 8, 128), lambda i: (i, 0)),
            pl.BlockSpec((TM * 8, 128), lambda i: (i, 0)),
            pl.BlockSpec((TOP_K, TM), lambda i: (0, i)),
            pl.BlockSpec((TOP_K, TM), lambda i: (0, i)),
        ),
        out_shape=out_shape,
        compiler_params=pltpu.CompilerParams(
            dimension_semantics=("arbitrary",), vmem_limit_bytes=_vmem_limit(56 << 20)),
        name="mix",
    )(x2d, oa, ob, gates, wua, wub, wout, gx, wq, kx, vx, wo, gm, wr_t, br)


def _moe_kernel(offs_ref, tok_ref, wt_ref, x2t_ref, hmt_ref, w1_ref, b1_ref, w2_ref, b2_ref, gf_ref,
                out_ref, yacc, xs, yst):
    j = pl.program_id(0)
    e = pl.program_id(1)
    TT = TT_MOE
    RB = RB_MOE
    NG = D_MODEL // 128

    @pl.when(e == 0)
    def _():
        yacc[...] = x2t_ref[...]

    start = offs_ref[j, e]
    n = offs_ref[j, e + 1] - start

    def row_block(blk, carry):
        r0 = start + blk * RB
        valid = n - blk * RB

        def gather8(gi, c):
            for u in range(8):
                r = gi * 8 + u
                t = tok_ref[0, 0, r0 + r]
                xs[pl.ds(pl.multiple_of(r * 8, 8), 8), :] = hmt_ref[pl.ds(pl.multiple_of(t * 8, 8), 8), :]
            return c

        lax.fori_loop(0, RB // 8, gather8, 0)
        xb = jnp.concatenate([xs[pl.ds(g, RB, stride=8), :] for g in range(NG)], axis=1).astype(BF16)
        hid = jnp.dot(xb, w1_ref[0], preferred_element_type=F32) + b1_ref[0]
        glu = jnp.minimum(hid[:, :D_FF], SWIGLU_LIMIT)
        lin = jnp.clip(hid[:, D_FF:], -SWIGLU_LIMIT, SWIGLU_LIMIT)
        act = glu * _sigmoid(SWIGLU_ALPHA * glu) * (lin + 1.0)
        ys = jnp.dot(act.astype(BF16), w2_ref[0], preferred_element_type=F32) + b2_ref[0]
        for g in range(NG):
            yst[pl.ds(g, RB, stride=8), :] = ys[:, g * 128:(g + 1) * 128]

        def scatter8(gi, c):
            for u in range(8):
                r = gi * 8 + u
                t = tok_ref[0, 0, r0 + r]
                w = jnp.where(r < valid, wt_ref[0, 0, r0 + r], 0.0)
                dst = pl.ds(pl.multiple_of(t * 8, 8), 8)
                yacc[dst, :] = yacc[dst, :] + w * yst[pl.ds(pl.multiple_of(r * 8, 8), 8), :]
            return c

        lax.fori_loop(0, RB // 8, scatter8, 0)
        return carry

    lax.fori_loop(0, (n + RB - 1) // RB, row_block, 0)

    @pl.when(e == N_EXPERTS - 1)
    def _():
        RC = 256

        def fin(ci, c):
            base = pl.multiple_of(ci * RC * 8, 8)
            yv = jnp.concatenate([yacc[pl.ds(base + g, RC, stride=8), :] for g in range(NG)], axis=1)
            out_ref[pl.ds(pl.multiple_of(ci * RC, 8), RC), :] = _rms(yv, gf_ref[...])
            return c

        lax.fori_loop(0, TT // RC, fin, 0)


def _moe(offs, tok, wt, x2t, hmt, w1, b1, w2, b2, g_final):
    T = x2t.shape[0] // 8
    TT = TT_MOE
    nt = T // TT
    LP = tok.shape[-1]
    one_buf = dict(pipeline_mode=pl.Buffered(1))
    grid_spec = pltpu.PrefetchScalarGridSpec(
        num_scalar_prefetch=1,
        grid=(nt, N_EXPERTS),
        in_specs=[
            pl.BlockSpec((1, 1, LP), lambda j, e, o: (j, 0, 0), memory_space=pltpu.SMEM),
            pl.BlockSpec((1, 1, LP), lambda j, e, o: (j, 0, 0), memory_space=pltpu.SMEM),
            pl.BlockSpec((TT * 8, 128), lambda j, e, o: (j, 0), **one_buf),
            pl.BlockSpec((TT * 8, 128), lambda j, e, o: (j, 0), **one_buf),
            pl.BlockSpec((1, D_MODEL, 2 * D_FF), lambda j, e, o: (e, 0, 0)),
            pl.BlockSpec((1, 1, 2 * D_FF), lambda j, e, o: (e, 0, 0)),
            pl.BlockSpec((1, D_FF, D_MODEL), lambda j, e, o: (e, 0, 0)),
            pl.BlockSpec((1, 1, D_MODEL), lambda j, e, o: (e, 0, 0)),
            pl.BlockSpec((1, D_MODEL), lambda j, e, o: (0, 0)),
        ],
        out_specs=pl.BlockSpec((TT, D_MODEL), lambda j, e, o: (j, 0), **one_buf),
        scratch_shapes=[
            pltpu.VMEM((TT * 8, 128), F32),
            pltpu.VMEM((RB_MOE * 8, 128), F32),
            pltpu.VMEM((RB_MOE * 8, 128), F32),
        ],
    )
    return pl.pallas_call(
        _moe_kernel,
        grid_spec=grid_spec,
        out_shape=jax.ShapeDtypeStruct((T, D_MODEL), F32),
        compiler_params=pltpu.CompilerParams(
            dimension_semantics=("arbitrary", "arbitrary"), vmem_limit_bytes=_vmem_limit(60 << 20)),
        name="moe",
    )(offs, tok, wt, x2t, hmt, w1, b1, w2, b2, g_final)


def _moe_lists(idx, wts):
    T = idx.shape[1]
    TT = TT_MOE
    nt = T // TT
    e_tile = idx.reshape(TOP_K, nt, TT).transpose(1, 0, 2).reshape(nt, TOP_K * TT)
    w_tile = wts.reshape(TOP_K, nt, TT).transpose(1, 0, 2).reshape(nt, TOP_K * TT)
    tok_local = jnp.tile(jnp.arange(TT, dtype=I32), TOP_K)
    order = jnp.argsort(e_tile, axis=1, stable=True)
    tok = jnp.take_along_axis(jnp.broadcast_to(tok_local, e_tile.shape), order, axis=1)
    wt = jnp.take_along_axis(w_tile, order, axis=1)
    counts = jnp.sum((e_tile[:, :, None] == jnp.arange(N_EXPERTS, dtype=I32)[None, None, :]).astype(I32), axis=1)
    offs = jnp.concatenate([jnp.zeros((nt, 1), I32), jnp.cumsum(counts, axis=1)], axis=1)
    offs = jnp.pad(offs, ((0, 0), (0, 40 - offs.shape[1])))
    pad = ((0, 0), (0, RB_MOE))
    return offs, jnp.pad(tok, pad)[:, None, :], jnp.pad(wt, pad)[:, None, :]


def kernel(x, mem, g_mix, w_in, rel_bias, lambda_q1, lambda_k1, lambda_q2, lambda_k2, g_subln, conv_w, a_log,
           dt_bias, g_out_b, w_up_a, w_up_b, w_out, g_xattn, g_mem, w_q_x, w_kv_x, w_o_x, g_moe, w_router,
           b_router, w_mlp1, b_mlp1, w_mlp2, b_mlp2, g_final):
    B_, S_, _ = x.shape
    assert B_ == 1 and S_ % TT_MOE == 0 and x.dtype == F32
    l = 0
    x2d = x.reshape(S_, D_MODEL)
    row = lambda v: v.reshape(1, -1).astype(F32)

    wi = w_in[l]
    ab_cols = jnp.pad(wi[:, 3584:3592], ((0, 0), (0, 120)))
    w_main = jnp.concatenate(
        [wi[:, 0:1024], wi[:, 1536:3072], wi[:, 3072:3584], wi[:, 3592:5640], ab_cols], axis=1).astype(BF16)
    w_vt = wi[:, 1024:1536].T.astype(BF16)
    w_abt = jnp.pad(wi[:, 3584:3592].T, ((0, 8), (0, 0))).astype(BF16)

    q, k, vt, qkvb, zb, gates, ab, abt = _inproj(x2d, row(g_mix[l]), w_main, w_vt, w_abt)

    lam = (jnp.exp(jnp.sum(lambda_q1[l].astype(F32) * lambda_k1[l].astype(F32)))
           - jnp.exp(jnp.sum(lambda_q2[l].astype(F32) * lambda_k2[l].astype(F32))) + LAMBDA_INIT).reshape(1)
    bias_d, bias_p = _attn_bias_tables(rel_bias)
    oa = _attention(lam, q, k, vt, bias_d, bias_p, row(g_subln[l]))

    lane_pad = lambda v: jnp.pad(v.astype(F32), (0, 128 - v.shape[0])).reshape(1, 128)
    row_bcast = lambda v: jnp.broadcast_to(jnp.pad(v.astype(F32), (0, 16 - v.shape[0]))[:, None], (16, TC_GDN))
    ob = _gdn(qkvb, zb, ab, abt, jnp.pad(conv_w[l].astype(F32), ((0, 4), (0, 0))),
              lane_pad(a_log[l]), lane_pad(dt_bias[l]), row_bcast(a_log[l]), row_bcast(dt_bias[l]),
              row(g_out_b[l]))

    kx, vx = _memkv(mem.reshape(-1, D_MODEL), row(g_mem[l]), w_kv_x[l].astype(BF16))
    br = jnp.broadcast_to(b_router[l].astype(F32)[:, None], (N_EXPERTS, 128))
    x2t, hmt, idx, wts = _mix(
        x2d, oa, ob, gates, w_up_a[l].astype(BF16), w_up_b[l].astype(BF16), w_out[l].astype(BF16),
        row(g_xattn[l]), w_q_x[l].astype(BF16), kx, vx, w_o_x[l].astype(BF16), row(g_moe[l]),
        w_router[l].T.astype(F32), br)

    offs, tok, wt = _moe_lists(idx, wts)
    out = _moe(offs, tok, wt, x2t, hmt, w_mlp1[l].astype(BF16), b_mlp1[l].astype(F32)[:, None, :],
               w_mlp2[l].astype(BF16), b_mlp2[l].astype(F32)[:, None, :], row(g_final))
    return out.reshape(B_, S_, D_MODEL)
```

```python
import functools
import math

import jax
import jax.numpy as jnp
from jax import lax
from jax.experimental import pallas as pl
from jax.experimental.pallas import tpu as pltpu

F32 = jnp.float32
BF16 = jnp.bfloat16
I32 = jnp.int32

D_MODEL = 1024
CHUNK = 64
EPS = 1e-6
A_HEADS = 4
A_QK_DIM = 64
A_V_DIM = 128
REL_BUCKETS = 32
REL_MAX_DIST = 128
B_HEADS = 4
B_DIM = 128
CONV_WIDTH = 4
X_HEADS = 4
X_HEAD_DIM = 256
N_EXPERTS = 32
TOP_K = 4
D_FF = 1024
SWIGLU_LIMIT = 7.0
SWIGLU_ALPHA = 1.702
LAMBDA_INIT = 0.8 - 0.6 * math.exp(-0.3 * 0)

LOG2E = 1.4426950408889634
NEG_BIG = -1e30

V7X_LANES = 128
V7X_SUBLANES = 8
V7X_VMEM_BYTES = 64 * 1024 * 1024

TM_PROJ = 512
TQ = 512
TK = 512
TC_GDN = 256
TM_MIX = 512
TT_MOE = 2048
RB_MOE = 128

C_QA, C_KA, C_QKVB, C_ZB, C_GATE, C_AB, C_END = 0, 512, 1024, 2560, 3072, 5120, 5248

_NT = (((1,), (1,)), ((), ()))
_TN = (((0,), (0,)), ((), ()))


def _rms(x, g):
    return x * lax.rsqrt(jnp.mean(x * x, axis=-1, keepdims=True) + EPS) * g


def _sigmoid(x):
    return 1.0 / (1.0 + jnp.exp(-x))


def _vmem_limit(nbytes):
    return int(min(nbytes, V7X_VMEM_BYTES - 4 * 1024 * 1024))


def _inproj_kernel(x_ref, g_ref, w_ref, wvt_ref, wabt_ref,
                   q_ref, k_ref, vt_ref, qkvb_ref, zb_ref, gates_ref, ab_ref, abt_ref):
    h = _rms(x_ref[...], g_ref[...]).astype(BF16)

    def mm(c0, c1):
        return jnp.dot(h, w_ref[:, c0:c1], preferred_element_type=F32)

    nb = TM_PROJ // TK
    qa = mm(C_QA, C_KA) * (A_QK_DIM ** -0.5 * LOG2E)
    ka = mm(C_KA, C_QKVB)
    vt = lax.dot_general(wvt_ref[...], h, _NT, preferred_element_type=F32)
    for hh in range(A_HEADS):
        cs = slice(hh * 128, (hh + 1) * 128)
        q_ref[hh] = qa[:, cs].astype(BF16)
        for b in range(nb):
            rs = slice(b * TK, (b + 1) * TK)
            k_ref[hh, b] = ka[rs, cs].astype(BF16)
            vt_ref[hh, b] = vt[cs, rs].astype(BF16)
    for j in range(3):
        qkvb_ref[:, j * 512:(j + 1) * 512] = mm(C_QKVB + j * 512, C_QKVB + (j + 1) * 512)
    zb_ref[...] = mm(C_ZB, C_GATE)
    for j in range(4):
        gates_ref[:, j * 512:(j + 1) * 512] = mm(C_GATE + j * 512, C_GATE + (j + 1) * 512)
    ab_ref[...] = mm(C_AB, C_END)
    abt_ref[...] = lax.dot_general(wabt_ref[...], h, _NT, preferred_element_type=F32)


def _inproj(x2d, g_mix, w_main, w_vt, w_abt):
    T = x2d.shape[0]
    n = T // TM_PROJ
    nkb = T // TK
    nb = TM_PROJ // TK
    full = lambda shape: pl.BlockSpec(shape, lambda i: (0,) * len(shape))
    out_shape = (
        jax.ShapeDtypeStruct((A_HEADS, T, 128), BF16),
        jax.ShapeDtypeStruct((A_HEADS, nkb, TK, 128), BF16),
        jax.ShapeDtypeStruct((A_HEADS, nkb, 128, TK), BF16),
        jax.ShapeDtypeStruct((T, 1536), F32),
        jax.ShapeDtypeStruct((T, 512), F32),
        jax.ShapeDtypeStruct((T, 2048), F32),
        jax.ShapeDtypeStruct((T, 128), F32),
        jax.ShapeDtypeStruct((16, T), F32),
    )
    out_specs = (
        pl.BlockSpec((A_HEADS, TM_PROJ, 128), lambda i: (0, i, 0)),
        pl.BlockSpec((A_HEADS, nb, TK, 128), lambda i: (0, i, 0, 0)),
        pl.BlockSpec((A_HEADS, nb, 128, TK), lambda i: (0, i, 0, 0)),
        pl.BlockSpec((TM_PROJ, 1536), lambda i: (i, 0)),
        pl.BlockSpec((TM_PROJ, 512), lambda i: (i, 0)),
        pl.BlockSpec((TM_PROJ, 2048), lambda i: (i, 0)),
        pl.BlockSpec((TM_PROJ, 128), lambda i: (i, 0)),
        pl.BlockSpec((16, TM_PROJ), lambda i: (0, i)),
    )
    return pl.pallas_call(
        _inproj_kernel,
        grid=(n,),
        in_specs=[
            pl.BlockSpec((TM_PROJ, D_MODEL), lambda i: (i, 0)),
            full((1, D_MODEL)),
            full(w_main.shape),
            full(w_vt.shape),
            full(w_abt.shape),
        ],
        out_specs=out_specs,
        out_shape=out_shape,
        compiler_params=pltpu.CompilerParams(
            dimension_semantics=("arbitrary",), vmem_limit_bytes=_vmem_limit(56 << 20)),
        name="inproj",
    )(x2d, g_mix, w_main, w_vt, w_abt)


def _attn_kernel(lam_ref, q_ref, k_ref, vt_ref, bd_ref, bp_ref, gs_ref, o_ref, m_sc, l_sc, acc_sc):
    qi = pl.program_id(1)
    q = q_ref[0]
    lane = lax.broadcasted_iota(I32, q.shape, 1)
    zero = jnp.zeros_like(q)
    qm = (jnp.where(lane < A_QK_DIM, q, zero), jnp.where(lane >= A_QK_DIM, q, zero))
    m_sc[...] = jnp.full(m_sc.shape, NEG_BIG, F32)
    l_sc[...] = jnp.zeros(l_sc.shape, F32)
    acc_sc[...] = jnp.zeros(acc_sc.shape, F32)

    def process(blocks):
        kbs = [k_ref[0, kb] for kb, _ in blocks]
        vtb = [vt_ref[0, kb] for kb, _ in blocks]
        for m in range(2):
            ss = []
            for kblk, (_, bias) in zip(kbs, blocks):
                s = lax.dot_general(kblk, qm[m], _NT, preferred_element_type=F32)
                ss.append(s if bias is None else s + bias[0])
            m_old = m_sc[m]
            m_new = m_old
            for s in ss:
                m_new = jnp.maximum(m_new, jnp.max(s, axis=0, keepdims=True))
            alpha = jnp.exp2(m_old - m_new)
            l_new = alpha * l_sc[m]
            acc = alpha * acc_sc[m]
            for s, vb in zip(ss, vtb):
                p = jnp.exp2(s - m_new)
                l_new = l_new + jnp.sum(p, axis=0, keepdims=True)
                acc = acc + jnp.dot(vb, p.astype(BF16), preferred_element_type=F32)
            l_sc[m] = l_new
            acc_sc[m] = acc
            m_sc[m] = m_new

    nfar = jnp.maximum(qi - 1, 0)

    def far_body(it, c):
        process([(2 * it, None), (2 * it + 1, None)])
        return c

    lax.fori_loop(0, nfar // 2, far_body, 0)

    @pl.when(nfar % 2 == 1)
    def _():
        process([(nfar - 1, None)])

    @pl.when(qi >= 1)
    def _():
        process([(qi - 1, bp_ref), (qi, bd_ref)])

    @pl.when(qi == 0)
    def _():
        process([(qi, bd_ref)])

    o = acc_sc[0] / l_sc[0] - lam_ref[0] * (acc_sc[1] / l_sc[1])
    ot = o.T
    o_ref[...] = (_rms(ot, gs_ref[...]) * (1.0 - LAMBDA_INIT)).astype(BF16)


def _attention(lam, q, k, vt, bias_d, bias_p, g_subln):
    T = q.shape[1]
    nq = T // TQ
    nkb = T // TK
    return pl.pallas_call(
        _attn_kernel,
        grid=(A_HEADS, nq),
        in_specs=[
            pl.BlockSpec(memory_space=pltpu.SMEM),
            pl.BlockSpec((1, TQ, 128), lambda h, i: (h, i, 0)),
            pl.BlockSpec((1, nkb, TK, 128), lambda h, i: (h, 0, 0, 0)),
            pl.BlockSpec((1, nkb, 128, TK), lambda h, i: (h, 0, 0, 0)),
            pl.BlockSpec((1, TK, TQ), lambda h, i: (h, 0, 0)),
            pl.BlockSpec((1, TK, TQ), lambda h, i: (h, 0, 0)),
            pl.BlockSpec((1, 128), lambda h, i: (0, 0)),
        ],
        out_specs=pl.BlockSpec((TQ, 128), lambda h, i: (i, h)),
        out_shape=jax.ShapeDtypeStruct((T, A_HEADS * A_V_DIM), BF16),
        scratch_shapes=[
            pltpu.VMEM((2, 1, TQ), F32),
            pltpu.VMEM((2, 1, TQ), F32),
            pltpu.VMEM((2, 128, TQ), F32),
        ],
        compiler_params=pltpu.CompilerParams(
            dimension_semantics=("arbitrary", "arbitrary"), vmem_limit_bytes=_vmem_limit(48 << 20)),
        name="attn",
    )(lam, q, k, vt, bias_d, bias_p, g_subln)


def _t5_bucket(rel):
    half = REL_BUCKETS // 2
    max_exact = half // 2
    ret = jnp.where(rel > 0, half, 0)
    n = jnp.abs(rel)
    large = max_exact + (jnp.log(jnp.maximum(n, 1).astype(F32) / max_exact)
                         / math.log(REL_MAX_DIST / max_exact) * (half - max_exact)).astype(I32)
    large = jnp.minimum(large, half - 1)
    return ret + jnp.where(n < max_exact, n, large)


def _attn_bias_tables(rel_bias):
    assert TK >= REL_MAX_DIST and TK == TQ and TK % CHUNK == 0
    kk = jnp.arange(TK, dtype=I32)[:, None]
    qq = jnp.arange(TQ, dtype=I32)[None, :]
    rb = rel_bias.astype(F32)
    far = rb[_t5_bucket(jnp.full((1,), -REL_MAX_DIST, I32))[0]]
    table = ((rb - far[None, :]) * LOG2E).T

    def lookup(rel):
        onehot = (_t5_bucket(rel).reshape(1, -1) == jnp.arange(REL_BUCKETS, dtype=I32)[:, None]).astype(F32)
        return jnp.dot(table, onehot, precision=lax.Precision.HIGHEST).reshape(A_HEADS, TK, TQ)

    bd = jnp.where((kk // CHUNK <= qq // CHUNK)[None], lookup(kk - qq), NEG_BIG)
    bp = lookup(kk - TK - qq)
    return bd, bp


def _gdn_kernel(x_ref, zb_ref, ab_ref, abt_ref, cw_ref, alc_ref, dtc_ref, alr_ref, dtr_ref, gob_ref,
                o_ref, xbuf, s_sc):
    i = pl.program_id(0)
    TC = TC_GDN
    NCH = TC // CHUNK
    HB = B_HEADS * CHUNK

    @pl.when(i == 0)
    def _():
        xbuf[0:8, :] = jnp.zeros((8, 1536), F32)
        s_sc[...] = jnp.zeros(s_sc.shape, F32)

    xbuf[8:8 + TC, :] = x_ref[...]
    y = cw_ref[3:4, :] * xbuf[8:8 + TC, :]
    for d in range(1, CONV_WIDTH):
        y = y + cw_ref[3 - d:4 - d, :] * xbuf[8 - d:8 - d + TC, :]
    xbuf[0:8, :] = x_ref[TC - 8:TC, :]
    a = y * _sigmoid(y)

    def l2n(t):
        return t * lax.rsqrt(jnp.sum(t * t, axis=-1, keepdims=True) + EPS)

    qh = [l2n(a[:, h * 128:(h + 1) * 128]) * (B_DIM ** -0.5) for h in range(B_HEADS)]
    kh = [l2n(a[:, 512 + h * 128:512 + (h + 1) * 128]) for h in range(B_HEADS)]
    vh = [a[:, 1024 + h * 128:1024 + (h + 1) * 128] for h in range(B_HEADS)]

    def softplus(t):
        return jnp.maximum(t, 0.0) + jnp.log(1.0 + jnp.exp(-jnp.abs(t)))

    ab = ab_ref[...]
    g_col = -jnp.exp(alc_ref[...]) * softplus(ab + dtc_ref[...])
    beta_col = _sigmoid(ab)
    g_row = -jnp.exp(alr_ref[...]) * softplus(abt_ref[...] + dtr_ref[...])

    rt = lax.broadcasted_iota(I32, (TC, TC), 0)
    ct = lax.broadcasted_iota(I32, (TC, TC), 1)
    same_chunk = (rt // CHUNK) == (ct // CHUNK)
    tril = jnp.where(same_chunk & (rt >= ct), 1.0, 0.0).astype(F32)
    triu = jnp.where(same_chunk & (rt <= ct), 1.0, 0.0).astype(F32)
    gc_col = jnp.dot(tril, g_col, precision=lax.Precision.HIGHEST, preferred_element_type=F32)
    gc_row = jnp.dot(g_row, triu, precision=lax.Precision.HIGHEST, preferred_element_type=F32)

    ri = lax.broadcasted_iota(I32, (HB, HB), 0)
    ci = lax.broadcasted_iota(I32, (HB, HB), 1)
    same_head = (ri // CHUNK) == (ci // CHUNK)
    incl = same_head & (ri >= ci)
    strict = same_head & (ri > ci)
    eye = jnp.where(ri == ci, 1.0, 0.0).astype(F32)

    def level_mask(s):
        return ((ri // (2 * s)) == (ci // (2 * s))) & (((ri // s) % 2) == 1) & (((ci // s) % 2) == 0)

    for c in range(NCH):
        rs = slice(c * CHUNK, (c + 1) * CHUNK)
        last = slice(c * CHUNK + CHUNK - 1, (c + 1) * CHUNK)
        cat0 = lambda parts: jnp.concatenate(parts, axis=0)
        K = cat0([kh[h][rs] for h in range(B_HEADS)])
        Q = cat0([qh[h][rs] for h in range(B_HEADS)])
        V = cat0([vh[h][rs] for h in range(B_HEADS)])
        beta = cat0([beta_col[rs, 4 + h:5 + h] for h in range(B_HEADS)])
        gcc = cat0([gc_col[rs, h:h + 1] for h in range(B_HEADS)])
        gl = cat0([jnp.broadcast_to(gc_col[last, h:h + 1], (CHUNK, 1)) for h in range(B_HEADS)])
        gcr = jnp.concatenate([gc_row[h:h + 1, rs] for h in range(B_HEADS)], axis=1)

        Kb = K * beta
        Vb = V * beta
        dec = jnp.where(incl, jnp.exp(jnp.where(incl, gcc - gcr, 0.0)), 0.0)
        L = jnp.where(strict, lax.dot_general(Kb, K, _NT, preferred_element_type=F32) * dec, 0.0)
        QK = lax.dot_general(Q, K, _NT, preferred_element_type=F32) * dec

        inv = eye - jnp.where(level_mask(1), L, 0.0)
        for s in (2, 4, 8, 16, 32):
            cs_ = jnp.where(level_mask(s), L, 0.0)
            t_ = jnp.dot(inv, cs_, preferred_element_type=F32)
            inv = inv - jnp.dot(t_, inv, preferred_element_type=F32)

        egc = jnp.exp(gcc)
        rhs = jnp.concatenate([Vb, Kb * egc], axis=1)
        sol = jnp.dot(inv, rhs, preferred_element_type=F32)
        U = sol[:, :B_DIM]
        W = sol[:, B_DIM:]
        Qd = Q * egc
        Kd = K * jnp.exp(gl - gcc)

        vnew = []
        ost = []
        for h in range(B_HEADS):
            hs = slice(h * CHUNK, (h + 1) * CHUNK)
            S = s_sc[h]
            vn = U[hs] - jnp.dot(W[hs], S, preferred_element_type=F32)
            ost.append(jnp.dot(Qd[hs], S, preferred_element_type=F32))
            s_sc[h] = S * jnp.exp(gl[h * CHUNK:h * CHUNK + 1, :]) + lax.dot_general(
                Kd[hs], vn, _TN, preferred_element_type=F32)
            vnew.append(vn)
        O = cat0(ost) + jnp.dot(QK, cat0(vnew), preferred_element_type=F32)
        for h in range(B_HEADS):
            oh = _rms(O[h * CHUNK:(h + 1) * CHUNK], gob_ref[...])
            z = zb_ref[rs, h * 128:(h + 1) * 128]
            o_ref[rs, h * 128:(h + 1) * 128] = (oh * (z * _sigmoid(z))).astype(BF16)


def _gdn(qkvb, zb, ab, abt, conv_w8, alc, dtc, alr, dtr, gob):
    T = qkvb.shape[0]
    TC = TC_GDN
    full = lambda shape: pl.BlockSpec(shape, lambda i: (0,) * len(shape))
    return pl.pallas_call(
        _gdn_kernel,
        grid=(T // TC,),
        in_specs=[
            pl.BlockSpec((TC, 1536), lambda i: (i, 0)),
            pl.BlockSpec((TC, 512), lambda i: (i, 0)),
            pl.BlockSpec((TC, 128), lambda i: (i, 0)),
            pl.BlockSpec((16, TC), lambda i: (0, i)),
            full((8, 1536)), full((1, 128)), full((1, 128)), full((16, TC)), full((16, TC)), full((1, 128)),
        ],
        out_specs=pl.BlockSpec((TC, 512), lambda i: (i, 0)),
        out_shape=jax.ShapeDtypeStruct((T, 512), BF16),
        scratch_shapes=[pltpu.VMEM((TC + 8, 1536), F32), pltpu.VMEM((B_HEADS, B_DIM, B_DIM), F32)],
        compiler_params=pltpu.CompilerParams(
            dimension_semantics=("arbitrary",), vmem_limit_bytes=_vmem_limit(48 << 20)),
        name="gdn",
    )(qkvb, zb, ab, abt, conv_w8, alc, dtc, alr, dtr, gob)


def _memkv_kernel(m_ref, g_ref, w_ref, k_ref, v_ref):
    hm = _rms(m_ref[...], g_ref[...]).astype(BF16)
    kv = jnp.dot(hm, w_ref[...], preferred_element_type=F32)
    k_ref[...] = kv[:, :D_MODEL].astype(BF16)
    v_ref[...] = kv[:, D_MODEL:].astype(BF16)


def _memkv(mem2d, g_mem, w_kv):
    n = mem2d.shape[0]
    return pl.pallas_call(
        _memkv_kernel,
        out_shape=(jax.ShapeDtypeStruct((n, D_MODEL), BF16), jax.ShapeDtypeStruct((n, D_MODEL), BF16)),
        compiler_params=pltpu.CompilerParams(vmem_limit_bytes=_vmem_limit(32 << 20)),
        name="memkv",
    )(mem2d, g_mem, w_kv)


def _mix_kernel(x_ref, oa_ref, ob_ref, gates_ref, wua_ref, wub_ref, wout_ref, gx_ref, wq_ref, kx_ref, vx_ref,
                wo_ref, gm_ref, wr_ref, br_ref, x2t_ref, hmt_ref, idx_ref, wts_ref):
    TM = TM_MIX
    ma = jnp.dot(oa_ref[...], wua_ref[...], preferred_element_type=F32)
    mb = jnp.dot(ob_ref[...], wub_ref[...], preferred_element_type=F32)
    merged = _sigmoid(gates_ref[:, :D_MODEL]) * ma + _sigmoid(gates_ref[:, D_MODEL:]) * mb
    x1 = x_ref[...] + jnp.dot(merged.astype(BF16), wout_ref[...], preferred_element_type=F32)

    hx = _rms(x1, gx_ref[...]).astype(BF16)
    heads = []
    for h in range(X_HEADS):
        cs = slice(h * X_HEAD_DIM, (h + 1) * X_HEAD_DIM)
        qh = jnp.dot(hx, wq_ref[:, cs], preferred_element_type=F32).astype(BF16)
        s = lax.dot_general(qh, kx_ref[:, cs], _NT, preferred_element_type=F32) * (X_HEAD_DIM ** -0.5)
        s = s - jnp.max(s, axis=-1, keepdims=True)
        p = jnp.exp(s)
        p = p / jnp.sum(p, axis=-1, keepdims=True)
        heads.append(jnp.dot(p.astype(BF16), vx_ref[:, cs], preferred_element_type=F32).astype(BF16))
    o = jnp.concatenate(heads, axis=1)
    x2 = x1 + jnp.dot(o, wo_ref[...], preferred_element_type=F32)
    hm = _rms(x2, gm_ref[...])

    for g in range(D_MODEL // 128):
        x2t_ref[pl.ds(g, TM, stride=8), :] = x2[:, g * 128:(g + 1) * 128]
        hmt_ref[pl.ds(g, TM, stride=8), :] = hm[:, g * 128:(g + 1) * 128]

    logits = lax.dot_general(wr_ref[...], hm, _NT, precision=lax.Precision.HIGHEST,
                             preferred_element_type=F32) + br_ref[:, 0:1]
    eidx = lax.broadcasted_iota(I32, logits.shape, 0)
    vals, idxs = [], []
    cur = logits
    for _ in range(TOP_K):
        mx = jnp.max(cur, axis=0, keepdims=True)
        ix = jnp.min(jnp.where(cur == mx, eidx, N_EXPERTS), axis=0, keepdims=True)
        vals.append(mx)
        idxs.append(ix)
        cur = jnp.where(eidx == ix, -jnp.inf, cur)
    ex = [jnp.exp(v - vals[0]) for v in vals]
    den = ex[0] + ex[1] + ex[2] + ex[3]
    idx_ref[...] = jnp.concatenate(idxs, axis=0)
    wts_ref[...] = jnp.concatenate([e / den for e in ex], axis=0)


def _mix(x2d, oa, ob, gates, wua, wub, wout, gx, wq, kx, vx, wo, gm, wr_t, br):
    T = x2d.shape[0]
    TM = TM_MIX
    full = lambda a: pl.BlockSpec(a.shape, lambda i: (0,) * a.ndim)
    out_shape = (
        jax.ShapeDtypeStruct((T * 8, 128), F32),
        jax.ShapeDtypeStruct((T * 8, 128), F32),
        jax.ShapeDtypeStruct((TOP_K, T), I32),
        jax.ShapeDtypeStruct((TOP_K, T), F32),
    )
    return pl.pallas_call(
        _mix_kernel,
        grid=(T // TM,),
        in_specs=[
            pl.BlockSpec((TM, D_MODEL), lambda i: (i, 0)),
            pl.BlockSpec((TM, 512), lambda i: (i, 0)),
            pl.BlockSpec((TM, 512), lambda i: (i, 0)),
            pl.BlockSpec((TM, 2048), lambda i: (i, 0)),
            full(wua), full(wub), full(wout), full(gx), full(wq), full(kx), full(vx), full(wo), full(gm),
            full(wr_t), full(br),
        ],
        out_specs=(
            pl.BlockSpec((TM * 8, 128), lambda i: (i, 0)),
            pl.BlockSpec((TM * 8, 128), lambda i: (i, 0)),
            pl.BlockSpec((TOP_K, TM), lambda i: (0, i)),
            pl.BlockSpec((TOP_K, TM), lambda i: (0, i)),
        ),
        out_shape=out_shape,
        compiler_params=pltpu.CompilerParams(
            dimension_semantics=("arbitrary",), vmem_limit_bytes=_vmem_limit(56 << 20)),
        name="mix",
    )(x2d, oa, ob, gates, wua, wub, wout, gx, wq, kx, vx, wo, gm, wr_t, br)


def _moe_kernel(offs_ref, tok_ref, wt_ref, x2t_ref, hmt_ref, w1_ref, b1_ref, w2_ref, b2_ref, gf_ref,
                out_ref, yacc, xs, yst):
    j = pl.program_id(0)
    e = pl.program_id(1)
    TT = TT_MOE
    RB = RB_MOE
    NG = D_MODEL // 128

    @pl.when(e == 0)
    def _():
        yacc[...] = x2t_ref[...]

    start = offs_ref[j, e]
    n = offs_ref[j, e + 1] - start

    def row_block(blk, carry):
        r0 = start + blk * RB
        valid = n - blk * RB

        def gather8(gi, c):
            for u in range(8):
                r = gi * 8 + u
                t = tok_ref[0, 0, r0 + r]
                xs[pl.ds(pl.multiple_of(r * 8, 8), 8), :] = hmt_ref[pl.ds(pl.multiple_of(t * 8, 8), 8), :]
            return c

        lax.fori_loop(0, RB // 8, gather8, 0)
        xb = jnp.concatenate([xs[pl.ds(g, RB, stride=8), :] for g in range(NG)], axis=1).astype(BF16)
        hid = jnp.dot(xb, w1_ref[0], preferred_element_type=F32) + b1_ref[0]
        glu = jnp.minimum(hid[:, :D_FF], SWIGLU_LIMIT)
        lin = jnp.clip(hid[:, D_FF:], -SWIGLU_LIMIT, SWIGLU_LIMIT)
        act = glu * _sigmoid(SWIGLU_ALPHA * glu) * (lin + 1.0)
        ys = jnp.dot(act.astype(BF16), w2_ref[0], preferred_element_type=F32) + b2_ref[0]
        for g in range(NG):
            yst[pl.ds(g, RB, stride=8), :] = ys[:, g * 128:(g + 1) * 128]

        def scatter8(gi, c):
            for u in range(8):
                r = gi * 8 + u
                t = tok_ref[0, 0, r0 + r]
                w = jnp.where(r < valid, wt_ref[0, 0, r0 + r], 0.0)
                dst = pl.ds(pl.multiple_of(t * 8, 8), 8)
                yacc[dst, :] = yacc[dst, :] + w * yst[pl.ds(pl.multiple_of(r * 8, 8), 8), :]
            return c

        lax.fori_loop(0, RB // 8, scatter8, 0)
        return carry

    lax.fori_loop(0, (n + RB - 1) // RB, row_block, 0)

    @pl.when(e == N_EXPERTS - 1)
    def _():
        RC = 256

        def fin(ci, c):
            base = pl.multiple_of(ci * RC * 8, 8)
            yv = jnp.concatenate([yacc[pl.ds(base + g, RC, stride=8), :] for g in range(NG)], axis=1)
            out_ref[pl.ds(pl.multiple_of(ci * RC, 8), RC), :] = _rms(yv, gf_ref[...])
            return c

        lax.fori_loop(0, TT // RC, fin, 0)


def _moe(offs, tok, wt, x2t, hmt, w1, b1, w2, b2, g_final):
    T = x2t.shape[0] // 8
    TT = TT_MOE
    nt = T // TT
    LP = tok.shape[-1]
    one_buf = dict(pipeline_mode=pl.Buffered(1))
    grid_spec = pltpu.PrefetchScalarGridSpec(
        num_scalar_prefetch=1,
        grid=(nt, N_EXPERTS),
        in_specs=[
            pl.BlockSpec((1, 1, LP), lambda j, e, o: (j, 0, 0), memory_space=pltpu.SMEM),
            pl.BlockSpec((1, 1, LP), lambda j, e, o: (j, 0, 0), memory_space=pltpu.SMEM),
            pl.BlockSpec((TT * 8, 128), lambda j, e, o: (j, 0), **one_buf),
            pl.BlockSpec((TT * 8, 128), lambda j, e, o: (j, 0), **one_buf),
            pl.BlockSpec((1, D_MODEL, 2 * D_FF), lambda j, e, o: (e, 0, 0)),
            pl.BlockSpec((1, 1, 2 * D_FF), lambda j, e, o: (e, 0, 0)),
            pl.BlockSpec((1, D_FF, D_MODEL), lambda j, e, o: (e, 0, 0)),
            pl.BlockSpec((1, 1, D_MODEL), lambda j, e, o: (e, 0, 0)),
            pl.BlockSpec((1, D_MODEL), lambda j, e, o: (0, 0)),
        ],
        out_specs=pl.BlockSpec((TT, D_MODEL), lambda j, e, o: (j, 0), **one_buf),
        scratch_shapes=[
            pltpu.VMEM((TT * 8, 128), F32),
            pltpu.VMEM((RB_MOE * 8, 128), F32),
            pltpu.VMEM((RB_MOE * 8, 128), F32),
        ],
    )
    return pl.pallas_call(
        _moe_kernel,
        grid_spec=grid_spec,
        out_shape=jax.ShapeDtypeStruct((T, D_MODEL), F32),
        compiler_params=pltpu.CompilerParams(
            dimension_semantics=("arbitrary", "arbitrary"), vmem_limit_bytes=_vmem_limit(60 << 20)),
        name="moe",
    )(offs, tok, wt, x2t, hmt, w1, b1, w2, b2, g_final)


def _moe_lists(idx, wts):
    T = idx.shape[1]
    TT = TT_MOE
    nt = T // TT
    e_tile = idx.reshape(TOP_K, nt, TT).transpose(1, 0, 2).reshape(nt, TOP_K * TT)
    w_tile = wts.reshape(TOP_K, nt, TT).transpose(1, 0, 2).reshape(nt, TOP_K * TT)
    tok_local = jnp.tile(jnp.arange(TT, dtype=I32), TOP_K)
    order = jnp.argsort(e_tile, axis=1, stable=True)
    tok = jnp.take_along_axis(jnp.broadcast_to(tok_local, e_tile.shape), order, axis=1)
    wt = jnp.take_along_axis(w_tile, order, axis=1)
    counts = jnp.sum((e_tile[:, :, None] == jnp.arange(N_EXPERTS, dtype=I32)[None, None, :]).astype(I32), axis=1)
    offs = jnp.concatenate([jnp.zeros((nt, 1), I32), jnp.cumsum(counts, axis=1)], axis=1)
    offs = jnp.pad(offs, ((0, 0), (0, 40 - offs.shape[1])))
    pad = ((0, 0), (0, RB_MOE))
    return offs, jnp.pad(tok, pad)[:, None, :], jnp.pad(wt, pad)[:, None, :]


def kernel(x, mem, g_mix, w_in, rel_bias, lambda_q1, lambda_k1, lambda_q2, lambda_k2, g_subln, conv_w, a_log,
           dt_bias, g_out_b, w_up_a, w_up_b, w_out, g_xattn, g_mem, w_q_x, w_kv_x, w_o_x, g_moe, w_router,
           b_router, w_mlp1, b_mlp1, w_mlp2, b_mlp2, g_final):
    B_, S_, _ = x.shape
    assert B_ == 1 and S_ % TT_MOE == 0 and x.dtype == F32
    l = 0
    x2d = x.reshape(S_, D_MODEL)
    row = lambda v: v.reshape(1, -1).astype(F32)

    wi = w_in[l]
    ab_cols = jnp.pad(wi[:, 3584:3592], ((0, 0), (0, 120)))
    w_main = jnp.concatenate(
        [wi[:, 0:1024], wi[:, 1536:3072], wi[:, 3072:3584], wi[:, 3592:5640], ab_cols], axis=1).astype(BF16)
    w_vt = wi[:, 1024:1536].T.astype(BF16)
    w_abt = jnp.pad(wi[:, 3584:3592].T, ((0, 8), (0, 0))).astype(BF16)

    q, k, vt, qkvb, zb, gates, ab, abt = _inproj(x2d, row(g_mix[l]), w_main, w_vt, w_abt)

    lam = (jnp.exp(jnp.sum(lambda_q1[l].astype(F32) * lambda_k1[l].astype(F32)))
           - jnp.exp(jnp.sum(lambda_q2[l].astype(F32) * lambda_k2[l].astype(F32))) + LAMBDA_INIT).reshape(1)
    bias_d, bias_p = _attn_bias_tables(rel_bias)
    oa = _attention(lam, q, k, vt, bias_d, bias_p, row(g_subln[l]))

    lane_pad = lambda v: jnp.pad(v.astype(F32), (0, 128 - v.shape[0])).reshape(1, 128)
    row_bcast = lambda v: jnp.broadcast_to(jnp.pad(v.astype(F32), (0, 16 - v.shape[0]))[:, None], (16, TC_GDN))
    ob = _gdn(qkvb, zb, ab, abt, jnp.pad(conv_w[l].astype(F32), ((0, 4), (0, 0))),
              lane_pad(a_log[l]), lane_pad(dt_bias[l]), row_bcast(a_log[l]), row_bcast(dt_bias[l]),
              row(g_out_b[l]))

    kx, vx = _memkv(mem.reshape(-1, D_MODEL), row(g_mem[l]), w_kv_x[l].astype(BF16))
    br = jnp.broadcast_to(b_router[l].astype(F32)[:, None], (N_EXPERTS, 128))
    x2t, hmt, idx, wts = _mix(
        x2d, oa, ob, gates, w_up_a[l].astype(BF16), w_up_b[l].astype(BF16), w_out[l].astype(BF16),
        row(g_xattn[l]), w_q_x[l].astype(BF16), kx, vx, w_o_x[l].astype(BF16), row(g_moe[l]),
        w_router[l].T.astype(F32), br)

    offs, tok, wt = _moe_lists(idx, wts)
    out = _moe(offs, tok, wt, x2t, hmt, w_mlp1[l].astype(BF16), b_mlp1[l].astype(F32)[:, None, :],
               w_mlp2[l].astype(BF16), b_mlp2[l].astype(F32)[:, None, :], row(g_final))
    return out.reshape(B_, S_, D_MODEL)
```

```python
import functools
import math

import jax
import jax.numpy as jnp
from jax import lax
from jax.experimental import pallas as pl
from jax.experimental.pallas import tpu as pltpu

F32 = jnp.float32
BF16 = jnp.bfloat16
I32 = jnp.int32

D_MODEL = 1024
CHUNK = 64
EPS = 1e-6
A_HEADS = 4
A_QK_DIM = 64
A_V_DIM = 128
REL_BUCKETS = 32
REL_MAX_DIST = 128
B_HEADS = 4
B_DIM = 128
CONV_WIDTH = 4
X_HEADS = 4
X_HEAD_DIM = 256
N_EXPERTS = 32
TOP_K = 4
D_FF = 1024
SWIGLU_LIMIT = 7.0
SWIGLU_ALPHA = 1.702
LAMBDA_INIT = 0.8 - 0.6 * math.exp(-0.3 * 0)

LOG2E = 1.4426950408889634
NEG_BIG = -1e30

V7X_LANES = 128
V7X_SUBLANES = 8
V7X_VMEM_BYTES = 64 * 1024 * 1024

TM_PROJ = 512
TQ = 512
TK = 512
TC_GDN = 256
TM_MIX = 512
TT_MOE = 2048
RB_MOE = 256

C_QA, C_KA, C_QKVB, C_ZB, C_GATE, C_AB, C_END = 0, 512, 1024, 2560, 3072, 5120, 5248

_NT = (((1,), (1,)), ((), ()))
_TN = (((0,), (0,)), ((), ()))


def _rms(x, g):
    return x * lax.rsqrt(jnp.mean(x * x, axis=-1, keepdims=True) + EPS) * g


def _sigmoid(x):
    return 1.0 / (1.0 + jnp.exp(-x))


def _vmem_limit(nbytes):
    return int(min(nbytes, V7X_VMEM_BYTES - 4 * 1024 * 1024))


def _inproj_kernel(x_ref, g_ref, w_ref, wvt_ref, wabt_ref,
                   q_ref, k_ref, vt_ref, qkvb_ref, zb_ref, gates_ref, ab_ref, abt_ref):
    h = _rms(x_ref[...], g_ref[...]).astype(BF16)

    def mm(c0, c1):
        return jnp.dot(h, w_ref[:, c0:c1], preferred_element_type=F32)

    nb = TM_PROJ // TK
    qa = mm(C_QA, C_KA) * (A_QK_DIM ** -0.5 * LOG2E)
    ka = mm(C_KA, C_QKVB)
    vt = lax.dot_general(wvt_ref[...], h, _NT, preferred_element_type=F32)
    for hh in range(A_HEADS):
        cs = slice(hh * 128, (hh + 1) * 128)
        q_ref[hh] = qa[:, cs].astype(BF16)
        for b in range(nb):
            rs = slice(b * TK, (b + 1) * TK)
            k_ref[hh, b] = ka[rs, cs].astype(BF16)
            vt_ref[hh, b] = vt[cs, rs].astype(BF16)
    for j in range(3):
        qkvb_ref[:, j * 512:(j + 1) * 512] = mm(C_QKVB + j * 512, C_QKVB + (j + 1) * 512)
    zb_ref[...] = mm(C_ZB, C_GATE)
    for j in range(4):
        gates_ref[:, j * 512:(j + 1) * 512] = mm(C_GATE + j * 512, C_GATE + (j + 1) * 512)
    ab_ref[...] = mm(C_AB, C_END)
    abt_ref[...] = lax.dot_general(wabt_ref[...], h, _NT, preferred_element_type=F32)


def _inproj(x2d, g_mix, w_main, w_vt, w_abt):
    T = x2d.shape[0]
    n = T // TM_PROJ
    nkb = T // TK
    nb = TM_PROJ // TK
    full = lambda shape: pl.BlockSpec(shape, lambda i: (0,) * len(shape))
    out_shape = (
        jax.ShapeDtypeStruct((A_HEADS, T, 128), BF16),
        jax.ShapeDtypeStruct((A_HEADS, nkb, TK, 128), BF16),
        jax.ShapeDtypeStruct((A_HEADS, nkb, 128, TK), BF16),
        jax.ShapeDtypeStruct((T, 1536), F32),
        jax.ShapeDtypeStruct((T, 512), F32),
        jax.ShapeDtypeStruct((T, 2048), F32),
        jax.ShapeDtypeStruct((T, 128), F32),
        jax.ShapeDtypeStruct((16, T), F32),
    )
    out_specs = (
        pl.BlockSpec((A_HEADS, TM_PROJ, 128), lambda i: (0, i, 0)),
        pl.BlockSpec((A_HEADS, nb, TK, 128), lambda i: (0, i, 0, 0)),
        pl.BlockSpec((A_HEADS, nb, 128, TK), lambda i: (0, i, 0, 0)),
        pl.BlockSpec((TM_PROJ, 1536), lambda i: (i, 0)),
        pl.BlockSpec((TM_PROJ, 512), lambda i: (i, 0)),
        pl.BlockSpec((TM_PROJ, 2048), lambda i: (i, 0)),
        pl.BlockSpec((TM_PROJ, 128), lambda i: (i, 0)),
        pl.BlockSpec((16, TM_PROJ), lambda i: (0, i)),
    )
    return pl.pallas_call(
        _inproj_kernel,
        grid=(n,),
        in_specs=[
            pl.BlockSpec((TM_PROJ, D_MODEL), lambda i: (i, 0)),
            full((1, D_MODEL)),
            full(w_main.shape),
            full(w_vt.shape),
            full(w_abt.shape),
        ],
        out_specs=out_specs,
        out_shape=out_shape,
        compiler_params=pltpu.CompilerParams(
            dimension_semantics=("arbitrary",), vmem_limit_bytes=_vmem_limit(56 << 20)),
        name="inproj",
    )(x2d, g_mix, w_main, w_vt, w_abt)


def _attn_kernel(lam_ref, q_ref, k_ref, vt_ref, bd_ref, bp_ref, gs_ref, o_ref, m_sc, l_sc, acc_sc, s_buf, pm_buf):
    qi = pl.program_id(1)
    q = q_ref[0]
    lane = lax.broadcasted_iota(I32, q.shape, 1)
    zero = jnp.zeros_like(q)
    qm = (jnp.where(lane < A_QK_DIM, q, zero), jnp.where(lane >= A_QK_DIM, q, zero))
    m_sc[...] = jnp.full(m_sc.shape, NEG_BIG, F32)
    l_sc[...] = jnp.zeros(l_sc.shape, F32)
    acc_sc[...] = jnp.zeros(acc_sc.shape, F32)

    nkb = k_ref.shape[1]

    def stage_a(slot, kb0):
        for j in range(2):
            kblk = k_ref[0, jnp.minimum(kb0 + j, nkb - 1)]
            for m in range(2):
                s = lax.dot_general(kblk, qm[m], _NT, preferred_element_type=F32)
                s_buf[slot, m, j] = s
                pm_buf[slot, m, j] = jnp.max(s, axis=0, keepdims=True)

    def stage_b(slot, kb0, biases):
        far = all(b is None for b in biases)
        for m in range(2):
            m_old = m_sc[m]
            m_new = m_old
            ss = []
            for j, bias in enumerate(biases):
                if far:
                    m_new = jnp.maximum(m_new, pm_buf[slot, m, j])
                else:
                    s = s_buf[slot, m, j]
                    s = s if bias is None else s + bias[0]
                    ss.append(s)
                    m_new = jnp.maximum(m_new, jnp.max(s, axis=0, keepdims=True))
            alpha = jnp.exp2(m_old - m_new)
            l_new = alpha * l_sc[m]
            acc = alpha * acc_sc[m]
            for j in range(len(biases)):
                s = s_buf[slot, m, j] if far else ss[j]
                p = jnp.exp2(s - m_new)
                l_new = l_new + jnp.sum(p, axis=0, keepdims=True)
                acc = acc + jnp.dot(vt_ref[0, kb0 + j], p.astype(BF16), preferred_element_type=F32)
            l_sc[m] = l_new
            acc_sc[m] = acc
            m_sc[m] = m_new

    far2 = [None, None]
    nfu = jnp.maximum(qi - 1, 0) // 2
    peel = nfu % 2

    @pl.when(peel == 1)
    def _():
        stage_a(0, 0)
        stage_b(0, 0, far2)

    stage_a(0, 2 * peel)

    def far_body(v, c):
        u = peel + 2 * v
        stage_a(1, 2 * u + 2)
        stage_b(0, 2 * u, far2)
        stage_a(0, 2 * u + 4)
        stage_b(1, 2 * u + 2, far2)
        return c

    lax.fori_loop(0, (nfu - peel) // 2, far_body, 0)
    kb0 = 2 * nfu

    @pl.when(qi % 2 == 1)
    def _():
        stage_b(0, kb0, [bp_ref, bd_ref])

    @pl.when((qi % 2 == 0) & (qi >= 2))
    def _():
        stage_a(1, kb0 + 2)
        stage_b(0, kb0, [None, bp_ref])
        stage_b(1, kb0 + 2, [bd_ref])

    @pl.when(qi == 0)
    def _():
        stage_b(0, kb0, [bd_ref])

    o = acc_sc[0] / l_sc[0] - lam_ref[0] * (acc_sc[1] / l_sc[1])
    ot = o.T
    o_ref[...] = (_rms(ot, gs_ref[...]) * (1.0 - LAMBDA_INIT)).astype(BF16)


def _attention(lam, q, k, vt, bias_d, bias_p, g_subln):
    T = q.shape[1]
    nq = T // TQ
    nkb = T // TK
    return pl.pallas_call(
        _attn_kernel,
        grid=(A_HEADS, nq),
        in_specs=[
            pl.BlockSpec(memory_space=pltpu.SMEM),
            pl.BlockSpec((1, TQ, 128), lambda h, i: (h, i, 0)),
            pl.BlockSpec((1, nkb, TK, 128), lambda h, i: (h, 0, 0, 0)),
            pl.BlockSpec((1, nkb, 128, TK), lambda h, i: (h, 0, 0, 0)),
            pl.BlockSpec((1, TK, TQ), lambda h, i: (h, 0, 0)),
            pl.BlockSpec((1, TK, TQ), lambda h, i: (h, 0, 0)),
            pl.BlockSpec((1, 128), lambda h, i: (0, 0)),
        ],
        out_specs=pl.BlockSpec((TQ, 128), lambda h, i: (i, h)),
        out_shape=jax.ShapeDtypeStruct((T, A_HEADS * A_V_DIM), BF16),
        scratch_shapes=[
            pltpu.VMEM((2, 1, TQ), F32),
            pltpu.VMEM((2, 1, TQ), F32),
            pltpu.VMEM((2, 128, TQ), F32),
            pltpu.VMEM((2, 2, 2, TK, TQ), F32),
            pltpu.VMEM((2, 2, 2, 1, TQ), F32),
        ],
        compiler_params=pltpu.CompilerParams(
            dimension_semantics=("arbitrary", "arbitrary"), vmem_limit_bytes=_vmem_limit(48 << 20)),
        name="attn",
    )(lam, q, k, vt, bias_d, bias_p, g_subln)


def _t5_bucket(rel):
    half = REL_BUCKETS // 2
    max_exact = half // 2
    ret = jnp.where(rel > 0, half, 0)
    n = jnp.abs(rel)
    large = max_exact + (jnp.log(jnp.maximum(n, 1).astype(F32) / max_exact)
                         / math.log(REL_MAX_DIST / max_exact) * (half - max_exact)).astype(I32)
    large = jnp.minimum(large, half - 1)
    return ret + jnp.where(n < max_exact, n, large)


def _attn_bias_tables(rel_bias):
    assert TK >= REL_MAX_DIST and TK == TQ and TK % CHUNK == 0
    kk = jnp.arange(TK, dtype=I32)[:, None]
    qq = jnp.arange(TQ, dtype=I32)[None, :]
    rb = rel_bias.astype(F32)
    far = rb[_t5_bucket(jnp.full((1,), -REL_MAX_DIST, I32))[0]]
    table = ((rb - far[None, :]) * LOG2E).T

    def lookup(rel):
        onehot = (_t5_bucket(rel).reshape(1, -1) == jnp.arange(REL_BUCKETS, dtype=I32)[:, None]).astype(F32)
        return jnp.dot(table, onehot, precision=lax.Precision.HIGHEST).reshape(A_HEADS, TK, TQ)

    bd = jnp.where((kk // CHUNK <= qq // CHUNK)[None], lookup(kk - qq), NEG_BIG)
    bp = lookup(kk - TK - qq)
    return bd, bp


def _gdn_kernel(x_ref, zb_ref, ab_ref, abt_ref, cw_ref, alc_ref, dtc_ref, alr_ref, dtr_ref, gob_ref,
                o_ref, xbuf, s_sc):
    i = pl.program_id(0)
    TC = TC_GDN
    NCH = TC // CHUNK
    HB = B_HEADS * CHUNK

    @pl.when(i == 0)
    def _():
        xbuf[0:8, :] = jnp.zeros((8, 1536), F32)
        s_sc[...] = jnp.zeros(s_sc.shape, F32)

    xbuf[8:8 + TC, :] = x_ref[...]
    y = cw_ref[3:4, :] * xbuf[8:8 + TC, :]
    for d in range(1, CONV_WIDTH):
        y = y + cw_ref[3 - d:4 - d, :] * xbuf[8 - d:8 - d + TC, :]
    xbuf[0:8, :] = x_ref[TC - 8:TC, :]
    a = y * _sigmoid(y)

    def l2n(t):
        return t * lax.rsqrt(jnp.sum(t * t, axis=-1, keepdims=True) + EPS)

    qh = [l2n(a[:, h * 128:(h + 1) * 128]) * (B_DIM ** -0.5) for h in range(B_HEADS)]
    kh = [l2n(a[:, 512 + h * 128:512 + (h + 1) * 128]) for h in range(B_HEADS)]
    vh = [a[:, 1024 + h * 128:1024 + (h + 1) * 128] for h in range(B_HEADS)]

    def softplus(t):
        return jnp.maximum(t, 0.0) + jnp.log(1.0 + jnp.exp(-jnp.abs(t)))

    ab = ab_ref[...]
    g_col = -jnp.exp(alc_ref[...]) * softplus(ab + dtc_ref[...])
    beta_col = _sigmoid(ab)
    g_row = -jnp.exp(alr_ref[...]) * softplus(abt_ref[...] + dtr_ref[...])

    rt = lax.broadcasted_iota(I32, (TC, TC), 0)
    ct = lax.broadcasted_iota(I32, (TC, TC), 1)
    same_chunk = (rt // CHUNK) == (ct // CHUNK)
    tril = jnp.where(same_chunk & (rt >= ct), 1.0, 0.0).astype(F32)
    triu = jnp.where(same_chunk & (rt <= ct), 1.0, 0.0).astype(F32)
    gc_col = jnp.dot(tril, g_col, precision=lax.Precision.HIGHEST, preferred_element_type=F32)
    gc_row = jnp.dot(g_row, triu, precision=lax.Precision.HIGHEST, preferred_element_type=F32)

    ri = lax.broadcasted_iota(I32, (HB, HB), 0)
    ci = lax.broadcasted_iota(I32, (HB, HB), 1)
    same_head = (ri // CHUNK) == (ci // CHUNK)
    incl = same_head & (ri >= ci)
    strict = same_head & (ri > ci)
    eye = jnp.where(ri == ci, 1.0, 0.0).astype(F32)

    def level_mask(s):
        return ((ri // (2 * s)) == (ci // (2 * s))) & (((ri // s) % 2) == 1) & (((ci // s) % 2) == 0)

    for c in range(NCH):
        rs = slice(c * CHUNK, (c + 1) * CHUNK)
        last = slice(c * CHUNK + CHUNK - 1, (c + 1) * CHUNK)
        cat0 = lambda parts: jnp.concatenate(parts, axis=0)
        K = cat0([kh[h][rs] for h in range(B_HEADS)])
        Q = cat0([qh[h][rs] for h in range(B_HEADS)])
        V = cat0([vh[h][rs] for h in range(B_HEADS)])
        beta = cat0([beta_col[rs, 4 + h:5 + h] for h in range(B_HEADS)])
        gcc = cat0([gc_col[rs, h:h + 1] for h in range(B_HEADS)])
        gl = cat0([jnp.broadcast_to(gc_col[last, h:h + 1], (CHUNK, 1)) for h in range(B_HEADS)])
        gcr = jnp.concatenate([gc_row[h:h + 1, rs] for h in range(B_HEADS)], axis=1)

        Kb = K * beta
        Vb = V * beta
        dec = jnp.where(incl, jnp.exp(jnp.where(incl, gcc - gcr, 0.0)), 0.0)
        L = jnp.where(strict, lax.dot_general(Kb, K, _NT, preferred_element_type=F32) * dec, 0.0)
        QK = lax.dot_general(Q, K, _NT, preferred_element_type=F32) * dec

        inv = eye - jnp.where(level_mask(1), L, 0.0)
        for s in (2, 4, 8, 16, 32):
            cs_ = jnp.where(level_mask(s), L, 0.0)
            t_ = jnp.dot(inv, cs_, preferred_element_type=F32)
            inv = inv - jnp.dot(t_, inv, preferred_element_type=F32)

        egc = jnp.exp(gcc)
        rhs = jnp.concatenate([Vb, Kb * egc], axis=1)
        sol = jnp.dot(inv, rhs, preferred_element_type=F32)
        U = sol[:, :B_DIM]
        W = sol[:, B_DIM:]
        Qd = Q * egc
        Kd = K * jnp.exp(gl - gcc)

        vnew = []
        ost = []
        for h in range(B_HEADS):
            hs = slice(h * CHUNK, (h + 1) * CHUNK)
            S = s_sc[h]
            vn = U[hs] - jnp.dot(W[hs], S, preferred_element_type=F32)
            ost.append(jnp.dot(Qd[hs], S, preferred_element_type=F32))
            s_sc[h] = S * jnp.exp(gl[h * CHUNK:h * CHUNK + 1, :]) + lax.dot_general(
                Kd[hs], vn, _TN, preferred_element_type=F32)
            vnew.append(vn)
        O = cat0(ost) + jnp.dot(QK, cat0(vnew), preferred_element_type=F32)
        for h in range(B_HEADS):
            oh = _rms(O[h * CHUNK:(h + 1) * CHUNK], gob_ref[...])
            z = zb_ref[rs, h * 128:(h + 1) * 128]
            o_ref[rs, h * 128:(h + 1) * 128] = (oh * (z * _sigmoid(z))).astype(BF16)


def _gdn(qkvb, zb, ab, abt, conv_w8, alc, dtc, alr, dtr, gob):
    T = qkvb.shape[0]
    TC = TC_GDN
    full = lambda shape: pl.BlockSpec(shape, lambda i: (0,) * len(shape))
    return pl.pallas_call(
        _gdn_kernel,
        grid=(T // TC,),
        in_specs=[
            pl.BlockSpec((TC, 1536), lambda i: (i, 0)),
            pl.BlockSpec((TC, 512), lambda i: (i, 0)),
            pl.BlockSpec((TC, 128), lambda i: (i, 0)),
            pl.BlockSpec((16, TC), lambda i: (0, i)),
            full((8, 1536)), full((1, 128)), full((1, 128)), full((16, TC)), full((16, TC)), full((1, 128)),
        ],
        out_specs=pl.BlockSpec((TC, 512), lambda i: (i, 0)),
        out_shape=jax.ShapeDtypeStruct((T, 512), BF16),
        scratch_shapes=[pltpu.VMEM((TC + 8, 1536), F32), pltpu.VMEM((B_HEADS, B_DIM, B_DIM), F32)],
        compiler_params=pltpu.CompilerParams(
            dimension_semantics=("arbitrary",), vmem_limit_bytes=_vmem_limit(48 << 20)),
        name="gdn",
    )(qkvb, zb, ab, abt, conv_w8, alc, dtc, alr, dtr, gob)


def _memkv_kernel(m_ref, g_ref, w_ref, k_ref, v_ref):
    hm = _rms(m_ref[...], g_ref[...]).astype(BF16)
    kv = jnp.dot(hm, w_ref[...], preferred_element_type=F32)
    k_ref[...] = kv[:, :D_MODEL].astype(BF16)
    v_ref[...] = kv[:, D_MODEL:].astype(BF16)


def _memkv(mem2d, g_mem, w_kv):
    n = mem2d.shape[0]
    return pl.pallas_call(
        _memkv_kernel,
        out_shape=(jax.ShapeDtypeStruct((n, D_MODEL), BF16), jax.ShapeDtypeStruct((n, D_MODEL), BF16)),
        compiler_params=pltpu.CompilerParams(vmem_limit_bytes=_vmem_limit(32 << 20)),
        name="memkv",
    )(mem2d, g_mem, w_kv)


def _mix_kernel(x_ref, oa_ref, ob_ref, gates_ref, wua_ref, wub_ref, wout_ref, gx_ref, wq_ref, kx_ref, vx_ref,
                wo_ref, gm_ref, wr_ref, br_ref, x2t_ref, hmt_ref, idx_ref, wts_ref):
    TM = TM_MIX
    ma = jnp.dot(oa_ref[...], wua_ref[...], preferred_element_type=F32)
    mb = jnp.dot(ob_ref[...], wub_ref[...], preferred_element_type=F32)
    merged = _sigmoid(gates_ref[:, :D_MODEL]) * ma + _sigmoid(gates_ref[:, D_MODEL:]) * mb
    x1 = x_ref[...] + jnp.dot(merged.astype(BF16), wout_ref[...], preferred_element_type=F32)

    hx = _rms(x1, gx_ref[...]).astype(BF16)
    heads = []
    for h in range(X_HEADS):
        cs = slice(h * X_HEAD_DIM, (h + 1) * X_HEAD_DIM)
        qh = jnp.dot(hx, wq_ref[:, cs], preferred_element_type=F32).astype(BF16)
        s = lax.dot_general(qh, kx_ref[:, cs], _NT, preferred_element_type=F32) * (X_HEAD_DIM ** -0.5)
        s = s - jnp.max(s, axis=-1, keepdims=True)
        p = jnp.exp(s)
        p = p / jnp.sum(p, axis=-1, keepdims=True)
        heads.append(jnp.dot(p.astype(BF16), vx_ref[:, cs], preferred_element_type=F32).astype(BF16))
    o = jnp.concatenate(heads, axis=1)
    x2 = x1 + jnp.dot(o, wo_ref[...], preferred_element_type=F32)
    hm = _rms(x2, gm_ref[...])

    for g in range(D_MODEL // 128):
        x2t_ref[pl.ds(g, TM, stride=8), :] = x2[:, g * 128:(g + 1) * 128]
        hmt_ref[pl.ds(g, TM, stride=8), :] = hm[:, g * 128:(g + 1) * 128]

    logits = lax.dot_general(wr_ref[...], hm, _NT, precision=lax.Precision.HIGHEST,
                             preferred_element_type=F32) + br_ref[:, 0:1]
    eidx = lax.broadcasted_iota(I32, logits.shape, 0)
    vals, idxs = [], []
    cur = logits
    for _ in range(TOP_K):
        mx = jnp.max(cur, axis=0, keepdims=True)
        ix = jnp.min(jnp.where(cur == mx, eidx, N_EXPERTS), axis=0, keepdims=True)
        vals.append(mx)
        idxs.append(ix)
        cur = jnp.where(eidx == ix, -jnp.inf, cur)
    ex = [jnp.exp(v - vals[0]) for v in vals]
    den = ex[0] + ex[1] + ex[2] + ex[3]
    idx_ref[...] = jnp.concatenate(idxs, axis=0)
    wts_ref[...] = jnp.concatenate([e / den for e in ex], axis=0)


def _mix(x2d, oa, ob, gates, wua, wub, wout, gx, wq, kx, vx, wo, gm, wr_t, br):
    T = x2d.shape[0]
    TM = TM_MIX
    full = lambda a: pl.BlockSpec(a.shape, lambda i: (0,) * a.ndim)
    out_shape = (
        jax.ShapeDtypeStruct((T * 8, 128), F32),
        jax.ShapeDtypeStruct((T * 8, 128), F32),
        jax.ShapeDtypeStruct((TOP_K, T), I32),
        jax.ShapeDtypeStruct((TOP_K, T), F32),
    )
    return pl.pallas_call(
        _mix_kernel,
        grid=(T // TM,),
        in_specs=[
            pl.BlockSpec((TM, D_MODEL), lambda i: (i, 0)),
            pl.BlockSpec((TM, 512), lambda i: (i, 0)),
            pl.BlockSpec((TM, 512), lambda i: (i, 0)),
            pl.BlockSpec((TM, 2048), lambda i: (i, 0)),
            full(wua), full(wub), full(wout), full(gx), full(wq), full(kx), full(vx), full(wo), full(gm),
            full(wr_t), full(br),
        ],
        out_specs=(
            pl.BlockSpec((TM * 8, 128), lambda i: (i, 0)),
            pl.BlockSpec((TM * 8, 128), lambda i: (i, 0)),
            pl.BlockSpec((TOP_K, TM), lambda i: (0, i)),
            pl.BlockSpec((TOP_K, TM), lambda i: (0, i)),
        ),
        out_shape=out_shape,
        compiler_params=pltpu.CompilerParams(
            dimension_semantics=("arbitrary",), vmem_limit_bytes=_vmem_limit(56 << 20)),
        name="mix",
    )(x2d, oa, ob, gates, wua, wub, wout, gx, wq, kx, vx, wo, gm, wr_t, br)


def _moe_kernel(offs_ref, tok_ref, wt_ref, x2t_ref, hmt_ref, w1_ref, b1_ref, w2_ref, b2_ref, gf_ref,
                out_ref, yacc, xs, yst):
    j = pl.program_id(0)
    e = pl.program_id(1)
    TT = TT_MOE
    NG = D_MODEL // 128

    @pl.when(e == 0)
    def _():
        yacc[0:TT * 8, :] = x2t_ref[...]
        yacc[TT * 8:TT * 8 + 8, :] = jnp.zeros((8, 128), F32)

    start = offs_ref[j, e]
    n = offs_ref[j, e + 1] - start

    def row_block(r0, valid, RB):
        def gather8(gi, c):
            for u in range(8):
                r = gi * 8 + u
                t = tok_ref[0, 0, r0 + r]
                xs[pl.ds(pl.multiple_of(r * 8, 8), 8), :] = hmt_ref[pl.ds(pl.multiple_of(t * 8, 8), 8), :]
            return c

        lax.fori_loop(0, RB // 8, gather8, 0)
        xb = jnp.concatenate([xs[pl.ds(g, RB, stride=8), :] for g in range(NG)], axis=1).astype(BF16)
        hid = jnp.dot(xb, w1_ref[0], preferred_element_type=F32) + b1_ref[0]
        glu = jnp.minimum(hid[:, :D_FF], SWIGLU_LIMIT)
        lin = jnp.clip(hid[:, D_FF:], -SWIGLU_LIMIT, SWIGLU_LIMIT)
        act = glu * _sigmoid(SWIGLU_ALPHA * glu) * (lin + 1.0)
        ys = jnp.dot(act.astype(BF16), w2_ref[0], preferred_element_type=F32) + b2_ref[0]
        for g in range(NG):
            yst[pl.ds(g, RB, stride=8), :] = ys[:, g * 128:(g + 1) * 128]

        def scatter8(gi, c):
            dsts, vals = [], []
            for u in range(8):
                r = gi * 8 + u
                t = jnp.where(r < valid, tok_ref[0, 0, r0 + r], TT)
                dst = pl.ds(pl.multiple_of(t * 8, 8), 8)
                dsts.append(dst)
                vals.append(yacc[dst, :] + wt_ref[0, 0, r0 + r] * yst[pl.ds(pl.multiple_of(r * 8, 8), 8), :])
            for dst, val in zip(dsts, vals):
                yacc[dst, :] = val
            return c

        lax.fori_loop(0, RB // 8, scatter8, 0)

    RBL, RBS = RB_MOE, RB_MOE // 2
    rem = n % RBL
    n_large = n // RBL + jnp.where(rem > RBS, 1, 0)

    def large_body(blk, c):
        row_block(start + blk * RBL, n - blk * RBL, RBL)
        return c

    lax.fori_loop(0, n_large, large_body, 0)

    @pl.when((rem > 0) & (rem <= RBS))
    def _():
        row_block(start + (n // RBL) * RBL, rem, RBS)

    @pl.when(e == N_EXPERTS - 1)
    def _():
        RC = 256

        def fin(ci, c):
            base = pl.multiple_of(ci * RC * 8, 8)
            yv = jnp.concatenate([yacc[pl.ds(base + g, RC, stride=8), :] for g in range(NG)], axis=1)
            out_ref[pl.ds(pl.multiple_of(ci * RC, 8), RC), :] = _rms(yv, gf_ref[...])
            return c

        lax.fori_loop(0, TT // RC, fin, 0)


def _moe(offs, tok, wt, x2t, hmt, w1, b1, w2, b2, g_final):
    T = x2t.shape[0] // 8
    TT = TT_MOE
    nt = T // TT
    LP = tok.shape[-1]
    one_buf = dict(pipeline_mode=pl.Buffered(1))
    grid_spec = pltpu.PrefetchScalarGridSpec(
        num_scalar_prefetch=1,
        grid=(nt, N_EXPERTS),
        in_specs=[
            pl.BlockSpec((1, 1, LP), lambda j, e, o: (j, 0, 0), memory_space=pltpu.SMEM),
            pl.BlockSpec((1, 1, LP), lambda j, e, o: (j, 0, 0), memory_space=pltpu.SMEM),
            pl.BlockSpec((TT * 8, 128), lambda j, e, o: (j, 0), **one_buf),
            pl.BlockSpec((TT * 8, 128), lambda j, e, o: (j, 0), **one_buf),
            pl.BlockSpec((1, D_MODEL, 2 * D_FF), lambda j, e, o: (e, 0, 0)),
            pl.BlockSpec((1, 1, 2 * D_FF), lambda j, e, o: (e, 0, 0)),
            pl.BlockSpec((1, D_FF, D_MODEL), lambda j, e, o: (e, 0, 0)),
            pl.BlockSpec((1, 1, D_MODEL), lambda j, e, o: (e, 0, 0)),
            pl.BlockSpec((1, D_MODEL), lambda j, e, o: (0, 0)),
        ],
        out_specs=pl.BlockSpec((TT, D_MODEL), lambda j, e, o: (j, 0), **one_buf),
        scratch_shapes=[
            pltpu.VMEM((TT * 8 + 8, 128), F32),
            pltpu.VMEM((RB_MOE * 8, 128), F32),
            pltpu.VMEM((RB_MOE * 8, 128), F32),
        ],
    )
    return pl.pallas_call(
        _moe_kernel,
        grid_spec=grid_spec,
        out_shape=jax.ShapeDtypeStruct((T, D_MODEL), F32),
        compiler_params=pltpu.CompilerParams(
            dimension_semantics=("arbitrary", "arbitrary"), vmem_limit_bytes=_vmem_limit(60 << 20)),
        name="moe",
    )(offs, tok, wt, x2t, hmt, w1, b1, w2, b2, g_final)


def _moe_lists(idx, wts):
    T = idx.shape[1]
    TT = TT_MOE
    nt = T // TT
    e_tile = idx.reshape(TOP_K, nt, TT).transpose(1, 0, 2).reshape(nt, TOP_K * TT)
    w_tile = wts.reshape(TOP_K, nt, TT).transpose(1, 0, 2).reshape(nt, TOP_K * TT)
    tok_local = jnp.tile(jnp.arange(TT, dtype=I32), TOP_K)
    order = jnp.argsort(e_tile, axis=1, stable=True)
    tok = jnp.take_along_axis(jnp.broadcast_to(tok_local, e_tile.shape), order, axis=1)
    wt = jnp.take_along_axis(w_tile, order, axis=1)
    counts = jnp.sum((e_tile[:, :, None] == jnp.arange(N_EXPERTS, dtype=I32)[None, None, :]).astype(I32), axis=1)
    offs = jnp.concatenate([jnp.zeros((nt, 1), I32), jnp.cumsum(counts, axis=1)], axis=1)
    offs = jnp.pad(offs, ((0, 0), (0, 40 - offs.shape[1])))
    pad = ((0, 0), (0, RB_MOE))
    return offs, jnp.pad(tok, pad)[:, None, :], jnp.pad(wt, pad)[:, None, :]


def kernel(x, mem, g_mix, w_in, rel_bias, lambda_q1, lambda_k1, lambda_q2, lambda_k2, g_subln, conv_w, a_log,
           dt_bias, g_out_b, w_up_a, w_up_b, w_out, g_xattn, g_mem, w_q_x, w_kv_x, w_o_x, g_moe, w_router,
           b_router, w_mlp1, b_mlp1, w_mlp2, b_mlp2, g_final):
    B_, S_, _ = x.shape
    assert B_ == 1 and S_ % TT_MOE == 0 and x.dtype == F32
    l = 0
    x2d = x.reshape(S_, D_MODEL)
    row = lambda v: v.reshape(1, -1).astype(F32)

    wi = w_in[l]
    ab_cols = jnp.pad(wi[:, 3584:3592], ((0, 0), (0, 120)))
    w_main = jnp.concatenate(
        [wi[:, 0:1024], wi[:, 1536:3072], wi[:, 3072:3584], wi[:, 3592:5640], ab_cols], axis=1).astype(BF16)
    w_vt = wi[:, 1024:1536].T.astype(BF16)
    w_abt = jnp.pad(wi[:, 3584:3592].T, ((0, 8), (0, 0))).astype(BF16)

    q, k, vt, qkvb, zb, gates, ab, abt = _inproj(x2d, row(g_mix[l]), w_main, w_vt, w_abt)

    lam = (jnp.exp(jnp.sum(lambda_q1[l].astype(F32) * lambda_k1[l].astype(F32)))
           - jnp.exp(jnp.sum(lambda_q2[l].astype(F32) * lambda_k2[l].astype(F32))) + LAMBDA_INIT).reshape(1)
    bias_d, bias_p = _attn_bias_tables(rel_bias)
    oa = _attention(lam, q, k, vt, bias_d, bias_p, row(g_subln[l]))

    lane_pad = lambda v: jnp.pad(v.astype(F32), (0, 128 - v.shape[0])).reshape(1, 128)
    row_bcast = lambda v: jnp.broadcast_to(jnp.pad(v.astype(F32), (0, 16 - v.shape[0]))[:, None], (16, TC_GDN))
    ob = _gdn(qkvb, zb, ab, abt, jnp.pad(conv_w[l].astype(F32), ((0, 4), (0, 0))),
              lane_pad(a_log[l]), lane_pad(dt_bias[l]), row_bcast(a_log[l]), row_bcast(dt_bias[l]),
              row(g_out_b[l]))

    kx, vx = _memkv(mem.reshape(-1, D_MODEL), row(g_mem[l]), w_kv_x[l].astype(BF16))
    br = jnp.broadcast_to(b_router[l].astype(F32)[:, None], (N_EXPERTS, 128))
    x2t, hmt, idx, wts = _mix(
        x2d, oa, ob, gates, w_up_a[l].astype(BF16), w_up_b[l].astype(BF16), w_out[l].astype(BF16),
        row(g_xattn[l]), w_q_x[l].astype(BF16), kx, vx, w_o_x[l].astype(BF16), row(g_moe[l]),
        w_router[l].T.astype(F32), br)

    offs, tok, wt = _moe_lists(idx, wts)
    out = _moe(offs, tok, wt, x2t, hmt, w_mlp1[l].astype(BF16), b_mlp1[l].astype(F32)[:, None, :],
               w_mlp2[l].astype(BF16), b_mlp2[l].astype(F32)[:, None, :], row(g_final))
    return out.reshape(B_, S_, D_MODEL)
```

```python
import functools
import math

import jax
import jax.numpy as jnp
from jax import lax
from jax.experimental import pallas as pl
from jax.experimental.pallas import tpu as pltpu

F32 = jnp.float32
BF16 = jnp.bfloat16
I32 = jnp.int32

D_MODEL = 1024
CHUNK = 64
EPS = 1e-6
A_HEADS = 4
A_QK_DIM = 64
A_V_DIM = 128
REL_BUCKETS = 32
REL_MAX_DIST = 128
B_HEADS = 4
B_DIM = 128
CONV_WIDTH = 4
X_HEADS = 4
X_HEAD_DIM = 256
N_EXPERTS = 32
TOP_K = 4
D_FF = 1024
SWIGLU_LIMIT = 7.0
SWIGLU_ALPHA = 1.702
LAMBDA_INIT = 0.8 - 0.6 * math.exp(-0.3 * 0)

LOG2E = 1.4426950408889634
NEG_BIG = -1e30

V7X_LANES = 128
V7X_SUBLANES = 8
V7X_VMEM_BYTES = 64 * 1024 * 1024

TM_PROJ = 512
TQ = 512
TK = 512
TC_GDN = 256
TM_MIX = 512
TT_MOE = 2048
RB_MOE = 128

C_QA, C_KA, C_QKVB, C_ZB, C_GATE, C_AB, C_END = 0, 512, 1024, 2560, 3072, 5120, 5248

_NT = (((1,), (1,)), ((), ()))
_TN = (((0,), (0,)), ((), ()))


def _rms(x, g):
    return x * lax.rsqrt(jnp.mean(x * x, axis=-1, keepdims=True) + EPS) * g


def _sigmoid(x):
    return 1.0 / (1.0 + jnp.exp(-x))


def _vmem_limit(nbytes):
    return int(min(nbytes, V7X_VMEM_BYTES - 4 * 1024 * 1024))


def _inproj_kernel(x_ref, g_ref, w_ref, wvt_ref, wabt_ref,
                   q_ref, k_ref, vt_ref, qkvb_ref, zb_ref, gates_ref, ab_ref, abt_ref):
    h = _rms(x_ref[...], g_ref[...]).astype(BF16)

    def mm(c0, c1):
        return jnp.dot(h, w_ref[:, c0:c1], preferred_element_type=F32)

    nb = TM_PROJ // TK
    qa = mm(C_QA, C_KA) * (A_QK_DIM ** -0.5 * LOG2E)
    ka = mm(C_KA, C_QKVB)
    vt = lax.dot_general(wvt_ref[...], h, _NT, preferred_element_type=F32)
    for hh in range(A_HEADS):
        cs = slice(hh * 128, (hh + 1) * 128)
        q_ref[hh] = qa[:, cs].astype(BF16)
        for b in range(nb):
            rs = slice(b * TK, (b + 1) * TK)
            k_ref[hh, b] = ka[rs, cs].astype(BF16)
            vt_ref[hh, b] = vt[cs, rs].astype(BF16)
    for j in range(3):
        qkvb_ref[:, j * 512:(j + 1) * 512] = mm(C_QKVB + j * 512, C_QKVB + (j + 1) * 512)
    zb_ref[...] = mm(C_ZB, C_GATE)
    for j in range(4):
        gates_ref[:, j * 512:(j + 1) * 512] = mm(C_GATE + j * 512, C_GATE + (j + 1) * 512)
    ab_ref[...] = mm(C_AB, C_END)
    abt_ref[...] = lax.dot_general(wabt_ref[...], h, _NT, preferred_element_type=F32)


def _inproj(x2d, g_mix, w_main, w_vt, w_abt):
    T = x2d.shape[0]
    n = T // TM_PROJ
    nkb = T // TK
    nb = TM_PROJ // TK
    full = lambda shape: pl.BlockSpec(shape, lambda i: (0,) * len(shape))
    out_shape = (
        jax.ShapeDtypeStruct((A_HEADS, T, 128), BF16),
        jax.ShapeDtypeStruct((A_HEADS, nkb, TK, 128), BF16),
        jax.ShapeDtypeStruct((A_HEADS, nkb, 128, TK), BF16),
        jax.ShapeDtypeStruct((T, 1536), F32),
        jax.ShapeDtypeStruct((T, 512), F32),
        jax.ShapeDtypeStruct((T, 2048), F32),
        jax.ShapeDtypeStruct((T, 128), F32),
        jax.ShapeDtypeStruct((16, T), F32),
    )
    out_specs = (
        pl.BlockSpec((A_HEADS, TM_PROJ, 128), lambda i: (0, i, 0)),
        pl.BlockSpec((A_HEADS, nb, TK, 128), lambda i: (0, i, 0, 0)),
        pl.BlockSpec((A_HEADS, nb, 128, TK), lambda i: (0, i, 0, 0)),
        pl.BlockSpec((TM_PROJ, 1536), lambda i: (i, 0)),
        pl.BlockSpec((TM_PROJ, 512), lambda i: (i, 0)),
        pl.BlockSpec((TM_PROJ, 2048), lambda i: (i, 0)),
        pl.BlockSpec((TM_PROJ, 128), lambda i: (i, 0)),
        pl.BlockSpec((16, TM_PROJ), lambda i: (0, i)),
    )
    return pl.pallas_call(
        _inproj_kernel,
        grid=(n,),
        in_specs=[
            pl.BlockSpec((TM_PROJ, D_MODEL), lambda i: (i, 0)),
            full((1, D_MODEL)),
            full(w_main.shape),
            full(w_vt.shape),
            full(w_abt.shape),
        ],
        out_specs=out_specs,
        out_shape=out_shape,
        compiler_params=pltpu.CompilerParams(
            dimension_semantics=("arbitrary",), vmem_limit_bytes=_vmem_limit(56 << 20)),
        name="inproj",
    )(x2d, g_mix, w_main, w_vt, w_abt)


def _attn_kernel(lam_ref, q_ref, k_ref, vt_ref, bd_ref, bp_ref, gs_ref, o_ref, m_sc, l_sc, acc_sc, s_buf, pm_buf):
    qi = pl.program_id(1)
    q = q_ref[0]
    lane = lax.broadcasted_iota(I32, q.shape, 1)
    zero = jnp.zeros_like(q)
    qm = (jnp.where(lane < A_QK_DIM, q, zero), jnp.where(lane >= A_QK_DIM, q, zero))
    m_sc[...] = jnp.full(m_sc.shape, NEG_BIG, F32)
    l_sc[...] = jnp.zeros(l_sc.shape, F32)
    acc_sc[...] = jnp.zeros(acc_sc.shape, F32)

    nkb = k_ref.shape[1]

    def stage_a(slot, kb0):
        for j in range(2):
            kblk = k_ref[0, jnp.minimum(kb0 + j, nkb - 1)]
            for m in range(2):
                s = lax.dot_general(kblk, qm[m], _NT, preferred_element_type=F32)
                s_buf[slot, m, j] = s
                pm_buf[slot, m, j] = jnp.max(s, axis=0, keepdims=True)

    def stage_b(slot, kb0, biases):
        far = all(b is None for b in biases)
        for m in range(2):
            m_old = m_sc[m]
            m_new = m_old
            ss = []
            for j, bias in enumerate(biases):
                if far:
                    m_new = jnp.maximum(m_new, pm_buf[slot, m, j])
                else:
                    s = s_buf[slot, m, j]
                    s = s if bias is None else s + bias[0]
                    ss.append(s)
                    m_new = jnp.maximum(m_new, jnp.max(s, axis=0, keepdims=True))
            alpha = jnp.exp2(m_old - m_new)
            l_new = alpha * l_sc[m]
            acc = alpha * acc_sc[m]
            for j in range(len(biases)):
                s = s_buf[slot, m, j] if far else ss[j]
                p = jnp.exp2(s - m_new)
                l_new = l_new + jnp.sum(p, axis=0, keepdims=True)
                acc = acc + jnp.dot(vt_ref[0, kb0 + j], p.astype(BF16), preferred_element_type=F32)
            l_sc[m] = l_new
            acc_sc[m] = acc
            m_sc[m] = m_new

    far2 = [None, None]
    nfu = jnp.maximum(qi - 1, 0) // 2
    peel = nfu % 2

    @pl.when(peel == 1)
    def _():
        stage_a(0, 0)
        stage_b(0, 0, far2)

    stage_a(0, 2 * peel)

    def far_body(v, c):
        u = peel + 2 * v
        stage_a(1, 2 * u + 2)
        stage_b(0, 2 * u, far2)
        stage_a(0, 2 * u + 4)
        stage_b(1, 2 * u + 2, far2)
        return c

    lax.fori_loop(0, (nfu - peel) // 2, far_body, 0)
    kb0 = 2 * nfu

    @pl.when(qi % 2 == 1)
    def _():
        stage_b(0, kb0, [bp_ref, bd_ref])

    @pl.when((qi % 2 == 0) & (qi >= 2))
    def _():
        stage_a(1, kb0 + 2)
        stage_b(0, kb0, [None, bp_ref])
        stage_b(1, kb0 + 2, [bd_ref])

    @pl.when(qi == 0)
    def _():
        stage_b(0, kb0, [bd_ref])

    o = acc_sc[0] / l_sc[0] - lam_ref[0] * (acc_sc[1] / l_sc[1])
    ot = o.T
    o_ref[...] = (_rms(ot, gs_ref[...]) * (1.0 - LAMBDA_INIT)).astype(BF16)


def _attention(lam, q, k, vt, bias_d, bias_p, g_subln):
    T = q.shape[1]
    nq = T // TQ
    nkb = T // TK
    return pl.pallas_call(
        _attn_kernel,
        grid=(A_HEADS, nq),
        in_specs=[
            pl.BlockSpec(memory_space=pltpu.SMEM),
            pl.BlockSpec((1, TQ, 128), lambda h, i: (h, i, 0)),
            pl.BlockSpec((1, nkb, TK, 128), lambda h, i: (h, 0, 0, 0)),
            pl.BlockSpec((1, nkb, 128, TK), lambda h, i: (h, 0, 0, 0)),
            pl.BlockSpec((1, TK, TQ), lambda h, i: (h, 0, 0)),
            pl.BlockSpec((1, TK, TQ), lambda h, i: (h, 0, 0)),
            pl.BlockSpec((1, 128), lambda h, i: (0, 0)),
        ],
        out_specs=pl.BlockSpec((TQ, 128), lambda h, i: (i, h)),
        out_shape=jax.ShapeDtypeStruct((T, A_HEADS * A_V_DIM), BF16),
        scratch_shapes=[
            pltpu.VMEM((2, 1, TQ), F32),
            pltpu.VMEM((2, 1, TQ), F32),
            pltpu.VMEM((2, 128, TQ), F32),
            pltpu.VMEM((2, 2, 2, TK, TQ), F32),
            pltpu.VMEM((2, 2, 2, 1, TQ), F32),
        ],
        compiler_params=pltpu.CompilerParams(
            dimension_semantics=("arbitrary", "arbitrary"), vmem_limit_bytes=_vmem_limit(48 << 20)),
        name="attn",
    )(lam, q, k, vt, bias_d, bias_p, g_subln)


def _t5_bucket(rel):
    half = REL_BUCKETS // 2
    max_exact = half // 2
    ret = jnp.where(rel > 0, half, 0)
    n = jnp.abs(rel)
    large = max_exact + (jnp.log(jnp.maximum(n, 1).astype(F32) / max_exact)
                         / math.log(REL_MAX_DIST / max_exact) * (half - max_exact)).astype(I32)
    large = jnp.minimum(large, half - 1)
    return ret + jnp.where(n < max_exact, n, large)


def _attn_bias_tables(rel_bias):
    assert TK >= REL_MAX_DIST and TK == TQ and TK % CHUNK == 0
    kk = jnp.arange(TK, dtype=I32)[:, None]
    qq = jnp.arange(TQ, dtype=I32)[None, :]
    rb = rel_bias.astype(F32)
    far = rb[_t5_bucket(jnp.full((1,), -REL_MAX_DIST, I32))[0]]
    table = ((rb - far[None, :]) * LOG2E).T

    def lookup(rel):
        onehot = (_t5_bucket(rel).reshape(1, -1) == jnp.arange(REL_BUCKETS, dtype=I32)[:, None]).astype(F32)
        return jnp.dot(table, onehot, precision=lax.Precision.HIGHEST).reshape(A_HEADS, TK, TQ)

    bd = jnp.where((kk // CHUNK <= qq // CHUNK)[None], lookup(kk - qq), NEG_BIG)
    bp = lookup(kk - TK - qq)
    return bd, bp


def _gdn_kernel(x_ref, zb_ref, ab_ref, abt_ref, cw_ref, alc_ref, dtc_ref, alr_ref, dtr_ref, gob_ref,
                o_ref, xbuf, s_sc):
    i = pl.program_id(0)
    TC = TC_GDN
    NCH = TC // CHUNK
    HB = B_HEADS * CHUNK

    @pl.when(i == 0)
    def _():
        xbuf[0:8, :] = jnp.zeros((8, 1536), F32)
        s_sc[...] = jnp.zeros(s_sc.shape, F32)

    xbuf[8:8 + TC, :] = x_ref[...]
    y = cw_ref[3:4, :] * xbuf[8:8 + TC, :]
    for d in range(1, CONV_WIDTH):
        y = y + cw_ref[3 - d:4 - d, :] * xbuf[8 - d:8 - d + TC, :]
    xbuf[0:8, :] = x_ref[TC - 8:TC, :]
    a = y * _sigmoid(y)

    def l2n(t):
        return t * lax.rsqrt(jnp.sum(t * t, axis=-1, keepdims=True) + EPS)

    qh = [l2n(a[:, h * 128:(h + 1) * 128]) * (B_DIM ** -0.5) for h in range(B_HEADS)]
    kh = [l2n(a[:, 512 + h * 128:512 + (h + 1) * 128]) for h in range(B_HEADS)]
    vh = [a[:, 1024 + h * 128:1024 + (h + 1) * 128] for h in range(B_HEADS)]

    def softplus(t):
        return jnp.maximum(t, 0.0) + jnp.log(1.0 + jnp.exp(-jnp.abs(t)))

    ab = ab_ref[...]
    g_col = -jnp.exp(alc_ref[...]) * softplus(ab + dtc_ref[...])
    beta_col = _sigmoid(ab)
    g_row = -jnp.exp(alr_ref[...]) * softplus(abt_ref[...] + dtr_ref[...])

    rt = lax.broadcasted_iota(I32, (TC, TC), 0)
    ct = lax.broadcasted_iota(I32, (TC, TC), 1)
    same_chunk = (rt // CHUNK) == (ct // CHUNK)
    tril = jnp.where(same_chunk & (rt >= ct), 1.0, 0.0).astype(F32)
    triu = jnp.where(same_chunk & (rt <= ct), 1.0, 0.0).astype(F32)
    gc_col = jnp.dot(tril, g_col, precision=lax.Precision.HIGHEST, preferred_element_type=F32)
    gc_row = jnp.dot(g_row, triu, precision=lax.Precision.HIGHEST, preferred_element_type=F32)

    ri = lax.broadcasted_iota(I32, (HB, HB), 0)
    ci = lax.broadcasted_iota(I32, (HB, HB), 1)
    same_head = (ri // CHUNK) == (ci // CHUNK)
    incl = same_head & (ri >= ci)
    strict = same_head & (ri > ci)
    eye = jnp.where(ri == ci, 1.0, 0.0).astype(F32)

    def level_mask(s):
        return ((ri // (2 * s)) == (ci // (2 * s))) & (((ri // s) % 2) == 1) & (((ci // s) % 2) == 0)

    cat0 = lambda parts: jnp.concatenate(parts, axis=0)
    heads = range(B_HEADS)
    chunks = range(NCH)

    Kc, Qc, glc, gccc, Lc, QKc, invc, rhsc = [], [], [], [], [], [], [], []
    for c in chunks:
        rs = slice(c * CHUNK, (c + 1) * CHUNK)
        last = slice(c * CHUNK + CHUNK - 1, (c + 1) * CHUNK)
        K = cat0([kh[h][rs] for h in heads])
        Q = cat0([qh[h][rs] for h in heads])
        V = cat0([vh[h][rs] for h in heads])
        beta = cat0([beta_col[rs, 4 + h:5 + h] for h in heads])
        gcc = cat0([gc_col[rs, h:h + 1] for h in heads])
        gl = cat0([jnp.broadcast_to(gc_col[last, h:h + 1], (CHUNK, 1)) for h in heads])
        gcr = jnp.concatenate([gc_row[h:h + 1, rs] for h in heads], axis=1)
        Kb = K * beta
        e = jnp.exp(jnp.where(incl, gcc - gcr, 0.0))
        L = lax.dot_general(Kb, K, _NT, preferred_element_type=F32) * jnp.where(strict, e, 0.0)
        QKc.append(lax.dot_general(Q, K, _NT, preferred_element_type=F32) * jnp.where(incl, e, 0.0))
        rhsc.append(jnp.concatenate([V * beta, Kb * jnp.exp(gcc)], axis=1))
        invc.append(eye - jnp.where(level_mask(1), L, 0.0))
        Kc.append(K); Qc.append(Q); glc.append(gl); gccc.append(gcc); Lc.append(L)

    for s in (2, 4, 8, 16, 32):
        msk = level_mask(s)
        ts = [jnp.dot(invc[c], jnp.where(msk, Lc[c], 0.0), preferred_element_type=F32) for c in chunks]
        invc = [invc[c] - jnp.dot(ts[c], invc[c], preferred_element_type=F32) for c in chunks]
    solc = [jnp.dot(invc[c], rhsc[c], preferred_element_type=F32) for c in chunks]

    for c in chunks:
        rs = slice(c * CHUNK, (c + 1) * CHUNK)
        U = solc[c][:, :B_DIM]
        W = solc[c][:, B_DIM:]
        Qd = Qc[c] * jnp.exp(gccc[c])
        Kd = Kc[c] * jnp.exp(glc[c] - gccc[c])
        vnew = []
        ost = []
        for h in heads:
            hs = slice(h * CHUNK, (h + 1) * CHUNK)
            S = s_sc[h]
            vn = U[hs] - jnp.dot(W[hs], S, preferred_element_type=F32)
            ost.append(jnp.dot(Qd[hs], S, preferred_element_type=F32))
            s_sc[h] = S * jnp.exp(glc[c][h * CHUNK:h * CHUNK + 1, :]) + lax.dot_general(
                Kd[hs], vn, _TN, preferred_element_type=F32)
            vnew.append(vn)
        O = cat0(ost) + jnp.dot(QKc[c], cat0(vnew), preferred_element_type=F32)
        for h in heads:
            oh = _rms(O[h * CHUNK:(h + 1) * CHUNK], gob_ref[...])
            z = zb_ref[rs, h * 128:(h + 1) * 128]
            o_ref[rs, h * 128:(h + 1) * 128] = (oh * (z * _sigmoid(z))).astype(BF16)


def _gdn(qkvb, zb, ab, abt, conv_w8, alc, dtc, alr, dtr, gob):
    T = qkvb.shape[0]
    TC = TC_GDN
    full = lambda shape: pl.BlockSpec(shape, lambda i: (0,) * len(shape))
    return pl.pallas_call(
        _gdn_kernel,
        grid=(T // TC,),
        in_specs=[
            pl.BlockSpec((TC, 1536), lambda i: (i, 0)),
            pl.BlockSpec((TC, 512), lambda i: (i, 0)),
            pl.BlockSpec((TC, 128), lambda i: (i, 0)),
            pl.BlockSpec((16, TC), lambda i: (0, i)),
            full((8, 1536)), full((1, 128)), full((1, 128)), full((16, TC)), full((16, TC)), full((1, 128)),
        ],
        out_specs=pl.BlockSpec((TC, 512), lambda i: (i, 0)),
        out_shape=jax.ShapeDtypeStruct((T, 512), BF16),
        scratch_shapes=[pltpu.VMEM((TC + 8, 1536), F32), pltpu.VMEM((B_HEADS, B_DIM, B_DIM), F32)],
        compiler_params=pltpu.CompilerParams(
            dimension_semantics=("arbitrary",), vmem_limit_bytes=_vmem_limit(48 << 20)),
        name="gdn",
    )(qkvb, zb, ab, abt, conv_w8, alc, dtc, alr, dtr, gob)


def _memkv_kernel(m_ref, g_ref, w_ref, k_ref, v_ref):
    hm = _rms(m_ref[...], g_ref[...]).astype(BF16)
    kv = jnp.dot(hm, w_ref[...], preferred_element_type=F32)
    k_ref[...] = kv[:, :D_MODEL].astype(BF16)
    v_ref[...] = kv[:, D_MODEL:].astype(BF16)


def _memkv(mem2d, g_mem, w_kv):
    n = mem2d.shape[0]
    return pl.pallas_call(
        _memkv_kernel,
        out_shape=(jax.ShapeDtypeStruct((n, D_MODEL), BF16), jax.ShapeDtypeStruct((n, D_MODEL), BF16)),
        compiler_params=pltpu.CompilerParams(vmem_limit_bytes=_vmem_limit(32 << 20)),
        name="memkv",
    )(mem2d, g_mem, w_kv)


def _mix_kernel(x_ref, oa_ref, ob_ref, gates_ref, wua_ref, wub_ref, wout_ref, gx_ref, wq_ref, kx_ref, vx_ref,
                wo_ref, gm_ref, wr_ref, br_ref, x2t_ref, hmt_ref, idx_ref, wts_ref):
    NSUB = 1
    TS = TM_MIX // NSUB
    for sub in range(NSUB):
        rows = slice(sub * TS, (sub + 1) * TS)
        ma = jnp.dot(oa_ref[rows, :], wua_ref[...], preferred_element_type=F32)
        mb = jnp.dot(ob_ref[rows, :], wub_ref[...], preferred_element_type=F32)
        merged = _sigmoid(gates_ref[rows, :D_MODEL]) * ma + _sigmoid(gates_ref[rows, D_MODEL:]) * mb
        x1 = x_ref[rows, :] + jnp.dot(merged.astype(BF16), wout_ref[...], preferred_element_type=F32)

        hx = _rms(x1, gx_ref[...]).astype(BF16)
        heads = []
        for h in range(X_HEADS):
            cs = slice(h * X_HEAD_DIM, (h + 1) * X_HEAD_DIM)
            qh = jnp.dot(hx, wq_ref[:, cs], preferred_element_type=F32).astype(BF16)
            s = lax.dot_general(qh, kx_ref[:, cs], _NT, preferred_element_type=F32) * (X_HEAD_DIM ** -0.5)
            s = s - jnp.max(s, axis=-1, keepdims=True)
            p = jnp.exp(s)
            p = p / jnp.sum(p, axis=-1, keepdims=True)
            heads.append(jnp.dot(p.astype(BF16), vx_ref[:, cs], preferred_element_type=F32).astype(BF16))
        o = jnp.concatenate(heads, axis=1)
        x2 = x1 + jnp.dot(o, wo_ref[...], preferred_element_type=F32)
        hm = _rms(x2, gm_ref[...])

        for g in range(D_MODEL // 128):
            x2t_ref[pl.ds(sub * TS * 8 + g, TS, stride=8), :] = x2[:, g * 128:(g + 1) * 128]
            hmt_ref[pl.ds(sub * TS * 8 + g, TS, stride=8), :] = hm[:, g * 128:(g + 1) * 128]

        logits = lax.dot_general(wr_ref[...], hm, _NT, precision=lax.Precision.HIGHEST,
                                 preferred_element_type=F32) + br_ref[:, 0:1]
        eidx = lax.broadcasted_iota(I32, logits.shape, 0)
        vals, idxs = [], []
        cur = logits
        for _ in range(TOP_K):
            mx = jnp.max(cur, axis=0, keepdims=True)
            ix = jnp.min(jnp.where(cur == mx, eidx, N_EXPERTS), axis=0, keepdims=True)
            vals.append(mx)
            idxs.append(ix)
            cur = jnp.where(eidx == ix, -jnp.inf, cur)
        ex = [jnp.exp(v - vals[0]) for v in vals]
        den = ex[0] + ex[1] + ex[2] + ex[3]
        idx_ref[:, rows] = jnp.concatenate(idxs, axis=0)
        wts_ref[:, rows] = jnp.concatenate([e / den for e in ex], axis=0)


def _mix(x2d, oa, ob, gates, wua, wub, wout, gx, wq, kx, vx, wo, gm, wr_t, br):
    T = x2d.shape[0]
    TM = TM_MIX
    full = lambda a: pl.BlockSpec(a.shape, lambda i: (0,) * a.ndim)
    out_shape = (
        jax.ShapeDtypeStruct((T * 8, 128), F32),
        jax.ShapeDtypeStruct((T * 8, 128), F32),
        jax.ShapeDtypeStruct((TOP_K, T), I32),
        jax.ShapeDtypeStruct((TOP_K, T), F32),
    )
    return pl.pallas_call(
        _mix_kernel,
        grid=(T // TM,),
        in_specs=[
            pl.BlockSpec((TM, D_MODEL), lambda i: (i, 0)),
            pl.BlockSpec((TM, 512), lambda i: (i, 0)),
            pl.BlockSpec((TM, 512), lambda i: (i, 0)),
            pl.BlockSpec((TM, 2048), lambda i: (i, 0)),
            full(wua), full(wub), full(wout), full(gx), full(wq), full(kx), full(vx), full(wo), full(gm),
            full(wr_t), full(br),
        ],
        out_specs=(
            pl.BlockSpec((TM * 8, 128), lambda i: (i, 0)),
            pl.BlockSpec((TM * 8, 128), lambda i: (i, 0)),
            pl.BlockSpec((TOP_K, TM), lambda i: (0, i)),
            pl.BlockSpec((TOP_K, TM), lambda i: (0, i)),
        ),
        out_shape=out_shape,
        compiler_params=pltpu.CompilerParams(
            dimension_semantics=("arbitrary",), vmem_limit_bytes=_vmem_limit(56 << 20)),
        name="mix",
    )(x2d, oa, ob, gates, wua, wub, wout, gx, wq, kx, vx, wo, gm, wr_t, br)


def _moe_kernel(blk_e_ref, first_ref, nxt_ref, slot_ref, nb_ref,
                tok_ref, wt_ref, x2t_hbm, hmt_ref, w1_hbm, w2_hbm, b1_ref, b2_ref, gf_ref,
                out_ref, yacc, xs_a, xs_b, yst_a, yst_b, w1buf, w2buf, wsem, xsem):
    j = pl.program_id(0)
    TT = TT_MOE
    RB = RB_MOE
    NG = D_MODEL // 128
    nbj = nb_ref[j]

    x2_copy = pltpu.make_async_copy(
        x2t_hbm.at[pl.ds(pl.multiple_of(j * TT * 8, 8), TT * 8), :], yacc.at[pl.ds(0, TT * 8), :], xsem.at[0])
    x2_copy.start()

    def w_copies(e, slot):
        return (pltpu.make_async_copy(w1_hbm.at[e], w1buf.at[slot], wsem.at[slot, 0]),
                pltpu.make_async_copy(w2_hbm.at[e], w2buf.at[slot], wsem.at[slot, 1]))

    for cp in w_copies(blk_e_ref[j, 0], slot_ref[j, 0]):
        cp.start()

    yacc[TT * 8:TT * 8 + 8, :] = jnp.zeros((8, 128), F32)
    yst_b[...] = jnp.zeros(yst_b.shape, F32)

    def gather(b, xs):
        for r in range(RB):
            t = jnp.minimum(tok_ref[0, 0, b * RB + r], TT - 1)
            xs[r * 8:(r + 1) * 8, :] = hmt_ref[pl.ds(pl.multiple_of(t * 8, 8), 8), :]

    def mlp(xs, yst, e, slot):
        xb = jnp.concatenate([xs[pl.ds(g, RB, stride=8), :] for g in range(NG)], axis=1).astype(BF16)
        hid = jnp.dot(xb, w1buf[slot], preferred_element_type=F32) + b1_ref[e]
        glu = jnp.minimum(hid[:, :D_FF], SWIGLU_LIMIT)
        lin = jnp.clip(hid[:, D_FF:], -SWIGLU_LIMIT, SWIGLU_LIMIT)
        act = glu * _sigmoid(SWIGLU_ALPHA * glu) * (lin + 1.0)
        ys = jnp.dot(act.astype(BF16), w2buf[slot], preferred_element_type=F32) + b2_ref[e]
        for g in range(NG):
            yst[pl.ds(g, RB, stride=8), :] = ys[:, g * 128:(g + 1) * 128]

    def scatter(b, yst, scale):
        for g0 in range(0, RB, 8):
            dsts, vals = [], []
            for r in range(g0, g0 + 8):
                dst = pl.ds(pl.multiple_of(tok_ref[0, 0, b * RB + r] * 8, 8), 8)
                dsts.append(dst)
                vals.append(yacc[dst, :] + (wt_ref[0, 0, b * RB + r] * scale) * yst[r * 8:(r + 1) * 8, :])
            for dst, val in zip(dsts, vals):
                yacc[dst, :] = val

    def step(b, xs_cur, yst_cur, xs_next, yst_prev):
        @pl.when(b < nbj)
        def _():
            e = blk_e_ref[j, b]
            slot = slot_ref[j, b]

            @pl.when(first_ref[j, b] == 1)
            def _():
                for cp in w_copies(e, slot):
                    cp.wait()
                nx = nxt_ref[j, b]

                @pl.when(nx >= 0)
                def _():
                    for cp in w_copies(nx, 1 - slot):
                        cp.start()

            gather(b + 1, xs_next)
            mlp(xs_cur, yst_cur, e, slot)
            scatter(jnp.maximum(b - 1, 0), yst_prev, jnp.where(b >= 1, 1.0, 0.0))

    gather(0, xs_a)
    x2_copy.wait()

    def pair_body(pp, c):
        step(2 * pp, xs_a, yst_a, xs_b, yst_b)
        step(2 * pp + 1, xs_b, yst_b, xs_a, yst_a)
        return c

    lax.fori_loop(0, (nbj + 1) // 2, pair_body, 0)

    @pl.when(nbj % 2 == 1)
    def _():
        scatter(nbj - 1, yst_a, 1.0)

    @pl.when(nbj % 2 == 0)
    def _():
        scatter(nbj - 1, yst_b, 1.0)

    RC = 256

    def fin(ci, c):
        base = pl.multiple_of(ci * RC * 8, 8)
        yv = jnp.concatenate([yacc[pl.ds(base + g, RC, stride=8), :] for g in range(NG)], axis=1)
        out_ref[pl.ds(pl.multiple_of(ci * RC, 8), RC), :] = _rms(yv, gf_ref[...])
        return c

    lax.fori_loop(0, TT // RC, fin, 0)


def _moe_nb_max():
    return TOP_K * TT_MOE // RB_MOE + N_EXPERTS


def _moe(lists, x2t, hmt, w1, b1, w2, b2, g_final):
    blk_e, first, nxt, slot, nb, tok, wt = lists
    T = x2t.shape[0] // 8
    TT = TT_MOE
    nt = T // TT
    LP = tok.shape[-1]
    smem_row = pl.BlockSpec((1, 1, LP), lambda j, *_: (j, 0, 0), memory_space=pltpu.SMEM)
    whole = lambda a: pl.BlockSpec(a.shape, lambda j, *_: (0,) * a.ndim)
    grid_spec = pltpu.PrefetchScalarGridSpec(
        num_scalar_prefetch=5,
        grid=(nt,),
        in_specs=[
            smem_row, smem_row,
            pl.BlockSpec(memory_space=pl.ANY),
            pl.BlockSpec((TT * 8, 128), lambda j, *_: (j, 0)),
            pl.BlockSpec(memory_space=pl.ANY), pl.BlockSpec(memory_space=pl.ANY),
            whole(b1), whole(b2), whole(g_final),
        ],
        out_specs=pl.BlockSpec((TT, D_MODEL), lambda j, *_: (j, 0), pipeline_mode=pl.Buffered(1)),
        scratch_shapes=[
            pltpu.VMEM((TT * 8 + 8, 128), F32),
            pltpu.VMEM((RB_MOE * 8, 128), F32), pltpu.VMEM((RB_MOE * 8, 128), F32),
            pltpu.VMEM((RB_MOE * 8, 128), F32), pltpu.VMEM((RB_MOE * 8, 128), F32),
            pltpu.VMEM((2, D_MODEL, 2 * D_FF), BF16),
            pltpu.VMEM((2, D_FF, D_MODEL), BF16),
            pltpu.SemaphoreType.DMA((2, 2)),
            pltpu.SemaphoreType.DMA((1,)),
        ],
    )
    return pl.pallas_call(
        _moe_kernel,
        grid_spec=grid_spec,
        out_shape=jax.ShapeDtypeStruct((T, D_MODEL), F32),
        compiler_params=pltpu.CompilerParams(
            dimension_semantics=("arbitrary",), vmem_limit_bytes=_vmem_limit(60 << 20)),
        name="moe",
    )(blk_e, first, nxt, slot, nb, tok, wt, x2t, hmt, w1, w2, b1, b2, g_final)


def _moe_lists(idx, wts):
    T = idx.shape[1]
    TT, RB = TT_MOE, RB_MOE
    nt = T // TT
    NB = _moe_nb_max()
    A = TOP_K * TT
    P = (NB + 1) * RB
    e_tile = idx.reshape(TOP_K, nt, TT).transpose(1, 0, 2).reshape(nt, A)
    w_tile = wts.reshape(TOP_K, nt, TT).transpose(1, 0, 2).reshape(nt, A)
    tok_local = jnp.tile(jnp.arange(TT, dtype=I32), TOP_K)
    order = jnp.argsort(e_tile, axis=1, stable=True)
    stok = jnp.take_along_axis(jnp.broadcast_to(tok_local, e_tile.shape), order, axis=1)
    swt = jnp.take_along_axis(w_tile, order, axis=1)

    experts = jnp.arange(N_EXPERTS, dtype=I32)
    counts = jnp.sum((e_tile[:, :, None] == experts[None, None, :]).astype(I32), axis=1)
    start = jnp.cumsum(counts, axis=1) - counts
    nblk = (counts + RB - 1) // RB
    blk_end = jnp.cumsum(nblk, axis=1)
    blk_start = blk_end - nblk
    nb = blk_end[:, -1]

    b = jnp.arange(NB, dtype=I32)
    blk_e = jnp.minimum(jnp.sum((blk_end[:, :, None] <= b[None, None, :]).astype(I32), axis=1), N_EXPERTS - 1)
    onehot = (blk_e[:, :, None] == experts[None, None, :]).astype(I32)
    per_block = lambda v: jnp.sum(onehot * v[:, None, :], axis=2)
    present = (nblk > 0).astype(I32)
    ordinal = jnp.cumsum(present, axis=1) - present
    later = (experts[None, :] > experts[:, None])[None] & (present[:, None, :] > 0)
    nxt_e = jnp.min(jnp.where(later, experts[None, None, :], N_EXPERTS), axis=2)
    nxt_e = jnp.where(nxt_e >= N_EXPERTS, -1, nxt_e)
    bs_b = per_block(blk_start)
    first = ((b[None, :] == bs_b) & (b[None, :] < nb[:, None])).astype(I32)
    slot = per_block(ordinal) % 2
    nxt = per_block(nxt_e)

    p = jnp.arange(P, dtype=I32)
    pb = jnp.minimum(p // RB, NB - 1)
    off = (pb[None, :] - jnp.take_along_axis(bs_b, pb[None, :].repeat(nt, 0), axis=1)) * RB + (p % RB)[None, :]
    cnt_p = jnp.take_along_axis(per_block(counts), pb[None, :].repeat(nt, 0), axis=1)
    src = jnp.take_along_axis(per_block(start), pb[None, :].repeat(nt, 0), axis=1) + off
    valid = (off < cnt_p) & (p[None, :] < (nb * RB)[:, None])
    src = jnp.clip(src, 0, A - 1)
    tok = jnp.where(valid, jnp.take_along_axis(stok, src, axis=1), TT)
    wt = jnp.where(valid, jnp.take_along_axis(swt, src, axis=1), 0.0)
    return blk_e, first, nxt, slot, nb, tok[:, None, :], wt[:, None, :]


def kernel(x, mem, g_mix, w_in, rel_bias, lambda_q1, lambda_k1, lambda_q2, lambda_k2, g_subln, conv_w, a_log,
           dt_bias, g_out_b, w_up_a, w_up_b, w_out, g_xattn, g_mem, w_q_x, w_kv_x, w_o_x, g_moe, w_router,
           b_router, w_mlp1, b_mlp1, w_mlp2, b_mlp2, g_final):
    B_, S_, _ = x.shape
    assert B_ == 1 and S_ % TT_MOE == 0 and x.dtype == F32
    l = 0
    x2d = x.reshape(S_, D_MODEL)
    row = lambda v: v.reshape(1, -1).astype(F32)

    wi = w_in[l]
    ab_cols = jnp.pad(wi[:, 3584:3592], ((0, 0), (0, 120)))
    w_main = jnp.concatenate(
        [wi[:, 0:1024], wi[:, 1536:3072], wi[:, 3072:3584], wi[:, 3592:5640], ab_cols], axis=1).astype(BF16)
    w_vt = wi[:, 1024:1536].T.astype(BF16)
    w_abt = jnp.pad(wi[:, 3584:3592].T, ((0, 8), (0, 0))).astype(BF16)

    q, k, vt, qkvb, zb, gates, ab, abt = _inproj(x2d, row(g_mix[l]), w_main, w_vt, w_abt)

    lam = (jnp.exp(jnp.sum(lambda_q1[l].astype(F32) * lambda_k1[l].astype(F32)))
           - jnp.exp(jnp.sum(lambda_q2[l].astype(F32) * lambda_k2[l].astype(F32))) + LAMBDA_INIT).reshape(1)
    bias_d, bias_p = _attn_bias_tables(rel_bias)
    oa = _attention(lam, q, k, vt, bias_d, bias_p, row(g_subln[l]))

    lane_pad = lambda v: jnp.pad(v.astype(F32), (0, 128 - v.shape[0])).reshape(1, 128)
    row_bcast = lambda v: jnp.broadcast_to(jnp.pad(v.astype(F32), (0, 16 - v.shape[0]))[:, None], (16, TC_GDN))
    ob = _gdn(qkvb, zb, ab, abt, jnp.pad(conv_w[l].astype(F32), ((0, 4), (0, 0))),
              lane_pad(a_log[l]), lane_pad(dt_bias[l]), row_bcast(a_log[l]), row_bcast(dt_bias[l]),
              row(g_out_b[l]))

    kx, vx = _memkv(mem.reshape(-1, D_MODEL), row(g_mem[l]), w_kv_x[l].astype(BF16))
    br = jnp.broadcast_to(b_router[l].astype(F32)[:, None], (N_EXPERTS, 128))
    x2t, hmt, idx, wts = _mix(
        x2d, oa, ob, gates, w_up_a[l].astype(BF16), w_up_b[l].astype(BF16), w_out[l].astype(BF16),
        row(g_xattn[l]), w_q_x[l].astype(BF16), kx, vx, w_o_x[l].astype(BF16), row(g_moe[l]),
        w_router[l].T.astype(F32), br)

    out = _moe(_moe_lists(idx, wts), x2t, hmt, w_mlp1[l].astype(BF16), b_mlp1[l].astype(F32)[:, None, :],
               w_mlp2[l].astype(BF16), b_mlp2[l].astype(F32)[:, None, :], row(g_final))
    return out.reshape(B_, S_, D_MODEL)
```

```python
import functools
import math

import jax
import jax.numpy as jnp
from jax import lax
from jax.experimental import pallas as pl
from jax.experimental.pallas import tpu as pltpu

F32 = jnp.float32
BF16 = jnp.bfloat16
I32 = jnp.int32

D_MODEL = 1024
CHUNK = 64
EPS = 1e-6
A_HEADS = 4
A_QK_DIM = 64
A_V_DIM = 128
REL_BUCKETS = 32
REL_MAX_DIST = 128
B_HEADS = 4
B_DIM = 128
CONV_WIDTH = 4
X_HEADS = 4
X_HEAD_DIM = 256
N_EXPERTS = 32
TOP_K = 4
D_FF = 1024
SWIGLU_LIMIT = 7.0
SWIGLU_ALPHA = 1.702
LAMBDA_INIT = 0.8 - 0.6 * math.exp(-0.3 * 0)

LOG2E = 1.4426950408889634
NEG_BIG = -1e30

V7X_LANES = 128
V7X_SUBLANES = 8
V7X_VMEM_BYTES = 64 * 1024 * 1024

TM_PROJ = 512
TQ = 512
TK = 512
TC_GDN = 256
TM_MIX = 512
TT_MOE = 2048
RB_MOE = 128
MOE_W_CHUNKS = 4

C_QA, C_KA, C_QKVB, C_ZB, C_GATE, C_AB, C_END = 0, 512, 1024, 2560, 3072, 5120, 5248

_NT = (((1,), (1,)), ((), ()))
_TN = (((0,), (0,)), ((), ()))


def _rms(x, g):
    return x * lax.rsqrt(jnp.mean(x * x, axis=-1, keepdims=True) + EPS) * g


def _sigmoid(x):
    return 1.0 / (1.0 + jnp.exp(-x))


def _vmem_limit(nbytes):
    return int(min(nbytes, V7X_VMEM_BYTES - 4 * 1024 * 1024))


def _inproj_kernel(x_ref, g_ref, w_ref, wvt_ref, wabt_ref,
                   q_ref, k_ref, vt_ref, qkvb_ref, zb_ref, gates_ref, ab_ref, abt_ref):
    h = _rms(x_ref[...], g_ref[...]).astype(BF16)

    def mm(c0, c1):
        return jnp.dot(h, w_ref[:, c0:c1], preferred_element_type=F32)

    nb = TM_PROJ // TK
    qa = mm(C_QA, C_KA) * (A_QK_DIM ** -0.5 * LOG2E)
    ka = mm(C_KA, C_QKVB)
    vt = lax.dot_general(wvt_ref[...], h, _NT, preferred_element_type=F32)
    for hh in range(A_HEADS):
        cs = slice(hh * 128, (hh + 1) * 128)
        q_ref[hh] = qa[:, cs].astype(BF16)
        for b in range(nb):
            rs = slice(b * TK, (b + 1) * TK)
            k_ref[hh, b] = ka[rs, cs].astype(BF16)
            vt_ref[hh, b] = vt[cs, rs].astype(BF16)
    for j in range(3):
        qkvb_ref[:, j * 512:(j + 1) * 512] = mm(C_QKVB + j * 512, C_QKVB + (j + 1) * 512)
    zb_ref[...] = mm(C_ZB, C_GATE)
    for j in range(4):
        gates_ref[:, j * 512:(j + 1) * 512] = mm(C_GATE + j * 512, C_GATE + (j + 1) * 512)
    ab_ref[...] = mm(C_AB, C_END)
    abt_ref[...] = lax.dot_general(wabt_ref[...], h, _NT, preferred_element_type=F32)


def _inproj(x2d, g_mix, w_main, w_vt, w_abt):
    T = x2d.shape[0]
    n = T // TM_PROJ
    nkb = T // TK
    nb = TM_PROJ // TK
    full = lambda shape: pl.BlockSpec(shape, lambda i: (0,) * len(shape))
    out_shape = (
        jax.ShapeDtypeStruct((A_HEADS, T, 128), BF16),
        jax.ShapeDtypeStruct((A_HEADS, nkb, TK, 128), BF16),
        jax.ShapeDtypeStruct((A_HEADS, nkb, 128, TK), BF16),
        jax.ShapeDtypeStruct((T, 1536), F32),
        jax.ShapeDtypeStruct((T, 512), F32),
        jax.ShapeDtypeStruct((T, 2048), F32),
        jax.ShapeDtypeStruct((T, 128), F32),
        jax.ShapeDtypeStruct((16, T), F32),
    )
    out_specs = (
        pl.BlockSpec((A_HEADS, TM_PROJ, 128), lambda i: (0, i, 0)),
        pl.BlockSpec((A_HEADS, nb, TK, 128), lambda i: (0, i, 0, 0)),
        pl.BlockSpec((A_HEADS, nb, 128, TK), lambda i: (0, i, 0, 0)),
        pl.BlockSpec((TM_PROJ, 1536), lambda i: (i, 0)),
        pl.BlockSpec((TM_PROJ, 512), lambda i: (i, 0)),
        pl.BlockSpec((TM_PROJ, 2048), lambda i: (i, 0)),
        pl.BlockSpec((TM_PROJ, 128), lambda i: (i, 0)),
        pl.BlockSpec((16, TM_PROJ), lambda i: (0, i)),
    )
    return pl.pallas_call(
        _inproj_kernel,
        grid=(n,),
        in_specs=[
            pl.BlockSpec((TM_PROJ, D_MODEL), lambda i: (i, 0)),
            full((1, D_MODEL)),
            full(w_main.shape),
            full(w_vt.shape),
            full(w_abt.shape),
        ],
        out_specs=out_specs,
        out_shape=out_shape,
        compiler_params=pltpu.CompilerParams(
            dimension_semantics=("arbitrary",), vmem_limit_bytes=_vmem_limit(56 << 20)),
        name="inproj",
    )(x2d, g_mix, w_main, w_vt, w_abt)


def _attn_kernel(lam_ref, q_ref, k_ref, vt_ref, bd_ref, bp_ref, gs_ref, o_ref, m_sc, l_sc, acc_sc, s_buf, pm_buf):
    qi = pl.program_id(1)
    q = q_ref[0]
    lane = lax.broadcasted_iota(I32, q.shape, 1)
    zero = jnp.zeros_like(q)
    qm = (jnp.where(lane < A_QK_DIM, q, zero), jnp.where(lane >= A_QK_DIM, q, zero))
    m_sc[...] = jnp.full(m_sc.shape, NEG_BIG, F32)
    l_sc[...] = jnp.zeros(l_sc.shape, F32)
    acc_sc[...] = jnp.zeros(acc_sc.shape, F32)

    nkb = k_ref.shape[1]

    def stage_a(slot, kb0):
        for j in range(2):
            kblk = k_ref[0, jnp.minimum(kb0 + j, nkb - 1)]
            for m in range(2):
                s = lax.dot_general(kblk, qm[m], _NT, preferred_element_type=F32)
                s_buf[slot, m, j] = s
                pm_buf[slot, m, j] = jnp.max(s, axis=0, keepdims=True)

    def stage_b(slot, kb0, biases):
        far = all(b is None for b in biases)
        for m in range(2):
            m_old = m_sc[m]
            m_new = m_old
            ss = []
            for j, bias in enumerate(biases):
                if far:
                    m_new = jnp.maximum(m_new, pm_buf[slot, m, j])
                else:
                    s = s_buf[slot, m, j]
                    s = s if bias is None else s + bias[0]
                    ss.append(s)
                    m_new = jnp.maximum(m_new, jnp.max(s, axis=0, keepdims=True))
            alpha = jnp.exp2(m_old - m_new)
            l_new = alpha * l_sc[m]
            acc = alpha * acc_sc[m]
            for j in range(len(biases)):
                s = s_buf[slot, m, j] if far else ss[j]
                p = jnp.exp2(s - m_new)
                l_new = l_new + jnp.sum(p, axis=0, keepdims=True)
                acc = acc + jnp.dot(vt_ref[0, kb0 + j], p.astype(BF16), preferred_element_type=F32)
            l_sc[m] = l_new
            acc_sc[m] = acc
            m_sc[m] = m_new

    far2 = [None, None]
    nfu = jnp.maximum(qi - 1, 0) // 2
    peel = nfu % 2

    @pl.when(peel == 1)
    def _():
        stage_a(0, 0)
        stage_b(0, 0, far2)

    stage_a(0, 2 * peel)

    def far_body(v, c):
        u = peel + 2 * v
        stage_a(1, 2 * u + 2)
        stage_b(0, 2 * u, far2)
        stage_a(0, 2 * u + 4)
        stage_b(1, 2 * u + 2, far2)
        return c

    lax.fori_loop(0, (nfu - peel) // 2, far_body, 0)
    kb0 = 2 * nfu

    @pl.when(qi % 2 == 1)
    def _():
        stage_b(0, kb0, [bp_ref, bd_ref])

    @pl.when((qi % 2 == 0) & (qi >= 2))
    def _():
        stage_a(1, kb0 + 2)
        stage_b(0, kb0, [None, bp_ref])
        stage_b(1, kb0 + 2, [bd_ref])

    @pl.when(qi == 0)
    def _():
        stage_b(0, kb0, [bd_ref])

    o = acc_sc[0] / l_sc[0] - lam_ref[0] * (acc_sc[1] / l_sc[1])
    ot = o.T
    o_ref[...] = (_rms(ot, gs_ref[...]) * (1.0 - LAMBDA_INIT)).astype(BF16)


def _attention(lam, q, k, vt, bias_d, bias_p, g_subln):
    T = q.shape[1]
    nq = T // TQ
    nkb = T // TK
    return pl.pallas_call(
        _attn_kernel,
        grid=(A_HEADS, nq),
        in_specs=[
            pl.BlockSpec(memory_space=pltpu.SMEM),
            pl.BlockSpec((1, TQ, 128), lambda h, i: (h, i, 0)),
            pl.BlockSpec((1, nkb, TK, 128), lambda h, i: (h, 0, 0, 0)),
            pl.BlockSpec((1, nkb, 128, TK), lambda h, i: (h, 0, 0, 0)),
            pl.BlockSpec((1, TK, TQ), lambda h, i: (h, 0, 0)),
            pl.BlockSpec((1, TK, TQ), lambda h, i: (h, 0, 0)),
            pl.BlockSpec((1, 128), lambda h, i: (0, 0)),
        ],
        out_specs=pl.BlockSpec((TQ, 128), lambda h, i: (i, h)),
        out_shape=jax.ShapeDtypeStruct((T, A_HEADS * A_V_DIM), BF16),
        scratch_shapes=[
            pltpu.VMEM((2, 1, TQ), F32),
            pltpu.VMEM((2, 1, TQ), F32),
            pltpu.VMEM((2, 128, TQ), F32),
            pltpu.VMEM((2, 2, 2, TK, TQ), F32),
            pltpu.VMEM((2, 2, 2, 1, TQ), F32),
        ],
        compiler_params=pltpu.CompilerParams(
            dimension_semantics=("arbitrary", "arbitrary"), vmem_limit_bytes=_vmem_limit(48 << 20)),
        name="attn",
    )(lam, q, k, vt, bias_d, bias_p, g_subln)


def _t5_bucket(rel):
    half = REL_BUCKETS // 2
    max_exact = half // 2
    ret = jnp.where(rel > 0, half, 0)
    n = jnp.abs(rel)
    large = max_exact + (jnp.log(jnp.maximum(n, 1).astype(F32) / max_exact)
                         / math.log(REL_MAX_DIST / max_exact) * (half - max_exact)).astype(I32)
    large = jnp.minimum(large, half - 1)
    return ret + jnp.where(n < max_exact, n, large)


def _attn_bias_tables(rel_bias):
    assert TK >= REL_MAX_DIST and TK == TQ and TK % CHUNK == 0
    kk = jnp.arange(TK, dtype=I32)[:, None]
    qq = jnp.arange(TQ, dtype=I32)[None, :]
    rb = rel_bias.astype(F32)
    far = rb[_t5_bucket(jnp.full((1,), -REL_MAX_DIST, I32))[0]]
    table = ((rb - far[None, :]) * LOG2E).T

    def lookup(rel):
        onehot = (_t5_bucket(rel).reshape(1, -1) == jnp.arange(REL_BUCKETS, dtype=I32)[:, None]).astype(F32)
        return jnp.dot(table, onehot, precision=lax.Precision.HIGHEST).reshape(A_HEADS, TK, TQ)

    bd = jnp.where((kk // CHUNK <= qq // CHUNK)[None], lookup(kk - qq), NEG_BIG)
    bp = lookup(kk - TK - qq)
    return bd, bp


def _gdn_kernel(x_ref, zb_ref, ab_ref, abt_ref, cw_ref, alc_ref, dtc_ref, alr_ref, dtr_ref, gob_ref,
                o_ref, xbuf, s_sc):
    i = pl.program_id(0)
    TC = TC_GDN
    NCH = TC // CHUNK
    HB = B_HEADS * CHUNK

    @pl.when(i == 0)
    def _():
        xbuf[0:8, :] = jnp.zeros((8, 1536), F32)
        s_sc[...] = jnp.zeros(s_sc.shape, F32)

    xbuf[8:8 + TC, :] = x_ref[...]
    y = cw_ref[3:4, :] * xbuf[8:8 + TC, :]
    for d in range(1, CONV_WIDTH):
        y = y + cw_ref[3 - d:4 - d, :] * xbuf[8 - d:8 - d + TC, :]
    xbuf[0:8, :] = x_ref[TC - 8:TC, :]
    a = y * _sigmoid(y)

    def l2n(t):
        return t * lax.rsqrt(jnp.sum(t * t, axis=-1, keepdims=True) + EPS)

    qh = [l2n(a[:, h * 128:(h + 1) * 128]) * (B_DIM ** -0.5) for h in range(B_HEADS)]
    kh = [l2n(a[:, 512 + h * 128:512 + (h + 1) * 128]) for h in range(B_HEADS)]
    vh = [a[:, 1024 + h * 128:1024 + (h + 1) * 128] for h in range(B_HEADS)]

    def softplus(t):
        return jnp.maximum(t, 0.0) + jnp.log(1.0 + jnp.exp(-jnp.abs(t)))

    ab = ab_ref[...]
    g_col = -jnp.exp(alc_ref[...]) * softplus(ab + dtc_ref[...])
    beta_col = _sigmoid(ab)
    g_row = -jnp.exp(alr_ref[...]) * softplus(abt_ref[...] + dtr_ref[...])

    rt = lax.broadcasted_iota(I32, (TC, TC), 0)
    ct = lax.broadcasted_iota(I32, (TC, TC), 1)
    same_chunk = (rt // CHUNK) == (ct // CHUNK)
    tril = jnp.where(same_chunk & (rt >= ct), 1.0, 0.0).astype(F32)
    triu = jnp.where(same_chunk & (rt <= ct), 1.0, 0.0).astype(F32)
    gc_col = jnp.dot(tril, g_col, precision=lax.Precision.HIGHEST, preferred_element_type=F32)
    gc_row = jnp.dot(g_row, triu, precision=lax.Precision.HIGHEST, preferred_element_type=F32)

    ri = lax.broadcasted_iota(I32, (HB, HB), 0)
    ci = lax.broadcasted_iota(I32, (HB, HB), 1)
    same_head = (ri // CHUNK) == (ci // CHUNK)
    incl = same_head & (ri >= ci)
    strict = same_head & (ri > ci)
    eye = jnp.where(ri == ci, 1.0, 0.0).astype(F32)

    def level_mask(s):
        return ((ri // (2 * s)) == (ci // (2 * s))) & (((ri // s) % 2) == 1) & (((ci // s) % 2) == 0)

    cat0 = lambda parts: jnp.concatenate(parts, axis=0)
    heads = range(B_HEADS)
    chunks = range(NCH)

    Kc, Qc, glc, gccc, Lc, QKc, invc, rhsc = [], [], [], [], [], [], [], []
    for c in chunks:
        rs = slice(c * CHUNK, (c + 1) * CHUNK)
        last = slice(c * CHUNK + CHUNK - 1, (c + 1) * CHUNK)
        K = cat0([kh[h][rs] for h in heads])
        Q = cat0([qh[h][rs] for h in heads])
        V = cat0([vh[h][rs] for h in heads])
        beta = cat0([beta_col[rs, 4 + h:5 + h] for h in heads])
        gcc = cat0([gc_col[rs, h:h + 1] for h in heads])
        gl = cat0([jnp.broadcast_to(gc_col[last, h:h + 1], (CHUNK, 1)) for h in heads])
        gcr = jnp.concatenate([gc_row[h:h + 1, rs] for h in heads], axis=1)
        Kb = K * beta
        e = jnp.exp(jnp.where(incl, gcc - gcr, 0.0))
        L = lax.dot_general(Kb, K, _NT, preferred_element_type=F32) * jnp.where(strict, e, 0.0)
        QKc.append(lax.dot_general(Q, K, _NT, preferred_element_type=F32) * jnp.where(incl, e, 0.0))
        rhsc.append(jnp.concatenate([V * beta, Kb * jnp.exp(gcc)], axis=1))
        invc.append(eye - jnp.where(level_mask(1), L, 0.0))
        Kc.append(K); Qc.append(Q); glc.append(gl); gccc.append(gcc); Lc.append(L)

    for s in (2, 4, 8, 16, 32):
        msk = level_mask(s)
        ts = [jnp.dot(invc[c], jnp.where(msk, Lc[c], 0.0), preferred_element_type=F32) for c in chunks]
        invc = [invc[c] - jnp.dot(ts[c], invc[c], preferred_element_type=F32) for c in chunks]
    solc = [jnp.dot(invc[c], rhsc[c], preferred_element_type=F32) for c in chunks]

    for c in chunks:
        rs = slice(c * CHUNK, (c + 1) * CHUNK)
        U = solc[c][:, :B_DIM]
        W = solc[c][:, B_DIM:]
        Qd = Qc[c] * jnp.exp(gccc[c])
        Kd = Kc[c] * jnp.exp(glc[c] - gccc[c])
        vnew = []
        ost = []
        for h in heads:
            hs = slice(h * CHUNK, (h + 1) * CHUNK)
            S = s_sc[h]
            vn = U[hs] - jnp.dot(W[hs], S, preferred_element_type=F32)
            ost.append(jnp.dot(Qd[hs], S, preferred_element_type=F32))
            s_sc[h] = S * jnp.exp(glc[c][h * CHUNK:h * CHUNK + 1, :]) + lax.dot_general(
                Kd[hs], vn, _TN, preferred_element_type=F32)
            vnew.append(vn)
        O = cat0(ost) + jnp.dot(QKc[c], cat0(vnew), preferred_element_type=F32)
        for h in heads:
            oh = _rms(O[h * CHUNK:(h + 1) * CHUNK], gob_ref[...])
            z = zb_ref[rs, h * 128:(h + 1) * 128]
            o_ref[rs, h * 128:(h + 1) * 128] = (oh * (z * _sigmoid(z))).astype(BF16)


def _gdn(qkvb, zb, ab, abt, conv_w8, alc, dtc, alr, dtr, gob):
    T = qkvb.shape[0]
    TC = TC_GDN
    full = lambda shape: pl.BlockSpec(shape, lambda i: (0,) * len(shape))
    return pl.pallas_call(
        _gdn_kernel,
        grid=(T // TC,),
        in_specs=[
            pl.BlockSpec((TC, 1536), lambda i: (i, 0)),
            pl.BlockSpec((TC, 512), lambda i: (i, 0)),
            pl.BlockSpec((TC, 128), lambda i: (i, 0)),
            pl.BlockSpec((16, TC), lambda i: (0, i)),
            full((8, 1536)), full((1, 128)), full((1, 128)), full((16, TC)), full((16, TC)), full((1, 128)),
        ],
        out_specs=pl.BlockSpec((TC, 512), lambda i: (i, 0)),
        out_shape=jax.ShapeDtypeStruct((T, 512), BF16),
        scratch_shapes=[pltpu.VMEM((TC + 8, 1536), F32), pltpu.VMEM((B_HEADS, B_DIM, B_DIM), F32)],
        compiler_params=pltpu.CompilerParams(
            dimension_semantics=("arbitrary",), vmem_limit_bytes=_vmem_limit(48 << 20)),
        name="gdn",
    )(qkvb, zb, ab, abt, conv_w8, alc, dtc, alr, dtr, gob)


def _memkv_kernel(m_ref, g_ref, w_ref, k_ref, v_ref):
    hm = _rms(m_ref[...], g_ref[...]).astype(BF16)
    kv = jnp.dot(hm, w_ref[...], preferred_element_type=F32)
    k_ref[...] = kv[:, :D_MODEL].astype(BF16)
    v_ref[...] = kv[:, D_MODEL:].astype(BF16)


def _memkv(mem2d, g_mem, w_kv):
    n = mem2d.shape[0]
    return pl.pallas_call(
        _memkv_kernel,
        out_shape=(jax.ShapeDtypeStruct((n, D_MODEL), BF16), jax.ShapeDtypeStruct((n, D_MODEL), BF16)),
        compiler_params=pltpu.CompilerParams(vmem_limit_bytes=_vmem_limit(32 << 20)),
        name="memkv",
    )(mem2d, g_mem, w_kv)


def _mix_kernel(x_ref, oa_ref, ob_ref, gates_ref, wua_ref, wub_ref, wout_ref, gx_ref, wq_ref, kx_ref, vx_ref,
                wo_ref, gm_ref, wr_ref, br_ref, x2t_ref, hmt_ref, idx_ref, wts_ref):
    NSUB = 1
    TS = TM_MIX // NSUB
    for sub in range(NSUB):
        rows = slice(sub * TS, (sub + 1) * TS)
        ma = jnp.dot(oa_ref[rows, :], wua_ref[...], preferred_element_type=F32)
        mb = jnp.dot(ob_ref[rows, :], wub_ref[...], preferred_element_type=F32)
        merged = _sigmoid(gates_ref[rows, :D_MODEL]) * ma + _sigmoid(gates_ref[rows, D_MODEL:]) * mb
        x1 = x_ref[rows, :] + jnp.dot(merged.astype(BF16), wout_ref[...], preferred_element_type=F32)

        hx = _rms(x1, gx_ref[...]).astype(BF16)
        heads = []
        for h in range(X_HEADS):
            cs = slice(h * X_HEAD_DIM, (h + 1) * X_HEAD_DIM)
            qh = jnp.dot(hx, wq_ref[:, cs], preferred_element_type=F32).astype(BF16)
            s = lax.dot_general(qh, kx_ref[:, cs], _NT, preferred_element_type=F32) * (X_HEAD_DIM ** -0.5)
            s = s - jnp.max(s, axis=-1, keepdims=True)
            p = jnp.exp(s)
            p = p / jnp.sum(p, axis=-1, keepdims=True)
            heads.append(jnp.dot(p.astype(BF16), vx_ref[:, cs], preferred_element_type=F32).astype(BF16))
        o = jnp.concatenate(heads, axis=1)
        x2 = x1 + jnp.dot(o, wo_ref[...], preferred_element_type=F32)
        hm = _rms(x2, gm_ref[...])

        for g in range(D_MODEL // 128):
            x2t_ref[pl.ds(sub * TS * 8 + g, TS, stride=8), :] = x2[:, g * 128:(g + 1) * 128]
            hmt_ref[pl.ds(sub * TS * 8 + g, TS, stride=8), :] = hm[:, g * 128:(g + 1) * 128]

        logits = lax.dot_general(wr_ref[...], hm, _NT, precision=lax.Precision.HIGHEST,
                                 preferred_element_type=F32) + br_ref[:, 0:1]
        eidx = lax.broadcasted_iota(I32, logits.shape, 0)
        vals, idxs = [], []
        cur = logits
        for _ in range(TOP_K):
            mx = jnp.max(cur, axis=0, keepdims=True)
            ix = jnp.min(jnp.where(cur == mx, eidx, N_EXPERTS), axis=0, keepdims=True)
            vals.append(mx)
            idxs.append(ix)
            cur = jnp.where(eidx == ix, -jnp.inf, cur)
        ex = [jnp.exp(v - vals[0]) for v in vals]
        den = ex[0] + ex[1] + ex[2] + ex[3]
        idx_ref[:, rows] = jnp.concatenate(idxs, axis=0)
        wts_ref[:, rows] = jnp.concatenate([e / den for e in ex], axis=0)


def _mix(x2d, oa, ob, gates, wua, wub, wout, gx, wq, kx, vx, wo, gm, wr_t, br):
    T = x2d.shape[0]
    TM = TM_MIX
    full = lambda a: pl.BlockSpec(a.shape, lambda i: (0,) * a.ndim)
    out_shape = (
        jax.ShapeDtypeStruct((T * 8, 128), F32),
        jax.ShapeDtypeStruct((T * 8, 128), F32),
        jax.ShapeDtypeStruct((TOP_K, T), I32),
        jax.ShapeDtypeStruct((TOP_K, T), F32),
    )
    return pl.pallas_call(
        _mix_kernel,
        grid=(T // TM,),
        in_specs=[
            pl.BlockSpec((TM, D_MODEL), lambda i: (i, 0)),
            pl.BlockSpec((TM, 512), lambda i: (i, 0)),
            pl.BlockSpec((TM, 512), lambda i: (i, 0)),
            pl.BlockSpec((TM, 2048), lambda i: (i, 0)),
            full(wua), full(wub), full(wout), full(gx), full(wq), full(kx), full(vx), full(wo), full(gm),
            full(wr_t), full(br),
        ],
        out_specs=(
            pl.BlockSpec((TM * 8, 128), lambda i: (i, 0)),
            pl.BlockSpec((TM * 8, 128), lambda i: (i, 0)),
            pl.BlockSpec((TOP_K, TM), lambda i: (0, i)),
            pl.BlockSpec((TOP_K, TM), lambda i: (0, i)),
        ),
        out_shape=out_shape,
        compiler_params=pltpu.CompilerParams(
            dimension_semantics=("arbitrary",), vmem_limit_bytes=_vmem_limit(56 << 20)),
        name="mix",
    )(x2d, oa, ob, gates, wua, wub, wout, gx, wq, kx, vx, wo, gm, wr_t, br)


def _moe_kernel(blk_e_ref, first_ref, nxt_ref, slot_ref, nb_ref,
                tok_ref, wt_ref, x2t_hbm, hmt_ref, w1_hbm, w2_hbm, b1_ref, b2_ref, gf_ref,
                out_ref, yacc, xs_a, xs_b, yst_a, yst_b, w1buf, w2buf, wsem, xsem):
    j = pl.program_id(0)
    TT = TT_MOE
    RB = RB_MOE
    NG = D_MODEL // 128
    nbj = nb_ref[j]

    x2_copy = pltpu.make_async_copy(
        x2t_hbm.at[pl.ds(pl.multiple_of(j * TT * 8, 8), TT * 8), :], yacc.at[pl.ds(0, TT * 8), :], xsem.at[0])
    x2_copy.start()

    def w_copies(e, slot):
        cps = []
        for c in range(MOE_W_CHUNKS):
            for k, (src, dst) in enumerate(((w1_hbm, w1buf), (w2_hbm, w2buf))):
                rows = pl.ds(c * (src.shape[1] // MOE_W_CHUNKS), src.shape[1] // MOE_W_CHUNKS)
                cps.append(pltpu.make_async_copy(src.at[e, rows], dst.at[slot, rows], wsem.at[slot, k, c]))
        return cps

    for cp in w_copies(blk_e_ref[j, 0], slot_ref[j, 0]):
        cp.start()

    yacc[TT * 8:TT * 8 + 8, :] = jnp.zeros((8, 128), F32)
    yst_b[...] = jnp.zeros(yst_b.shape, F32)

    def gather(b, xs):
        for r in range(RB):
            t = jnp.minimum(tok_ref[0, 0, b * RB + r], TT - 1)
            xs[r * 8:(r + 1) * 8, :] = hmt_ref[pl.ds(pl.multiple_of(t * 8, 8), 8), :]

    def mlp(xs, yst, e, slot):
        xb = jnp.concatenate([xs[pl.ds(g, RB, stride=8), :] for g in range(NG)], axis=1).astype(BF16)
        hid = jnp.dot(xb, w1buf[slot], preferred_element_type=F32) + b1_ref[e]
        glu = jnp.minimum(hid[:, :D_FF], SWIGLU_LIMIT)
        lin = jnp.clip(hid[:, D_FF:], -SWIGLU_LIMIT, SWIGLU_LIMIT)
        act = glu * _sigmoid(SWIGLU_ALPHA * glu) * (lin + 1.0)
        ys = jnp.dot(act.astype(BF16), w2buf[slot], preferred_element_type=F32) + b2_ref[e]
        for g in range(NG):
            yst[pl.ds(g, RB, stride=8), :] = ys[:, g * 128:(g + 1) * 128]

    def scatter(b, yst, scale):
        for g0 in range(0, RB, 8):
            dsts, vals = [], []
            for r in range(g0, g0 + 8):
                dst = pl.ds(pl.multiple_of(tok_ref[0, 0, b * RB + r] * 8, 8), 8)
                dsts.append(dst)
                vals.append(yacc[dst, :] + (wt_ref[0, 0, b * RB + r] * scale) * yst[r * 8:(r + 1) * 8, :])
            for dst, val in zip(dsts, vals):
                yacc[dst, :] = val

    def step(b, xs_cur, yst_cur, xs_next, yst_prev):
        @pl.when(b < nbj)
        def _():
            e = blk_e_ref[j, b]
            slot = slot_ref[j, b]

            @pl.when(first_ref[j, b] == 1)
            def _():
                for cp in w_copies(e, slot):
                    cp.wait()
                nx = nxt_ref[j, b]

                @pl.when(nx >= 0)
                def _():
                    for cp in w_copies(nx, 1 - slot):
                        cp.start()

            gather(b + 1, xs_next)
            mlp(xs_cur, yst_cur, e, slot)
            scatter(jnp.maximum(b - 1, 0), yst_prev, jnp.where(b >= 1, 1.0, 0.0))

    gather(0, xs_a)
    x2_copy.wait()

    def pair_body(pp, c):
        step(2 * pp, xs_a, yst_a, xs_b, yst_b)
        step(2 * pp + 1, xs_b, yst_b, xs_a, yst_a)
        return c

    lax.fori_loop(0, (nbj + 1) // 2, pair_body, 0)

    @pl.when(nbj % 2 == 1)
    def _():
        scatter(nbj - 1, yst_a, 1.0)

    @pl.when(nbj % 2 == 0)
    def _():
        scatter(nbj - 1, yst_b, 1.0)

    RC = 256

    def fin(ci, c):
        base = pl.multiple_of(ci * RC * 8, 8)
        yv = jnp.concatenate([yacc[pl.ds(base + g, RC, stride=8), :] for g in range(NG)], axis=1)
        out_ref[pl.ds(pl.multiple_of(ci * RC, 8), RC), :] = _rms(yv, gf_ref[...])
        return c

    lax.fori_loop(0, TT // RC, fin, 0)


def _moe_nb_max():
    return TOP_K * TT_MOE // RB_MOE + N_EXPERTS


def _moe(lists, x2t, hmt, w1, b1, w2, b2, g_final):
    blk_e, first, nxt, slot, nb, tok, wt = lists
    T = x2t.shape[0] // 8
    TT = TT_MOE
    nt = T // TT
    LP = tok.shape[-1]
    smem_row = pl.BlockSpec((1, 1, LP), lambda j, *_: (j, 0, 0), memory_space=pltpu.SMEM)
    whole = lambda a: pl.BlockSpec(a.shape, lambda j, *_: (0,) * a.ndim)
    grid_spec = pltpu.PrefetchScalarGridSpec(
        num_scalar_prefetch=5,
        grid=(nt,),
        in_specs=[
            smem_row, smem_row,
            pl.BlockSpec(memory_space=pl.ANY),
            pl.BlockSpec((TT * 8, 128), lambda j, *_: (j, 0)),
            pl.BlockSpec(memory_space=pl.ANY), pl.BlockSpec(memory_space=pl.ANY),
            whole(b1), whole(b2), whole(g_final),
        ],
        out_specs=pl.BlockSpec((TT, D_MODEL), lambda j, *_: (j, 0), pipeline_mode=pl.Buffered(1)),
        scratch_shapes=[
            pltpu.VMEM((TT * 8 + 8, 128), F32),
            pltpu.VMEM((RB_MOE * 8, 128), F32), pltpu.VMEM((RB_MOE * 8, 128), F32),
            pltpu.VMEM((RB_MOE * 8, 128), F32), pltpu.VMEM((RB_MOE * 8, 128), F32),
            pltpu.VMEM((2, D_MODEL, 2 * D_FF), BF16),
            pltpu.VMEM((2, D_FF, D_MODEL), BF16),
            pltpu.SemaphoreType.DMA((2, 2, MOE_W_CHUNKS)),
            pltpu.SemaphoreType.DMA((1,)),
        ],
    )
    return pl.pallas_call(
        _moe_kernel,
        grid_spec=grid_spec,
        out_shape=jax.ShapeDtypeStruct((T, D_MODEL), F32),
        compiler_params=pltpu.CompilerParams(
            dimension_semantics=("arbitrary",), vmem_limit_bytes=_vmem_limit(60 << 20)),
        name="moe",
    )(blk_e, first, nxt, slot, nb, tok, wt, x2t, hmt, w1, w2, b1, b2, g_final)


def _moe_lists(idx, wts):
    T = idx.shape[1]
    TT, RB = TT_MOE, RB_MOE
    nt = T // TT
    NB = _moe_nb_max()
    A = TOP_K * TT
    e_tile = idx.reshape(TOP_K, nt, TT).transpose(1, 0, 2).reshape(nt, A)
    w_tile = wts.reshape(TOP_K, nt, TT).transpose(1, 0, 2).reshape(nt, A)
    tok_local = jnp.tile(jnp.arange(TT, dtype=I32), TOP_K)
    order = jnp.argsort(e_tile, axis=1, stable=True)
    stok = jnp.take_along_axis(jnp.broadcast_to(tok_local, e_tile.shape), order, axis=1)
    swt = jnp.take_along_axis(w_tile, order, axis=1)

    experts = jnp.arange(N_EXPERTS, dtype=I32)
    counts = jnp.sum((e_tile[:, :, None] == experts[None, None, :]).astype(I32), axis=1)
    start = jnp.cumsum(counts, axis=1) - counts
    nblk = (counts + RB - 1) // RB
    blk_end = jnp.cumsum(nblk, axis=1)
    blk_start = blk_end - nblk
    nb = blk_end[:, -1]

    b = jnp.arange(NB, dtype=I32)
    blk_e = jnp.minimum(jnp.sum((blk_end[:, :, None] <= b[None, None, :]).astype(I32), axis=1), N_EXPERTS - 1)
    onehot = (blk_e[:, :, None] == experts[None, None, :]).astype(I32)
    per_block = lambda v: jnp.sum(onehot * v[:, None, :], axis=2)
    present = (nblk > 0).astype(I32)
    ordinal = jnp.cumsum(present, axis=1) - present
    later = (experts[None, :] > experts[:, None])[None] & (present[:, None, :] > 0)
    nxt_e = jnp.min(jnp.where(later, experts[None, None, :], N_EXPERTS), axis=2)
    nxt_e = jnp.where(nxt_e >= N_EXPERTS, -1, nxt_e)
    bs_b = per_block(blk_start)
    first = ((b[None, :] == bs_b) & (b[None, :] < nb[:, None])).astype(I32)
    slot = per_block(ordinal) % 2
    nxt = per_block(nxt_e)

    r = jnp.arange(RB, dtype=I32)[None, None, :]
    off = (b[None, :] - bs_b)[:, :, None] * RB + r
    valid = (off < per_block(counts)[:, :, None]) & (b[None, :, None] < nb[:, None, None])
    src = jnp.clip(per_block(start)[:, :, None] + off, 0, A - 1).reshape(nt, NB * RB)
    valid = valid.reshape(nt, NB * RB)
    tok = jnp.where(valid, jnp.take_along_axis(stok, src, axis=1), TT)
    wt = jnp.where(valid, jnp.take_along_axis(swt, src, axis=1), 0.0)
    tok = jnp.pad(tok, ((0, 0), (0, RB)), constant_values=TT)
    wt = jnp.pad(wt, ((0, 0), (0, RB)))
    return blk_e, first, nxt, slot, nb, tok[:, None, :], wt[:, None, :]


def kernel(x, mem, g_mix, w_in, rel_bias, lambda_q1, lambda_k1, lambda_q2, lambda_k2, g_subln, conv_w, a_log,
           dt_bias, g_out_b, w_up_a, w_up_b, w_out, g_xattn, g_mem, w_q_x, w_kv_x, w_o_x, g_moe, w_router,
           b_router, w_mlp1, b_mlp1, w_mlp2, b_mlp2, g_final):
    B_, S_, _ = x.shape
    assert B_ == 1 and S_ % TT_MOE == 0 and x.dtype == F32
    l = 0
    x2d = x.reshape(S_, D_MODEL)
    row = lambda v: v.reshape(1, -1).astype(F32)

    wi = w_in[l]
    ab_cols = jnp.pad(wi[:, 3584:3592], ((0, 0), (0, 120)))
    w_main = jnp.concatenate(
        [wi[:, 0:1024], wi[:, 1536:3072], wi[:, 3072:3584], wi[:, 3592:5640], ab_cols], axis=1).astype(BF16)
    w_vt = wi[:, 1024:1536].T.astype(BF16)
    w_abt = jnp.pad(wi[:, 3584:3592].T, ((0, 8), (0, 0))).astype(BF16)

    q, k, vt, qkvb, zb, gates, ab, abt = _inproj(x2d, row(g_mix[l]), w_main, w_vt, w_abt)

    lam = (jnp.exp(jnp.sum(lambda_q1[l].astype(F32) * lambda_k1[l].astype(F32)))
           - jnp.exp(jnp.sum(lambda_q2[l].astype(F32) * lambda_k2[l].astype(F32))) + LAMBDA_INIT).reshape(1)
    bias_d, bias_p = _attn_bias_tables(rel_bias)
    oa = _attention(lam, q, k, vt, bias_d, bias_p, row(g_subln[l]))

    lane_pad = lambda v: jnp.pad(v.astype(F32), (0, 128 - v.shape[0])).reshape(1, 128)
    row_bcast = lambda v: jnp.broadcast_to(jnp.pad(v.astype(F32), (0, 16 - v.shape[0]))[:, None], (16, TC_GDN))
    ob = _gdn(qkvb, zb, ab, abt, jnp.pad(conv_w[l].astype(F32), ((0, 4), (0, 0))),
              lane_pad(a_log[l]), lane_pad(dt_bias[l]), row_bcast(a_log[l]), row_bcast(dt_bias[l]),
              row(g_out_b[l]))

    kx, vx = _memkv(mem.reshape(-1, D_MODEL), row(g_mem[l]), w_kv_x[l].astype(BF16))
    br = jnp.broadcast_to(b_router[l].astype(F32)[:, None], (N_EXPERTS, 128))
    x2t, hmt, idx, wts = _mix(
        x2d, oa, ob, gates, w_up_a[l].astype(BF16), w_up_b[l].astype(BF16), w_out[l].astype(BF16),
        row(g_xattn[l]), w_q_x[l].astype(BF16), kx, vx, w_o_x[l].astype(BF16), row(g_moe[l]),
        w_router[l].T.astype(F32), br)

    out = _moe(_moe_lists(idx, wts), x2t, hmt, w_mlp1[l].astype(BF16), b_mlp1[l].astype(F32)[:, None, :],
               w_mlp2[l].astype(BF16), b_mlp2[l].astype(F32)[:, None, :], row(g_final))
    return out.reshape(B_, S_, D_MODEL)
```

```python
import functools
import math

import jax
import jax.numpy as jnp
from jax import lax
from jax.experimental import pallas as pl
from jax.experimental.pallas import tpu as pltpu

F32 = jnp.float32
BF16 = jnp.bfloat16
I32 = jnp.int32

D_MODEL = 1024
CHUNK = 64
EPS = 1e-6
A_HEADS = 4
A_QK_DIM = 64
A_V_DIM = 128
REL_BUCKETS = 32
REL_MAX_DIST = 128
B_HEADS = 4
B_DIM = 128
CONV_WIDTH = 4
X_HEADS = 4
X_HEAD_DIM = 256
N_EXPERTS = 32
TOP_K = 4
D_FF = 1024
SWIGLU_LIMIT = 7.0
SWIGLU_ALPHA = 1.702
LAMBDA_INIT = 0.8 - 0.6 * math.exp(-0.3 * 0)

LOG2E = 1.4426950408889634
NEG_BIG = -1e30

V7X_LANES = 128
V7X_SUBLANES = 8
V7X_VMEM_BYTES = 64 * 1024 * 1024

TM_PROJ = 512
TQ = 512
TK = 512
TC_GDN = 256
TM_MIX = 512
TT_MOE = 2048
RB_MOE = 128
MOE_W_CHUNKS = 4

C_QA, C_KA, C_QKVB, C_ZB, C_GATE, C_AB, C_END = 0, 512, 1024, 2560, 3072, 5120, 5248

_NT = (((1,), (1,)), ((), ()))
_TN = (((0,), (0,)), ((), ()))


def _rms(x, g):
    return x * lax.rsqrt(jnp.mean(x * x, axis=-1, keepdims=True) + EPS) * g


def _sigmoid(x):
    return 1.0 / (1.0 + jnp.exp(-x))


def _vmem_limit(nbytes):
    return int(min(nbytes, V7X_VMEM_BYTES - 4 * 1024 * 1024))


def _inproj_kernel(x_ref, g_ref, w_ref, wvt_ref, wabt_ref,
                   q_ref, k_ref, vt_ref, qkvb_ref, zb_ref, gates_ref, ab_ref, abt_ref):
    h = _rms(x_ref[...], g_ref[...]).astype(BF16)

    def mm(c0, c1):
        return jnp.dot(h, w_ref[:, c0:c1], preferred_element_type=F32)

    nb = TM_PROJ // TK
    qa = mm(C_QA, C_KA) * (A_QK_DIM ** -0.5 * LOG2E)
    ka = mm(C_KA, C_QKVB)
    vt = lax.dot_general(wvt_ref[...], h, _NT, preferred_element_type=F32)
    for hh in range(A_HEADS):
        cs = slice(hh * 128, (hh + 1) * 128)
        q_ref[hh] = qa[:, cs].astype(BF16)
        for b in range(nb):
            rs = slice(b * TK, (b + 1) * TK)
            k_ref[hh, b] = ka[rs, cs].astype(BF16)
            vt_ref[hh, b] = vt[cs, rs].astype(BF16)
    for j in range(3):
        qkvb_ref[:, j * 512:(j + 1) * 512] = mm(C_QKVB + j * 512, C_QKVB + (j + 1) * 512)
    zb_ref[...] = mm(C_ZB, C_GATE)
    for j in range(4):
        gates_ref[:, j * 512:(j + 1) * 512] = mm(C_GATE + j * 512, C_GATE + (j + 1) * 512)
    ab_ref[...] = mm(C_AB, C_END)
    abt_ref[...] = lax.dot_general(wabt_ref[...], h, _NT, preferred_element_type=F32)


def _inproj(x2d, g_mix, w_main, w_vt, w_abt):
    T = x2d.shape[0]
    n = T // TM_PROJ
    nkb = T // TK
    nb = TM_PROJ // TK
    full = lambda shape: pl.BlockSpec(shape, lambda i: (0,) * len(shape))
    out_shape = (
        jax.ShapeDtypeStruct((A_HEADS, T, 128), BF16),
        jax.ShapeDtypeStruct((A_HEADS, nkb, TK, 128), BF16),
        jax.ShapeDtypeStruct((A_HEADS, nkb, 128, TK), BF16),
        jax.ShapeDtypeStruct((T, 1536), F32),
        jax.ShapeDtypeStruct((T, 512), F32),
        jax.ShapeDtypeStruct((T, 2048), F32),
        jax.ShapeDtypeStruct((T, 128), F32),
        jax.ShapeDtypeStruct((16, T), F32),
    )
    out_specs = (
        pl.BlockSpec((A_HEADS, TM_PROJ, 128), lambda i: (0, i, 0)),
        pl.BlockSpec((A_HEADS, nb, TK, 128), lambda i: (0, i, 0, 0)),
        pl.BlockSpec((A_HEADS, nb, 128, TK), lambda i: (0, i, 0, 0)),
        pl.BlockSpec((TM_PROJ, 1536), lambda i: (i, 0)),
        pl.BlockSpec((TM_PROJ, 512), lambda i: (i, 0)),
        pl.BlockSpec((TM_PROJ, 2048), lambda i: (i, 0)),
        pl.BlockSpec((TM_PROJ, 128), lambda i: (i, 0)),
        pl.BlockSpec((16, TM_PROJ), lambda i: (0, i)),
    )
    return pl.pallas_call(
        _inproj_kernel,
        grid=(n,),
        in_specs=[
            pl.BlockSpec((TM_PROJ, D_MODEL), lambda i: (i, 0)),
            full((1, D_MODEL)),
            full(w_main.shape),
            full(w_vt.shape),
            full(w_abt.shape),
        ],
        out_specs=out_specs,
        out_shape=out_shape,
        compiler_params=pltpu.CompilerParams(
            dimension_semantics=("arbitrary",), vmem_limit_bytes=_vmem_limit(56 << 20)),
        name="inproj",
    )(x2d, g_mix, w_main, w_vt, w_abt)


def _attn_kernel(lam_ref, q_ref, k_ref, vt_ref, bd_ref, bp_ref, gs_ref, o_ref, m_sc, l_sc, acc_sc, s_buf, pm_buf):
    qi = pl.program_id(1)
    q = q_ref[0]
    lane = lax.broadcasted_iota(I32, q.shape, 1)
    zero = jnp.zeros_like(q)
    qm = (jnp.where(lane < A_QK_DIM, q, zero), jnp.where(lane >= A_QK_DIM, q, zero))
    m_sc[...] = jnp.full(m_sc.shape, NEG_BIG, F32)
    l_sc[...] = jnp.zeros(l_sc.shape, F32)
    acc_sc[...] = jnp.zeros(acc_sc.shape, F32)

    nkb = k_ref.shape[1]

    def stage_a(slot, kb0):
        for j in range(2):
            kblk = k_ref[0, jnp.minimum(kb0 + j, nkb - 1)]
            for m in range(2):
                s = lax.dot_general(kblk, qm[m], _NT, preferred_element_type=F32)
                s_buf[slot, m, j] = s
                pm_buf[slot, m, j] = jnp.max(s, axis=0, keepdims=True)

    def stage_b(slot, kb0, biases):
        far = all(b is None for b in biases)
        for m in range(2):
            m_old = m_sc[m]
            m_new = m_old
            ss = []
            for j, bias in enumerate(biases):
                if far:
                    m_new = jnp.maximum(m_new, pm_buf[slot, m, j])
                else:
                    s = s_buf[slot, m, j]
                    s = s if bias is None else s + bias[0]
                    ss.append(s)
                    m_new = jnp.maximum(m_new, jnp.max(s, axis=0, keepdims=True))
            alpha = jnp.exp2(m_old - m_new)
            l_new = alpha * l_sc[m]
            acc = alpha * acc_sc[m]
            for j in range(len(biases)):
                s = s_buf[slot, m, j] if far else ss[j]
                p = jnp.exp2(s - m_new)
                l_new = l_new + jnp.sum(p, axis=0, keepdims=True)
                acc = acc + jnp.dot(vt_ref[0, kb0 + j], p.astype(BF16), preferred_element_type=F32)
            l_sc[m] = l_new
            acc_sc[m] = acc
            m_sc[m] = m_new

    far2 = [None, None]
    nfu = jnp.maximum(qi - 1, 0) // 2
    peel = nfu % 2

    @pl.when(peel == 1)
    def _():
        stage_a(0, 0)
        stage_b(0, 0, far2)

    stage_a(0, 2 * peel)

    def far_body(v, c):
        u = peel + 2 * v
        stage_a(1, 2 * u + 2)
        stage_b(0, 2 * u, far2)
        stage_a(0, 2 * u + 4)
        stage_b(1, 2 * u + 2, far2)
        return c

    lax.fori_loop(0, (nfu - peel) // 2, far_body, 0)
    kb0 = 2 * nfu

    @pl.when(qi % 2 == 1)
    def _():
        stage_b(0, kb0, [bp_ref, bd_ref])

    @pl.when((qi % 2 == 0) & (qi >= 2))
    def _():
        stage_a(1, kb0 + 2)
        stage_b(0, kb0, [None, bp_ref])
        stage_b(1, kb0 + 2, [bd_ref])

    @pl.when(qi == 0)
    def _():
        stage_b(0, kb0, [bd_ref])

    o = acc_sc[0] / l_sc[0] - lam_ref[0] * (acc_sc[1] / l_sc[1])
    ot = o.T
    o_ref[...] = (_rms(ot, gs_ref[...]) * (1.0 - LAMBDA_INIT)).astype(BF16)


def _attention(lam, q, k, vt, bias_d, bias_p, g_subln):
    T = q.shape[1]
    nq = T // TQ
    nkb = T // TK
    return pl.pallas_call(
        _attn_kernel,
        grid=(A_HEADS, nq),
        in_specs=[
            pl.BlockSpec(memory_space=pltpu.SMEM),
            pl.BlockSpec((1, TQ, 128), lambda h, i: (h, i, 0)),
            pl.BlockSpec((1, nkb, TK, 128), lambda h, i: (h, 0, 0, 0)),
            pl.BlockSpec((1, nkb, 128, TK), lambda h, i: (h, 0, 0, 0)),
            pl.BlockSpec((1, TK, TQ), lambda h, i: (h, 0, 0)),
            pl.BlockSpec((1, TK, TQ), lambda h, i: (h, 0, 0)),
            pl.BlockSpec((1, 128), lambda h, i: (0, 0)),
        ],
        out_specs=pl.BlockSpec((TQ, 128), lambda h, i: (i, h)),
        out_shape=jax.ShapeDtypeStruct((T, A_HEADS * A_V_DIM), BF16),
        scratch_shapes=[
            pltpu.VMEM((2, 1, TQ), F32),
            pltpu.VMEM((2, 1, TQ), F32),
            pltpu.VMEM((2, 128, TQ), F32),
            pltpu.VMEM((2, 2, 2, TK, TQ), F32),
            pltpu.VMEM((2, 2, 2, 1, TQ), F32),
        ],
        compiler_params=pltpu.CompilerParams(
            dimension_semantics=("arbitrary", "arbitrary"), vmem_limit_bytes=_vmem_limit(48 << 20)),
        name="attn",
    )(lam, q, k, vt, bias_d, bias_p, g_subln)


def _t5_bucket(rel):
    half = REL_BUCKETS // 2
    max_exact = half // 2
    ret = jnp.where(rel > 0, half, 0)
    n = jnp.abs(rel)
    large = max_exact + (jnp.log(jnp.maximum(n, 1).astype(F32) / max_exact)
                         / math.log(REL_MAX_DIST / max_exact) * (half - max_exact)).astype(I32)
    large = jnp.minimum(large, half - 1)
    return ret + jnp.where(n < max_exact, n, large)


def _attn_bias_tables(rel_bias):
    assert TK >= REL_MAX_DIST and TK == TQ and TK % CHUNK == 0
    kk = jnp.arange(TK, dtype=I32)[:, None]
    qq = jnp.arange(TQ, dtype=I32)[None, :]
    rb = rel_bias.astype(F32)
    far = rb[_t5_bucket(jnp.full((1,), -REL_MAX_DIST, I32))[0]]
    table = ((rb - far[None, :]) * LOG2E).T

    def lookup(rel):
        onehot = (_t5_bucket(rel).reshape(1, -1) == jnp.arange(REL_BUCKETS, dtype=I32)[:, None]).astype(F32)
        return jnp.dot(table, onehot, precision=lax.Precision.HIGHEST).reshape(A_HEADS, TK, TQ)

    bd = jnp.where((kk // CHUNK <= qq // CHUNK)[None], lookup(kk - qq), NEG_BIG)
    bp = lookup(kk - TK - qq)
    return bd, bp


def _gdn_kernel(x_ref, zb_ref, ab_ref, abt_ref, cw_ref, alc_ref, dtc_ref, alr_ref, dtr_ref, gob_ref,
                o_ref, xbuf, s_sc):
    i = pl.program_id(0)
    TC = TC_GDN
    NCH = TC // CHUNK
    HB = B_HEADS * CHUNK

    @pl.when(i == 0)
    def _():
        xbuf[0:8, :] = jnp.zeros((8, 1536), F32)
        s_sc[...] = jnp.zeros(s_sc.shape, F32)

    xbuf[8:8 + TC, :] = x_ref[...]
    y = cw_ref[3:4, :] * xbuf[8:8 + TC, :]
    for d in range(1, CONV_WIDTH):
        y = y + cw_ref[3 - d:4 - d, :] * xbuf[8 - d:8 - d + TC, :]
    xbuf[0:8, :] = x_ref[TC - 8:TC, :]
    a = y * _sigmoid(y)

    def l2n(t):
        return t * lax.rsqrt(jnp.sum(t * t, axis=-1, keepdims=True) + EPS)

    qh = [l2n(a[:, h * 128:(h + 1) * 128]) * (B_DIM ** -0.5) for h in range(B_HEADS)]
    kh = [l2n(a[:, 512 + h * 128:512 + (h + 1) * 128]) for h in range(B_HEADS)]
    vh = [a[:, 1024 + h * 128:1024 + (h + 1) * 128] for h in range(B_HEADS)]

    def softplus(t):
        return jnp.maximum(t, 0.0) + jnp.log(1.0 + jnp.exp(-jnp.abs(t)))

    ab = ab_ref[...]
    g_col = -jnp.exp(alc_ref[...]) * softplus(ab + dtc_ref[...])
    beta_col = _sigmoid(ab)
    g_row = -jnp.exp(alr_ref[...]) * softplus(abt_ref[...] + dtr_ref[...])

    rt = lax.broadcasted_iota(I32, (TC, TC), 0)
    ct = lax.broadcasted_iota(I32, (TC, TC), 1)
    same_chunk = (rt // CHUNK) == (ct // CHUNK)
    tril = jnp.where(same_chunk & (rt >= ct), 1.0, 0.0).astype(F32)
    triu = jnp.where(same_chunk & (rt <= ct), 1.0, 0.0).astype(F32)
    gc_col = jnp.dot(tril, g_col, precision=lax.Precision.HIGHEST, preferred_element_type=F32)
    gc_row = jnp.dot(g_row, triu, precision=lax.Precision.HIGHEST, preferred_element_type=F32)

    ri = lax.broadcasted_iota(I32, (HB, HB), 0)
    ci = lax.broadcasted_iota(I32, (HB, HB), 1)
    same_head = (ri // CHUNK) == (ci // CHUNK)
    incl = same_head & (ri >= ci)
    strict = same_head & (ri > ci)
    eye = jnp.where(ri == ci, 1.0, 0.0).astype(F32)

    def level_mask(s):
        return ((ri // (2 * s)) == (ci // (2 * s))) & (((ri // s) % 2) == 1) & (((ci // s) % 2) == 0)

    cat0 = lambda parts: jnp.concatenate(parts, axis=0)
    heads = range(B_HEADS)
    chunks = range(NCH)

    Kc, Qc, glc, gccc, Lc, QKc, invc, rhsc = [], [], [], [], [], [], [], []
    for c in chunks:
        rs = slice(c * CHUNK, (c + 1) * CHUNK)
        last = slice(c * CHUNK + CHUNK - 1, (c + 1) * CHUNK)
        K = cat0([kh[h][rs] for h in heads])
        Q = cat0([qh[h][rs] for h in heads])
        V = cat0([vh[h][rs] for h in heads])
        beta = cat0([beta_col[rs, 4 + h:5 + h] for h in heads])
        gcc = cat0([gc_col[rs, h:h + 1] for h in heads])
        gl = cat0([jnp.broadcast_to(gc_col[last, h:h + 1], (CHUNK, 1)) for h in heads])
        gcr = jnp.concatenate([gc_row[h:h + 1, rs] for h in heads], axis=1)
        Kb = K * beta
        e = jnp.exp(jnp.where(incl, gcc - gcr, 0.0))
        L = lax.dot_general(Kb, K, _NT, preferred_element_type=F32) * jnp.where(strict, e, 0.0)
        QKc.append(lax.dot_general(Q, K, _NT, preferred_element_type=F32) * jnp.where(incl, e, 0.0))
        rhsc.append(jnp.concatenate([V * beta, Kb * jnp.exp(gcc)], axis=1))
        invc.append(eye - jnp.where(level_mask(1), L, 0.0))
        Kc.append(K); Qc.append(Q); glc.append(gl); gccc.append(gcc); Lc.append(L)

    for s in (2, 4, 8, 16, 32):
        msk = level_mask(s)
        ts = [jnp.dot(invc[c], jnp.where(msk, Lc[c], 0.0), preferred_element_type=F32) for c in chunks]
        invc = [invc[c] - jnp.dot(ts[c], invc[c], preferred_element_type=F32) for c in chunks]
    solc = [jnp.dot(invc[c], rhsc[c], preferred_element_type=F32) for c in chunks]

    for c in chunks:
        rs = slice(c * CHUNK, (c + 1) * CHUNK)
        U = solc[c][:, :B_DIM]
        W = solc[c][:, B_DIM:]
        Qd = Qc[c] * jnp.exp(gccc[c])
        Kd = Kc[c] * jnp.exp(glc[c] - gccc[c])
        vnew = []
        ost = []
        for h in heads:
            hs = slice(h * CHUNK, (h + 1) * CHUNK)
            S = s_sc[h]
            vn = U[hs] - jnp.dot(W[hs], S, preferred_element_type=F32)
            ost.append(jnp.dot(Qd[hs], S, preferred_element_type=F32))
            s_sc[h] = S * jnp.exp(glc[c][h * CHUNK:h * CHUNK + 1, :]) + lax.dot_general(
                Kd[hs], vn, _TN, preferred_element_type=F32)
            vnew.append(vn)
        O = cat0(ost) + jnp.dot(QKc[c], cat0(vnew), preferred_element_type=F32)
        for h in heads:
            oh = _rms(O[h * CHUNK:(h + 1) * CHUNK], gob_ref[...])
            z = zb_ref[rs, h * 128:(h + 1) * 128]
            o_ref[rs, h * 128:(h + 1) * 128] = (oh * (z * _sigmoid(z))).astype(BF16)


def _gdn(qkvb, zb, ab, abt, conv_w8, alc, dtc, alr, dtr, gob):
    T = qkvb.shape[0]
    TC = TC_GDN
    full = lambda shape: pl.BlockSpec(shape, lambda i: (0,) * len(shape))
    return pl.pallas_call(
        _gdn_kernel,
        grid=(T // TC,),
        in_specs=[
            pl.BlockSpec((TC, 1536), lambda i: (i, 0)),
            pl.BlockSpec((TC, 512), lambda i: (i, 0)),
            pl.BlockSpec((TC, 128), lambda i: (i, 0)),
            pl.BlockSpec((16, TC), lambda i: (0, i)),
            full((8, 1536)), full((1, 128)), full((1, 128)), full((16, TC)), full((16, TC)), full((1, 128)),
        ],
        out_specs=pl.BlockSpec((TC, 512), lambda i: (i, 0)),
        out_shape=jax.ShapeDtypeStruct((T, 512), BF16),
        scratch_shapes=[pltpu.VMEM((TC + 8, 1536), F32), pltpu.VMEM((B_HEADS, B_DIM, B_DIM), F32)],
        compiler_params=pltpu.CompilerParams(
            dimension_semantics=("arbitrary",), vmem_limit_bytes=_vmem_limit(48 << 20)),
        name="gdn",
    )(qkvb, zb, ab, abt, conv_w8, alc, dtc, alr, dtr, gob)


def _memkv_kernel(m_ref, g_ref, w_ref, k_ref, v_ref):
    hm = _rms(m_ref[...], g_ref[...]).astype(BF16)
    kv = jnp.dot(hm, w_ref[...], preferred_element_type=F32)
    k_ref[...] = kv[:, :D_MODEL].astype(BF16)
    v_ref[...] = kv[:, D_MODEL:].astype(BF16)


def _memkv(mem2d, g_mem, w_kv):
    n = mem2d.shape[0]
    return pl.pallas_call(
        _memkv_kernel,
        out_shape=(jax.ShapeDtypeStruct((n, D_MODEL), BF16), jax.ShapeDtypeStruct((n, D_MODEL), BF16)),
        compiler_params=pltpu.CompilerParams(vmem_limit_bytes=_vmem_limit(32 << 20)),
        name="memkv",
    )(mem2d, g_mem, w_kv)


def _mix_kernel(x_ref, oa_ref, ob_ref, gates_ref, wua_ref, wub_ref, wout_ref, gx_ref, wq_ref, kx_ref, vx_ref,
                wo_ref, gm_ref, wr_ref, br_ref, x2t_ref, hmt_ref, idx_ref, wts_ref):
    NSUB = 1
    TS = TM_MIX // NSUB
    for sub in range(NSUB):
        rows = slice(sub * TS, (sub + 1) * TS)
        ma = jnp.dot(oa_ref[rows, :], wua_ref[...], preferred_element_type=F32)
        mb = jnp.dot(ob_ref[rows, :], wub_ref[...], preferred_element_type=F32)
        merged = _sigmoid(gates_ref[rows, :D_MODEL]) * ma + _sigmoid(gates_ref[rows, D_MODEL:]) * mb
        x1 = x_ref[rows, :] + jnp.dot(merged.astype(BF16), wout_ref[...], preferred_element_type=F32)

        hx = _rms(x1, gx_ref[...]).astype(BF16)
        heads = []
        for h in range(X_HEADS):
            cs = slice(h * X_HEAD_DIM, (h + 1) * X_HEAD_DIM)
            qh = jnp.dot(hx, wq_ref[:, cs], preferred_element_type=F32).astype(BF16)
            s = lax.dot_general(qh, kx_ref[:, cs], _NT, preferred_element_type=F32) * (X_HEAD_DIM ** -0.5)
            s = s - jnp.max(s, axis=-1, keepdims=True)
            p = jnp.exp(s)
            p = p / jnp.sum(p, axis=-1, keepdims=True)
            heads.append(jnp.dot(p.astype(BF16), vx_ref[:, cs], preferred_element_type=F32).astype(BF16))
        o = jnp.concatenate(heads, axis=1)
        x2 = x1 + jnp.dot(o, wo_ref[...], preferred_element_type=F32)
        hm = _rms(x2, gm_ref[...])

        for g in range(D_MODEL // 128):
            x2t_ref[pl.ds(sub * TS * 8 + g, TS, stride=8), :] = x2[:, g * 128:(g + 1) * 128]
            hmt_ref[pl.ds(sub * TS * 8 + g, TS, stride=8), :] = hm[:, g * 128:(g + 1) * 128]

        logits = lax.dot_general(wr_ref[...], hm, _NT, precision=lax.Precision.HIGHEST,
                                 preferred_element_type=F32) + br_ref[:, 0:1]
        eidx = lax.broadcasted_iota(I32, logits.shape, 0)
        vals, idxs = [], []
        cur = logits
        for _ in range(TOP_K):
            mx = jnp.max(cur, axis=0, keepdims=True)
            ix = jnp.min(jnp.where(cur == mx, eidx, N_EXPERTS), axis=0, keepdims=True)
            vals.append(mx)
            idxs.append(ix)
            cur = jnp.where(eidx == ix, -jnp.inf, cur)
        ex = [jnp.exp(v - vals[0]) for v in vals]
        den = ex[0] + ex[1] + ex[2] + ex[3]
        idx_ref[:, rows] = jnp.concatenate(idxs, axis=0)
        wts_ref[:, rows] = jnp.concatenate([e / den for e in ex], axis=0)


def _mix(x2d, oa, ob, gates, wua, wub, wout, gx, wq, kx, vx, wo, gm, wr_t, br):
    T = x2d.shape[0]
    TM = TM_MIX
    full = lambda a: pl.BlockSpec(a.shape, lambda i: (0,) * a.ndim)
    out_shape = (
        jax.ShapeDtypeStruct((T * 8, 128), F32),
        jax.ShapeDtypeStruct((T * 8, 128), F32),
        jax.ShapeDtypeStruct((TOP_K, T), I32),
        jax.ShapeDtypeStruct((TOP_K, T), F32),
    )
    return pl.pallas_call(
        _mix_kernel,
        grid=(T // TM,),
        in_specs=[
            pl.BlockSpec((TM, D_MODEL), lambda i: (i, 0)),
            pl.BlockSpec((TM, 512), lambda i: (i, 0)),
            pl.BlockSpec((TM, 512), lambda i: (i, 0)),
            pl.BlockSpec((TM, 2048), lambda i: (i, 0)),
            full(wua), full(wub), full(wout), full(gx), full(wq), full(kx), full(vx), full(wo), full(gm),
            full(wr_t), full(br),
        ],
        out_specs=(
            pl.BlockSpec((TM * 8, 128), lambda i: (i, 0)),
            pl.BlockSpec((TM * 8, 128), lambda i: (i, 0)),
            pl.BlockSpec((TOP_K, TM), lambda i: (0, i)),
            pl.BlockSpec((TOP_K, TM), lambda i: (0, i)),
        ),
        out_shape=out_shape,
        compiler_params=pltpu.CompilerParams(
            dimension_semantics=("arbitrary",), vmem_limit_bytes=_vmem_limit(56 << 20)),
        name="mix",
    )(x2d, oa, ob, gates, wua, wub, wout, gx, wq, kx, vx, wo, gm, wr_t, br)


def _moe_kernel(blk_e_ref, first_ref, nxt_ref, slot_ref, nb_ref,
                tok_ref, wt_ref, x2t_hbm, hmt_ref, w1_hbm, w2_hbm, b1_ref, b2_ref, gf_ref,
                out_ref, yacc, xs, yst, w1buf, w2buf, wsem, xsem):
    j = pl.program_id(0)
    TT = TT_MOE
    RB = RB_MOE
    NG = D_MODEL // 128
    nbj = nb_ref[j]

    x2_copy = pltpu.make_async_copy(
        x2t_hbm.at[pl.ds(pl.multiple_of(j * TT * 8, 8), TT * 8), :], yacc.at[pl.ds(0, TT * 8), :], xsem.at[0])
    x2_copy.start()

    def w_copies(e, slot):
        cps = []
        for c in range(MOE_W_CHUNKS):
            for k, (src, dst) in enumerate(((w1_hbm, w1buf), (w2_hbm, w2buf))):
                rows = pl.ds(c * (src.shape[1] // MOE_W_CHUNKS), src.shape[1] // MOE_W_CHUNKS)
                cps.append(pltpu.make_async_copy(src.at[e, rows], dst.at[slot, rows], wsem.at[slot, k, c]))
        return cps

    for cp in w_copies(blk_e_ref[j, 0], slot_ref[j, 0]):
        cp.start()

    yacc[TT * 8:TT * 8 + 8, :] = jnp.zeros((8, 128), F32)

    def gather(b, xs):
        def group(gi, c):
            for u in range(8):
                r = gi * 8 + u
                t = jnp.minimum(tok_ref[0, 0, b * RB + r], TT - 1)
                xs[pl.ds(pl.multiple_of(r * 8, 8), 8), :] = hmt_ref[pl.ds(pl.multiple_of(t * 8, 8), 8), :]
            return c

        lax.fori_loop(0, RB // 8, group, 0)

    def mlp(xs, yst, e, slot):
        xb = jnp.concatenate([xs[pl.ds(g, RB, stride=8), :] for g in range(NG)], axis=1).astype(BF16)
        hid = jnp.dot(xb, w1buf[slot], preferred_element_type=F32) + b1_ref[e]
        glu = jnp.minimum(hid[:, :D_FF], SWIGLU_LIMIT)
        lin = jnp.clip(hid[:, D_FF:], -SWIGLU_LIMIT, SWIGLU_LIMIT)
        act = glu * _sigmoid(SWIGLU_ALPHA * glu) * (lin + 1.0)
        ys = jnp.dot(act.astype(BF16), w2buf[slot], preferred_element_type=F32) + b2_ref[e]
        for g in range(NG):
            yst[pl.ds(g, RB, stride=8), :] = ys[:, g * 128:(g + 1) * 128]

    def scatter(b, yst):
        def group(gi, c):
            dsts, vals = [], []
            for u in range(8):
                r = gi * 8 + u
                dst = pl.ds(pl.multiple_of(tok_ref[0, 0, b * RB + r] * 8, 8), 8)
                dsts.append(dst)
                vals.append(yacc[dst, :] + wt_ref[0, 0, b * RB + r] * yst[pl.ds(pl.multiple_of(r * 8, 8), 8), :])
            for dst, val in zip(dsts, vals):
                yacc[dst, :] = val
            return c

        lax.fori_loop(0, RB // 8, group, 0)

    x2_copy.wait()

    def block_body(b, c):
        e = blk_e_ref[j, b]
        slot = slot_ref[j, b]

        @pl.when(first_ref[j, b] == 1)
        def _():
            for cp in w_copies(e, slot):
                cp.wait()
            nx = nxt_ref[j, b]

            @pl.when(nx >= 0)
            def _():
                for cp in w_copies(nx, 1 - slot):
                    cp.start()

        gather(b, xs)
        mlp(xs, yst, e, slot)
        scatter(b, yst)
        return c

    lax.fori_loop(0, nbj, block_body, 0)

    RC = 256

    def fin(ci, c):
        base = pl.multiple_of(ci * RC * 8, 8)
        yv = jnp.concatenate([yacc[pl.ds(base + g, RC, stride=8), :] for g in range(NG)], axis=1)
        out_ref[pl.ds(pl.multiple_of(ci * RC, 8), RC), :] = _rms(yv, gf_ref[...])
        return c

    lax.fori_loop(0, TT // RC, fin, 0)


def _moe_nb_max():
    return TOP_K * TT_MOE // RB_MOE + N_EXPERTS


def _moe(lists, x2t, hmt, w1, b1, w2, b2, g_final):
    blk_e, first, nxt, slot, nb, tok, wt = lists
    T = x2t.shape[0] // 8
    TT = TT_MOE
    nt = T // TT
    LP = tok.shape[-1]
    smem_row = pl.BlockSpec((1, 1, LP), lambda j, *_: (j, 0, 0), memory_space=pltpu.SMEM)
    whole = lambda a: pl.BlockSpec(a.shape, lambda j, *_: (0,) * a.ndim)
    grid_spec = pltpu.PrefetchScalarGridSpec(
        num_scalar_prefetch=5,
        grid=(nt,),
        in_specs=[
            smem_row, smem_row,
            pl.BlockSpec(memory_space=pl.ANY),
            pl.BlockSpec((TT * 8, 128), lambda j, *_: (j, 0)),
            pl.BlockSpec(memory_space=pl.ANY), pl.BlockSpec(memory_space=pl.ANY),
            whole(b1), whole(b2), whole(g_final),
        ],
        out_specs=pl.BlockSpec((TT, D_MODEL), lambda j, *_: (j, 0), pipeline_mode=pl.Buffered(1)),
        scratch_shapes=[
            pltpu.VMEM((TT * 8 + 8, 128), F32),
            pltpu.VMEM((RB_MOE * 8, 128), F32), pltpu.VMEM((RB_MOE * 8, 128), F32),
            pltpu.VMEM((2, D_MODEL, 2 * D_FF), BF16),
            pltpu.VMEM((2, D_FF, D_MODEL), BF16),
            pltpu.SemaphoreType.DMA((2, 2, MOE_W_CHUNKS)),
            pltpu.SemaphoreType.DMA((1,)),
        ],
    )
    return pl.pallas_call(
        _moe_kernel,
        grid_spec=grid_spec,
        out_shape=jax.ShapeDtypeStruct((T, D_MODEL), F32),
        compiler_params=pltpu.CompilerParams(
            dimension_semantics=("arbitrary",), vmem_limit_bytes=_vmem_limit(60 << 20)),
        name="moe",
    )(blk_e, first, nxt, slot, nb, tok, wt, x2t, hmt, w1, w2, b1, b2, g_final)


def _moe_lists(idx, wts):
    T = idx.shape[1]
    TT, RB = TT_MOE, RB_MOE
    nt = T // TT
    NB = _moe_nb_max()
    A = TOP_K * TT
    e_tile = idx.reshape(TOP_K, nt, TT).transpose(1, 0, 2).reshape(nt, A)
    w_tile = wts.reshape(TOP_K, nt, TT).transpose(1, 0, 2).reshape(nt, A)
    tok_local = jnp.tile(jnp.arange(TT, dtype=I32), TOP_K)
    order = jnp.argsort(e_tile, axis=1, stable=True)
    stok = jnp.take_along_axis(jnp.broadcast_to(tok_local, e_tile.shape), order, axis=1)
    swt = jnp.take_along_axis(w_tile, order, axis=1)

    experts = jnp.arange(N_EXPERTS, dtype=I32)
    counts = jnp.sum((e_tile[:, :, None] == experts[None, None, :]).astype(I32), axis=1)
    start = jnp.cumsum(counts, axis=1) - counts
    nblk = (counts + RB - 1) // RB
    blk_end = jnp.cumsum(nblk, axis=1)
    blk_start = blk_end - nblk
    nb = blk_end[:, -1]

    b = jnp.arange(NB, dtype=I32)
    blk_e = jnp.minimum(jnp.sum((blk_end[:, :, None] <= b[None, None, :]).astype(I32), axis=1), N_EXPERTS - 1)
    onehot = (blk_e[:, :, None] == experts[None, None, :]).astype(I32)
    per_block = lambda v: jnp.sum(onehot * v[:, None, :], axis=2)
    present = (nblk > 0).astype(I32)
    ordinal = jnp.cumsum(present, axis=1) - present
    later = (experts[None, :] > experts[:, None])[None] & (present[:, None, :] > 0)
    nxt_e = jnp.min(jnp.where(later, experts[None, None, :], N_EXPERTS), axis=2)
    nxt_e = jnp.where(nxt_e >= N_EXPERTS, -1, nxt_e)
    bs_b = per_block(blk_start)
    first = ((b[None, :] == bs_b) & (b[None, :] < nb[:, None])).astype(I32)
    slot = per_block(ordinal) % 2
    nxt = per_block(nxt_e)

    r = jnp.arange(RB, dtype=I32)[None, None, :]
    off = (b[None, :] - bs_b)[:, :, None] * RB + r
    valid = (off < per_block(counts)[:, :, None]) & (b[None, :, None] < nb[:, None, None])
    src = jnp.clip(per_block(start)[:, :, None] + off, 0, A - 1).reshape(nt, NB * RB)
    valid = valid.reshape(nt, NB * RB)
    tok = jnp.where(valid, jnp.take_along_axis(stok, src, axis=1), TT)
    wt = jnp.where(valid, jnp.take_along_axis(swt, src, axis=1), 0.0)
    tok = jnp.pad(tok, ((0, 0), (0, RB)), constant_values=TT)
    wt = jnp.pad(wt, ((0, 0), (0, RB)))
    return blk_e, first, nxt, slot, nb, tok[:, None, :], wt[:, None, :]


def kernel(x, mem, g_mix, w_in, rel_bias, lambda_q1, lambda_k1, lambda_q2, lambda_k2, g_subln, conv_w, a_log,
           dt_bias, g_out_b, w_up_a, w_up_b, w_out, g_xattn, g_mem, w_q_x, w_kv_x, w_o_x, g_moe, w_router,
           b_router, w_mlp1, b_mlp1, w_mlp2, b_mlp2, g_final):
    B_, S_, _ = x.shape
    assert B_ == 1 and S_ % TT_MOE == 0 and x.dtype == F32
    l = 0
    x2d = x.reshape(S_, D_MODEL)
    row = lambda v: v.reshape(1, -1).astype(F32)

    wi = w_in[l]
    ab_cols = jnp.pad(wi[:, 3584:3592], ((0, 0), (0, 120)))
    w_main = jnp.concatenate(
        [wi[:, 0:1024], wi[:, 1536:3072], wi[:, 3072:3584], wi[:, 3592:5640], ab_cols], axis=1).astype(BF16)
    w_vt = wi[:, 1024:1536].T.astype(BF16)
    w_abt = jnp.pad(wi[:, 3584:3592].T, ((0, 8), (0, 0))).astype(BF16)

    q, k, vt, qkvb, zb, gates, ab, abt = _inproj(x2d, row(g_mix[l]), w_main, w_vt, w_abt)

    lam = (jnp.exp(jnp.sum(lambda_q1[l].astype(F32) * lambda_k1[l].astype(F32)))
           - jnp.exp(jnp.sum(lambda_q2[l].astype(F32) * lambda_k2[l].astype(F32))) + LAMBDA_INIT).reshape(1)
    bias_d, bias_p = _attn_bias_tables(rel_bias)
    oa = _attention(lam, q, k, vt, bias_d, bias_p, row(g_subln[l]))

    lane_pad = lambda v: jnp.pad(v.astype(F32), (0, 128 - v.shape[0])).reshape(1, 128)
    row_bcast = lambda v: jnp.broadcast_to(jnp.pad(v.astype(F32), (0, 16 - v.shape[0]))[:, None], (16, TC_GDN))
    ob = _gdn(qkvb, zb, ab, abt, jnp.pad(conv_w[l].astype(F32), ((0, 4), (0, 0))),
              lane_pad(a_log[l]), lane_pad(dt_bias[l]), row_bcast(a_log[l]), row_bcast(dt_bias[l]),
              row(g_out_b[l]))

    kx, vx = _memkv(mem.reshape(-1, D_MODEL), row(g_mem[l]), w_kv_x[l].astype(BF16))
    br = jnp.broadcast_to(b_router[l].astype(F32)[:, None], (N_EXPERTS, 128))
    x2t, hmt, idx, wts = _mix(
        x2d, oa, ob, gates, w_up_a[l].astype(BF16), w_up_b[l].astype(BF16), w_out[l].astype(BF16),
        row(g_xattn[l]), w_q_x[l].astype(BF16), kx, vx, w_o_x[l].astype(BF16), row(g_moe[l]),
        w_router[l].T.astype(F32), br)

    out = _moe(_moe_lists(idx, wts), x2t, hmt, w_mlp1[l].astype(BF16), b_mlp1[l].astype(F32)[:, None, :],
               w_mlp2[l].astype(BF16), b_mlp2[l].astype(F32)[:, None, :], row(g_final))
    return out.reshape(B_, S_, D_MODEL)
```

```python
import functools
import math

import jax
import jax.numpy as jnp
from jax import lax
from jax.experimental import pallas as pl
from jax.experimental.pallas import tpu as pltpu

F32 = jnp.float32
BF16 = jnp.bfloat16
I32 = jnp.int32

D_MODEL = 1024
CHUNK = 64
EPS = 1e-6
A_HEADS = 4
A_QK_DIM = 64
A_V_DIM = 128
REL_BUCKETS = 32
REL_MAX_DIST = 128
B_HEADS = 4
B_DIM = 128
CONV_WIDTH = 4
X_HEADS = 4
X_HEAD_DIM = 256
N_EXPERTS = 32
TOP_K = 4
D_FF = 1024
SWIGLU_LIMIT = 7.0
SWIGLU_ALPHA = 1.702
LAMBDA_INIT = 0.8 - 0.6 * math.exp(-0.3 * 0)

LOG2E = 1.4426950408889634
NEG_BIG = -1e30

V7X_LANES = 128
V7X_SUBLANES = 8
V7X_VMEM_BYTES = 64 * 1024 * 1024

TM_PROJ = 512
TQ = 512
TK = 512
TC_GDN = 256
TM_MIX = 512
TT_MOE = 2048
RB_MOE = 128
MOE_W_CHUNKS = 4

C_QA, C_KA, C_QKVB, C_ZB, C_GATE, C_AB, C_END = 0, 512, 1024, 2560, 3072, 5120, 5248

_NT = (((1,), (1,)), ((), ()))
_TN = (((0,), (0,)), ((), ()))


def _rms(x, g):
    return x * lax.rsqrt(jnp.mean(x * x, axis=-1, keepdims=True) + EPS) * g


def _sigmoid(x):
    return 1.0 / (1.0 + jnp.exp(-x))


def _vmem_limit(nbytes):
    return int(min(nbytes, V7X_VMEM_BYTES - 4 * 1024 * 1024))


def _inproj_kernel(x_ref, g_ref, w_ref, wvt_ref, wabt_ref,
                   q_ref, k_ref, vt_ref, qkvb_ref, zb_ref, gates_ref, ab_ref, abt_ref):
    h = _rms(x_ref[...], g_ref[...]).astype(BF16)

    def mm(c0, c1):
        return jnp.dot(h, w_ref[:, c0:c1], preferred_element_type=F32)

    nb = TM_PROJ // TK
    qa = mm(C_QA, C_KA) * (A_QK_DIM ** -0.5 * LOG2E)
    ka = mm(C_KA, C_QKVB)
    vt = lax.dot_general(wvt_ref[...], h, _NT, preferred_element_type=F32)
    for hh in range(A_HEADS):
        cs = slice(hh * 128, (hh + 1) * 128)
        q_ref[hh] = qa[:, cs].astype(BF16)
        for b in range(nb):
            rs = slice(b * TK, (b + 1) * TK)
            k_ref[hh, b] = ka[rs, cs].astype(BF16)
            vt_ref[hh, b] = vt[cs, rs].astype(BF16)
    for j in range(3):
        qkvb_ref[:, j * 512:(j + 1) * 512] = mm(C_QKVB + j * 512, C_QKVB + (j + 1) * 512)
    zb_ref[...] = mm(C_ZB, C_GATE)
    for j in range(4):
        gates_ref[:, j * 512:(j + 1) * 512] = mm(C_GATE + j * 512, C_GATE + (j + 1) * 512)
    ab_ref[...] = mm(C_AB, C_END)
    abt_ref[...] = lax.dot_general(wabt_ref[...], h, _NT, preferred_element_type=F32)


def _inproj(x2d, g_mix, w_main, w_vt, w_abt):
    T = x2d.shape[0]
    n = T // TM_PROJ
    nkb = T // TK
    nb = TM_PROJ // TK
    full = lambda shape: pl.BlockSpec(shape, lambda i: (0,) * len(shape))
    out_shape = (
        jax.ShapeDtypeStruct((A_HEADS, T, 128), BF16),
        jax.ShapeDtypeStruct((A_HEADS, nkb, TK, 128), BF16),
        jax.ShapeDtypeStruct((A_HEADS, nkb, 128, TK), BF16),
        jax.ShapeDtypeStruct((T, 1536), F32),
        jax.ShapeDtypeStruct((T, 512), F32),
        jax.ShapeDtypeStruct((T, 2048), F32),
        jax.ShapeDtypeStruct((T, 128), F32),
        jax.ShapeDtypeStruct((16, T), F32),
    )
    out_specs = (
        pl.BlockSpec((A_HEADS, TM_PROJ, 128), lambda i: (0, i, 0)),
        pl.BlockSpec((A_HEADS, nb, TK, 128), lambda i: (0, i, 0, 0)),
        pl.BlockSpec((A_HEADS, nb, 128, TK), lambda i: (0, i, 0, 0)),
        pl.BlockSpec((TM_PROJ, 1536), lambda i: (i, 0)),
        pl.BlockSpec((TM_PROJ, 512), lambda i: (i, 0)),
        pl.BlockSpec((TM_PROJ, 2048), lambda i: (i, 0)),
        pl.BlockSpec((TM_PROJ, 128), lambda i: (i, 0)),
        pl.BlockSpec((16, TM_PROJ), lambda i: (0, i)),
    )
    return pl.pallas_call(
        _inproj_kernel,
        grid=(n,),
        in_specs=[
            pl.BlockSpec((TM_PROJ, D_MODEL), lambda i: (i, 0)),
            full((1, D_MODEL)),
            full(w_main.shape),
            full(w_vt.shape),
            full(w_abt.shape),
        ],
        out_specs=out_specs,
        out_shape=out_shape,
        compiler_params=pltpu.CompilerParams(
            dimension_semantics=("arbitrary",), vmem_limit_bytes=_vmem_limit(56 << 20)),
        name="inproj",
    )(x2d, g_mix, w_main, w_vt, w_abt)


def _attn_kernel(lam_ref, q_ref, k_ref, vt_ref, bd_ref, bp_ref, gs_ref, o_ref, m_sc, l_sc, acc_sc, s_buf, pm_buf):
    qi = pl.program_id(1)
    q = q_ref[0]
    lane = lax.broadcasted_iota(I32, q.shape, 1)
    zero = jnp.zeros_like(q)
    qm = (jnp.where(lane < A_QK_DIM, q, zero), jnp.where(lane >= A_QK_DIM, q, zero))
    m_sc[...] = jnp.full(m_sc.shape, NEG_BIG, F32)
    l_sc[...] = jnp.zeros(l_sc.shape, F32)
    acc_sc[...] = jnp.zeros(acc_sc.shape, F32)

    nkb = k_ref.shape[1]

    def stage_a(slot, kb0):
        for j in range(2):
            kblk = k_ref[0, jnp.minimum(kb0 + j, nkb - 1)]
            for m in range(2):
                s = lax.dot_general(kblk, qm[m], _NT, preferred_element_type=F32)
                s_buf[slot, m, j] = s
                pm_buf[slot, m, j] = jnp.max(s, axis=0, keepdims=True)

    def stage_b(slot, kb0, biases):
        far = all(b is None for b in biases)
        for m in range(2):
            m_old = m_sc[m]
            m_new = m_old
            ss = []
            for j, bias in enumerate(biases):
                if far:
                    m_new = jnp.maximum(m_new, pm_buf[slot, m, j])
                else:
                    s = s_buf[slot, m, j]
                    s = s if bias is None else s + bias[0]
                    ss.append(s)
                    m_new = jnp.maximum(m_new, jnp.max(s, axis=0, keepdims=True))
            alpha = jnp.exp2(m_old - m_new)
            l_new = alpha * l_sc[m]
            acc = alpha * acc_sc[m]
            for j in range(len(biases)):
                s = s_buf[slot, m, j] if far else ss[j]
                p = jnp.exp2(s - m_new)
                l_new = l_new + jnp.sum(p, axis=0, keepdims=True)
                acc = acc + jnp.dot(vt_ref[0, kb0 + j], p.astype(BF16), preferred_element_type=F32)
            l_sc[m] = l_new
            acc_sc[m] = acc
            m_sc[m] = m_new

    far2 = [None, None]
    nfu = jnp.maximum(qi - 1, 0) // 2
    peel = nfu % 2

    @pl.when(peel == 1)
    def _():
        stage_a(0, 0)
        stage_b(0, 0, far2)

    stage_a(0, 2 * peel)

    def far_body(v, c):
        u = peel + 2 * v
        stage_a(1, 2 * u + 2)
        stage_b(0, 2 * u, far2)
        stage_a(0, 2 * u + 4)
        stage_b(1, 2 * u + 2, far2)
        return c

    lax.fori_loop(0, (nfu - peel) // 2, far_body, 0)
    kb0 = 2 * nfu

    @pl.when(qi % 2 == 1)
    def _():
        stage_b(0, kb0, [bp_ref, bd_ref])

    @pl.when((qi % 2 == 0) & (qi >= 2))
    def _():
        stage_a(1, kb0 + 2)
        stage_b(0, kb0, [None, bp_ref])
        stage_b(1, kb0 + 2, [bd_ref])

    @pl.when(qi == 0)
    def _():
        stage_b(0, kb0, [bd_ref])

    o = acc_sc[0] / l_sc[0] - lam_ref[0] * (acc_sc[1] / l_sc[1])
    ot = o.T
    o_ref[...] = (_rms(ot, gs_ref[...]) * (1.0 - LAMBDA_INIT)).astype(BF16)


def _attention(lam, q, k, vt, bias_d, bias_p, g_subln):
    T = q.shape[1]
    nq = T // TQ
    nkb = T // TK
    return pl.pallas_call(
        _attn_kernel,
        grid=(A_HEADS, nq),
        in_specs=[
            pl.BlockSpec(memory_space=pltpu.SMEM),
            pl.BlockSpec((1, TQ, 128), lambda h, i: (h, i, 0)),
            pl.BlockSpec((1, nkb, TK, 128), lambda h, i: (h, 0, 0, 0)),
            pl.BlockSpec((1, nkb, 128, TK), lambda h, i: (h, 0, 0, 0)),
            pl.BlockSpec((1, TK, TQ), lambda h, i: (h, 0, 0)),
            pl.BlockSpec((1, TK, TQ), lambda h, i: (h, 0, 0)),
            pl.BlockSpec((1, 128), lambda h, i: (0, 0)),
        ],
        out_specs=pl.BlockSpec((TQ, 128), lambda h, i: (i, h)),
        out_shape=jax.ShapeDtypeStruct((T, A_HEADS * A_V_DIM), BF16),
        scratch_shapes=[
            pltpu.VMEM((2, 1, TQ), F32),
            pltpu.VMEM((2, 1, TQ), F32),
            pltpu.VMEM((2, 128, TQ), F32),
            pltpu.VMEM((2, 2, 2, TK, TQ), F32),
            pltpu.VMEM((2, 2, 2, 1, TQ), F32),
        ],
        compiler_params=pltpu.CompilerParams(
            dimension_semantics=("arbitrary", "arbitrary"), vmem_limit_bytes=_vmem_limit(48 << 20)),
        name="attn",
    )(lam, q, k, vt, bias_d, bias_p, g_subln)


def _t5_bucket(rel):
    half = REL_BUCKETS // 2
    max_exact = half // 2
    ret = jnp.where(rel > 0, half, 0)
    n = jnp.abs(rel)
    large = max_exact + (jnp.log(jnp.maximum(n, 1).astype(F32) / max_exact)
                         / math.log(REL_MAX_DIST / max_exact) * (half - max_exact)).astype(I32)
    large = jnp.minimum(large, half - 1)
    return ret + jnp.where(n < max_exact, n, large)


def _attn_bias_tables(rel_bias):
    assert TK >= REL_MAX_DIST and TK == TQ and TK % CHUNK == 0
    kk = jnp.arange(TK, dtype=I32)[:, None]
    qq = jnp.arange(TQ, dtype=I32)[None, :]
    rb = rel_bias.astype(F32)
    far = rb[_t5_bucket(jnp.full((1,), -REL_MAX_DIST, I32))[0]]
    table = ((rb - far[None, :]) * LOG2E).T

    def lookup(rel):
        onehot = (_t5_bucket(rel).reshape(1, -1) == jnp.arange(REL_BUCKETS, dtype=I32)[:, None]).astype(F32)
        return jnp.dot(table, onehot, precision=lax.Precision.HIGHEST).reshape(A_HEADS, TK, TQ)

    bd = jnp.where((kk // CHUNK <= qq // CHUNK)[None], lookup(kk - qq), NEG_BIG)
    bp = lookup(kk - TK - qq)
    return bd, bp


def _gdn_kernel(x_ref, zb_ref, ab_ref, abt_ref, cw_ref, alc_ref, dtc_ref, alr_ref, dtr_ref, gob_ref,
                o_ref, xbuf, s_sc):
    i = pl.program_id(0)
    TC = TC_GDN
    NCH = TC // CHUNK
    HB = B_HEADS * CHUNK

    @pl.when(i == 0)
    def _():
        xbuf[0:8, :] = jnp.zeros((8, 1536), F32)
        s_sc[...] = jnp.zeros(s_sc.shape, F32)

    xbuf[8:8 + TC, :] = x_ref[...]
    y = cw_ref[3:4, :] * xbuf[8:8 + TC, :]
    for d in range(1, CONV_WIDTH):
        y = y + cw_ref[3 - d:4 - d, :] * xbuf[8 - d:8 - d + TC, :]
    xbuf[0:8, :] = x_ref[TC - 8:TC, :]
    a = y * _sigmoid(y)

    def l2n(t):
        return t * lax.rsqrt(jnp.sum(t * t, axis=-1, keepdims=True) + EPS)

    qh = [l2n(a[:, h * 128:(h + 1) * 128]) * (B_DIM ** -0.5) for h in range(B_HEADS)]
    kh = [l2n(a[:, 512 + h * 128:512 + (h + 1) * 128]) for h in range(B_HEADS)]
    vh = [a[:, 1024 + h * 128:1024 + (h + 1) * 128] for h in range(B_HEADS)]

    def softplus(t):
        return jnp.maximum(t, 0.0) + jnp.log(1.0 + jnp.exp(-jnp.abs(t)))

    ab = ab_ref[...]
    g_col = -jnp.exp(alc_ref[...]) * softplus(ab + dtc_ref[...])
    beta_col = _sigmoid(ab)
    g_row = -jnp.exp(alr_ref[...]) * softplus(abt_ref[...] + dtr_ref[...])

    rt = lax.broadcasted_iota(I32, (TC, TC), 0)
    ct = lax.broadcasted_iota(I32, (TC, TC), 1)
    same_chunk = (rt // CHUNK) == (ct // CHUNK)
    tril = jnp.where(same_chunk & (rt >= ct), 1.0, 0.0).astype(F32)
    triu = jnp.where(same_chunk & (rt <= ct), 1.0, 0.0).astype(F32)
    gc_col = jnp.dot(tril, g_col, precision=lax.Precision.HIGHEST, preferred_element_type=F32)
    gc_row = jnp.dot(g_row, triu, precision=lax.Precision.HIGHEST, preferred_element_type=F32)

    ri = lax.broadcasted_iota(I32, (HB, HB), 0)
    ci = lax.broadcasted_iota(I32, (HB, HB), 1)
    same_head = (ri // CHUNK) == (ci // CHUNK)
    incl = same_head & (ri >= ci)
    strict = same_head & (ri > ci)
    eye = jnp.where(ri == ci, 1.0, 0.0).astype(F32)

    def level_mask(s):
        return ((ri // (2 * s)) == (ci // (2 * s))) & (((ri // s) % 2) == 1) & (((ci // s) % 2) == 0)

    cat0 = lambda parts: jnp.concatenate(parts, axis=0)
    heads = range(B_HEADS)
    chunks = range(NCH)

    Kc, Qc, glc, gccc, Lc, QKc, invc, rhsc = [], [], [], [], [], [], [], []
    for c in chunks:
        rs = slice(c * CHUNK, (c + 1) * CHUNK)
        last = slice(c * CHUNK + CHUNK - 1, (c + 1) * CHUNK)
        K = cat0([kh[h][rs] for h in heads])
        Q = cat0([qh[h][rs] for h in heads])
        V = cat0([vh[h][rs] for h in heads])
        beta = cat0([beta_col[rs, 4 + h:5 + h] for h in heads])
        gcc = cat0([gc_col[rs, h:h + 1] for h in heads])
        gl = cat0([jnp.broadcast_to(gc_col[last, h:h + 1], (CHUNK, 1)) for h in heads])
        gcr = jnp.concatenate([gc_row[h:h + 1, rs] for h in heads], axis=1)
        Kb = K * beta
        e = jnp.exp(jnp.where(incl, gcc - gcr, 0.0))
        L = lax.dot_general(Kb, K, _NT, preferred_element_type=F32) * jnp.where(strict, e, 0.0)
        QKc.append(lax.dot_general(Q, K, _NT, preferred_element_type=F32) * jnp.where(incl, e, 0.0))
        rhsc.append(jnp.concatenate([V * beta, Kb * jnp.exp(gcc)], axis=1))
        invc.append(eye - jnp.where(level_mask(1), L, 0.0))
        Kc.append(K); Qc.append(Q); glc.append(gl); gccc.append(gcc); Lc.append(L)

    for s in (2, 4, 8, 16, 32):
        msk = level_mask(s)
        ts = [jnp.dot(invc[c], jnp.where(msk, Lc[c], 0.0), preferred_element_type=F32) for c in chunks]
        invc = [invc[c] - jnp.dot(ts[c], invc[c], preferred_element_type=F32) for c in chunks]
    solc = [jnp.dot(invc[c], rhsc[c], preferred_element_type=F32) for c in chunks]

    for c in chunks:
        rs = slice(c * CHUNK, (c + 1) * CHUNK)
        U = solc[c][:, :B_DIM]
        W = solc[c][:, B_DIM:]
        Qd = Qc[c] * jnp.exp(gccc[c])
        Kd = Kc[c] * jnp.exp(glc[c] - gccc[c])
        vnew = []
        ost = []
        for h in heads:
            hs = slice(h * CHUNK, (h + 1) * CHUNK)
            S = s_sc[h]
            vn = U[hs] - jnp.dot(W[hs], S, preferred_element_type=F32)
            ost.append(jnp.dot(Qd[hs], S, preferred_element_type=F32))
            s_sc[h] = S * jnp.exp(glc[c][h * CHUNK:h * CHUNK + 1, :]) + lax.dot_general(
                Kd[hs], vn, _TN, preferred_element_type=F32)
            vnew.append(vn)
        O = cat0(ost) + jnp.dot(QKc[c], cat0(vnew), preferred_element_type=F32)
        for h in heads:
            oh = _rms(O[h * CHUNK:(h + 1) * CHUNK], gob_ref[...])
            z = zb_ref[rs, h * 128:(h + 1) * 128]
            o_ref[rs, h * 128:(h + 1) * 128] = (oh * (z * _sigmoid(z))).astype(BF16)


def _gdn(qkvb, zb, ab, abt, conv_w8, alc, dtc, alr, dtr, gob):
    T = qkvb.shape[0]
    TC = TC_GDN
    full = lambda shape: pl.BlockSpec(shape, lambda i: (0,) * len(shape))
    return pl.pallas_call(
        _gdn_kernel,
        grid=(T // TC,),
        in_specs=[
            pl.BlockSpec((TC, 1536), lambda i: (i, 0)),
            pl.BlockSpec((TC, 512), lambda i: (i, 0)),
            pl.BlockSpec((TC, 128), lambda i: (i, 0)),
            pl.BlockSpec((16, TC), lambda i: (0, i)),
            full((8, 1536)), full((1, 128)), full((1, 128)), full((16, TC)), full((16, TC)), full((1, 128)),
        ],
        out_specs=pl.BlockSpec((TC, 512), lambda i: (i, 0)),
        out_shape=jax.ShapeDtypeStruct((T, 512), BF16),
        scratch_shapes=[pltpu.VMEM((TC + 8, 1536), F32), pltpu.VMEM((B_HEADS, B_DIM, B_DIM), F32)],
        compiler_params=pltpu.CompilerParams(
            dimension_semantics=("arbitrary",), vmem_limit_bytes=_vmem_limit(48 << 20)),
        name="gdn",
    )(qkvb, zb, ab, abt, conv_w8, alc, dtc, alr, dtr, gob)


def _memkv_kernel(m_ref, g_ref, w_ref, k_ref, v_ref):
    hm = _rms(m_ref[...], g_ref[...]).astype(BF16)
    kv = jnp.dot(hm, w_ref[...], preferred_element_type=F32)
    k_ref[...] = kv[:, :D_MODEL].astype(BF16)
    v_ref[...] = kv[:, D_MODEL:].astype(BF16)


def _memkv(mem2d, g_mem, w_kv):
    n = mem2d.shape[0]
    return pl.pallas_call(
        _memkv_kernel,
        out_shape=(jax.ShapeDtypeStruct((n, D_MODEL), BF16), jax.ShapeDtypeStruct((n, D_MODEL), BF16)),
        compiler_params=pltpu.CompilerParams(vmem_limit_bytes=_vmem_limit(32 << 20)),
        name="memkv",
    )(mem2d, g_mem, w_kv)


def _mix_kernel(x_ref, oa_ref, ob_ref, gates_ref, wua_ref, wub_ref, wout_ref, gx_ref, wq_ref, kx_ref, vx_ref,
                wo_ref, gm_ref, wr_ref, br_ref, x2t_ref, hmt_ref, idx_ref, wts_ref):
    NSUB = 2
    TS = TM_MIX // NSUB
    subs = range(NSUB)
    rows = [slice(sub * TS, (sub + 1) * TS) for sub in subs]
    dotf = functools.partial(jnp.dot, preferred_element_type=F32)

    ma = [dotf(oa_ref[rows[g], :], wua_ref[...]) for g in subs]
    mb = [dotf(ob_ref[rows[g], :], wub_ref[...]) for g in subs]
    merged = [(_sigmoid(gates_ref[rows[g], :D_MODEL]) * ma[g]
               + _sigmoid(gates_ref[rows[g], D_MODEL:]) * mb[g]).astype(BF16) for g in subs]
    x1 = [x_ref[rows[g], :] + dotf(merged[g], wout_ref[...]) for g in subs]
    hx = [_rms(x1[g], gx_ref[...]).astype(BF16) for g in subs]

    heads = [[] for _ in subs]
    for h in range(X_HEADS):
        cs = slice(h * X_HEAD_DIM, (h + 1) * X_HEAD_DIM)
        qh = [dotf(hx[g], wq_ref[:, cs]).astype(BF16) for g in subs]
        s = [lax.dot_general(qh[g], kx_ref[:, cs], _NT, preferred_element_type=F32) * (X_HEAD_DIM ** -0.5)
             for g in subs]
        for g in subs:
            sg = s[g] - jnp.max(s[g], axis=-1, keepdims=True)
            p = jnp.exp(sg)
            p = p / jnp.sum(p, axis=-1, keepdims=True)
            heads[g].append(dotf(p.astype(BF16), vx_ref[:, cs]).astype(BF16))
    x2 = [x1[g] + dotf(jnp.concatenate(heads[g], axis=1), wo_ref[...]) for g in subs]
    hm = [_rms(x2[g], gm_ref[...]) for g in subs]

    for g in subs:
        for c in range(D_MODEL // 128):
            x2t_ref[pl.ds(g * TS * 8 + c, TS, stride=8), :] = x2[g][:, c * 128:(c + 1) * 128]
            hmt_ref[pl.ds(g * TS * 8 + c, TS, stride=8), :] = hm[g][:, c * 128:(c + 1) * 128]

        logits = lax.dot_general(wr_ref[...], hm[g], _NT, precision=lax.Precision.HIGHEST,
                                 preferred_element_type=F32) + br_ref[:, 0:1]
        eidx = lax.broadcasted_iota(I32, logits.shape, 0)
        vals, idxs = [], []
        cur = logits
        for _ in range(TOP_K):
            mx = jnp.max(cur, axis=0, keepdims=True)
            ix = jnp.min(jnp.where(cur == mx, eidx, N_EXPERTS), axis=0, keepdims=True)
            vals.append(mx)
            idxs.append(ix)
            cur = jnp.where(eidx == ix, -jnp.inf, cur)
        ex = [jnp.exp(v - vals[0]) for v in vals]
        den = ex[0] + ex[1] + ex[2] + ex[3]
        idx_ref[:, rows[g]] = jnp.concatenate(idxs, axis=0)
        wts_ref[:, rows[g]] = jnp.concatenate([e / den for e in ex], axis=0)


def _mix(x2d, oa, ob, gates, wua, wub, wout, gx, wq, kx, vx, wo, gm, wr_t, br):
    T = x2d.shape[0]
    TM = TM_MIX
    full = lambda a: pl.BlockSpec(a.shape, lambda i: (0,) * a.ndim)
    out_shape = (
        jax.ShapeDtypeStruct((T * 8, 128), F32),
        jax.ShapeDtypeStruct((T * 8, 128), F32),
        jax.ShapeDtypeStruct((TOP_K, T), I32),
        jax.ShapeDtypeStruct((TOP_K, T), F32),
    )
    return pl.pallas_call(
        _mix_kernel,
        grid=(T // TM,),
        in_specs=[
            pl.BlockSpec((TM, D_MODEL), lambda i: (i, 0)),
            pl.BlockSpec((TM, 512), lambda i: (i, 0)),
            pl.BlockSpec((TM, 512), lambda i: (i, 0)),
            pl.BlockSpec((TM, 2048), lambda i: (i, 0)),
            full(wua), full(wub), full(wout), full(gx), full(wq), full(kx), full(vx), full(wo), full(gm),
            full(wr_t), full(br),
        ],
        out_specs=(
            pl.BlockSpec((TM * 8, 128), lambda i: (i, 0)),
            pl.BlockSpec((TM * 8, 128), lambda i: (i, 0)),
            pl.BlockSpec((TOP_K, TM), lambda i: (0, i)),
            pl.BlockSpec((TOP_K, TM), lambda i: (0, i)),
        ),
        out_shape=out_shape,
        compiler_params=pltpu.CompilerParams(
            dimension_semantics=("arbitrary",), vmem_limit_bytes=_vmem_limit(56 << 20)),
        name="mix",
    )(x2d, oa, ob, gates, wua, wub, wout, gx, wq, kx, vx, wo, gm, wr_t, br)


def _moe_kernel(blk_e_ref, first_ref, nxt_ref, slot_ref, nb_ref,
                tok_ref, wt_ref, x2t_hbm, hmt_ref, w1_hbm, w2_hbm, b1_ref, b2_ref, gf_ref,
                out_ref, yacc, xs_a, xs_b, yst_a, yst_b, w1buf, w2buf, wsem, xsem):
    j = pl.program_id(0)
    TT = TT_MOE
    RB = RB_MOE
    NG = D_MODEL // 128
    nbj = nb_ref[j]

    x2_copy = pltpu.make_async_copy(
        x2t_hbm.at[pl.ds(pl.multiple_of(j * TT * 8, 8), TT * 8), :], yacc.at[pl.ds(0, TT * 8), :], xsem.at[0])
    x2_copy.start()

    def w_copies(e, slot):
        cps = []
        for c in range(MOE_W_CHUNKS):
            for k, (src, dst) in enumerate(((w1_hbm, w1buf), (w2_hbm, w2buf))):
                rows = pl.ds(c * (src.shape[1] // MOE_W_CHUNKS), src.shape[1] // MOE_W_CHUNKS)
                cps.append(pltpu.make_async_copy(src.at[e, rows], dst.at[slot, rows], wsem.at[slot, k, c]))
        return cps

    for cp in w_copies(blk_e_ref[j, 0], slot_ref[j, 0]):
        cp.start()

    yacc[TT * 8:TT * 8 + 8, :] = jnp.zeros((8, 128), F32)
    yst_b[...] = jnp.zeros(yst_b.shape, F32)

    def gather(b, xs):
        for r in range(RB):
            t8 = jnp.minimum(tok_ref[0, 0, b * RB + r], (TT - 1) * 8)
            xs[r * 8:(r + 1) * 8, :] = hmt_ref[pl.ds(pl.multiple_of(t8, 8), 8), :]

    def mlp(xs, yst, e, slot):
        xb = jnp.concatenate([xs[pl.ds(g, RB, stride=8), :] for g in range(NG)], axis=1).astype(BF16)
        hid = jnp.dot(xb, w1buf[slot], preferred_element_type=F32) + b1_ref[e]
        glu = jnp.minimum(hid[:, :D_FF], SWIGLU_LIMIT)
        lin = jnp.clip(hid[:, D_FF:], -SWIGLU_LIMIT, SWIGLU_LIMIT)
        act = glu * _sigmoid(SWIGLU_ALPHA * glu) * (lin + 1.0)
        ys = jnp.dot(act.astype(BF16), w2buf[slot], preferred_element_type=F32) + b2_ref[e]
        for g in range(NG):
            yst[pl.ds(g, RB, stride=8), :] = ys[:, g * 128:(g + 1) * 128]

    def scatter(b, yst):
        for g0 in range(0, RB, 8):
            dsts, vals = [], []
            for r in range(g0, g0 + 8):
                dst = pl.ds(pl.multiple_of(tok_ref[0, 0, b * RB + r], 8), 8)
                dsts.append(dst)
                vals.append(yacc[dst, :] + wt_ref[0, 0, b * RB + r] * yst[r * 8:(r + 1) * 8, :])
            for dst, val in zip(dsts, vals):
                yacc[dst, :] = val

    def step(b, xs_cur, yst_cur, xs_next, yst_prev):
        @pl.when(b < nbj)
        def _():
            e = blk_e_ref[j, b]
            slot = slot_ref[j, b]

            @pl.when(first_ref[j, b] == 1)
            def _():
                for cp in w_copies(e, slot):
                    cp.wait()
                nx = nxt_ref[j, b]

                @pl.when(nx >= 0)
                def _():
                    for cp in w_copies(nx, 1 - slot):
                        cp.start()

            gather(b + 1, xs_next)
            mlp(xs_cur, yst_cur, e, slot)
            scatter(jnp.maximum(b - 1, 0), yst_prev)

    gather(0, xs_a)
    x2_copy.wait()

    def pair_body(pp, c):
        step(2 * pp, xs_a, yst_a, xs_b, yst_b)
        step(2 * pp + 1, xs_b, yst_b, xs_a, yst_a)
        return c

    lax.fori_loop(0, (nbj + 1) // 2, pair_body, 0)

    @pl.when(nbj % 2 == 1)
    def _():
        scatter(nbj - 1, yst_a)

    @pl.when(nbj % 2 == 0)
    def _():
        scatter(nbj - 1, yst_b)

    RC = 256

    def fin(ci, c):
        base = pl.multiple_of(ci * RC * 8, 8)
        yv = jnp.concatenate([yacc[pl.ds(base + g, RC, stride=8), :] for g in range(NG)], axis=1)
        out_ref[pl.ds(pl.multiple_of(ci * RC, 8), RC), :] = _rms(yv, gf_ref[...])
        return c

    lax.fori_loop(0, TT // RC, fin, 0)


def _moe_nb_max():
    return TOP_K * TT_MOE // RB_MOE + N_EXPERTS


def _moe(lists, x2t, hmt, w1, b1, w2, b2, g_final):
    blk_e, first, nxt, slot, nb, tok, wt = lists
    T = x2t.shape[0] // 8
    TT = TT_MOE
    nt = T // TT
    LP = tok.shape[-1]
    smem_row = pl.BlockSpec((1, 1, LP), lambda j, *_: (j, 0, 0), memory_space=pltpu.SMEM)
    whole = lambda a: pl.BlockSpec(a.shape, lambda j, *_: (0,) * a.ndim)
    grid_spec = pltpu.PrefetchScalarGridSpec(
        num_scalar_prefetch=5,
        grid=(nt,),
        in_specs=[
            smem_row, smem_row,
            pl.BlockSpec(memory_space=pl.ANY),
            pl.BlockSpec((TT * 8, 128), lambda j, *_: (j, 0)),
            pl.BlockSpec(memory_space=pl.ANY), pl.BlockSpec(memory_space=pl.ANY),
            whole(b1), whole(b2), whole(g_final),
        ],
        out_specs=pl.BlockSpec((TT, D_MODEL), lambda j, *_: (j, 0), pipeline_mode=pl.Buffered(1)),
        scratch_shapes=[
            pltpu.VMEM((TT * 8 + 8, 128), F32),
            pltpu.VMEM((RB_MOE * 8, 128), F32), pltpu.VMEM((RB_MOE * 8, 128), F32),
            pltpu.VMEM((RB_MOE * 8, 128), F32), pltpu.VMEM((RB_MOE * 8, 128), F32),
            pltpu.VMEM((2, D_MODEL, 2 * D_FF), BF16),
            pltpu.VMEM((2, D_FF, D_MODEL), BF16),
            pltpu.SemaphoreType.DMA((2, 2, MOE_W_CHUNKS)),
            pltpu.SemaphoreType.DMA((1,)),
        ],
    )
    return pl.pallas_call(
        _moe_kernel,
        grid_spec=grid_spec,
        out_shape=jax.ShapeDtypeStruct((T, D_MODEL), F32),
        compiler_params=pltpu.CompilerParams(
            dimension_semantics=("arbitrary",), vmem_limit_bytes=_vmem_limit(60 << 20)),
        name="moe",
    )(blk_e, first, nxt, slot, nb, tok, wt, x2t, hmt, w1, w2, b1, b2, g_final)


def _moe_lists(idx, wts):
    T = idx.shape[1]
    TT, RB = TT_MOE, RB_MOE
    nt = T // TT
    NB = _moe_nb_max()
    A = TOP_K * TT
    e_tile = idx.reshape(TOP_K, nt, TT).transpose(1, 0, 2).reshape(nt, A)
    w_tile = wts.reshape(TOP_K, nt, TT).transpose(1, 0, 2).reshape(nt, A)
    tok_local = jnp.tile(jnp.arange(TT, dtype=I32), TOP_K)
    order = jnp.argsort(e_tile, axis=1, stable=True)
    stok = jnp.take_along_axis(jnp.broadcast_to(tok_local, e_tile.shape), order, axis=1)
    swt = jnp.take_along_axis(w_tile, order, axis=1)

    experts = jnp.arange(N_EXPERTS, dtype=I32)
    counts = jnp.sum((e_tile[:, :, None] == experts[None, None, :]).astype(I32), axis=1)
    start = jnp.cumsum(counts, axis=1) - counts
    nblk = (counts + RB - 1) // RB
    blk_end = jnp.cumsum(nblk, axis=1)
    blk_start = blk_end - nblk
    nb = blk_end[:, -1]

    b = jnp.arange(NB, dtype=I32)
    blk_e = jnp.minimum(jnp.sum((blk_end[:, :, None] <= b[None, None, :]).astype(I32), axis=1), N_EXPERTS - 1)
    onehot = (blk_e[:, :, None] == experts[None, None, :]).astype(I32)
    per_block = lambda v: jnp.sum(onehot * v[:, None, :], axis=2)
    present = (nblk > 0).astype(I32)
    ordinal = jnp.cumsum(present, axis=1) - present
    later = (experts[None, :] > experts[:, None])[None] & (present[:, None, :] > 0)
    nxt_e = jnp.min(jnp.where(later, experts[None, None, :], N_EXPERTS), axis=2)
    nxt_e = jnp.where(nxt_e >= N_EXPERTS, -1, nxt_e)
    bs_b = per_block(blk_start)
    first = ((b[None, :] == bs_b) & (b[None, :] < nb[:, None])).astype(I32)
    slot = per_block(ordinal) % 2
    nxt = per_block(nxt_e)

    r = jnp.arange(RB, dtype=I32)[None, None, :]
    off = (b[None, :] - bs_b)[:, :, None] * RB + r
    valid = (off < per_block(counts)[:, :, None]) & (b[None, :, None] < nb[:, None, None])
    src = jnp.clip(per_block(start)[:, :, None] + off, 0, A - 1).reshape(nt, NB * RB)
    valid = valid.reshape(nt, NB * RB)
    tok = jnp.where(valid, jnp.take_along_axis(stok, src, axis=1), TT)
    wt = jnp.where(valid, jnp.take_along_axis(swt, src, axis=1), 0.0)
    tok = jnp.pad(tok, ((0, 0), (0, RB)), constant_values=TT)
    wt = jnp.pad(wt, ((0, 0), (0, RB)))
    return blk_e, first, nxt, slot, nb, (tok * 8)[:, None, :], wt[:, None, :]


def kernel(x, mem, g_mix, w_in, rel_bias, lambda_q1, lambda_k1, lambda_q2, lambda_k2, g_subln, conv_w, a_log,
           dt_bias, g_out_b, w_up_a, w_up_b, w_out, g_xattn, g_mem, w_q_x, w_kv_x, w_o_x, g_moe, w_router,
           b_router, w_mlp1, b_mlp1, w_mlp2, b_mlp2, g_final):
    B_, S_, _ = x.shape
    assert B_ == 1 and S_ % TT_MOE == 0 and x.dtype == F32
    l = 0
    x2d = x.reshape(S_, D_MODEL)
    row = lambda v: v.reshape(1, -1).astype(F32)

    wi = w_in[l]
    ab_cols = jnp.pad(wi[:, 3584:3592], ((0, 0), (0, 120)))
    w_main = jnp.concatenate(
        [wi[:, 0:1024], wi[:, 1536:3072], wi[:, 3072:3584], wi[:, 3592:5640], ab_cols], axis=1).astype(BF16)
    w_vt = wi[:, 1024:1536].T.astype(BF16)
    w_abt = jnp.pad(wi[:, 3584:3592].T, ((0, 8), (0, 0))).astype(BF16)

    q, k, vt, qkvb, zb, gates, ab, abt = _inproj(x2d, row(g_mix[l]), w_main, w_vt, w_abt)

    lam = (jnp.exp(jnp.sum(lambda_q1[l].astype(F32) * lambda_k1[l].astype(F32)))
           - jnp.exp(jnp.sum(lambda_q2[l].astype(F32) * lambda_k2[l].astype(F32))) + LAMBDA_INIT).reshape(1)
    bias_d, bias_p = _attn_bias_tables(rel_bias)
    oa = _attention(lam, q, k, vt, bias_d, bias_p, row(g_subln[l]))

    lane_pad = lambda v: jnp.pad(v.astype(F32), (0, 128 - v.shape[0])).reshape(1, 128)
    row_bcast = lambda v: jnp.broadcast_to(jnp.pad(v.astype(F32), (0, 16 - v.shape[0]))[:, None], (16, TC_GDN))
    ob = _gdn(qkvb, zb, ab, abt, jnp.pad(conv_w[l].astype(F32), ((0, 4), (0, 0))),
              lane_pad(a_log[l]), lane_pad(dt_bias[l]), row_bcast(a_log[l]), row_bcast(dt_bias[l]),
              row(g_out_b[l]))

    kx, vx = _memkv(mem.reshape(-1, D_MODEL), row(g_mem[l]), w_kv_x[l].astype(BF16))
    br = jnp.broadcast_to(b_router[l].astype(F32)[:, None], (N_EXPERTS, 128))
    x2t, hmt, idx, wts = _mix(
        x2d, oa, ob, gates, w_up_a[l].astype(BF16), w_up_b[l].astype(BF16), w_out[l].astype(BF16),
        row(g_xattn[l]), w_q_x[l].astype(BF16), kx, vx, w_o_x[l].astype(BF16), row(g_moe[l]),
        w_router[l].T.astype(F32), br)

    out = _moe(_moe_lists(idx, wts), x2t, hmt, w_mlp1[l].astype(BF16), b_mlp1[l].astype(F32)[:, None, :],
               w_mlp2[l].astype(BF16), b_mlp2[l].astype(F32)[:, None, :], row(g_final))
    return out.reshape(B_, S_, D_MODEL)
```

```python
import functools
import math

import jax
import jax.numpy as jnp
from jax import lax
from jax.experimental import pallas as pl
from jax.experimental.pallas import tpu as pltpu

F32 = jnp.float32
BF16 = jnp.bfloat16
I32 = jnp.int32

D_MODEL = 1024
CHUNK = 64
EPS = 1e-6
A_HEADS = 4
A_QK_DIM = 64
A_V_DIM = 128
REL_BUCKETS = 32
REL_MAX_DIST = 128
B_HEADS = 4
B_DIM = 128
CONV_WIDTH = 4
X_HEADS = 4
X_HEAD_DIM = 256
N_EXPERTS = 32
TOP_K = 4
D_FF = 1024
SWIGLU_LIMIT = 7.0
SWIGLU_ALPHA = 1.702
LAMBDA_INIT = 0.8 - 0.6 * math.exp(-0.3 * 0)

LOG2E = 1.4426950408889634
NEG_BIG = -1e30

V7X_LANES = 128
V7X_SUBLANES = 8
V7X_VMEM_BYTES = 64 * 1024 * 1024

TM_PROJ = 512
TQ = 512
TK = 512
TC_GDN = 256
TM_MIX = 512
TT_MOE = 2048
RB_MOE = 128
MOE_W_CHUNKS = 4

C_QA, C_KA, C_QKVB, C_ZB, C_GATE, C_AB, C_END = 0, 512, 1024, 2560, 3072, 5120, 5248

_NT = (((1,), (1,)), ((), ()))
_TN = (((0,), (0,)), ((), ()))


def _rms(x, g):
    return x * lax.rsqrt(jnp.mean(x * x, axis=-1, keepdims=True) + EPS) * g


def _sigmoid(x):
    return 1.0 / (1.0 + jnp.exp(-x))


def _vmem_limit(nbytes):
    return int(min(nbytes, V7X_VMEM_BYTES - 4 * 1024 * 1024))


def _inproj_kernel(x_ref, g_ref, w_ref, wvt_ref, wabt_ref,
                   q_ref, k_ref, vt_ref, qkvb_ref, zb_ref, gates_ref, ab_ref, abt_ref):
    h = _rms(x_ref[...], g_ref[...]).astype(BF16)

    def mm(c0, c1):
        return jnp.dot(h, w_ref[:, c0:c1], preferred_element_type=F32)

    nb = TM_PROJ // TK
    qa = mm(C_QA, C_KA) * (A_QK_DIM ** -0.5 * LOG2E)
    ka = mm(C_KA, C_QKVB)
    vt = lax.dot_general(wvt_ref[...], h, _NT, preferred_element_type=F32)
    for hh in range(A_HEADS):
        cs = slice(hh * 128, (hh + 1) * 128)
        q_ref[hh] = qa[:, cs].astype(BF16)
        for b in range(nb):
            rs = slice(b * TK, (b + 1) * TK)
            k_ref[hh, b] = ka[rs, cs].astype(BF16)
            vt_ref[hh, b] = vt[cs, rs].astype(BF16)
    for j in range(3):
        qkvb_ref[:, j * 512:(j + 1) * 512] = mm(C_QKVB + j * 512, C_QKVB + (j + 1) * 512)
    zb_ref[...] = mm(C_ZB, C_GATE)
    for j in range(4):
        gates_ref[:, j * 512:(j + 1) * 512] = mm(C_GATE + j * 512, C_GATE + (j + 1) * 512)
    ab_ref[...] = mm(C_AB, C_END)
    abt_ref[...] = lax.dot_general(wabt_ref[...], h, _NT, preferred_element_type=F32)


def _inproj(x2d, g_mix, w_main, w_vt, w_abt):
    T = x2d.shape[0]
    n = T // TM_PROJ
    nkb = T // TK
    nb = TM_PROJ // TK
    full = lambda shape: pl.BlockSpec(shape, lambda i: (0,) * len(shape))
    out_shape = (
        jax.ShapeDtypeStruct((A_HEADS, T, 128), BF16),
        jax.ShapeDtypeStruct((A_HEADS, nkb, TK, 128), BF16),
        jax.ShapeDtypeStruct((A_HEADS, nkb, 128, TK), BF16),
        jax.ShapeDtypeStruct((T, 1536), F32),
        jax.ShapeDtypeStruct((T, 512), F32),
        jax.ShapeDtypeStruct((T, 2048), F32),
        jax.ShapeDtypeStruct((T, 128), F32),
        jax.ShapeDtypeStruct((16, T), F32),
    )
    out_specs = (
        pl.BlockSpec((A_HEADS, TM_PROJ, 128), lambda i: (0, i, 0)),
        pl.BlockSpec((A_HEADS, nb, TK, 128), lambda i: (0, i, 0, 0)),
        pl.BlockSpec((A_HEADS, nb, 128, TK), lambda i: (0, i, 0, 0)),
        pl.BlockSpec((TM_PROJ, 1536), lambda i: (i, 0)),
        pl.BlockSpec((TM_PROJ, 512), lambda i: (i, 0)),
        pl.BlockSpec((TM_PROJ, 2048), lambda i: (i, 0)),
        pl.BlockSpec((TM_PROJ, 128), lambda i: (i, 0)),
        pl.BlockSpec((16, TM_PROJ), lambda i: (0, i)),
    )
    return pl.pallas_call(
        _inproj_kernel,
        grid=(n,),
        in_specs=[
            pl.BlockSpec((TM_PROJ, D_MODEL), lambda i: (i, 0)),
            full((1, D_MODEL)),
            full(w_main.shape),
            full(w_vt.shape),
            full(w_abt.shape),
        ],
        out_specs=out_specs,
        out_shape=out_shape,
        compiler_params=pltpu.CompilerParams(
            dimension_semantics=("arbitrary",), vmem_limit_bytes=_vmem_limit(56 << 20)),
        name="inproj",
    )(x2d, g_mix, w_main, w_vt, w_abt)


def _attn_kernel(lam_ref, q_ref, k_ref, vt_ref, bd_ref, bp_ref, gs_ref, o_ref, m_sc, l_sc, acc_sc, s_buf, pm_buf):
    qi = pl.program_id(1)
    q = q_ref[0]
    lane = lax.broadcasted_iota(I32, q.shape, 1)
    zero = jnp.zeros_like(q)
    qm = (jnp.where(lane < A_QK_DIM, q, zero), jnp.where(lane >= A_QK_DIM, q, zero))
    m_sc[...] = jnp.full(m_sc.shape, NEG_BIG, F32)
    l_sc[...] = jnp.zeros(l_sc.shape, F32)
    acc_sc[...] = jnp.zeros(acc_sc.shape, F32)

    nkb = k_ref.shape[1]

    def stage_a(slot, kb0):
        for j in range(2):
            kblk = k_ref[0, jnp.minimum(kb0 + j, nkb - 1)]
            for m in range(2):
                s = lax.dot_general(kblk, qm[m], _NT, preferred_element_type=F32)
                s_buf[slot, m, j] = s
                pm_buf[slot, m, j] = jnp.max(s, axis=0, keepdims=True)

    def stage_b(slot, kb0, biases):
        far = all(b is None for b in biases)
        for m in range(2):
            m_old = m_sc[m]
            m_new = m_old
            ss = []
            for j, bias in enumerate(biases):
                if far:
                    m_new = jnp.maximum(m_new, pm_buf[slot, m, j])
                else:
                    s = s_buf[slot, m, j]
                    s = s if bias is None else s + bias[0]
                    ss.append(s)
                    m_new = jnp.maximum(m_new, jnp.max(s, axis=0, keepdims=True))
            alpha = jnp.exp2(m_old - m_new)
            l_new = alpha * l_sc[m]
            acc = alpha * acc_sc[m]
            for j in range(len(biases)):
                s = s_buf[slot, m, j] if far else ss[j]
                p = jnp.exp2(s - m_new)
                l_new = l_new + jnp.sum(p, axis=0, keepdims=True)
                acc = acc + jnp.dot(vt_ref[0, kb0 + j], p.astype(BF16), preferred_element_type=F32)
            l_sc[m] = l_new
            acc_sc[m] = acc
            m_sc[m] = m_new

    def far_pair(slot_a, kb_a, slot_b, kb_b):
        def a_piece(j, m):
            kblk = k_ref[0, jnp.minimum(kb_a + j, nkb - 1)]
            s = lax.dot_general(kblk, qm[m], _NT, preferred_element_type=F32)
            s_buf[slot_a, m, j] = s
            pm_buf[slot_a, m, j] = jnp.max(s, axis=0, keepdims=True)

        for m in range(2):
            a_piece(0, m)
            m_old = m_sc[m]
            m_new = jnp.maximum(jnp.maximum(m_old, pm_buf[slot_b, m, 0]), pm_buf[slot_b, m, 1])
            alpha = jnp.exp2(m_old - m_new)
            l_new = alpha * l_sc[m]
            acc = alpha * acc_sc[m]
            for j in range(2):
                if j == 1:
                    a_piece(1, m)
                p = jnp.exp2(s_buf[slot_b, m, j] - m_new)
                l_new = l_new + jnp.sum(p, axis=0, keepdims=True)
                acc = acc + jnp.dot(vt_ref[0, kb_b + j], p.astype(BF16), preferred_element_type=F32)
            l_sc[m] = l_new
            acc_sc[m] = acc
            m_sc[m] = m_new

    far2 = [None, None]
    nfu = jnp.maximum(qi - 1, 0) // 2
    peel = nfu % 2

    @pl.when(peel == 1)
    def _():
        stage_a(0, 0)
        stage_b(0, 0, far2)

    stage_a(0, 2 * peel)

    def far_body(v, c):
        u = peel + 2 * v
        far_pair(1, 2 * u + 2, 0, 2 * u)
        far_pair(0, 2 * u + 4, 1, 2 * u + 2)
        return c

    lax.fori_loop(0, (nfu - peel) // 2, far_body, 0)
    kb0 = 2 * nfu

    @pl.when(qi % 2 == 1)
    def _():
        stage_b(0, kb0, [bp_ref, bd_ref])

    @pl.when((qi % 2 == 0) & (qi >= 2))
    def _():
        stage_a(1, kb0 + 2)
        stage_b(0, kb0, [None, bp_ref])
        stage_b(1, kb0 + 2, [bd_ref])

    @pl.when(qi == 0)
    def _():
        stage_b(0, kb0, [bd_ref])

    o = acc_sc[0] / l_sc[0] - lam_ref[0] * (acc_sc[1] / l_sc[1])
    ot = o.T
    o_ref[...] = (_rms(ot, gs_ref[...]) * (1.0 - LAMBDA_INIT)).astype(BF16)


def _attention(lam, q, k, vt, bias_d, bias_p, g_subln):
    T = q.shape[1]
    nq = T // TQ
    nkb = T // TK
    return pl.pallas_call(
        _attn_kernel,
        grid=(A_HEADS, nq),
        in_specs=[
            pl.BlockSpec(memory_space=pltpu.SMEM),
            pl.BlockSpec((1, TQ, 128), lambda h, i: (h, i, 0)),
            pl.BlockSpec((1, nkb, TK, 128), lambda h, i: (h, 0, 0, 0)),
            pl.BlockSpec((1, nkb, 128, TK), lambda h, i: (h, 0, 0, 0)),
            pl.BlockSpec((1, TK, TQ), lambda h, i: (h, 0, 0)),
            pl.BlockSpec((1, TK, TQ), lambda h, i: (h, 0, 0)),
            pl.BlockSpec((1, 128), lambda h, i: (0, 0)),
        ],
        out_specs=pl.BlockSpec((TQ, 128), lambda h, i: (i, h)),
        out_shape=jax.ShapeDtypeStruct((T, A_HEADS * A_V_DIM), BF16),
        scratch_shapes=[
            pltpu.VMEM((2, 1, TQ), F32),
            pltpu.VMEM((2, 1, TQ), F32),
            pltpu.VMEM((2, 128, TQ), F32),
            pltpu.VMEM((2, 2, 2, TK, TQ), F32),
            pltpu.VMEM((2, 2, 2, 1, TQ), F32),
        ],
        compiler_params=pltpu.CompilerParams(
            dimension_semantics=("arbitrary", "arbitrary"), vmem_limit_bytes=_vmem_limit(48 << 20)),
        name="attn",
    )(lam, q, k, vt, bias_d, bias_p, g_subln)


def _t5_bucket(rel):
    half = REL_BUCKETS // 2
    max_exact = half // 2
    ret = jnp.where(rel > 0, half, 0)
    n = jnp.abs(rel)
    large = max_exact + (jnp.log(jnp.maximum(n, 1).astype(F32) / max_exact)
                         / math.log(REL_MAX_DIST / max_exact) * (half - max_exact)).astype(I32)
    large = jnp.minimum(large, half - 1)
    return ret + jnp.where(n < max_exact, n, large)


def _attn_bias_tables(rel_bias):
    assert TK >= REL_MAX_DIST and TK == TQ and TK % CHUNK == 0
    kk = jnp.arange(TK, dtype=I32)[:, None]
    qq = jnp.arange(TQ, dtype=I32)[None, :]
    rb = rel_bias.astype(F32)
    far = rb[_t5_bucket(jnp.full((1,), -REL_MAX_DIST, I32))[0]]
    table = ((rb - far[None, :]) * LOG2E).T

    def lookup(rel):
        onehot = (_t5_bucket(rel).reshape(1, -1) == jnp.arange(REL_BUCKETS, dtype=I32)[:, None]).astype(F32)
        return jnp.dot(table, onehot, precision=lax.Precision.HIGHEST).reshape(A_HEADS, TK, TQ)

    bd = jnp.where((kk // CHUNK <= qq // CHUNK)[None], lookup(kk - qq), NEG_BIG)
    bp = lookup(kk - TK - qq)
    return bd, bp


def _gdn_kernel(x_ref, zb_ref, ab_ref, abt_ref, cw_ref, alc_ref, dtc_ref, alr_ref, dtr_ref, gob_ref,
                o_ref, xbuf, s_sc):
    i = pl.program_id(0)
    TC = TC_GDN
    NCH = TC // CHUNK
    HB = B_HEADS * CHUNK

    @pl.when(i == 0)
    def _():
        xbuf[0:8, :] = jnp.zeros((8, 1536), F32)
        s_sc[...] = jnp.zeros(s_sc.shape, F32)

    xbuf[8:8 + TC, :] = x_ref[...]
    y = cw_ref[3:4, :] * xbuf[8:8 + TC, :]
    for d in range(1, CONV_WIDTH):
        y = y + cw_ref[3 - d:4 - d, :] * xbuf[8 - d:8 - d + TC, :]
    xbuf[0:8, :] = x_ref[TC - 8:TC, :]
    a = y * _sigmoid(y)

    def l2n(t):
        return t * lax.rsqrt(jnp.sum(t * t, axis=-1, keepdims=True) + EPS)

    qh = [l2n(a[:, h * 128:(h + 1) * 128]) * (B_DIM ** -0.5) for h in range(B_HEADS)]
    kh = [l2n(a[:, 512 + h * 128:512 + (h + 1) * 128]) for h in range(B_HEADS)]
    vh = [a[:, 1024 + h * 128:1024 + (h + 1) * 128] for h in range(B_HEADS)]

    def softplus(t):
        return jnp.maximum(t, 0.0) + jnp.log(1.0 + jnp.exp(-jnp.abs(t)))

    ab = ab_ref[...]
    g_col = -jnp.exp(alc_ref[...]) * softplus(ab + dtc_ref[...])
    beta_col = _sigmoid(ab)
    g_row = -jnp.exp(alr_ref[...]) * softplus(abt_ref[...] + dtr_ref[...])

    rt = lax.broadcasted_iota(I32, (TC, TC), 0)
    ct = lax.broadcasted_iota(I32, (TC, TC), 1)
    same_chunk = (rt // CHUNK) == (ct // CHUNK)
    tril = jnp.where(same_chunk & (rt >= ct), 1.0, 0.0).astype(F32)
    triu = jnp.where(same_chunk & (rt <= ct), 1.0, 0.0).astype(F32)
    gc_col = jnp.dot(tril, g_col, precision=lax.Precision.HIGHEST, preferred_element_type=F32)
    gc_row = jnp.dot(g_row, triu, precision=lax.Precision.HIGHEST, preferred_element_type=F32)

    ri = lax.broadcasted_iota(I32, (HB, HB), 0)
    ci = lax.broadcasted_iota(I32, (HB, HB), 1)
    same_head = (ri // CHUNK) == (ci // CHUNK)
    incl = same_head & (ri >= ci)
    strict = same_head & (ri > ci)
    eye = jnp.where(ri == ci, 1.0, 0.0).astype(F32)

    def level_mask(s):
        return ((ri // (2 * s)) == (ci // (2 * s))) & (((ri // s) % 2) == 1) & (((ci // s) % 2) == 0)

    cat0 = lambda parts: jnp.concatenate(parts, axis=0)
    heads = range(B_HEADS)
    chunks = range(NCH)

    Kc, Qc, glc, gccc, Lc, QKc, invc, rhsc = [], [], [], [], [], [], [], []
    for c in chunks:
        rs = slice(c * CHUNK, (c + 1) * CHUNK)
        last = slice(c * CHUNK + CHUNK - 1, (c + 1) * CHUNK)
        K = cat0([kh[h][rs] for h in heads])
        Q = cat0([qh[h][rs] for h in heads])
        V = cat0([vh[h][rs] for h in heads])
        beta = cat0([beta_col[rs, 4 + h:5 + h] for h in heads])
        gcc = cat0([gc_col[rs, h:h + 1] for h in heads])
        gl = cat0([jnp.broadcast_to(gc_col[last, h:h + 1], (CHUNK, 1)) for h in heads])
        gcr = jnp.concatenate([gc_row[h:h + 1, rs] for h in heads], axis=1)
        Kb = K * beta
        e = jnp.exp(jnp.where(incl, gcc - gcr, 0.0))
        L = lax.dot_general(Kb, K, _NT, preferred_element_type=F32) * jnp.where(strict, e, 0.0)
        QKc.append(lax.dot_general(Q, K, _NT, preferred_element_type=F32) * jnp.where(incl, e, 0.0))
        rhsc.append(jnp.concatenate([V * beta, Kb * jnp.exp(gcc)], axis=1))
        invc.append(eye - jnp.where(level_mask(1), L, 0.0))
        Kc.append(K); Qc.append(Q); glc.append(gl); gccc.append(gcc); Lc.append(L)

    for s in (2, 4, 8, 16, 32):
        msk = level_mask(s)
        ts = [jnp.dot(invc[c], jnp.where(msk, Lc[c], 0.0), preferred_element_type=F32) for c in chunks]
        invc = [invc[c] - jnp.dot(ts[c], invc[c], preferred_element_type=F32) for c in chunks]
    solc = [jnp.dot(invc[c], rhsc[c], preferred_element_type=F32) for c in chunks]

    for c in chunks:
        rs = slice(c * CHUNK, (c + 1) * CHUNK)
        U = solc[c][:, :B_DIM]
        W = solc[c][:, B_DIM:]
        Qd = Qc[c] * jnp.exp(gccc[c])
        Kd = Kc[c] * jnp.exp(glc[c] - gccc[c])
        vnew = []
        ost = []
        for h in heads:
            hs = slice(h * CHUNK, (h + 1) * CHUNK)
            S = s_sc[h]
            vn = U[hs] - jnp.dot(W[hs], S, preferred_element_type=F32)
            ost.append(jnp.dot(Qd[hs], S, preferred_element_type=F32))
            s_sc[h] = S * jnp.exp(glc[c][h * CHUNK:h * CHUNK + 1, :]) + lax.dot_general(
                Kd[hs], vn, _TN, preferred_element_type=F32)
            vnew.append(vn)
        O = cat0(ost) + jnp.dot(QKc[c], cat0(vnew), preferred_element_type=F32)
        for h in heads:
            oh = _rms(O[h * CHUNK:(h + 1) * CHUNK], gob_ref[...])
            z = zb_ref[rs, h * 128:(h + 1) * 128]
            o_ref[rs, h * 128:(h + 1) * 128] = (oh * (z * _sigmoid(z))).astype(BF16)


def _gdn(qkvb, zb, ab, abt, conv_w8, alc, dtc, alr, dtr, gob):
    T = qkvb.shape[0]
    TC = TC_GDN
    full = lambda shape: pl.BlockSpec(shape, lambda i: (0,) * len(shape))
    return pl.pallas_call(
        _gdn_kernel,
        grid=(T // TC,),
        in_specs=[
            pl.BlockSpec((TC, 1536), lambda i: (i, 0)),
            pl.BlockSpec((TC, 512), lambda i: (i, 0)),
            pl.BlockSpec((TC, 128), lambda i: (i, 0)),
            pl.BlockSpec((16, TC), lambda i: (0, i)),
            full((8, 1536)), full((1, 128)), full((1, 128)), full((16, TC)), full((16, TC)), full((1, 128)),
        ],
        out_specs=pl.BlockSpec((TC, 512), lambda i: (i, 0)),
        out_shape=jax.ShapeDtypeStruct((T, 512), BF16),
        scratch_shapes=[pltpu.VMEM((TC + 8, 1536), F32), pltpu.VMEM((B_HEADS, B_DIM, B_DIM), F32)],
        compiler_params=pltpu.CompilerParams(
            dimension_semantics=("arbitrary",), vmem_limit_bytes=_vmem_limit(48 << 20)),
        name="gdn",
    )(qkvb, zb, ab, abt, conv_w8, alc, dtc, alr, dtr, gob)


def _memkv_kernel(m_ref, g_ref, w_ref, k_ref, v_ref):
    hm = _rms(m_ref[...], g_ref[...]).astype(BF16)
    kv = jnp.dot(hm, w_ref[...], preferred_element_type=F32)
    k_ref[...] = kv[:, :D_MODEL].astype(BF16)
    v_ref[...] = kv[:, D_MODEL:].astype(BF16)


def _memkv(mem2d, g_mem, w_kv):
    n = mem2d.shape[0]
    return pl.pallas_call(
        _memkv_kernel,
        out_shape=(jax.ShapeDtypeStruct((n, D_MODEL), BF16), jax.ShapeDtypeStruct((n, D_MODEL), BF16)),
        compiler_params=pltpu.CompilerParams(vmem_limit_bytes=_vmem_limit(32 << 20)),
        name="memkv",
    )(mem2d, g_mem, w_kv)


def _mix_kernel(x_ref, oa_ref, ob_ref, gates_ref, wua_ref, wub_ref, wout_ref, gx_ref, wq_ref, kx_ref, vx_ref,
                wo_ref, gm_ref, wr_ref, br_ref, x2t_ref, hmt_ref, idx_ref, wts_ref):
    NSUB = 2
    TS = TM_MIX // NSUB
    subs = range(NSUB)
    rows = [slice(sub * TS, (sub + 1) * TS) for sub in subs]
    dotf = functools.partial(jnp.dot, preferred_element_type=F32)

    ma = [dotf(oa_ref[rows[g], :], wua_ref[...]) for g in subs]
    mb = [dotf(ob_ref[rows[g], :], wub_ref[...]) for g in subs]
    merged = [(_sigmoid(gates_ref[rows[g], :D_MODEL]) * ma[g]
               + _sigmoid(gates_ref[rows[g], D_MODEL:]) * mb[g]).astype(BF16) for g in subs]
    x1 = [x_ref[rows[g], :] + dotf(merged[g], wout_ref[...]) for g in subs]
    hx = [_rms(x1[g], gx_ref[...]).astype(BF16) for g in subs]

    heads = [[] for _ in subs]
    for h in range(X_HEADS):
        cs = slice(h * X_HEAD_DIM, (h + 1) * X_HEAD_DIM)
        qh = [dotf(hx[g], wq_ref[:, cs]).astype(BF16) for g in subs]
        s = [lax.dot_general(qh[g], kx_ref[:, cs], _NT, preferred_element_type=F32) * (X_HEAD_DIM ** -0.5)
             for g in subs]
        for g in subs:
            sg = s[g] - jnp.max(s[g], axis=-1, keepdims=True)
            p = jnp.exp(sg)
            p = p / jnp.sum(p, axis=-1, keepdims=True)
            heads[g].append(dotf(p.astype(BF16), vx_ref[:, cs]).astype(BF16))
    x2 = [x1[g] + dotf(jnp.concatenate(heads[g], axis=1), wo_ref[...]) for g in subs]
    hm = [_rms(x2[g], gm_ref[...]) for g in subs]

    for g in subs:
        for c in range(D_MODEL // 128):
            x2t_ref[pl.ds(g * TS * 8 + c, TS, stride=8), :] = x2[g][:, c * 128:(c + 1) * 128]
            hmt_ref[pl.ds(g * TS * 8 + c, TS, stride=8), :] = hm[g][:, c * 128:(c + 1) * 128]

        logits = lax.dot_general(wr_ref[...], hm[g], _NT, precision=lax.Precision.HIGHEST,
                                 preferred_element_type=F32) + br_ref[:, 0:1]
        eidx = lax.broadcasted_iota(I32, logits.shape, 0)
        vals, idxs = [], []
        cur = logits
        for _ in range(TOP_K):
            mx = jnp.max(cur, axis=0, keepdims=True)
            ix = jnp.min(jnp.where(cur == mx, eidx, N_EXPERTS), axis=0, keepdims=True)
            vals.append(mx)
            idxs.append(ix)
            cur = jnp.where(eidx == ix, -jnp.inf, cur)
        ex = [jnp.exp(v - vals[0]) for v in vals]
        den = ex[0] + ex[1] + ex[2] + ex[3]
        idx_ref[:, rows[g]] = jnp.concatenate(idxs, axis=0)
        wts_ref[:, rows[g]] = jnp.concatenate([e / den for e in ex], axis=0)


def _mix(x2d, oa, ob, gates, wua, wub, wout, gx, wq, kx, vx, wo, gm, wr_t, br):
    T = x2d.shape[0]
    TM = TM_MIX
    full = lambda a: pl.BlockSpec(a.shape, lambda i: (0,) * a.ndim)
    out_shape = (
        jax.ShapeDtypeStruct((T * 8, 128), F32),
        jax.ShapeDtypeStruct((T * 8, 128), F32),
        jax.ShapeDtypeStruct((TOP_K, T), I32),
        jax.ShapeDtypeStruct((TOP_K, T), F32),
    )
    return pl.pallas_call(
        _mix_kernel,
        grid=(T // TM,),
        in_specs=[
            pl.BlockSpec((TM, D_MODEL), lambda i: (i, 0)),
            pl.BlockSpec((TM, 512), lambda i: (i, 0)),
            pl.BlockSpec((TM, 512), lambda i: (i, 0)),
            pl.BlockSpec((TM, 2048), lambda i: (i, 0)),
            full(wua), full(wub), full(wout), full(gx), full(wq), full(kx), full(vx), full(wo), full(gm),
            full(wr_t), full(br),
        ],
        out_specs=(
            pl.BlockSpec((TM * 8, 128), lambda i: (i, 0)),
            pl.BlockSpec((TM * 8, 128), lambda i: (i, 0)),
            pl.BlockSpec((TOP_K, TM), lambda i: (0, i)),
            pl.BlockSpec((TOP_K, TM), lambda i: (0, i)),
        ),
        out_shape=out_shape,
        compiler_params=pltpu.CompilerParams(
            dimension_semantics=("arbitrary",), vmem_limit_bytes=_vmem_limit(56 << 20)),
        name="mix",
    )(x2d, oa, ob, gates, wua, wub, wout, gx, wq, kx, vx, wo, gm, wr_t, br)


def _moe_kernel(blk_e_ref, first_ref, nxt_ref, slot_ref, nb_ref,
                tok_ref, wt_ref, x2t_hbm, hmt_ref, w1_hbm, w2_hbm, b1_ref, b2_ref, gf_ref,
                out_ref, yacc, xs_a, xs_b, yst_a, yst_b, w1buf, w2buf, wsem, xsem):
    j = pl.program_id(0)
    TT = TT_MOE
    RB = RB_MOE
    NG = D_MODEL // 128
    nbj = nb_ref[j]

    x2_copy = pltpu.make_async_copy(
        x2t_hbm.at[pl.ds(pl.multiple_of(j * TT * 8, 8), TT * 8), :], yacc.at[pl.ds(0, TT * 8), :], xsem.at[0])
    x2_copy.start()

    def w_copies(e, slot):
        cps = []
        for c in range(MOE_W_CHUNKS):
            for k, (src, dst) in enumerate(((w1_hbm, w1buf), (w2_hbm, w2buf))):
                rows = pl.ds(c * (src.shape[1] // MOE_W_CHUNKS), src.shape[1] // MOE_W_CHUNKS)
                cps.append(pltpu.make_async_copy(src.at[e, rows], dst.at[slot, rows], wsem.at[slot, k, c]))
        return cps

    for cp in w_copies(blk_e_ref[j, 0], slot_ref[j, 0]):
        cp.start()

    yacc[TT * 8:TT * 8 + 8, :] = jnp.zeros((8, 128), F32)
    yst_b[...] = jnp.zeros(yst_b.shape, F32)

    def gather(b, xs):
        for r in range(RB):
            t8 = jnp.minimum(tok_ref[0, 0, b * RB + r], (TT - 1) * 8)
            xs[r * 8:(r + 1) * 8, :] = hmt_ref[pl.ds(pl.multiple_of(t8, 8), 8), :]

    def mlp(xs, yst, e, slot):
        xb = jnp.concatenate([xs[pl.ds(g, RB, stride=8), :] for g in range(NG)], axis=1).astype(BF16)
        hid = jnp.dot(xb, w1buf[slot], preferred_element_type=F32) + b1_ref[e]
        glu = jnp.minimum(hid[:, :D_FF], SWIGLU_LIMIT)
        lin = jnp.clip(hid[:, D_FF:], -SWIGLU_LIMIT, SWIGLU_LIMIT)
        act = glu * _sigmoid(SWIGLU_ALPHA * glu) * (lin + 1.0)
        ys = jnp.dot(act.astype(BF16), w2buf[slot], preferred_element_type=F32) + b2_ref[e]
        for g in range(NG):
            yst[pl.ds(g, RB, stride=8), :] = ys[:, g * 128:(g + 1) * 128]

    def scatter(b, yst):
        for g0 in range(0, RB, 8):
            dsts, vals = [], []
            for r in range(g0, g0 + 8):
                dst = pl.ds(pl.multiple_of(tok_ref[0, 0, b * RB + r], 8), 8)
                dsts.append(dst)
                vals.append(yacc[dst, :] + wt_ref[0, 0, b * RB + r] * yst[r * 8:(r + 1) * 8, :])
            for dst, val in zip(dsts, vals):
                yacc[dst, :] = val

    def step(b, xs_cur, yst_cur, xs_next, yst_prev):
        @pl.when(b < nbj)
        def _():
            e = blk_e_ref[j, b]
            slot = slot_ref[j, b]

            @pl.when(first_ref[j, b] == 1)
            def _():
                for cp in w_copies(e, slot):
                    cp.wait()
                nx = nxt_ref[j, b]

                @pl.when(nx >= 0)
                def _():
                    for cp in w_copies(nx, 1 - slot):
                        cp.start()

            gather(b + 1, xs_next)
            mlp(xs_cur, yst_cur, e, slot)
            scatter(jnp.maximum(b - 1, 0), yst_prev)

    gather(0, xs_a)
    x2_copy.wait()

    def pair_body(pp, c):
        step(2 * pp, xs_a, yst_a, xs_b, yst_b)
        step(2 * pp + 1, xs_b, yst_b, xs_a, yst_a)
        return c

    lax.fori_loop(0, (nbj + 1) // 2, pair_body, 0)

    @pl.when(nbj % 2 == 1)
    def _():
        scatter(nbj - 1, yst_a)

    @pl.when(nbj % 2 == 0)
    def _():
        scatter(nbj - 1, yst_b)

    RC = 256

    def fin(ci, c):
        base = pl.multiple_of(ci * RC * 8, 8)
        yv = jnp.concatenate([yacc[pl.ds(base + g, RC, stride=8), :] for g in range(NG)], axis=1)
        out_ref[pl.ds(pl.multiple_of(ci * RC, 8), RC), :] = _rms(yv, gf_ref[...])
        return c

    lax.fori_loop(0, TT // RC, fin, 0)


def _moe_nb_max():
    return TOP_K * TT_MOE // RB_MOE + N_EXPERTS


def _moe(lists, x2t, hmt, w1, b1, w2, b2, g_final):
    blk_e, first, nxt, slot, nb, tok, wt = lists
    T = x2t.shape[0] // 8
    TT = TT_MOE
    nt = T // TT
    LP = tok.shape[-1]
    smem_row = pl.BlockSpec((1, 1, LP), lambda j, *_: (j, 0, 0), memory_space=pltpu.SMEM)
    whole = lambda a: pl.BlockSpec(a.shape, lambda j, *_: (0,) * a.ndim)
    grid_spec = pltpu.PrefetchScalarGridSpec(
        num_scalar_prefetch=5,
        grid=(nt,),
        in_specs=[
            smem_row, smem_row,
            pl.BlockSpec(memory_space=pl.ANY),
            pl.BlockSpec((TT * 8, 128), lambda j, *_: (j, 0)),
            pl.BlockSpec(memory_space=pl.ANY), pl.BlockSpec(memory_space=pl.ANY),
            whole(b1), whole(b2), whole(g_final),
        ],
        out_specs=pl.BlockSpec((TT, D_MODEL), lambda j, *_: (j, 0), pipeline_mode=pl.Buffered(1)),
        scratch_shapes=[
            pltpu.VMEM((TT * 8 + 8, 128), F32),
            pltpu.VMEM((RB_MOE * 8, 128), F32), pltpu.VMEM((RB_MOE * 8, 128), F32),
            pltpu.VMEM((RB_MOE * 8, 128), F32), pltpu.VMEM((RB_MOE * 8, 128), F32),
            pltpu.VMEM((2, D_MODEL, 2 * D_FF), BF16),
            pltpu.VMEM((2, D_FF, D_MODEL), BF16),
            pltpu.SemaphoreType.DMA((2, 2, MOE_W_CHUNKS)),
            pltpu.SemaphoreType.DMA((1,)),
        ],
    )
    return pl.pallas_call(
        _moe_kernel,
        grid_spec=grid_spec,
        out_shape=jax.ShapeDtypeStruct((T, D_MODEL), F32),
        compiler_params=pltpu.CompilerParams(
            dimension_semantics=("arbitrary",), vmem_limit_bytes=_vmem_limit(60 << 20)),
        name="moe",
    )(blk_e, first, nxt, slot, nb, tok, wt, x2t, hmt, w1, w2, b1, b2, g_final)


def _moe_lists(idx, wts):
    T = idx.shape[1]
    TT, RB = TT_MOE, RB_MOE
    nt = T // TT
    NB = _moe_nb_max()
    A = TOP_K * TT
    e_tile = idx.reshape(TOP_K, nt, TT).transpose(1, 0, 2).reshape(nt, A)
    w_tile = wts.reshape(TOP_K, nt, TT).transpose(1, 0, 2).reshape(nt, A)
    tok_local = jnp.tile(jnp.arange(TT, dtype=I32), TOP_K)
    order = jnp.argsort(e_tile, axis=1, stable=True)
    stok = jnp.take_along_axis(jnp.broadcast_to(tok_local, e_tile.shape), order, axis=1)
    swt = jnp.take_along_axis(w_tile, order, axis=1)

    experts = jnp.arange(N_EXPERTS, dtype=I32)
    counts = jnp.sum((e_tile[:, :, None] == experts[None, None, :]).astype(I32), axis=1)
    start = jnp.cumsum(counts, axis=1) - counts
    nblk = (counts + RB - 1) // RB
    blk_end = jnp.cumsum(nblk, axis=1)
    blk_start = blk_end - nblk
    nb = blk_end[:, -1]

    b = jnp.arange(NB, dtype=I32)
    blk_e = jnp.minimum(jnp.sum((blk_end[:, :, None] <= b[None, None, :]).astype(I32), axis=1), N_EXPERTS - 1)
    onehot = (blk_e[:, :, None] == experts[None, None, :]).astype(I32)
    per_block = lambda v: jnp.sum(onehot * v[:, None, :], axis=2)
    present = (nblk > 0).astype(I32)
    ordinal = jnp.cumsum(present, axis=1) - present
    later = (experts[None, :] > experts[:, None])[None] & (present[:, None, :] > 0)
    nxt_e = jnp.min(jnp.where(later, experts[None, None, :], N_EXPERTS), axis=2)
    nxt_e = jnp.where(nxt_e >= N_EXPERTS, -1, nxt_e)
    bs_b = per_block(blk_start)
    first = ((b[None, :] == bs_b) & (b[None, :] < nb[:, None])).astype(I32)
    slot = per_block(ordinal) % 2
    nxt = per_block(nxt_e)

    r = jnp.arange(RB, dtype=I32)[None, None, :]
    off = (b[None, :] - bs_b)[:, :, None] * RB + r
    valid = (off < per_block(counts)[:, :, None]) & (b[None, :, None] < nb[:, None, None])
    src = jnp.clip(per_block(start)[:, :, None] + off, 0, A - 1).reshape(nt, NB * RB)
    valid = valid.reshape(nt, NB * RB)
    tok = jnp.where(valid, jnp.take_along_axis(stok, src, axis=1), TT)
    wt = jnp.where(valid, jnp.take_along_axis(swt, src, axis=1), 0.0)
    tok = jnp.pad(tok, ((0, 0), (0, RB)), constant_values=TT)
    wt = jnp.pad(wt, ((0, 0), (0, RB)))
    return blk_e, first, nxt, slot, nb, (tok * 8)[:, None, :], wt[:, None, :]


def kernel(x, mem, g_mix, w_in, rel_bias, lambda_q1, lambda_k1, lambda_q2, lambda_k2, g_subln, conv_w, a_log,
           dt_bias, g_out_b, w_up_a, w_up_b, w_out, g_xattn, g_mem, w_q_x, w_kv_x, w_o_x, g_moe, w_router,
           b_router, w_mlp1, b_mlp1, w_mlp2, b_mlp2, g_final):
    B_, S_, _ = x.shape
    assert B_ == 1 and S_ % TT_MOE == 0 and x.dtype == F32
    l = 0
    x2d = x.reshape(S_, D_MODEL)
    row = lambda v: v.reshape(1, -1).astype(F32)

    wi = w_in[l]
    ab_cols = jnp.pad(wi[:, 3584:3592], ((0, 0), (0, 120)))
    w_main = jnp.concatenate(
        [wi[:, 0:1024], wi[:, 1536:3072], wi[:, 3072:3584], wi[:, 3592:5640], ab_cols], axis=1).astype(BF16)
    w_vt = wi[:, 1024:1536].T.astype(BF16)
    w_abt = jnp.pad(wi[:, 3584:3592].T, ((0, 8), (0, 0))).astype(BF16)

    q, k, vt, qkvb, zb, gates, ab, abt = _inproj(x2d, row(g_mix[l]), w_main, w_vt, w_abt)

    lam = (jnp.exp(jnp.sum(lambda_q1[l].astype(F32) * lambda_k1[l].astype(F32)))
           - jnp.exp(jnp.sum(lambda_q2[l].astype(F32) * lambda_k2[l].astype(F32))) + LAMBDA_INIT).reshape(1)
    bias_d, bias_p = _attn_bias_tables(rel_bias)
    oa = _attention(lam, q, k, vt, bias_d, bias_p, row(g_subln[l]))

    lane_pad = lambda v: jnp.pad(v.astype(F32), (0, 128 - v.shape[0])).reshape(1, 128)
    row_bcast = lambda v: jnp.broadcast_to(jnp.pad(v.astype(F32), (0, 16 - v.shape[0]))[:, None], (16, TC_GDN))
    ob = _gdn(qkvb, zb, ab, abt, jnp.pad(conv_w[l].astype(F32), ((0, 4), (0, 0))),
              lane_pad(a_log[l]), lane_pad(dt_bias[l]), row_bcast(a_log[l]), row_bcast(dt_bias[l]),
              row(g_out_b[l]))

    kx, vx = _memkv(mem.reshape(-1, D_MODEL), row(g_mem[l]), w_kv_x[l].astype(BF16))
    br = jnp.broadcast_to(b_router[l].astype(F32)[:, None], (N_EXPERTS, 128))
    x2t, hmt, idx, wts = _mix(
        x2d, oa, ob, gates, w_up_a[l].astype(BF16), w_up_b[l].astype(BF16), w_out[l].astype(BF16),
        row(g_xattn[l]), w_q_x[l].astype(BF16), kx, vx, w_o_x[l].astype(BF16), row(g_moe[l]),
        w_router[l].T.astype(F32), br)

    out = _moe(_moe_lists(idx, wts), x2t, hmt, w_mlp1[l].astype(BF16), b_mlp1[l].astype(F32)[:, None, :],
               w_mlp2[l].astype(BF16), b_mlp2[l].astype(F32)[:, None, :], row(g_final))
    return out.reshape(B_, S_, D_MODEL)
```

```python
import functools
import math

import jax
import jax.numpy as jnp
from jax import lax
from jax.experimental import pallas as pl
from jax.experimental.pallas import tpu as pltpu

F32 = jnp.float32
BF16 = jnp.bfloat16
I32 = jnp.int32

D_MODEL = 1024
CHUNK = 64
EPS = 1e-6
A_HEADS = 4
A_QK_DIM = 64
A_V_DIM = 128
REL_BUCKETS = 32
REL_MAX_DIST = 128
B_HEADS = 4
B_DIM = 128
CONV_WIDTH = 4
X_HEADS = 4
X_HEAD_DIM = 256
N_EXPERTS = 32
TOP_K = 4
D_FF = 1024
SWIGLU_LIMIT = 7.0
SWIGLU_ALPHA = 1.702
LAMBDA_INIT = 0.8 - 0.6 * math.exp(-0.3 * 0)

LOG2E = 1.4426950408889634
NEG_BIG = -1e30

V7X_LANES = 128
V7X_SUBLANES = 8
V7X_VMEM_BYTES = 64 * 1024 * 1024

TM_PROJ = 512
TQ = 512
TK = 512
TC_GDN = 256
TM_MIX = 512
TT_MOE = 2048
RB_MOE = 128
MOE_W_CHUNKS = 4
MOE_MLP_PIECES = 4

C_QA, C_KA, C_QKVB, C_ZB, C_GATE, C_AB, C_END = 0, 512, 1024, 2560, 3072, 5120, 5248

_NT = (((1,), (1,)), ((), ()))
_TN = (((0,), (0,)), ((), ()))


def _rms(x, g):
    return x * lax.rsqrt(jnp.mean(x * x, axis=-1, keepdims=True) + EPS) * g


def _sigmoid(x):
    return 1.0 / (1.0 + jnp.exp(-x))


def _vmem_limit(nbytes):
    return int(min(nbytes, V7X_VMEM_BYTES - 4 * 1024 * 1024))


def _inproj_kernel(x_ref, g_ref, w_ref, wvt_ref, wabt_ref,
                   q_ref, k_ref, vt_ref, qkvb_ref, zb_ref, gates_ref, ab_ref, abt_ref):
    h = _rms(x_ref[...], g_ref[...]).astype(BF16)

    def mm(c0, c1):
        return jnp.dot(h, w_ref[:, c0:c1], preferred_element_type=F32)

    nb = TM_PROJ // TK
    qa = mm(C_QA, C_KA) * (A_QK_DIM ** -0.5 * LOG2E)
    ka = mm(C_KA, C_QKVB)
    vt = lax.dot_general(wvt_ref[...], h, _NT, preferred_element_type=F32)
    for hh in range(A_HEADS):
        cs = slice(hh * 128, (hh + 1) * 128)
        q_ref[hh] = qa[:, cs].astype(BF16)
        for b in range(nb):
            rs = slice(b * TK, (b + 1) * TK)
            k_ref[hh, b] = ka[rs, cs].astype(BF16)
            vt_ref[hh, b] = vt[cs, rs].astype(BF16)
    for j in range(3):
        qkvb_ref[:, j * 512:(j + 1) * 512] = mm(C_QKVB + j * 512, C_QKVB + (j + 1) * 512)
    zb_ref[...] = mm(C_ZB, C_GATE)
    for j in range(4):
        gates_ref[:, j * 512:(j + 1) * 512] = mm(C_GATE + j * 512, C_GATE + (j + 1) * 512)
    ab_ref[...] = mm(C_AB, C_END)
    abt_ref[...] = lax.dot_general(wabt_ref[...], h, _NT, preferred_element_type=F32)


def _inproj(x2d, g_mix, w_main, w_vt, w_abt):
    T = x2d.shape[0]
    n = T // TM_PROJ
    nkb = T // TK
    nb = TM_PROJ // TK
    full = lambda shape: pl.BlockSpec(shape, lambda i: (0,) * len(shape))
    out_shape = (
        jax.ShapeDtypeStruct((A_HEADS, T, 128), BF16),
        jax.ShapeDtypeStruct((A_HEADS, nkb, TK, 128), BF16),
        jax.ShapeDtypeStruct((A_HEADS, nkb, 128, TK), BF16),
        jax.ShapeDtypeStruct((T, 1536), F32),
        jax.ShapeDtypeStruct((T, 512), F32),
        jax.ShapeDtypeStruct((T, 2048), F32),
        jax.ShapeDtypeStruct((T, 128), F32),
        jax.ShapeDtypeStruct((16, T), F32),
    )
    out_specs = (
        pl.BlockSpec((A_HEADS, TM_PROJ, 128), lambda i: (0, i, 0)),
        pl.BlockSpec((A_HEADS, nb, TK, 128), lambda i: (0, i, 0, 0)),
        pl.BlockSpec((A_HEADS, nb, 128, TK), lambda i: (0, i, 0, 0)),
        pl.BlockSpec((TM_PROJ, 1536), lambda i: (i, 0)),
        pl.BlockSpec((TM_PROJ, 512), lambda i: (i, 0)),
        pl.BlockSpec((TM_PROJ, 2048), lambda i: (i, 0)),
        pl.BlockSpec((TM_PROJ, 128), lambda i: (i, 0)),
        pl.BlockSpec((16, TM_PROJ), lambda i: (0, i)),
    )
    return pl.pallas_call(
        _inproj_kernel,
        grid=(n,),
        in_specs=[
            pl.BlockSpec((TM_PROJ, D_MODEL), lambda i: (i, 0)),
            full((1, D_MODEL)),
            full(w_main.shape),
            full(w_vt.shape),
            full(w_abt.shape),
        ],
        out_specs=out_specs,
        out_shape=out_shape,
        compiler_params=pltpu.CompilerParams(
            dimension_semantics=("arbitrary",), vmem_limit_bytes=_vmem_limit(56 << 20)),
        name="inproj",
    )(x2d, g_mix, w_main, w_vt, w_abt)


def _attn_kernel(lam_ref, q_ref, k_ref, vt_ref, bd_ref, bp_ref, gs_ref, o_ref, m_sc, l_sc, acc_sc, s_buf, pm_buf):
    qi = pl.program_id(1)
    q = q_ref[0]
    lane = lax.broadcasted_iota(I32, q.shape, 1)
    zero = jnp.zeros_like(q)
    qm = (jnp.where(lane < A_QK_DIM, q, zero), jnp.where(lane >= A_QK_DIM, q, zero))
    m_sc[...] = jnp.full(m_sc.shape, NEG_BIG, F32)
    l_sc[...] = jnp.zeros(l_sc.shape, F32)
    acc_sc[...] = jnp.zeros(acc_sc.shape, F32)

    nkb = k_ref.shape[1]

    def stage_a(slot, kb0):
        for j in range(2):
            kblk = k_ref[0, jnp.minimum(kb0 + j, nkb - 1)]
            for m in range(2):
                s = lax.dot_general(kblk, qm[m], _NT, preferred_element_type=F32)
                s_buf[slot, m, j] = s
                pm_buf[slot, m, j] = jnp.max(s, axis=0, keepdims=True)

    def stage_b(slot, kb0, biases):
        far = all(b is None for b in biases)
        for m in range(2):
            m_old = m_sc[m]
            m_new = m_old
            ss = []
            for j, bias in enumerate(biases):
                if far:
                    m_new = jnp.maximum(m_new, pm_buf[slot, m, j])
                else:
                    s = s_buf[slot, m, j]
                    s = s if bias is None else s + bias[0]
                    ss.append(s)
                    m_new = jnp.maximum(m_new, jnp.max(s, axis=0, keepdims=True))
            alpha = jnp.exp2(m_old - m_new)
            l_new = alpha * l_sc[m]
            acc = alpha * acc_sc[m]
            for j in range(len(biases)):
                s = s_buf[slot, m, j] if far else ss[j]
                p = jnp.exp2(s - m_new)
                l_new = l_new + jnp.sum(p, axis=0, keepdims=True)
                acc = acc + jnp.dot(vt_ref[0, kb0 + j], p.astype(BF16), preferred_element_type=F32)
            l_sc[m] = l_new
            acc_sc[m] = acc
            m_sc[m] = m_new

    def far_pair(slot_a, kb_a, slot_b, kb_b):
        def a_piece(j, m):
            kblk = k_ref[0, jnp.minimum(kb_a + j, nkb - 1)]
            s = lax.dot_general(kblk, qm[m], _NT, preferred_element_type=F32)
            s_buf[slot_a, m, j] = s
            pm_buf[slot_a, m, j] = jnp.max(s, axis=0, keepdims=True)

        for m in range(2):
            a_piece(0, m)
            m_old = m_sc[m]
            m_new = jnp.maximum(jnp.maximum(m_old, pm_buf[slot_b, m, 0]), pm_buf[slot_b, m, 1])
            alpha = jnp.exp2(m_old - m_new)
            l_new = alpha * l_sc[m]
            acc = alpha * acc_sc[m]
            for j in range(2):
                if j == 1:
                    a_piece(1, m)
                p = jnp.exp2(s_buf[slot_b, m, j] - m_new)
                l_new = l_new + jnp.sum(p, axis=0, keepdims=True)
                acc = acc + jnp.dot(vt_ref[0, kb_b + j], p.astype(BF16), preferred_element_type=F32)
            l_sc[m] = l_new
            acc_sc[m] = acc
            m_sc[m] = m_new

    far2 = [None, None]
    nfu = jnp.maximum(qi - 1, 0) // 2
    peel = nfu % 2

    @pl.when(peel == 1)
    def _():
        stage_a(0, 0)
        stage_b(0, 0, far2)

    stage_a(0, 2 * peel)

    def far_body(v, c):
        u = peel + 2 * v
        far_pair(1, 2 * u + 2, 0, 2 * u)
        far_pair(0, 2 * u + 4, 1, 2 * u + 2)
        return c

    lax.fori_loop(0, (nfu - peel) // 2, far_body, 0)
    kb0 = 2 * nfu

    @pl.when(qi % 2 == 1)
    def _():
        stage_b(0, kb0, [bp_ref, bd_ref])

    @pl.when((qi % 2 == 0) & (qi >= 2))
    def _():
        stage_a(1, kb0 + 2)
        stage_b(0, kb0, [None, bp_ref])
        stage_b(1, kb0 + 2, [bd_ref])

    @pl.when(qi == 0)
    def _():
        stage_b(0, kb0, [bd_ref])

    o = acc_sc[0] / l_sc[0] - lam_ref[0] * (acc_sc[1] / l_sc[1])
    ot = o.T
    o_ref[...] = (_rms(ot, gs_ref[...]) * (1.0 - LAMBDA_INIT)).astype(BF16)


def _attention(lam, q, k, vt, bias_d, bias_p, g_subln):
    T = q.shape[1]
    nq = T // TQ
    nkb = T // TK
    return pl.pallas_call(
        _attn_kernel,
        grid=(A_HEADS, nq),
        in_specs=[
            pl.BlockSpec(memory_space=pltpu.SMEM),
            pl.BlockSpec((1, TQ, 128), lambda h, i: (h, i, 0)),
            pl.BlockSpec((1, nkb, TK, 128), lambda h, i: (h, 0, 0, 0)),
            pl.BlockSpec((1, nkb, 128, TK), lambda h, i: (h, 0, 0, 0)),
            pl.BlockSpec((1, TK, TQ), lambda h, i: (h, 0, 0)),
            pl.BlockSpec((1, TK, TQ), lambda h, i: (h, 0, 0)),
            pl.BlockSpec((1, 128), lambda h, i: (0, 0)),
        ],
        out_specs=pl.BlockSpec((TQ, 128), lambda h, i: (i, h)),
        out_shape=jax.ShapeDtypeStruct((T, A_HEADS * A_V_DIM), BF16),
        scratch_shapes=[
            pltpu.VMEM((2, 1, TQ), F32),
            pltpu.VMEM((2, 1, TQ), F32),
            pltpu.VMEM((2, 128, TQ), F32),
            pltpu.VMEM((2, 2, 2, TK, TQ), F32),
            pltpu.VMEM((2, 2, 2, 1, TQ), F32),
        ],
        compiler_params=pltpu.CompilerParams(
            dimension_semantics=("arbitrary", "arbitrary"), vmem_limit_bytes=_vmem_limit(48 << 20)),
        name="attn",
    )(lam, q, k, vt, bias_d, bias_p, g_subln)


def _t5_bucket(rel):
    half = REL_BUCKETS // 2
    max_exact = half // 2
    ret = jnp.where(rel > 0, half, 0)
    n = jnp.abs(rel)
    large = max_exact + (jnp.log(jnp.maximum(n, 1).astype(F32) / max_exact)
                         / math.log(REL_MAX_DIST / max_exact) * (half - max_exact)).astype(I32)
    large = jnp.minimum(large, half - 1)
    return ret + jnp.where(n < max_exact, n, large)


def _attn_bias_tables(rel_bias):
    assert TK >= REL_MAX_DIST and TK == TQ and TK % CHUNK == 0
    kk = jnp.arange(TK, dtype=I32)[:, None]
    qq = jnp.arange(TQ, dtype=I32)[None, :]
    rb = rel_bias.astype(F32)
    far = rb[_t5_bucket(jnp.full((1,), -REL_MAX_DIST, I32))[0]]
    table = ((rb - far[None, :]) * LOG2E).T

    def lookup(rel):
        onehot = (_t5_bucket(rel).reshape(1, -1) == jnp.arange(REL_BUCKETS, dtype=I32)[:, None]).astype(F32)
        return jnp.dot(table, onehot, precision=lax.Precision.HIGHEST).reshape(A_HEADS, TK, TQ)

    bd = jnp.where((kk // CHUNK <= qq // CHUNK)[None], lookup(kk - qq), NEG_BIG)
    bp = lookup(kk - TK - qq)
    return bd, bp


def _gdn_kernel(x_ref, zb_ref, ab_ref, abt_ref, cw_ref, alc_ref, dtc_ref, alr_ref, dtr_ref, gob_ref,
                o_ref, xbuf, s_sc):
    i = pl.program_id(0)
    TC = TC_GDN
    NCH = TC // CHUNK
    HB = B_HEADS * CHUNK

    @pl.when(i == 0)
    def _():
        xbuf[0:8, :] = jnp.zeros((8, 1536), F32)
        s_sc[...] = jnp.zeros(s_sc.shape, F32)

    xbuf[8:8 + TC, :] = x_ref[...]
    y = cw_ref[3:4, :] * xbuf[8:8 + TC, :]
    for d in range(1, CONV_WIDTH):
        y = y + cw_ref[3 - d:4 - d, :] * xbuf[8 - d:8 - d + TC, :]
    xbuf[0:8, :] = x_ref[TC - 8:TC, :]
    a = y * _sigmoid(y)

    def l2n(t):
        return t * lax.rsqrt(jnp.sum(t * t, axis=-1, keepdims=True) + EPS)

    qh = [l2n(a[:, h * 128:(h + 1) * 128]) * (B_DIM ** -0.5) for h in range(B_HEADS)]
    kh = [l2n(a[:, 512 + h * 128:512 + (h + 1) * 128]) for h in range(B_HEADS)]
    vh = [a[:, 1024 + h * 128:1024 + (h + 1) * 128] for h in range(B_HEADS)]

    def softplus(t):
        return jnp.maximum(t, 0.0) + jnp.log(1.0 + jnp.exp(-jnp.abs(t)))

    ab = ab_ref[...]
    g_col = -jnp.exp(alc_ref[...]) * softplus(ab + dtc_ref[...])
    beta_col = _sigmoid(ab)
    g_row = -jnp.exp(alr_ref[...]) * softplus(abt_ref[...] + dtr_ref[...])

    rt = lax.broadcasted_iota(I32, (TC, TC), 0)
    ct = lax.broadcasted_iota(I32, (TC, TC), 1)
    same_chunk = (rt // CHUNK) == (ct // CHUNK)
    tril = jnp.where(same_chunk & (rt >= ct), 1.0, 0.0).astype(F32)
    triu = jnp.where(same_chunk & (rt <= ct), 1.0, 0.0).astype(F32)
    gc_col = jnp.dot(tril, g_col, precision=lax.Precision.HIGHEST, preferred_element_type=F32)
    gc_row = jnp.dot(g_row, triu, precision=lax.Precision.HIGHEST, preferred_element_type=F32)

    ri = lax.broadcasted_iota(I32, (HB, HB), 0)
    ci = lax.broadcasted_iota(I32, (HB, HB), 1)
    same_head = (ri // CHUNK) == (ci // CHUNK)
    incl = same_head & (ri >= ci)
    strict = same_head & (ri > ci)
    eye = jnp.where(ri == ci, 1.0, 0.0).astype(F32)

    def level_mask(s):
        return ((ri // (2 * s)) == (ci // (2 * s))) & (((ri // s) % 2) == 1) & (((ci // s) % 2) == 0)

    cat0 = lambda parts: jnp.concatenate(parts, axis=0)
    heads = range(B_HEADS)
    chunks = range(NCH)

    Kc, Qc, glc, gccc, Lc, QKc, invc, rhsc = [], [], [], [], [], [], [], []
    for c in chunks:
        rs = slice(c * CHUNK, (c + 1) * CHUNK)
        last = slice(c * CHUNK + CHUNK - 1, (c + 1) * CHUNK)
        K = cat0([kh[h][rs] for h in heads])
        Q = cat0([qh[h][rs] for h in heads])
        V = cat0([vh[h][rs] for h in heads])
        beta = cat0([beta_col[rs, 4 + h:5 + h] for h in heads])
        gcc = cat0([gc_col[rs, h:h + 1] for h in heads])
        gl = cat0([jnp.broadcast_to(gc_col[last, h:h + 1], (CHUNK, 1)) for h in heads])
        gcr = jnp.concatenate([gc_row[h:h + 1, rs] for h in heads], axis=1)
        Kb = K * beta
        e = jnp.exp(jnp.where(incl, gcc - gcr, 0.0))
        L = lax.dot_general(Kb, K, _NT, preferred_element_type=F32) * jnp.where(strict, e, 0.0)
        QKc.append(lax.dot_general(Q, K, _NT, preferred_element_type=F32) * jnp.where(incl, e, 0.0))
        rhsc.append(jnp.concatenate([V * beta, Kb * jnp.exp(gcc)], axis=1))
        invc.append(eye - jnp.where(level_mask(1), L, 0.0))
        Kc.append(K); Qc.append(Q); glc.append(gl); gccc.append(gcc); Lc.append(L)

    for s in (2, 4, 8, 16, 32):
        msk = level_mask(s)
        ts = [jnp.dot(invc[c], jnp.where(msk, Lc[c], 0.0), preferred_element_type=F32) for c in chunks]
        invc = [invc[c] - jnp.dot(ts[c], invc[c], preferred_element_type=F32) for c in chunks]
    solc = [jnp.dot(invc[c], rhsc[c], preferred_element_type=F32) for c in chunks]

    for c in chunks:
        rs = slice(c * CHUNK, (c + 1) * CHUNK)
        U = solc[c][:, :B_DIM]
        W = solc[c][:, B_DIM:]
        Qd = Qc[c] * jnp.exp(gccc[c])
        Kd = Kc[c] * jnp.exp(glc[c] - gccc[c])
        vnew = []
        ost = []
        for h in heads:
            hs = slice(h * CHUNK, (h + 1) * CHUNK)
            S = s_sc[h]
            vn = U[hs] - jnp.dot(W[hs], S, preferred_element_type=F32)
            ost.append(jnp.dot(Qd[hs], S, preferred_element_type=F32))
            s_sc[h] = S * jnp.exp(glc[c][h * CHUNK:h * CHUNK + 1, :]) + lax.dot_general(
                Kd[hs], vn, _TN, preferred_element_type=F32)
            vnew.append(vn)
        O = cat0(ost) + jnp.dot(QKc[c], cat0(vnew), preferred_element_type=F32)
        for h in heads:
            oh = _rms(O[h * CHUNK:(h + 1) * CHUNK], gob_ref[...])
            z = zb_ref[rs, h * 128:(h + 1) * 128]
            o_ref[rs, h * 128:(h + 1) * 128] = (oh * (z * _sigmoid(z))).astype(BF16)


def _gdn(qkvb, zb, ab, abt, conv_w8, alc, dtc, alr, dtr, gob):
    T = qkvb.shape[0]
    TC = TC_GDN
    full = lambda shape: pl.BlockSpec(shape, lambda i: (0,) * len(shape))
    return pl.pallas_call(
        _gdn_kernel,
        grid=(T // TC,),
        in_specs=[
            pl.BlockSpec((TC, 1536), lambda i: (i, 0)),
            pl.BlockSpec((TC, 512), lambda i: (i, 0)),
            pl.BlockSpec((TC, 128), lambda i: (i, 0)),
            pl.BlockSpec((16, TC), lambda i: (0, i)),
            full((8, 1536)), full((1, 128)), full((1, 128)), full((16, TC)), full((16, TC)), full((1, 128)),
        ],
        out_specs=pl.BlockSpec((TC, 512), lambda i: (i, 0)),
        out_shape=jax.ShapeDtypeStruct((T, 512), BF16),
        scratch_shapes=[pltpu.VMEM((TC + 8, 1536), F32), pltpu.VMEM((B_HEADS, B_DIM, B_DIM), F32)],
        compiler_params=pltpu.CompilerParams(
            dimension_semantics=("arbitrary",), vmem_limit_bytes=_vmem_limit(48 << 20)),
        name="gdn",
    )(qkvb, zb, ab, abt, conv_w8, alc, dtc, alr, dtr, gob)


def _memkv_kernel(m_ref, g_ref, w_ref, k_ref, v_ref):
    hm = _rms(m_ref[...], g_ref[...]).astype(BF16)
    kv = jnp.dot(hm, w_ref[...], preferred_element_type=F32)
    k_ref[...] = kv[:, :D_MODEL].astype(BF16)
    v_ref[...] = kv[:, D_MODEL:].astype(BF16)


def _memkv(mem2d, g_mem, w_kv):
    n = mem2d.shape[0]
    return pl.pallas_call(
        _memkv_kernel,
        out_shape=(jax.ShapeDtypeStruct((n, D_MODEL), BF16), jax.ShapeDtypeStruct((n, D_MODEL), BF16)),
        compiler_params=pltpu.CompilerParams(vmem_limit_bytes=_vmem_limit(32 << 20)),
        name="memkv",
    )(mem2d, g_mem, w_kv)


def _mix_kernel(x_ref, oa_ref, ob_ref, gates_ref, wua_ref, wub_ref, wout_ref, gx_ref, wq_ref, kx_ref, vx_ref,
                wo_ref, gm_ref, wr_ref, br_ref, x2t_ref, hmt_ref, idx_ref, wts_ref):
    NSUB = 2
    TS = TM_MIX // NSUB
    subs = range(NSUB)
    rows = [slice(sub * TS, (sub + 1) * TS) for sub in subs]
    dotf = functools.partial(jnp.dot, preferred_element_type=F32)

    ma = [dotf(oa_ref[rows[g], :], wua_ref[...]) for g in subs]
    mb = [dotf(ob_ref[rows[g], :], wub_ref[...]) for g in subs]
    merged = [(_sigmoid(gates_ref[rows[g], :D_MODEL]) * ma[g]
               + _sigmoid(gates_ref[rows[g], D_MODEL:]) * mb[g]).astype(BF16) for g in subs]
    x1 = [x_ref[rows[g], :] + dotf(merged[g], wout_ref[...]) for g in subs]
    hx = [_rms(x1[g], gx_ref[...]).astype(BF16) for g in subs]

    heads = [[] for _ in subs]
    for h in range(X_HEADS):
        cs = slice(h * X_HEAD_DIM, (h + 1) * X_HEAD_DIM)
        qh = [dotf(hx[g], wq_ref[:, cs]).astype(BF16) for g in subs]
        s = [lax.dot_general(qh[g], kx_ref[:, cs], _NT, preferred_element_type=F32) * (X_HEAD_DIM ** -0.5)
             for g in subs]
        for g in subs:
            sg = s[g] - jnp.max(s[g], axis=-1, keepdims=True)
            p = jnp.exp(sg)
            p = p / jnp.sum(p, axis=-1, keepdims=True)
            heads[g].append(dotf(p.astype(BF16), vx_ref[:, cs]).astype(BF16))
    x2 = [x1[g] + dotf(jnp.concatenate(heads[g], axis=1), wo_ref[...]) for g in subs]
    hm = [_rms(x2[g], gm_ref[...]) for g in subs]

    for g in subs:
        for c in range(D_MODEL // 128):
            x2t_ref[pl.ds(g * TS * 8 + c, TS, stride=8), :] = x2[g][:, c * 128:(c + 1) * 128]
            hmt_ref[pl.ds(g * TS * 8 + c, TS, stride=8), :] = hm[g][:, c * 128:(c + 1) * 128]

        logits = lax.dot_general(wr_ref[...], hm[g], _NT, precision=lax.Precision.HIGHEST,
                                 preferred_element_type=F32) + br_ref[:, 0:1]
        eidx = lax.broadcasted_iota(I32, logits.shape, 0)
        vals, idxs = [], []
        cur = logits
        for _ in range(TOP_K):
            mx = jnp.max(cur, axis=0, keepdims=True)
            ix = jnp.min(jnp.where(cur == mx, eidx, N_EXPERTS), axis=0, keepdims=True)
            vals.append(mx)
            idxs.append(ix)
            cur = jnp.where(eidx == ix, -jnp.inf, cur)
        ex = [jnp.exp(v - vals[0]) for v in vals]
        den = ex[0] + ex[1] + ex[2] + ex[3]
        idx_ref[:, rows[g]] = jnp.concatenate(idxs, axis=0)
        wts_ref[:, rows[g]] = jnp.concatenate([e / den for e in ex], axis=0)


def _mix(x2d, oa, ob, gates, wua, wub, wout, gx, wq, kx, vx, wo, gm, wr_t, br):
    T = x2d.shape[0]
    TM = TM_MIX
    full = lambda a: pl.BlockSpec(a.shape, lambda i: (0,) * a.ndim)
    out_shape = (
        jax.ShapeDtypeStruct((T * 8, 128), F32),
        jax.ShapeDtypeStruct((T * 8, 128), F32),
        jax.ShapeDtypeStruct((TOP_K, T), I32),
        jax.ShapeDtypeStruct((TOP_K, T), F32),
    )
    return pl.pallas_call(
        _mix_kernel,
        grid=(T // TM,),
        in_specs=[
            pl.BlockSpec((TM, D_MODEL), lambda i: (i, 0)),
            pl.BlockSpec((TM, 512), lambda i: (i, 0)),
            pl.BlockSpec((TM, 512), lambda i: (i, 0)),
            pl.BlockSpec((TM, 2048), lambda i: (i, 0)),
            full(wua), full(wub), full(wout), full(gx), full(wq), full(kx), full(vx), full(wo), full(gm),
            full(wr_t), full(br),
        ],
        out_specs=(
            pl.BlockSpec((TM * 8, 128), lambda i: (i, 0)),
            pl.BlockSpec((TM * 8, 128), lambda i: (i, 0)),
            pl.BlockSpec((TOP_K, TM), lambda i: (0, i)),
            pl.BlockSpec((TOP_K, TM), lambda i: (0, i)),
        ),
        out_shape=out_shape,
        compiler_params=pltpu.CompilerParams(
            dimension_semantics=("arbitrary",), vmem_limit_bytes=_vmem_limit(56 << 20)),
        name="mix",
    )(x2d, oa, ob, gates, wua, wub, wout, gx, wq, kx, vx, wo, gm, wr_t, br)


def _moe_kernel(blk_e_ref, first_ref, nxt_ref, slot_ref, nb_ref,
                tok_ref, wt_ref, x2t_hbm, hmt_ref, w1_hbm, w2_hbm, b1_ref, b2_ref, gf_ref,
                out_ref, yacc, xs_a, xs_b, yst_a, yst_b, w1buf, w2buf, wsem, xsem):
    j = pl.program_id(0)
    TT = TT_MOE
    RB = RB_MOE
    NG = D_MODEL // 128
    nbj = nb_ref[j]

    x2_copy = pltpu.make_async_copy(
        x2t_hbm.at[pl.ds(pl.multiple_of(j * TT * 8, 8), TT * 8), :], yacc.at[pl.ds(0, TT * 8), :], xsem.at[0])
    x2_copy.start()

    def w_copies(e, slot):
        cps = []
        for c in range(MOE_W_CHUNKS):
            for k, (src, dst) in enumerate(((w1_hbm, w1buf), (w2_hbm, w2buf))):
                rows = pl.ds(c * (src.shape[1] // MOE_W_CHUNKS), src.shape[1] // MOE_W_CHUNKS)
                cps.append(pltpu.make_async_copy(src.at[e, rows], dst.at[slot, rows], wsem.at[slot, k, c]))
        return cps

    for cp in w_copies(blk_e_ref[j, 0], slot_ref[j, 0]):
        cp.start()

    yacc[TT * 8:TT * 8 + 8, :] = jnp.zeros((8, 128), F32)
    yst_b[...] = jnp.zeros(yst_b.shape, F32)

    def gather(b, xs, r0=0, r1=RB):
        for r in range(r0, r1):
            t8 = jnp.minimum(tok_ref[0, 0, b * RB + r], (TT - 1) * 8)
            xs[r * 8:(r + 1) * 8, :] = hmt_ref[pl.ds(pl.multiple_of(t8, 8), 8), :]

    def scatter(b, yst, r0=0, r1=RB):
        for g0 in range(r0, r1, 8):
            dsts, vals = [], []
            for r in range(g0, g0 + 8):
                dst = pl.ds(pl.multiple_of(tok_ref[0, 0, b * RB + r], 8), 8)
                dsts.append(dst)
                vals.append(yacc[dst, :] + wt_ref[0, 0, b * RB + r] * yst[r * 8:(r + 1) * 8, :])
            for dst, val in zip(dsts, vals):
                yacc[dst, :] = val

    def step(b, xs_cur, yst_cur, xs_next, yst_prev):
        @pl.when(b < nbj)
        def _():
            e = blk_e_ref[j, b]
            slot = slot_ref[j, b]

            @pl.when(first_ref[j, b] == 1)
            def _():
                for cp in w_copies(e, slot):
                    cp.wait()
                nx = nxt_ref[j, b]

                @pl.when(nx >= 0)
                def _():
                    for cp in w_copies(nx, 1 - slot):
                        cp.start()

            MP = MOE_MLP_PIECES
            FC, OC, RS = D_FF // MP, D_MODEL // MP, RB // (2 * MP)
            bprev = jnp.maximum(b - 1, 0)

            def side(piece):
                gather(b + 1, xs_next, piece * RS, (piece + 1) * RS)
                scatter(bprev, yst_prev, piece * RS, (piece + 1) * RS)

            xb = jnp.concatenate([xs_cur[pl.ds(g, RB, stride=8), :] for g in range(NG)], axis=1).astype(BF16)
            b1 = b1_ref[e]
            b2 = b2_ref[e]
            acts = []
            for c in range(MP):
                side(c)
                cg = slice(c * FC, (c + 1) * FC)
                cl = slice(D_FF + c * FC, D_FF + (c + 1) * FC)
                glu = jnp.dot(xb, w1buf[slot, :, cg], preferred_element_type=F32) + b1[:, cg]
                lin = jnp.dot(xb, w1buf[slot, :, cl], preferred_element_type=F32) + b1[:, cl]
                glu = jnp.minimum(glu, SWIGLU_LIMIT)
                lin = jnp.clip(lin, -SWIGLU_LIMIT, SWIGLU_LIMIT)
                acts.append((glu * _sigmoid(SWIGLU_ALPHA * glu) * (lin + 1.0)).astype(BF16))
            act = jnp.concatenate(acts, axis=1)
            for c in range(MP):
                side(MP + c)
                co = slice(c * OC, (c + 1) * OC)
                ys = jnp.dot(act, w2buf[slot, :, co], preferred_element_type=F32) + b2[:, co]
                for k in range(OC // 128):
                    g = c * (OC // 128) + k
                    yst_cur[pl.ds(g, RB, stride=8), :] = ys[:, k * 128:(k + 1) * 128]

    gather(0, xs_a)
    x2_copy.wait()

    def pair_body(pp, c):
        step(2 * pp, xs_a, yst_a, xs_b, yst_b)
        step(2 * pp + 1, xs_b, yst_b, xs_a, yst_a)
        return c

    lax.fori_loop(0, (nbj + 1) // 2, pair_body, 0)

    @pl.when(nbj % 2 == 1)
    def _():
        scatter(nbj - 1, yst_a)

    @pl.when(nbj % 2 == 0)
    def _():
        scatter(nbj - 1, yst_b)

    RC = 256

    def fin(ci, c):
        base = pl.multiple_of(ci * RC * 8, 8)
        yv = jnp.concatenate([yacc[pl.ds(base + g, RC, stride=8), :] for g in range(NG)], axis=1)
        out_ref[pl.ds(pl.multiple_of(ci * RC, 8), RC), :] = _rms(yv, gf_ref[...])
        return c

    lax.fori_loop(0, TT // RC, fin, 0)


def _moe_nb_max():
    return TOP_K * TT_MOE // RB_MOE + N_EXPERTS


def _moe(lists, x2t, hmt, w1, b1, w2, b2, g_final):
    blk_e, first, nxt, slot, nb, tok, wt = lists
    T = x2t.shape[0] // 8
    TT = TT_MOE
    nt = T // TT
    LP = tok.shape[-1]
    smem_row = pl.BlockSpec((1, 1, LP), lambda j, *_: (j, 0, 0), memory_space=pltpu.SMEM)
    whole = lambda a: pl.BlockSpec(a.shape, lambda j, *_: (0,) * a.ndim)
    grid_spec = pltpu.PrefetchScalarGridSpec(
        num_scalar_prefetch=5,
        grid=(nt,),
        in_specs=[
            smem_row, smem_row,
            pl.BlockSpec(memory_space=pl.ANY),
            pl.BlockSpec((TT * 8, 128), lambda j, *_: (j, 0)),
            pl.BlockSpec(memory_space=pl.ANY), pl.BlockSpec(memory_space=pl.ANY),
            whole(b1), whole(b2), whole(g_final),
        ],
        out_specs=pl.BlockSpec((TT, D_MODEL), lambda j, *_: (j, 0), pipeline_mode=pl.Buffered(1)),
        scratch_shapes=[
            pltpu.VMEM((TT * 8 + 8, 128), F32),
            pltpu.VMEM((RB_MOE * 8, 128), F32), pltpu.VMEM((RB_MOE * 8, 128), F32),
            pltpu.VMEM((RB_MOE * 8, 128), F32), pltpu.VMEM((RB_MOE * 8, 128), F32),
            pltpu.VMEM((2, D_MODEL, 2 * D_FF), BF16),
            pltpu.VMEM((2, D_FF, D_MODEL), BF16),
            pltpu.SemaphoreType.DMA((2, 2, MOE_W_CHUNKS)),
            pltpu.SemaphoreType.DMA((1,)),
        ],
    )
    return pl.pallas_call(
        _moe_kernel,
        grid_spec=grid_spec,
        out_shape=jax.ShapeDtypeStruct((T, D_MODEL), F32),
        compiler_params=pltpu.CompilerParams(
            dimension_semantics=("arbitrary",), vmem_limit_bytes=_vmem_limit(60 << 20)),
        name="moe",
    )(blk_e, first, nxt, slot, nb, tok, wt, x2t, hmt, w1, w2, b1, b2, g_final)


def _moe_lists(idx, wts):
    T = idx.shape[1]
    TT, RB = TT_MOE, RB_MOE
    nt = T // TT
    NB = _moe_nb_max()
    A = TOP_K * TT
    e_tile = idx.reshape(TOP_K, nt, TT).transpose(1, 0, 2).reshape(nt, A)
    w_tile = wts.reshape(TOP_K, nt, TT).transpose(1, 0, 2).reshape(nt, A)
    tok_local = jnp.tile(jnp.arange(TT, dtype=I32), TOP_K)
    order = jnp.argsort(e_tile, axis=1, stable=True)
    stok = jnp.take_along_axis(jnp.broadcast_to(tok_local, e_tile.shape), order, axis=1)
    swt = jnp.take_along_axis(w_tile, order, axis=1)

    experts = jnp.arange(N_EXPERTS, dtype=I32)
    counts = jnp.sum((e_tile[:, :, None] == experts[None, None, :]).astype(I32), axis=1)
    start = jnp.cumsum(counts, axis=1) - counts
    nblk = (counts + RB - 1) // RB
    blk_end = jnp.cumsum(nblk, axis=1)
    blk_start = blk_end - nblk
    nb = blk_end[:, -1]

    b = jnp.arange(NB, dtype=I32)
    blk_e = jnp.minimum(jnp.sum((blk_end[:, :, None] <= b[None, None, :]).astype(I32), axis=1), N_EXPERTS - 1)
    onehot = (blk_e[:, :, None] == experts[None, None, :]).astype(I32)
    per_block = lambda v: jnp.sum(onehot * v[:, None, :], axis=2)
    present = (nblk > 0).astype(I32)
    ordinal = jnp.cumsum(present, axis=1) - present
    later = (experts[None, :] > experts[:, None])[None] & (present[:, None, :] > 0)
    nxt_e = jnp.min(jnp.where(later, experts[None, None, :], N_EXPERTS), axis=2)
    nxt_e = jnp.where(nxt_e >= N_EXPERTS, -1, nxt_e)
    bs_b = per_block(blk_start)
    first = ((b[None, :] == bs_b) & (b[None, :] < nb[:, None])).astype(I32)
    slot = per_block(ordinal) % 2
    nxt = per_block(nxt_e)

    r = jnp.arange(RB, dtype=I32)[None, None, :]
    off = (b[None, :] - bs_b)[:, :, None] * RB + r
    valid = (off < per_block(counts)[:, :, None]) & (b[None, :, None] < nb[:, None, None])
    src = jnp.clip(per_block(start)[:, :, None] + off, 0, A - 1).reshape(nt, NB * RB)
    valid = valid.reshape(nt, NB * RB)
    tok = jnp.where(valid, jnp.take_along_axis(stok, src, axis=1), TT)
    wt = jnp.where(valid, jnp.take_along_axis(swt, src, axis=1), 0.0)
    tok = jnp.pad(tok, ((0, 0), (0, RB)), constant_values=TT)
    wt = jnp.pad(wt, ((0, 0), (0, RB)))
    return blk_e, first, nxt, slot, nb, (tok * 8)[:, None, :], wt[:, None, :]


def kernel(x, mem, g_mix, w_in, rel_bias, lambda_q1, lambda_k1, lambda_q2, lambda_k2, g_subln, conv_w, a_log,
           dt_bias, g_out_b, w_up_a, w_up_b, w_out, g_xattn, g_mem, w_q_x, w_kv_x, w_o_x, g_moe, w_router,
           b_router, w_mlp1, b_mlp1, w_mlp2, b_mlp2, g_final):
    B_, S_, _ = x.shape
    assert B_ == 1 and S_ % TT_MOE == 0 and x.dtype == F32
    l = 0
    x2d = x.reshape(S_, D_MODEL)
    row = lambda v: v.reshape(1, -1).astype(F32)

    wi = w_in[l]
    ab_cols = jnp.pad(wi[:, 3584:3592], ((0, 0), (0, 120)))
    w_main = jnp.concatenate(
        [wi[:, 0:1024], wi[:, 1536:3072], wi[:, 3072:3584], wi[:, 3592:5640], ab_cols], axis=1).astype(BF16)
    w_vt = wi[:, 1024:1536].T.astype(BF16)
    w_abt = jnp.pad(wi[:, 3584:3592].T, ((0, 8), (0, 0))).astype(BF16)

    q, k, vt, qkvb, zb, gates, ab, abt = _inproj(x2d, row(g_mix[l]), w_main, w_vt, w_abt)

    lam = (jnp.exp(jnp.sum(lambda_q1[l].astype(F32) * lambda_k1[l].astype(F32)))
           - jnp.exp(jnp.sum(lambda_q2[l].astype(F32) * lambda_k2[l].astype(F32))) + LAMBDA_INIT).reshape(1)
    bias_d, bias_p = _attn_bias_tables(rel_bias)
    oa = _attention(lam, q, k, vt, bias_d, bias_p, row(g_subln[l]))

    lane_pad = lambda v: jnp.pad(v.astype(F32), (0, 128 - v.shape[0])).reshape(1, 128)
    row_bcast = lambda v: jnp.broadcast_to(jnp.pad(v.astype(F32), (0, 16 - v.shape[0]))[:, None], (16, TC_GDN))
    ob = _gdn(qkvb, zb, ab, abt, jnp.pad(conv_w[l].astype(F32), ((0, 4), (0, 0))),
              lane_pad(a_log[l]), lane_pad(dt_bias[l]), row_bcast(a_log[l]), row_bcast(dt_bias[l]),
              row(g_out_b[l]))

    kx, vx = _memkv(mem.reshape(-1, D_MODEL), row(g_mem[l]), w_kv_x[l].astype(BF16))
    br = jnp.broadcast_to(b_router[l].astype(F32)[:, None], (N_EXPERTS, 128))
    x2t, hmt, idx, wts = _mix(
        x2d, oa, ob, gates, w_up_a[l].astype(BF16), w_up_b[l].astype(BF16), w_out[l].astype(BF16),
        row(g_xattn[l]), w_q_x[l].astype(BF16), kx, vx, w_o_x[l].astype(BF16), row(g_moe[l]),
        w_router[l].T.astype(F32), br)

    out = _moe(_moe_lists(idx, wts), x2t, hmt, w_mlp1[l].astype(BF16), b_mlp1[l].astype(F32)[:, None, :],
               w_mlp2[l].astype(BF16), b_mlp2[l].astype(F32)[:, None, :], row(g_final))
    return out.reshape(B_, S_, D_MODEL)
```

```python
import functools
import math

import jax
import jax.numpy as jnp
from jax import lax
from jax.experimental import pallas as pl
from jax.experimental.pallas import tpu as pltpu

F32 = jnp.float32
BF16 = jnp.bfloat16
I32 = jnp.int32

D_MODEL = 1024
CHUNK = 64
EPS = 1e-6
A_HEADS = 4
A_QK_DIM = 64
A_V_DIM = 128
REL_BUCKETS = 32
REL_MAX_DIST = 128
B_HEADS = 4
B_DIM = 128
CONV_WIDTH = 4
X_HEADS = 4
X_HEAD_DIM = 256
N_EXPERTS = 32
TOP_K = 4
D_FF = 1024
SWIGLU_LIMIT = 7.0
SWIGLU_ALPHA = 1.702
LAMBDA_INIT = 0.8 - 0.6 * math.exp(-0.3 * 0)

LOG2E = 1.4426950408889634
NEG_BIG = -1e30

V7X_LANES = 128
V7X_SUBLANES = 8
V7X_VMEM_BYTES = 64 * 1024 * 1024

TM_PROJ = 512
TQ = 512
TK = 512
TC_GDN = 256
TM_MIX = 512
TT_MOE = 2048
RB_MOE = 128
MOE_W_CHUNKS = 4

C_QA, C_KA, C_QKVB, C_ZB, C_GATE, C_AB, C_END = 0, 512, 1024, 2560, 3072, 5120, 5248

_NT = (((1,), (1,)), ((), ()))
_TN = (((0,), (0,)), ((), ()))


def _rms(x, g):
    return x * lax.rsqrt(jnp.mean(x * x, axis=-1, keepdims=True) + EPS) * g


def _sigmoid(x):
    return 1.0 / (1.0 + jnp.exp(-x))


def _vmem_limit(nbytes):
    return int(min(nbytes, V7X_VMEM_BYTES - 4 * 1024 * 1024))


def _inproj_kernel(x_ref, g_ref, w_ref, wvt_ref, wabt_ref,
                   q_ref, k_ref, vt_ref, qkvb_ref, zb_ref, gates_ref, ab_ref, abt_ref):
    h = _rms(x_ref[...], g_ref[...]).astype(BF16)

    def mm(c0, c1):
        return jnp.dot(h, w_ref[:, c0:c1], preferred_element_type=F32)

    nb = TM_PROJ // TK
    qa = mm(C_QA, C_KA) * (A_QK_DIM ** -0.5 * LOG2E)
    ka = mm(C_KA, C_QKVB)
    vt = lax.dot_general(wvt_ref[...], h, _NT, preferred_element_type=F32)
    for hh in range(A_HEADS):
        cs = slice(hh * 128, (hh + 1) * 128)
        q_ref[hh] = qa[:, cs].astype(BF16)
        for b in range(nb):
            rs = slice(b * TK, (b + 1) * TK)
            k_ref[hh, b] = ka[rs, cs].astype(BF16)
            vt_ref[hh, b] = vt[cs, rs].astype(BF16)
    for j in range(3):
        qkvb_ref[:, j * 512:(j + 1) * 512] = mm(C_QKVB + j * 512, C_QKVB + (j + 1) * 512)
    zb_ref[...] = mm(C_ZB, C_GATE)
    for j in range(4):
        gates_ref[:, j * 512:(j + 1) * 512] = mm(C_GATE + j * 512, C_GATE + (j + 1) * 512)
    ab_ref[...] = mm(C_AB, C_END)
    abt_ref[...] = lax.dot_general(wabt_ref[...], h, _NT, preferred_element_type=F32)


def _inproj(x2d, g_mix, w_main, w_vt, w_abt):
    T = x2d.shape[0]
    n = T // TM_PROJ
    nkb = T // TK
    nb = TM_PROJ // TK
    full = lambda shape: pl.BlockSpec(shape, lambda i: (0,) * len(shape))
    out_shape = (
        jax.ShapeDtypeStruct((A_HEADS, T, 128), BF16),
        jax.ShapeDtypeStruct((A_HEADS, nkb, TK, 128), BF16),
        jax.ShapeDtypeStruct((A_HEADS, nkb, 128, TK), BF16),
        jax.ShapeDtypeStruct((T, 1536), F32),
        jax.ShapeDtypeStruct((T, 512), F32),
        jax.ShapeDtypeStruct((T, 2048), F32),
        jax.ShapeDtypeStruct((T, 128), F32),
        jax.ShapeDtypeStruct((16, T), F32),
    )
    out_specs = (
        pl.BlockSpec((A_HEADS, TM_PROJ, 128), lambda i: (0, i, 0)),
        pl.BlockSpec((A_HEADS, nb, TK, 128), lambda i: (0, i, 0, 0)),
        pl.BlockSpec((A_HEADS, nb, 128, TK), lambda i: (0, i, 0, 0)),
        pl.BlockSpec((TM_PROJ, 1536), lambda i: (i, 0)),
        pl.BlockSpec((TM_PROJ, 512), lambda i: (i, 0)),
        pl.BlockSpec((TM_PROJ, 2048), lambda i: (i, 0)),
        pl.BlockSpec((TM_PROJ, 128), lambda i: (i, 0)),
        pl.BlockSpec((16, TM_PROJ), lambda i: (0, i)),
    )
    return pl.pallas_call(
        _inproj_kernel,
        grid=(n,),
        in_specs=[
            pl.BlockSpec((TM_PROJ, D_MODEL), lambda i: (i, 0)),
            full((1, D_MODEL)),
            full(w_main.shape),
            full(w_vt.shape),
            full(w_abt.shape),
        ],
        out_specs=out_specs,
        out_shape=out_shape,
        compiler_params=pltpu.CompilerParams(
            dimension_semantics=("arbitrary",), vmem_limit_bytes=_vmem_limit(56 << 20)),
        name="inproj",
    )(x2d, g_mix, w_main, w_vt, w_abt)


def _attn_kernel(lam_ref, q_ref, k_ref, vt_ref, bd_ref, bp_ref, gs_ref, o_ref, m_sc, l_sc, acc_sc, s_buf, pm_buf):
    qi = pl.program_id(1)
    q = q_ref[0]
    lane = lax.broadcasted_iota(I32, q.shape, 1)
    zero = jnp.zeros_like(q)
    qm = (jnp.where(lane < A_QK_DIM, q, zero), jnp.where(lane >= A_QK_DIM, q, zero))
    m_sc[...] = jnp.full(m_sc.shape, NEG_BIG, F32)
    l_sc[...] = jnp.zeros(l_sc.shape, F32)
    acc_sc[...] = jnp.zeros(acc_sc.shape, F32)

    nkb = k_ref.shape[1]

    def stage_a(slot, kb0):
        for j in range(2):
            kblk = k_ref[0, jnp.minimum(kb0 + j, nkb - 1)]
            for m in range(2):
                s = lax.dot_general(kblk, qm[m], _NT, preferred_element_type=F32)
                s_buf[slot, m, j] = s
                pm_buf[slot, m, j] = jnp.max(s, axis=0, keepdims=True)

    def stage_b(slot, kb0, biases):
        far = all(b is None for b in biases)
        for m in range(2):
            m_old = m_sc[m]
            m_new = m_old
            ss = []
            for j, bias in enumerate(biases):
                if far:
                    m_new = jnp.maximum(m_new, pm_buf[slot, m, j])
                else:
                    s = s_buf[slot, m, j]
                    s = s if bias is None else s + bias[0]
                    ss.append(s)
                    m_new = jnp.maximum(m_new, jnp.max(s, axis=0, keepdims=True))
            alpha = jnp.exp2(m_old - m_new)
            l_new = alpha * l_sc[m]
            acc = alpha * acc_sc[m]
            for j in range(len(biases)):
                s = s_buf[slot, m, j] if far else ss[j]
                p = jnp.exp2(s - m_new)
                l_new = l_new + jnp.sum(p, axis=0, keepdims=True)
                acc = acc + jnp.dot(vt_ref[0, kb0 + j], p.astype(BF16), preferred_element_type=F32)
            l_sc[m] = l_new
            acc_sc[m] = acc
            m_sc[m] = m_new

    def far_pair(slot_a, kb_a, slot_b, kb_b):
        def a_piece(j, m):
            kblk = k_ref[0, jnp.minimum(kb_a + j, nkb - 1)]
            s = lax.dot_general(kblk, qm[m], _NT, preferred_element_type=F32)
            s_buf[slot_a, m, j] = s
            pm_buf[slot_a, m, j] = jnp.max(s, axis=0, keepdims=True)

        for m in range(2):
            a_piece(0, m)
            m_old = m_sc[m]
            m_new = jnp.maximum(jnp.maximum(m_old, pm_buf[slot_b, m, 0]), pm_buf[slot_b, m, 1])
            alpha = jnp.exp2(m_old - m_new)
            l_new = alpha * l_sc[m]
            acc = alpha * acc_sc[m]
            for j in range(2):
                if j == 1:
                    a_piece(1, m)
                p = jnp.exp2(s_buf[slot_b, m, j] - m_new)
                l_new = l_new + jnp.sum(p, axis=0, keepdims=True)
                acc = acc + jnp.dot(vt_ref[0, kb_b + j], p.astype(BF16), preferred_element_type=F32)
            l_sc[m] = l_new
            acc_sc[m] = acc
            m_sc[m] = m_new

    far2 = [None, None]
    nfu = jnp.maximum(qi - 1, 0) // 2
    peel = nfu % 2

    @pl.when(peel == 1)
    def _():
        stage_a(0, 0)
        stage_b(0, 0, far2)

    stage_a(0, 2 * peel)

    def far_body(v, c):
        u = peel + 2 * v
        far_pair(1, 2 * u + 2, 0, 2 * u)
        far_pair(0, 2 * u + 4, 1, 2 * u + 2)
        return c

    lax.fori_loop(0, (nfu - peel) // 2, far_body, 0)
    kb0 = 2 * nfu

    @pl.when(qi % 2 == 1)
    def _():
        stage_b(0, kb0, [bp_ref, bd_ref])

    @pl.when((qi % 2 == 0) & (qi >= 2))
    def _():
        stage_a(1, kb0 + 2)
        stage_b(0, kb0, [None, bp_ref])
        stage_b(1, kb0 + 2, [bd_ref])

    @pl.when(qi == 0)
    def _():
        stage_b(0, kb0, [bd_ref])

    o = acc_sc[0] / l_sc[0] - lam_ref[0] * (acc_sc[1] / l_sc[1])
    ot = o.T
    o_ref[...] = (_rms(ot, gs_ref[...]) * (1.0 - LAMBDA_INIT)).astype(BF16)


def _attention(lam, q, k, vt, bias_d, bias_p, g_subln):
    T = q.shape[1]
    nq = T // TQ
    nkb = T // TK
    return pl.pallas_call(
        _attn_kernel,
        grid=(A_HEADS, nq),
        in_specs=[
            pl.BlockSpec(memory_space=pltpu.SMEM),
            pl.BlockSpec((1, TQ, 128), lambda h, i: (h, i, 0)),
            pl.BlockSpec((1, nkb, TK, 128), lambda h, i: (h, 0, 0, 0)),
            pl.BlockSpec((1, nkb, 128, TK), lambda h, i: (h, 0, 0, 0)),
            pl.BlockSpec((1, TK, TQ), lambda h, i: (h, 0, 0)),
            pl.BlockSpec((1, TK, TQ), lambda h, i: (h, 0, 0)),
            pl.BlockSpec((1, 128), lambda h, i: (0, 0)),
        ],
        out_specs=pl.BlockSpec((TQ, 128), lambda h, i: (i, h)),
        out_shape=jax.ShapeDtypeStruct((T, A_HEADS * A_V_DIM), BF16),
        scratch_shapes=[
            pltpu.VMEM((2, 1, TQ), F32),
            pltpu.VMEM((2, 1, TQ), F32),
            pltpu.VMEM((2, 128, TQ), F32),
            pltpu.VMEM((2, 2, 2, TK, TQ), F32),
            pltpu.VMEM((2, 2, 2, 1, TQ), F32),
        ],
        compiler_params=pltpu.CompilerParams(
            dimension_semantics=("arbitrary", "arbitrary"), vmem_limit_bytes=_vmem_limit(48 << 20)),
        name="attn",
    )(lam, q, k, vt, bias_d, bias_p, g_subln)


def _t5_bucket(rel):
    half = REL_BUCKETS // 2
    max_exact = half // 2
    ret = jnp.where(rel > 0, half, 0)
    n = jnp.abs(rel)
    large = max_exact + (jnp.log(jnp.maximum(n, 1).astype(F32) / max_exact)
                         / math.log(REL_MAX_DIST / max_exact) * (half - max_exact)).astype(I32)
    large = jnp.minimum(large, half - 1)
    return ret + jnp.where(n < max_exact, n, large)


def _attn_bias_tables(rel_bias):
    assert TK >= REL_MAX_DIST and TK == TQ and TK % CHUNK == 0
    kk = jnp.arange(TK, dtype=I32)[:, None]
    qq = jnp.arange(TQ, dtype=I32)[None, :]
    rb = rel_bias.astype(F32)
    far = rb[_t5_bucket(jnp.full((1,), -REL_MAX_DIST, I32))[0]]
    table = ((rb - far[None, :]) * LOG2E).T

    def lookup(rel):
        onehot = (_t5_bucket(rel).reshape(1, -1) == jnp.arange(REL_BUCKETS, dtype=I32)[:, None]).astype(F32)
        return jnp.dot(table, onehot, precision=lax.Precision.HIGHEST).reshape(A_HEADS, TK, TQ)

    bd = jnp.where((kk // CHUNK <= qq // CHUNK)[None], lookup(kk - qq), NEG_BIG)
    bp = lookup(kk - TK - qq)
    return bd, bp


def _gdn_kernel(x_ref, zb_ref, ab_ref, abt_ref, cw_ref, alc_ref, dtc_ref, alr_ref, dtr_ref, gob_ref,
                o_ref, xbuf, s_sc):
    i = pl.program_id(0)
    TC = TC_GDN
    NCH = TC // CHUNK
    HB = B_HEADS * CHUNK

    @pl.when(i == 0)
    def _():
        xbuf[0:8, :] = jnp.zeros((8, 1536), F32)
        s_sc[...] = jnp.zeros(s_sc.shape, F32)

    xbuf[8:8 + TC, :] = x_ref[...]
    y = cw_ref[3:4, :] * xbuf[8:8 + TC, :]
    for d in range(1, CONV_WIDTH):
        y = y + cw_ref[3 - d:4 - d, :] * xbuf[8 - d:8 - d + TC, :]
    xbuf[0:8, :] = x_ref[TC - 8:TC, :]
    a = y * _sigmoid(y)

    def l2n(t):
        return t * lax.rsqrt(jnp.sum(t * t, axis=-1, keepdims=True) + EPS)

    qh = [l2n(a[:, h * 128:(h + 1) * 128]) * (B_DIM ** -0.5) for h in range(B_HEADS)]
    kh = [l2n(a[:, 512 + h * 128:512 + (h + 1) * 128]) for h in range(B_HEADS)]
    vh = [a[:, 1024 + h * 128:1024 + (h + 1) * 128] for h in range(B_HEADS)]

    def softplus(t):
        return jnp.maximum(t, 0.0) + jnp.log(1.0 + jnp.exp(-jnp.abs(t)))

    ab = ab_ref[...]
    g_col = -jnp.exp(alc_ref[...]) * softplus(ab + dtc_ref[...])
    beta_col = _sigmoid(ab)
    g_row = -jnp.exp(alr_ref[...]) * softplus(abt_ref[...] + dtr_ref[...])

    rt = lax.broadcasted_iota(I32, (TC, TC), 0)
    ct = lax.broadcasted_iota(I32, (TC, TC), 1)
    same_chunk = (rt // CHUNK) == (ct // CHUNK)
    tril = jnp.where(same_chunk & (rt >= ct), 1.0, 0.0).astype(F32)
    triu = jnp.where(same_chunk & (rt <= ct), 1.0, 0.0).astype(F32)
    gc_col = jnp.dot(tril, g_col, precision=lax.Precision.HIGHEST, preferred_element_type=F32)
    gc_row = jnp.dot(g_row, triu, precision=lax.Precision.HIGHEST, preferred_element_type=F32)

    ri = lax.broadcasted_iota(I32, (HB, HB), 0)
    ci = lax.broadcasted_iota(I32, (HB, HB), 1)
    same_head = (ri // CHUNK) == (ci // CHUNK)
    incl = same_head & (ri >= ci)
    strict = same_head & (ri > ci)
    eye = jnp.where(ri == ci, 1.0, 0.0).astype(F32)

    def level_mask(s):
        return ((ri // (2 * s)) == (ci // (2 * s))) & (((ri // s) % 2) == 1) & (((ci // s) % 2) == 0)

    cat0 = lambda parts: jnp.concatenate(parts, axis=0)
    heads = range(B_HEADS)
    chunks = range(NCH)

    Kc, Qc, glc, gccc, Lc, QKc, invc, rhsc = [], [], [], [], [], [], [], []
    for c in chunks:
        rs = slice(c * CHUNK, (c + 1) * CHUNK)
        last = slice(c * CHUNK + CHUNK - 1, (c + 1) * CHUNK)
        K = cat0([kh[h][rs] for h in heads])
        Q = cat0([qh[h][rs] for h in heads])
        V = cat0([vh[h][rs] for h in heads])
        beta = cat0([beta_col[rs, 4 + h:5 + h] for h in heads])
        gcc = cat0([gc_col[rs, h:h + 1] for h in heads])
        gl = cat0([jnp.broadcast_to(gc_col[last, h:h + 1], (CHUNK, 1)) for h in heads])
        gcr = jnp.concatenate([gc_row[h:h + 1, rs] for h in heads], axis=1)
        Kb = K * beta
        e = jnp.exp(jnp.where(incl, gcc - gcr, 0.0))
        L = lax.dot_general(Kb, K, _NT, preferred_element_type=F32) * jnp.where(strict, e, 0.0)
        QKc.append(lax.dot_general(Q, K, _NT, preferred_element_type=F32) * jnp.where(incl, e, 0.0))
        rhsc.append(jnp.concatenate([V * beta, Kb * jnp.exp(gcc)], axis=1))
        invc.append(eye - jnp.where(level_mask(1), L, 0.0))
        Kc.append(K); Qc.append(Q); glc.append(gl); gccc.append(gcc); Lc.append(L)

    for s in (2, 4, 8, 16, 32):
        msk = level_mask(s)
        ts = [jnp.dot(invc[c], jnp.where(msk, Lc[c], 0.0), preferred_element_type=F32) for c in chunks]
        invc = [invc[c] - jnp.dot(ts[c], invc[c], preferred_element_type=F32) for c in chunks]
    solc = [jnp.dot(invc[c], rhsc[c], preferred_element_type=F32) for c in chunks]

    for c in chunks:
        rs = slice(c * CHUNK, (c + 1) * CHUNK)
        U = solc[c][:, :B_DIM]
        W = solc[c][:, B_DIM:]
        Qd = Qc[c] * jnp.exp(gccc[c])
        Kd = Kc[c] * jnp.exp(glc[c] - gccc[c])
        vnew = []
        ost = []
        for h in heads:
            hs = slice(h * CHUNK, (h + 1) * CHUNK)
            S = s_sc[h]
            vn = U[hs] - jnp.dot(W[hs], S, preferred_element_type=F32)
            ost.append(jnp.dot(Qd[hs], S, preferred_element_type=F32))
            s_sc[h] = S * jnp.exp(glc[c][h * CHUNK:h * CHUNK + 1, :]) + lax.dot_general(
                Kd[hs], vn, _TN, preferred_element_type=F32)
            vnew.append(vn)
        O = cat0(ost) + jnp.dot(QKc[c], cat0(vnew), preferred_element_type=F32)
        for h in heads:
            oh = _rms(O[h * CHUNK:(h + 1) * CHUNK], gob_ref[...])
            z = zb_ref[rs, h * 128:(h + 1) * 128]
            o_ref[rs, h * 128:(h + 1) * 128] = (oh * (z * _sigmoid(z))).astype(BF16)


def _gdn(qkvb, zb, ab, abt, conv_w8, alc, dtc, alr, dtr, gob):
    T = qkvb.shape[0]
    TC = TC_GDN
    full = lambda shape: pl.BlockSpec(shape, lambda i: (0,) * len(shape))
    return pl.pallas_call(
        _gdn_kernel,
        grid=(T // TC,),
        in_specs=[
            pl.BlockSpec((TC, 1536), lambda i: (i, 0)),
            pl.BlockSpec((TC, 512), lambda i: (i, 0)),
            pl.BlockSpec((TC, 128), lambda i: (i, 0)),
            pl.BlockSpec((16, TC), lambda i: (0, i)),
            full((8, 1536)), full((1, 128)), full((1, 128)), full((16, TC)), full((16, TC)), full((1, 128)),
        ],
        out_specs=pl.BlockSpec((TC, 512), lambda i: (i, 0)),
        out_shape=jax.ShapeDtypeStruct((T, 512), BF16),
        scratch_shapes=[pltpu.VMEM((TC + 8, 1536), F32), pltpu.VMEM((B_HEADS, B_DIM, B_DIM), F32)],
        compiler_params=pltpu.CompilerParams(
            dimension_semantics=("arbitrary",), vmem_limit_bytes=_vmem_limit(48 << 20)),
        name="gdn",
    )(qkvb, zb, ab, abt, conv_w8, alc, dtc, alr, dtr, gob)


def _memkv_kernel(m_ref, g_ref, w_ref, k_ref, v_ref):
    hm = _rms(m_ref[...], g_ref[...]).astype(BF16)
    kv = jnp.dot(hm, w_ref[...], preferred_element_type=F32)
    k_ref[...] = kv[:, :D_MODEL].astype(BF16)
    v_ref[...] = kv[:, D_MODEL:].astype(BF16)


def _memkv(mem2d, g_mem, w_kv):
    n = mem2d.shape[0]
    return pl.pallas_call(
        _memkv_kernel,
        out_shape=(jax.ShapeDtypeStruct((n, D_MODEL), BF16), jax.ShapeDtypeStruct((n, D_MODEL), BF16)),
        compiler_params=pltpu.CompilerParams(vmem_limit_bytes=_vmem_limit(32 << 20)),
        name="memkv",
    )(mem2d, g_mem, w_kv)


def _mix_kernel(x_ref, oa_ref, ob_ref, gates_ref, wua_ref, wub_ref, wout_ref, gx_ref, wq_ref, kx_ref, vx_ref,
                wo_ref, gm_ref, wr_ref, br_ref, x2t_ref, hmt_ref, idx_ref, wts_ref):
    NSUB = 2
    TS = TM_MIX // NSUB
    subs = range(NSUB)
    rows = [slice(sub * TS, (sub + 1) * TS) for sub in subs]
    dotf = functools.partial(jnp.dot, preferred_element_type=F32)

    ma = [dotf(oa_ref[rows[g], :], wua_ref[...]) for g in subs]
    mb = [dotf(ob_ref[rows[g], :], wub_ref[...]) for g in subs]
    merged = [(_sigmoid(gates_ref[rows[g], :D_MODEL]) * ma[g]
               + _sigmoid(gates_ref[rows[g], D_MODEL:]) * mb[g]).astype(BF16) for g in subs]
    x1 = [x_ref[rows[g], :] + dotf(merged[g], wout_ref[...]) for g in subs]
    hx = [_rms(x1[g], gx_ref[...]).astype(BF16) for g in subs]

    heads = [[] for _ in subs]
    for h in range(X_HEADS):
        cs = slice(h * X_HEAD_DIM, (h + 1) * X_HEAD_DIM)
        qh = [dotf(hx[g], wq_ref[:, cs]).astype(BF16) for g in subs]
        s = [lax.dot_general(qh[g], kx_ref[:, cs], _NT, preferred_element_type=F32) * (X_HEAD_DIM ** -0.5)
             for g in subs]
        for g in subs:
            sg = s[g] - jnp.max(s[g], axis=-1, keepdims=True)
            p = jnp.exp(sg)
            p = p / jnp.sum(p, axis=-1, keepdims=True)
            heads[g].append(dotf(p.astype(BF16), vx_ref[:, cs]).astype(BF16))
    x2 = [x1[g] + dotf(jnp.concatenate(heads[g], axis=1), wo_ref[...]) for g in subs]
    hm = [_rms(x2[g], gm_ref[...]) for g in subs]

    for g in subs:
        for c in range(D_MODEL // 128):
            x2t_ref[pl.ds(g * TS * 8 + c, TS, stride=8), :] = x2[g][:, c * 128:(c + 1) * 128]
            hmt_ref[pl.ds(g * TS * 8 + c, TS, stride=8), :] = hm[g][:, c * 128:(c + 1) * 128]

        logits = lax.dot_general(wr_ref[...], hm[g], _NT, precision=lax.Precision.HIGHEST,
                                 preferred_element_type=F32) + br_ref[:, 0:1]
        eidx = lax.broadcasted_iota(I32, logits.shape, 0)
        vals, idxs = [], []
        cur = logits
        for _ in range(TOP_K):
            mx = jnp.max(cur, axis=0, keepdims=True)
            ix = jnp.min(jnp.where(cur == mx, eidx, N_EXPERTS), axis=0, keepdims=True)
            vals.append(mx)
            idxs.append(ix)
            cur = jnp.where(eidx == ix, -jnp.inf, cur)
        ex = [jnp.exp(v - vals[0]) for v in vals]
        den = ex[0] + ex[1] + ex[2] + ex[3]
        idx_ref[:, rows[g]] = jnp.concatenate(idxs, axis=0)
        wts_ref[:, rows[g]] = jnp.concatenate([e / den for e in ex], axis=0)


def _mix(x2d, oa, ob, gates, wua, wub, wout, gx, wq, kx, vx, wo, gm, wr_t, br):
    T = x2d.shape[0]
    TM = TM_MIX
    full = lambda a: pl.BlockSpec(a.shape, lambda i: (0,) * a.ndim)
    out_shape = (
        jax.ShapeDtypeStruct((T * 8, 128), F32),
        jax.ShapeDtypeStruct((T * 8, 128), F32),
        jax.ShapeDtypeStruct((TOP_K, T), I32),
        jax.ShapeDtypeStruct((TOP_K, T), F32),
    )
    return pl.pallas_call(
        _mix_kernel,
        grid=(T // TM,),
        in_specs=[
            pl.BlockSpec((TM, D_MODEL), lambda i: (i, 0)),
            pl.BlockSpec((TM, 512), lambda i: (i, 0)),
            pl.BlockSpec((TM, 512), lambda i: (i, 0)),
            pl.BlockSpec((TM, 2048), lambda i: (i, 0)),
            full(wua), full(wub), full(wout), full(gx), full(wq), full(kx), full(vx), full(wo), full(gm),
            full(wr_t), full(br),
        ],
        out_specs=(
            pl.BlockSpec((TM * 8, 128), lambda i: (i, 0)),
            pl.BlockSpec((TM * 8, 128), lambda i: (i, 0)),
            pl.BlockSpec((TOP_K, TM), lambda i: (0, i)),
            pl.BlockSpec((TOP_K, TM), lambda i: (0, i)),
        ),
        out_shape=out_shape,
        compiler_params=pltpu.CompilerParams(
            dimension_semantics=("arbitrary",), vmem_limit_bytes=_vmem_limit(56 << 20)),
        name="mix",
    )(x2d, oa, ob, gates, wua, wub, wout, gx, wq, kx, vx, wo, gm, wr_t, br)


def _moe_kernel(blk_e_ref, first_ref, nxt_ref, slot_ref, nb_ref,
                tok_ref, wt_ref, x2t_hbm, hmt_ref, w1_hbm, w2_hbm, b1_ref, b2_ref, gf_ref,
                out_ref, yacc, xs_a, xs_b, yst_a, yst_b, w1buf, w2buf, wsem, xsem):
    j = pl.program_id(0)
    TT = TT_MOE
    RB = RB_MOE
    NG = D_MODEL // 128
    nbj = nb_ref[j]

    x2_copy = pltpu.make_async_copy(
        x2t_hbm.at[pl.ds(pl.multiple_of(j * TT * 8, 8), TT * 8), :], yacc.at[pl.ds(0, TT * 8), :], xsem.at[0])
    x2_copy.start()

    def w_copies(e, slot):
        cps = []
        for c in range(MOE_W_CHUNKS):
            for k, (src, dst) in enumerate(((w1_hbm, w1buf), (w2_hbm, w2buf))):
                rows = pl.ds(c * (src.shape[1] // MOE_W_CHUNKS), src.shape[1] // MOE_W_CHUNKS)
                cps.append(pltpu.make_async_copy(src.at[e, rows], dst.at[slot, rows], wsem.at[slot, k, c]))
        return cps

    for cp in w_copies(blk_e_ref[j, 0], slot_ref[j, 0]):
        cp.start()

    yacc[TT * 8:TT * 8 + 8, :] = jnp.zeros((8, 128), F32)
    yst_b[...] = jnp.zeros(yst_b.shape, F32)

    def gather(b, xs):
        for r in range(RB):
            t8 = jnp.minimum(tok_ref[0, 0, b * RB + r], (TT - 1) * 8)
            xs[r * 8:(r + 1) * 8, :] = hmt_ref[pl.ds(pl.multiple_of(t8, 8), 8), :]

    def mlp(xs, yst, e, slot):
        xb = jnp.concatenate([xs[pl.ds(g, RB, stride=8), :] for g in range(NG)], axis=1).astype(BF16)
        hid = jnp.dot(xb, w1buf[slot], preferred_element_type=F32) + b1_ref[e]
        glu = jnp.minimum(hid[:, :D_FF], SWIGLU_LIMIT)
        lin = jnp.clip(hid[:, D_FF:], -SWIGLU_LIMIT, SWIGLU_LIMIT)
        act = glu * _sigmoid(SWIGLU_ALPHA * glu) * (lin + 1.0)
        ys = jnp.dot(act.astype(BF16), w2buf[slot], preferred_element_type=F32) + b2_ref[e]
        for g in range(NG):
            yst[pl.ds(g, RB, stride=8), :] = ys[:, g * 128:(g + 1) * 128]

    def scatter(b, yst):
        for g0 in range(0, RB, 8):
            dsts, vals = [], []
            for r in range(g0, g0 + 8):
                dst = pl.ds(pl.multiple_of(tok_ref[0, 0, b * RB + r], 8), 8)
                dsts.append(dst)
                vals.append(yacc[dst, :] + wt_ref[0, 0, b * RB + r] * yst[r * 8:(r + 1) * 8, :])
            for dst, val in zip(dsts, vals):
                yacc[dst, :] = val

    def step(b, xs_cur, yst_cur, xs_next, yst_prev):
        @pl.when(b < nbj)
        def _():
            e = blk_e_ref[j, b]
            slot = slot_ref[j, b]

            @pl.when(first_ref[j, b] == 1)
            def _():
                for cp in w_copies(e, slot):
                    cp.wait()
                nx = nxt_ref[j, b]

                @pl.when(nx >= 0)
                def _():
                    for cp in w_copies(nx, 1 - slot):
                        cp.start()

            for s in range(2):
                @pl.when(slot == s)
                def _(s=s):
                    gather(b + 1, xs_next)
                    mlp(xs_cur, yst_cur, e, s)
                    scatter(jnp.maximum(b - 1, 0), yst_prev)

    gather(0, xs_a)
    x2_copy.wait()

    def pair_body(pp, c):
        step(2 * pp, xs_a, yst_a, xs_b, yst_b)
        step(2 * pp + 1, xs_b, yst_b, xs_a, yst_a)
        return c

    lax.fori_loop(0, (nbj + 1) // 2, pair_body, 0)

    @pl.when(nbj % 2 == 1)
    def _():
        scatter(nbj - 1, yst_a)

    @pl.when(nbj % 2 == 0)
    def _():
        scatter(nbj - 1, yst_b)

    RC = 256

    def fin(ci, c):
        base = pl.multiple_of(ci * RC * 8, 8)
        yv = jnp.concatenate([yacc[pl.ds(base + g, RC, stride=8), :] for g in range(NG)], axis=1)
        out_ref[pl.ds(pl.multiple_of(ci * RC, 8), RC), :] = _rms(yv, gf_ref[...])
        return c

    lax.fori_loop(0, TT // RC, fin, 0)


def _moe_nb_max():
    return TOP_K * TT_MOE // RB_MOE + N_EXPERTS


def _moe(lists, x2t, hmt, w1, b1, w2, b2, g_final):
    blk_e, first, nxt, slot, nb, tok, wt = lists
    T = x2t.shape[0] // 8
    TT = TT_MOE
    nt = T // TT
    LP = tok.shape[-1]
    smem_row = pl.BlockSpec((1, 1, LP), lambda j, *_: (j, 0, 0), memory_space=pltpu.SMEM)
    whole = lambda a: pl.BlockSpec(a.shape, lambda j, *_: (0,) * a.ndim)
    grid_spec = pltpu.PrefetchScalarGridSpec(
        num_scalar_prefetch=5,
        grid=(nt,),
        in_specs=[
            smem_row, smem_row,
            pl.BlockSpec(memory_space=pl.ANY),
            pl.BlockSpec((TT * 8, 128), lambda j, *_: (j, 0)),
            pl.BlockSpec(memory_space=pl.ANY), pl.BlockSpec(memory_space=pl.ANY),
            whole(b1), whole(b2), whole(g_final),
        ],
        out_specs=pl.BlockSpec((TT, D_MODEL), lambda j, *_: (j, 0), pipeline_mode=pl.Buffered(1)),
        scratch_shapes=[
            pltpu.VMEM((TT * 8 + 8, 128), F32),
            pltpu.VMEM((RB_MOE * 8, 128), F32), pltpu.VMEM((RB_MOE * 8, 128), F32),
            pltpu.VMEM((RB_MOE * 8, 128), F32), pltpu.VMEM((RB_MOE * 8, 128), F32),
            pltpu.VMEM((2, D_MODEL, 2 * D_FF), BF16),
            pltpu.VMEM((2, D_FF, D_MODEL), BF16),
            pltpu.SemaphoreType.DMA((2, 2, MOE_W_CHUNKS)),
            pltpu.SemaphoreType.DMA((1,)),
        ],
    )
    return pl.pallas_call(
        _moe_kernel,
        grid_spec=grid_spec,
        out_shape=jax.ShapeDtypeStruct((T, D_MODEL), F32),
        compiler_params=pltpu.CompilerParams(
            dimension_semantics=("arbitrary",), vmem_limit_bytes=_vmem_limit(60 << 20)),
        name="moe",
    )(blk_e, first, nxt, slot, nb, tok, wt, x2t, hmt, w1, w2, b1, b2, g_final)


def _moe_lists(idx, wts):
    T = idx.shape[1]
    TT, RB = TT_MOE, RB_MOE
    nt = T // TT
    NB = _moe_nb_max()
    A = TOP_K * TT
    e_tile = idx.reshape(TOP_K, nt, TT).transpose(1, 0, 2).reshape(nt, A)
    w_tile = wts.reshape(TOP_K, nt, TT).transpose(1, 0, 2).reshape(nt, A)
    tok_local = jnp.tile(jnp.arange(TT, dtype=I32), TOP_K)
    order = jnp.argsort(e_tile, axis=1, stable=True)
    stok = jnp.take_along_axis(jnp.broadcast_to(tok_local, e_tile.shape), order, axis=1)
    swt = jnp.take_along_axis(w_tile, order, axis=1)

    experts = jnp.arange(N_EXPERTS, dtype=I32)
    counts = jnp.sum((e_tile[:, :, None] == experts[None, None, :]).astype(I32), axis=1)
    start = jnp.cumsum(counts, axis=1) - counts
    nblk = (counts + RB - 1) // RB
    blk_end = jnp.cumsum(nblk, axis=1)
    blk_start = blk_end - nblk
    nb = blk_end[:, -1]

    b = jnp.arange(NB, dtype=I32)
    blk_e = jnp.minimum(jnp.sum((blk_end[:, :, None] <= b[None, None, :]).astype(I32), axis=1), N_EXPERTS - 1)
    onehot = (blk_e[:, :, None] == experts[None, None, :]).astype(I32)
    per_block = lambda v: jnp.sum(onehot * v[:, None, :], axis=2)
    present = (nblk > 0).astype(I32)
    ordinal = jnp.cumsum(present, axis=1) - present
    later = (experts[None, :] > experts[:, None])[None] & (present[:, None, :] > 0)
    nxt_e = jnp.min(jnp.where(later, experts[None, None, :], N_EXPERTS), axis=2)
    nxt_e = jnp.where(nxt_e >= N_EXPERTS, -1, nxt_e)
    bs_b = per_block(blk_start)
    first = ((b[None, :] == bs_b) & (b[None, :] < nb[:, None])).astype(I32)
    slot = per_block(ordinal) % 2
    nxt = per_block(nxt_e)

    r = jnp.arange(RB, dtype=I32)[None, None, :]
    off = (b[None, :] - bs_b)[:, :, None] * RB + r
    valid = (off < per_block(counts)[:, :, None]) & (b[None, :, None] < nb[:, None, None])
    src = jnp.clip(per_block(start)[:, :, None] + off, 0, A - 1).reshape(nt, NB * RB)
    valid = valid.reshape(nt, NB * RB)
    tok = jnp.where(valid, jnp.take_along_axis(stok, src, axis=1), TT)
    wt = jnp.where(valid, jnp.take_along_axis(swt, src, axis=1), 0.0)
    tok = jnp.pad(tok, ((0, 0), (0, RB)), constant_values=TT)
    wt = jnp.pad(wt, ((0, 0), (0, RB)))
    return blk_e, first, nxt, slot, nb, (tok * 8)[:, None, :], wt[:, None, :]


def kernel(x, mem, g_mix, w_in, rel_bias, lambda_q1, lambda_k1, lambda_q2, lambda_k2, g_subln, conv_w, a_log,
           dt_bias, g_out_b, w_up_a, w_up_b, w_out, g_xattn, g_mem, w_q_x, w_kv_x, w_o_x, g_moe, w_router,
           b_router, w_mlp1, b_mlp1, w_mlp2, b_mlp2, g_final):
    B_, S_, _ = x.shape
    assert B_ == 1 and S_ % TT_MOE == 0 and x.dtype == F32
    l = 0
    x2d = x.reshape(S_, D_MODEL)
    row = lambda v: v.reshape(1, -1).astype(F32)

    wi = w_in[l]
    ab_cols = jnp.pad(wi[:, 3584:3592], ((0, 0), (0, 120)))
    w_main = jnp.concatenate(
        [wi[:, 0:1024], wi[:, 1536:3072], wi[:, 3072:3584], wi[:, 3592:5640], ab_cols], axis=1).astype(BF16)
    w_vt = wi[:, 1024:1536].T.astype(BF16)
    w_abt = jnp.pad(wi[:, 3584:3592].T, ((0, 8), (0, 0))).astype(BF16)

    q, k, vt, qkvb, zb, gates, ab, abt = _inproj(x2d, row(g_mix[l]), w_main, w_vt, w_abt)

    lam = (jnp.exp(jnp.sum(lambda_q1[l].astype(F32) * lambda_k1[l].astype(F32)))
           - jnp.exp(jnp.sum(lambda_q2[l].astype(F32) * lambda_k2[l].astype(F32))) + LAMBDA_INIT).reshape(1)
    bias_d, bias_p = _attn_bias_tables(rel_bias)
    oa = _attention(lam, q, k, vt, bias_d, bias_p, row(g_subln[l]))

    lane_pad = lambda v: jnp.pad(v.astype(F32), (0, 128 - v.shape[0])).reshape(1, 128)
    row_bcast = lambda v: jnp.broadcast_to(jnp.pad(v.astype(F32), (0, 16 - v.shape[0]))[:, None], (16, TC_GDN))
    ob = _gdn(qkvb, zb, ab, abt, jnp.pad(conv_w[l].astype(F32), ((0, 4), (0, 0))),
              lane_pad(a_log[l]), lane_pad(dt_bias[l]), row_bcast(a_log[l]), row_bcast(dt_bias[l]),
              row(g_out_b[l]))

    kx, vx = _memkv(mem.reshape(-1, D_MODEL), row(g_mem[l]), w_kv_x[l].astype(BF16))
    br = jnp.broadcast_to(b_router[l].astype(F32)[:, None], (N_EXPERTS, 128))
    x2t, hmt, idx, wts = _mix(
        x2d, oa, ob, gates, w_up_a[l].astype(BF16), w_up_b[l].astype(BF16), w_out[l].astype(BF16),
        row(g_xattn[l]), w_q_x[l].astype(BF16), kx, vx, w_o_x[l].astype(BF16), row(g_moe[l]),
        w_router[l].T.astype(F32), br)

    out = _moe(_moe_lists(idx, wts), x2t, hmt, w_mlp1[l].astype(BF16), b_mlp1[l].astype(F32)[:, None, :],
               w_mlp2[l].astype(BF16), b_mlp2[l].astype(F32)[:, None, :], row(g_final))
    return out.reshape(B_, S_, D_MODEL)
```

```python
import functools
import math

import jax
import jax.numpy as jnp
from jax import lax
from jax.experimental import pallas as pl
from jax.experimental.pallas import tpu as pltpu

F32 = jnp.float32
BF16 = jnp.bfloat16
I32 = jnp.int32

D_MODEL = 1024
CHUNK = 64
EPS = 1e-6
A_HEADS = 4
A_QK_DIM = 64
A_V_DIM = 128
REL_BUCKETS = 32
REL_MAX_DIST = 128
B_HEADS = 4
B_DIM = 128
CONV_WIDTH = 4
X_HEADS = 4
X_HEAD_DIM = 256
N_EXPERTS = 32
TOP_K = 4
D_FF = 1024
SWIGLU_LIMIT = 7.0
SWIGLU_ALPHA = 1.702
LAMBDA_INIT = 0.8 - 0.6 * math.exp(-0.3 * 0)

LOG2E = 1.4426950408889634
NEG_BIG = -1e30

V7X_LANES = 128
V7X_SUBLANES = 8
V7X_VMEM_BYTES = 64 * 1024 * 1024

TM_PROJ = 512
TQ = 512
TK = 512
TC_GDN = 256
TM_MIX = 512
TT_MOE = 2048
RB_MOE = 128
MOE_W_CHUNKS = 4

C_QA, C_KA, C_QKVB, C_ZB, C_GATE, C_AB, C_END = 0, 512, 1024, 2560, 3072, 5120, 5248

_NT = (((1,), (1,)), ((), ()))
_TN = (((0,), (0,)), ((), ()))


def _rms(x, g):
    return x * lax.rsqrt(jnp.mean(x * x, axis=-1, keepdims=True) + EPS) * g


def _sigmoid(x):
    return 1.0 / (1.0 + jnp.exp(-x))


def _vmem_limit(nbytes):
    return int(min(nbytes, V7X_VMEM_BYTES - 4 * 1024 * 1024))


def _inproj_kernel(x_ref, g_ref, w_ref, wvt_ref, wabt_ref,
                   q_ref, k_ref, vt_ref, qkvb_ref, zb_ref, gates_ref, ab_ref, abt_ref):
    h = _rms(x_ref[...], g_ref[...]).astype(BF16)

    def mm(c0, c1):
        return jnp.dot(h, w_ref[:, c0:c1], preferred_element_type=F32)

    nb = TM_PROJ // TK
    qa = mm(C_QA, C_KA) * (A_QK_DIM ** -0.5 * LOG2E)
    ka = mm(C_KA, C_QKVB)
    vt = lax.dot_general(wvt_ref[...], h, _NT, preferred_element_type=F32)
    for hh in range(A_HEADS):
        cs = slice(hh * 128, (hh + 1) * 128)
        q_ref[hh] = qa[:, cs].astype(BF16)
        for b in range(nb):
            rs = slice(b * TK, (b + 1) * TK)
            k_ref[hh, b] = ka[rs, cs].astype(BF16)
            vt_ref[hh, b] = vt[cs, rs].astype(BF16)
    for j in range(3):
        qkvb_ref[:, j * 512:(j + 1) * 512] = mm(C_QKVB + j * 512, C_QKVB + (j + 1) * 512)
    zb_ref[...] = mm(C_ZB, C_GATE)
    for j in range(4):
        gates_ref[:, j * 512:(j + 1) * 512] = mm(C_GATE + j * 512, C_GATE + (j + 1) * 512)
    ab_ref[...] = mm(C_AB, C_END)
    abt_ref[...] = lax.dot_general(wabt_ref[...], h, _NT, preferred_element_type=F32)


def _inproj(x2d, g_mix, w_main, w_vt, w_abt):
    T = x2d.shape[0]
    n = T // TM_PROJ
    nkb = T // TK
    nb = TM_PROJ // TK
    full = lambda shape: pl.BlockSpec(shape, lambda i: (0,) * len(shape))
    out_shape = (
        jax.ShapeDtypeStruct((A_HEADS, T, 128), BF16),
        jax.ShapeDtypeStruct((A_HEADS, nkb, TK, 128), BF16),
        jax.ShapeDtypeStruct((A_HEADS, nkb, 128, TK), BF16),
        jax.ShapeDtypeStruct((T, 1536), F32),
        jax.ShapeDtypeStruct((T, 512), F32),
        jax.ShapeDtypeStruct((T, 2048), F32),
        jax.ShapeDtypeStruct((T, 128), F32),
        jax.ShapeDtypeStruct((16, T), F32),
    )
    out_specs = (
        pl.BlockSpec((A_HEADS, TM_PROJ, 128), lambda i: (0, i, 0)),
        pl.BlockSpec((A_HEADS, nb, TK, 128), lambda i: (0, i, 0, 0)),
        pl.BlockSpec((A_HEADS, nb, 128, TK), lambda i: (0, i, 0, 0)),
        pl.BlockSpec((TM_PROJ, 1536), lambda i: (i, 0)),
        pl.BlockSpec((TM_PROJ, 512), lambda i: (i, 0)),
        pl.BlockSpec((TM_PROJ, 2048), lambda i: (i, 0)),
        pl.BlockSpec((TM_PROJ, 128), lambda i: (i, 0)),
        pl.BlockSpec((16, TM_PROJ), lambda i: (0, i)),
    )
    return pl.pallas_call(
        _inproj_kernel,
        grid=(n,),
        in_specs=[
            pl.BlockSpec((TM_PROJ, D_MODEL), lambda i: (i, 0)),
            full((1, D_MODEL)),
            full(w_main.shape),
            full(w_vt.shape),
            full(w_abt.shape),
        ],
        out_specs=out_specs,
        out_shape=out_shape,
        compiler_params=pltpu.CompilerParams(
            dimension_semantics=("arbitrary",), vmem_limit_bytes=_vmem_limit(56 << 20)),
        name="inproj",
    )(x2d, g_mix, w_main, w_vt, w_abt)


def _attn_kernel(lam_ref, q_ref, k_ref, vt_ref, bd_ref, bp_ref, gs_ref, o_ref, m_sc, l_sc, acc_sc, s_buf, pm_buf):
    qi = pl.program_id(1)
    q = q_ref[0]
    lane = lax.broadcasted_iota(I32, q.shape, 1)
    zero = jnp.zeros_like(q)
    qm = (jnp.where(lane < A_QK_DIM, q, zero), jnp.where(lane >= A_QK_DIM, q, zero))
    m_sc[...] = jnp.full(m_sc.shape, NEG_BIG, F32)
    l_sc[...] = jnp.zeros(l_sc.shape, F32)
    acc_sc[...] = jnp.zeros(acc_sc.shape, F32)

    nkb = k_ref.shape[1]

    def stage_a(slot, kb0):
        for j in range(2):
            kblk = k_ref[0, jnp.minimum(kb0 + j, nkb - 1)]
            for m in range(2):
                s = lax.dot_general(kblk, qm[m], _NT, preferred_element_type=F32)
                s_buf[slot, m, j] = s
                pm_buf[slot, m, j] = jnp.max(s, axis=0, keepdims=True)

    def stage_b(slot, kb0, biases):
        far = all(b is None for b in biases)
        for m in range(2):
            m_old = m_sc[m]
            m_new = m_old
            ss = []
            for j, bias in enumerate(biases):
                if far:
                    m_new = jnp.maximum(m_new, pm_buf[slot, m, j])
                else:
                    s = s_buf[slot, m, j]
                    s = s if bias is None else s + bias[0]
                    ss.append(s)
                    m_new = jnp.maximum(m_new, jnp.max(s, axis=0, keepdims=True))
            alpha = jnp.exp2(m_old - m_new)
            l_new = alpha * l_sc[m]
            acc = alpha * acc_sc[m]
            for j in range(len(biases)):
                s = s_buf[slot, m, j] if far else ss[j]
                p = jnp.exp2(s - m_new)
                l_new = l_new + jnp.sum(p, axis=0, keepdims=True)
                acc = acc + jnp.dot(vt_ref[0, kb0 + j], p.astype(BF16), preferred_element_type=F32)
            l_sc[m] = l_new
            acc_sc[m] = acc
            m_sc[m] = m_new

    def far_pair(slot_a, kb_a, slot_b, kb_b):
        def a_piece(j, m):
            kblk = k_ref[0, jnp.minimum(kb_a + j, nkb - 1)]
            s = lax.dot_general(kblk, qm[m], _NT, preferred_element_type=F32)
            s_buf[slot_a, m, j] = s
            pm_buf[slot_a, m, j] = jnp.max(s, axis=0, keepdims=True)

        for m in range(2):
            a_piece(0, m)
            m_old = m_sc[m]
            m_new = jnp.maximum(jnp.maximum(m_old, pm_buf[slot_b, m, 0]), pm_buf[slot_b, m, 1])
            alpha = jnp.exp2(m_old - m_new)
            l_new = alpha * l_sc[m]
            acc = alpha * acc_sc[m]
            for j in range(2):
                if j == 1:
                    a_piece(1, m)
                p = jnp.exp2(s_buf[slot_b, m, j] - m_new)
                l_new = l_new + jnp.sum(p, axis=0, keepdims=True)
                acc = acc + jnp.dot(vt_ref[0, kb_b + j], p.astype(BF16), preferred_element_type=F32)
            l_sc[m] = l_new
            acc_sc[m] = acc
            m_sc[m] = m_new

    far2 = [None, None]
    nfu = jnp.maximum(qi - 1, 0) // 2
    peel = nfu % 2

    @pl.when(peel == 1)
    def _():
        stage_a(0, 0)
        stage_b(0, 0, far2)

    stage_a(0, 2 * peel)

    def far_body(v, c):
        u = peel + 2 * v
        far_pair(1, 2 * u + 2, 0, 2 * u)
        far_pair(0, 2 * u + 4, 1, 2 * u + 2)
        return c

    lax.fori_loop(0, (nfu - peel) // 2, far_body, 0)
    kb0 = 2 * nfu

    @pl.when(qi % 2 == 1)
    def _():
        stage_b(0, kb0, [bp_ref, bd_ref])

    @pl.when((qi % 2 == 0) & (qi >= 2))
    def _():
        stage_a(1, kb0 + 2)
        stage_b(0, kb0, [None, bp_ref])
        stage_b(1, kb0 + 2, [bd_ref])

    @pl.when(qi == 0)
    def _():
        stage_b(0, kb0, [bd_ref])

    o = acc_sc[0] / l_sc[0] - lam_ref[0] * (acc_sc[1] / l_sc[1])
    ot = o.T
    o_ref[...] = (_rms(ot, gs_ref[...]) * (1.0 - LAMBDA_INIT)).astype(BF16)


def _attention(lam, q, k, vt, bias_d, bias_p, g_subln):
    T = q.shape[1]
    nq = T // TQ
    nkb = T // TK
    return pl.pallas_call(
        _attn_kernel,
        grid=(A_HEADS, nq),
        in_specs=[
            pl.BlockSpec(memory_space=pltpu.SMEM),
            pl.BlockSpec((1, TQ, 128), lambda h, i: (h, i, 0)),
            pl.BlockSpec((1, nkb, TK, 128), lambda h, i: (h, 0, 0, 0)),
            pl.BlockSpec((1, nkb, 128, TK), lambda h, i: (h, 0, 0, 0)),
            pl.BlockSpec((1, TK, TQ), lambda h, i: (h, 0, 0)),
            pl.BlockSpec((1, TK, TQ), lambda h, i: (h, 0, 0)),
            pl.BlockSpec((1, 128), lambda h, i: (0, 0)),
        ],
        out_specs=pl.BlockSpec((TQ, 128), lambda h, i: (i, h)),
        out_shape=jax.ShapeDtypeStruct((T, A_HEADS * A_V_DIM), BF16),
        scratch_shapes=[
            pltpu.VMEM((2, 1, TQ), F32),
            pltpu.VMEM((2, 1, TQ), F32),
            pltpu.VMEM((2, 128, TQ), F32),
            pltpu.VMEM((2, 2, 2, TK, TQ), F32),
            pltpu.VMEM((2, 2, 2, 1, TQ), F32),
        ],
        compiler_params=pltpu.CompilerParams(
            dimension_semantics=("arbitrary", "arbitrary"), vmem_limit_bytes=_vmem_limit(48 << 20)),
        name="attn",
    )(lam, q, k, vt, bias_d, bias_p, g_subln)


def _t5_bucket(rel):
    half = REL_BUCKETS // 2
    max_exact = half // 2
    ret = jnp.where(rel > 0, half, 0)
    n = jnp.abs(rel)
    large = max_exact + (jnp.log(jnp.maximum(n, 1).astype(F32) / max_exact)
                         / math.log(REL_MAX_DIST / max_exact) * (half - max_exact)).astype(I32)
    large = jnp.minimum(large, half - 1)
    return ret + jnp.where(n < max_exact, n, large)


def _attn_bias_tables(rel_bias):
    assert TK >= REL_MAX_DIST and TK == TQ and TK % CHUNK == 0
    kk = jnp.arange(TK, dtype=I32)[:, None]
    qq = jnp.arange(TQ, dtype=I32)[None, :]
    rb = rel_bias.astype(F32)
    far = rb[_t5_bucket(jnp.full((1,), -REL_MAX_DIST, I32))[0]]
    table = ((rb - far[None, :]) * LOG2E).T

    def lookup(rel):
        onehot = (_t5_bucket(rel).reshape(1, -1) == jnp.arange(REL_BUCKETS, dtype=I32)[:, None]).astype(F32)
        return jnp.dot(table, onehot, precision=lax.Precision.HIGHEST).reshape(A_HEADS, TK, TQ)

    bd = jnp.where((kk // CHUNK <= qq // CHUNK)[None], lookup(kk - qq), NEG_BIG)
    bp = lookup(kk - TK - qq)
    return bd, bp


def _gdn_kernel(x_ref, zb_ref, ab_ref, abt_ref, cw_ref, alc_ref, dtc_ref, alr_ref, dtr_ref, gob_ref,
                o_ref, xbuf, s_sc):
    i = pl.program_id(0)
    TC = TC_GDN
    NCH = TC // CHUNK
    HB = B_HEADS * CHUNK

    @pl.when(i == 0)
    def _():
        xbuf[0:8, :] = jnp.zeros((8, 1536), F32)
        s_sc[...] = jnp.zeros(s_sc.shape, F32)

    xbuf[8:8 + TC, :] = x_ref[...]
    y = cw_ref[3:4, :] * xbuf[8:8 + TC, :]
    for d in range(1, CONV_WIDTH):
        y = y + cw_ref[3 - d:4 - d, :] * xbuf[8 - d:8 - d + TC, :]
    xbuf[0:8, :] = x_ref[TC - 8:TC, :]
    a = y * _sigmoid(y)

    def l2n(t):
        return t * lax.rsqrt(jnp.sum(t * t, axis=-1, keepdims=True) + EPS)

    qh = [l2n(a[:, h * 128:(h + 1) * 128]) * (B_DIM ** -0.5) for h in range(B_HEADS)]
    kh = [l2n(a[:, 512 + h * 128:512 + (h + 1) * 128]) for h in range(B_HEADS)]
    vh = [a[:, 1024 + h * 128:1024 + (h + 1) * 128] for h in range(B_HEADS)]

    def softplus(t):
        return jnp.maximum(t, 0.0) + jnp.log(1.0 + jnp.exp(-jnp.abs(t)))

    ab = ab_ref[...]
    g_col = -jnp.exp(alc_ref[...]) * softplus(ab + dtc_ref[...])
    beta_col = _sigmoid(ab)
    g_row = -jnp.exp(alr_ref[...]) * softplus(abt_ref[...] + dtr_ref[...])

    rt = lax.broadcasted_iota(I32, (TC, TC), 0)
    ct = lax.broadcasted_iota(I32, (TC, TC), 1)
    same_chunk = (rt // CHUNK) == (ct // CHUNK)
    tril = jnp.where(same_chunk & (rt >= ct), 1.0, 0.0).astype(F32)
    triu = jnp.where(same_chunk & (rt <= ct), 1.0, 0.0).astype(F32)
    gc_col = jnp.dot(tril, g_col, precision=lax.Precision.HIGHEST, preferred_element_type=F32)
    gc_row = jnp.dot(g_row, triu, precision=lax.Precision.HIGHEST, preferred_element_type=F32)

    ri = lax.broadcasted_iota(I32, (HB, HB), 0)
    ci = lax.broadcasted_iota(I32, (HB, HB), 1)
    same_head = (ri // CHUNK) == (ci // CHUNK)
    incl = same_head & (ri >= ci)
    strict = same_head & (ri > ci)
    eye = jnp.where(ri == ci, 1.0, 0.0).astype(F32)

    def level_mask(s):
        return ((ri // (2 * s)) == (ci // (2 * s))) & (((ri // s) % 2) == 1) & (((ci // s) % 2) == 0)

    cat0 = lambda parts: jnp.concatenate(parts, axis=0)
    heads = range(B_HEADS)
    chunks = range(NCH)

    Kc, Qc, glc, gccc, Lc, QKc, invc, rhsc = [], [], [], [], [], [], [], []
    for c in chunks:
        rs = slice(c * CHUNK, (c + 1) * CHUNK)
        last = slice(c * CHUNK + CHUNK - 1, (c + 1) * CHUNK)
        K = cat0([kh[h][rs] for h in heads])
        Q = cat0([qh[h][rs] for h in heads])
        V = cat0([vh[h][rs] for h in heads])
        beta = cat0([beta_col[rs, 4 + h:5 + h] for h in heads])
        gcc = cat0([gc_col[rs, h:h + 1] for h in heads])
        gl = cat0([jnp.broadcast_to(gc_col[last, h:h + 1], (CHUNK, 1)) for h in heads])
        gcr = jnp.concatenate([gc_row[h:h + 1, rs] for h in heads], axis=1)
        Kb = K * beta
        e = jnp.exp(jnp.where(incl, gcc - gcr, 0.0))
        L = lax.dot_general(Kb, K, _NT, preferred_element_type=F32) * jnp.where(strict, e, 0.0)
        QKc.append(lax.dot_general(Q, K, _NT, preferred_element_type=F32) * jnp.where(incl, e, 0.0))
        rhsc.append(jnp.concatenate([V * beta, Kb * jnp.exp(gcc)], axis=1))
        invc.append(eye - jnp.where(level_mask(1), L, 0.0))
        Kc.append(K); Qc.append(Q); glc.append(gl); gccc.append(gcc); Lc.append(L)

    for s in (2, 4, 8, 16, 32):
        msk = level_mask(s)
        ts = [jnp.dot(invc[c], jnp.where(msk, Lc[c], 0.0), preferred_element_type=F32) for c in chunks]
        invc = [invc[c] - jnp.dot(ts[c], invc[c], preferred_element_type=F32) for c in chunks]
    solc = [jnp.dot(invc[c], rhsc[c], preferred_element_type=F32) for c in chunks]

    for c in chunks:
        rs = slice(c * CHUNK, (c + 1) * CHUNK)
        U = solc[c][:, :B_DIM]
        W = solc[c][:, B_DIM:]
        Qd = Qc[c] * jnp.exp(gccc[c])
        Kd = Kc[c] * jnp.exp(glc[c] - gccc[c])
        vnew = []
        ost = []
        for h in heads:
            hs = slice(h * CHUNK, (h + 1) * CHUNK)
            S = s_sc[h]
            vn = U[hs] - jnp.dot(W[hs], S, preferred_element_type=F32)
            ost.append(jnp.dot(Qd[hs], S, preferred_element_type=F32))
            s_sc[h] = S * jnp.exp(glc[c][h * CHUNK:h * CHUNK + 1, :]) + lax.dot_general(
                Kd[hs], vn, _TN, preferred_element_type=F32)
            vnew.append(vn)
        O = cat0(ost) + jnp.dot(QKc[c], cat0(vnew), preferred_element_type=F32)
        for h in heads:
            oh = _rms(O[h * CHUNK:(h + 1) * CHUNK], gob_ref[...])
            z = zb_ref[rs, h * 128:(h + 1) * 128]
            o_ref[rs, h * 128:(h + 1) * 128] = (oh * (z * _sigmoid(z))).astype(BF16)


def _gdn(qkvb, zb, ab, abt, conv_w8, alc, dtc, alr, dtr, gob):
    T = qkvb.shape[0]
    TC = TC_GDN
    full = lambda shape: pl.BlockSpec(shape, lambda i: (0,) * len(shape))
    return pl.pallas_call(
        _gdn_kernel,
        grid=(T // TC,),
        in_specs=[
            pl.BlockSpec((TC, 1536), lambda i: (i, 0)),
            pl.BlockSpec((TC, 512), lambda i: (i, 0)),
            pl.BlockSpec((TC, 128), lambda i: (i, 0)),
            pl.BlockSpec((16, TC), lambda i: (0, i)),
            full((8, 1536)), full((1, 128)), full((1, 128)), full((16, TC)), full((16, TC)), full((1, 128)),
        ],
        out_specs=pl.BlockSpec((TC, 512), lambda i: (i, 0)),
        out_shape=jax.ShapeDtypeStruct((T, 512), BF16),
        scratch_shapes=[pltpu.VMEM((TC + 8, 1536), F32), pltpu.VMEM((B_HEADS, B_DIM, B_DIM), F32)],
        compiler_params=pltpu.CompilerParams(
            dimension_semantics=("arbitrary",), vmem_limit_bytes=_vmem_limit(48 << 20)),
        name="gdn",
    )(qkvb, zb, ab, abt, conv_w8, alc, dtc, alr, dtr, gob)


def _memkv_kernel(m_ref, g_ref, w_ref, k_ref, v_ref):
    hm = _rms(m_ref[...], g_ref[...]).astype(BF16)
    kv = jnp.dot(hm, w_ref[...], preferred_element_type=F32)
    k_ref[...] = kv[:, :D_MODEL].astype(BF16)
    v_ref[...] = kv[:, D_MODEL:].astype(BF16)


def _memkv(mem2d, g_mem, w_kv):
    n = mem2d.shape[0]
    return pl.pallas_call(
        _memkv_kernel,
        out_shape=(jax.ShapeDtypeStruct((n, D_MODEL), BF16), jax.ShapeDtypeStruct((n, D_MODEL), BF16)),
        compiler_params=pltpu.CompilerParams(vmem_limit_bytes=_vmem_limit(32 << 20)),
        name="memkv",
    )(mem2d, g_mem, w_kv)


def _mix_kernel(x_ref, oa_ref, ob_ref, gates_ref, wua_ref, wub_ref, wout_ref, gx_ref, wq_ref, kx_ref, vx_ref,
                wo_ref, gm_ref, wr_ref, br_ref, x2t_ref, hmt_ref, idx_ref, wts_ref):
    NSUB = 2
    TS = TM_MIX // NSUB
    subs = range(NSUB)
    rows = [slice(sub * TS, (sub + 1) * TS) for sub in subs]
    dotf = functools.partial(jnp.dot, preferred_element_type=F32)

    ma = [dotf(oa_ref[rows[g], :], wua_ref[...]) for g in subs]
    mb = [dotf(ob_ref[rows[g], :], wub_ref[...]) for g in subs]
    merged = [(_sigmoid(gates_ref[rows[g], :D_MODEL]) * ma[g]
               + _sigmoid(gates_ref[rows[g], D_MODEL:]) * mb[g]).astype(BF16) for g in subs]
    x1 = [x_ref[rows[g], :] + dotf(merged[g], wout_ref[...]) for g in subs]
    hx = [_rms(x1[g], gx_ref[...]).astype(BF16) for g in subs]

    heads = [[] for _ in subs]
    for h in range(X_HEADS):
        cs = slice(h * X_HEAD_DIM, (h + 1) * X_HEAD_DIM)
        qh = [dotf(hx[g], wq_ref[:, cs]).astype(BF16) for g in subs]
        s = [lax.dot_general(qh[g], kx_ref[:, cs], _NT, preferred_element_type=F32) * (X_HEAD_DIM ** -0.5)
             for g in subs]
        for g in subs:
            sg = s[g] - jnp.max(s[g], axis=-1, keepdims=True)
            p = jnp.exp(sg)
            p = p / jnp.sum(p, axis=-1, keepdims=True)
            heads[g].append(dotf(p.astype(BF16), vx_ref[:, cs]).astype(BF16))
    x2 = [x1[g] + dotf(jnp.concatenate(heads[g], axis=1), wo_ref[...]) for g in subs]
    hm = [_rms(x2[g], gm_ref[...]) for g in subs]

    for g in subs:
        for c in range(D_MODEL // 128):
            x2t_ref[pl.ds(g * TS * 8 + c, TS, stride=8), :] = x2[g][:, c * 128:(c + 1) * 128]
            hmt_ref[pl.ds(g * TS * 8 + c, TS, stride=8), :] = hm[g][:, c * 128:(c + 1) * 128]

        logits = lax.dot_general(wr_ref[...], hm[g], _NT, precision=lax.Precision.HIGHEST,
                                 preferred_element_type=F32) + br_ref[:, 0:1]
        eidx = lax.broadcasted_iota(I32, logits.shape, 0)
        vals, idxs = [], []
        cur = logits
        for _ in range(TOP_K):
            mx = jnp.max(cur, axis=0, keepdims=True)
            ix = jnp.min(jnp.where(cur == mx, eidx, N_EXPERTS), axis=0, keepdims=True)
            vals.append(mx)
            idxs.append(ix)
            cur = jnp.where(eidx == ix, -jnp.inf, cur)
        ex = [jnp.exp(v - vals[0]) for v in vals]
        den = ex[0] + ex[1] + ex[2] + ex[3]
        idx_ref[:, rows[g]] = jnp.concatenate(idxs, axis=0)
        wts_ref[:, rows[g]] = jnp.concatenate([e / den for e in ex], axis=0)


def _mix(x2d, oa, ob, gates, wua, wub, wout, gx, wq, kx, vx, wo, gm, wr_t, br):
    T = x2d.shape[0]
    TM = TM_MIX
    full = lambda a: pl.BlockSpec(a.shape, lambda i: (0,) * a.ndim)
    out_shape = (
        jax.ShapeDtypeStruct((T * 8, 128), F32),
        jax.ShapeDtypeStruct((T * 8, 128), F32),
        jax.ShapeDtypeStruct((TOP_K, T), I32),
        jax.ShapeDtypeStruct((TOP_K, T), F32),
    )
    return pl.pallas_call(
        _mix_kernel,
        grid=(T // TM,),
        in_specs=[
            pl.BlockSpec((TM, D_MODEL), lambda i: (i, 0)),
            pl.BlockSpec((TM, 512), lambda i: (i, 0)),
            pl.BlockSpec((TM, 512), lambda i: (i, 0)),
            pl.BlockSpec((TM, 2048), lambda i: (i, 0)),
            full(wua), full(wub), full(wout), full(gx), full(wq), full(kx), full(vx), full(wo), full(gm),
            full(wr_t), full(br),
        ],
        out_specs=(
            pl.BlockSpec((TM * 8, 128), lambda i: (i, 0)),
            pl.BlockSpec((TM * 8, 128), lambda i: (i, 0)),
            pl.BlockSpec((TOP_K, TM), lambda i: (0, i)),
            pl.BlockSpec((TOP_K, TM), lambda i: (0, i)),
        ),
        out_shape=out_shape,
        compiler_params=pltpu.CompilerParams(
            dimension_semantics=("arbitrary",), vmem_limit_bytes=_vmem_limit(56 << 20)),
        name="mix",
    )(x2d, oa, ob, gates, wua, wub, wout, gx, wq, kx, vx, wo, gm, wr_t, br)


def _moe_kernel(blk_e_ref, first_ref, nxt_ref, slot_ref, r0_ref, valid_ref, nb_ref,
                tok_ref, wt_ref, x2t_hbm, hmt_ref, w1_hbm, w2_hbm, b1_ref, b2_ref, gf_ref,
                out_ref, yacc, xs_a, xs_b, yst_a, yst_b, w1buf, w2buf, wsem, xsem):
    j = pl.program_id(0)
    TT = TT_MOE
    RB = RB_MOE
    NG = D_MODEL // 128
    nbj = nb_ref[j]

    x2_copy = pltpu.make_async_copy(
        x2t_hbm.at[pl.ds(pl.multiple_of(j * TT * 8, 8), TT * 8), :], yacc.at[pl.ds(0, TT * 8), :], xsem.at[0])
    x2_copy.start()

    def w_copies(e, slot):
        cps = []
        for c in range(MOE_W_CHUNKS):
            for k, (src, dst) in enumerate(((w1_hbm, w1buf), (w2_hbm, w2buf))):
                rows = pl.ds(c * (src.shape[1] // MOE_W_CHUNKS), src.shape[1] // MOE_W_CHUNKS)
                cps.append(pltpu.make_async_copy(src.at[e, rows], dst.at[slot, rows], wsem.at[slot, k, c]))
        return cps

    for cp in w_copies(blk_e_ref[j, 0], slot_ref[j, 0]):
        cp.start()

    yacc[TT * 8:TT * 8 + 8, :] = jnp.zeros((8, 128), F32)
    yst_b[...] = jnp.zeros(yst_b.shape, F32)

    def gather(b, xs):
        r0 = r0_ref[j, b]
        for r in range(RB):
            xs[r * 8:(r + 1) * 8, :] = hmt_ref[pl.ds(pl.multiple_of(tok_ref[0, 0, r0 + r], 8), 8), :]

    def mlp(xs, yst, e, slot):
        xb = jnp.concatenate([xs[pl.ds(g, RB, stride=8), :] for g in range(NG)], axis=1).astype(BF16)
        hid = jnp.dot(xb, w1buf[slot], preferred_element_type=F32) + b1_ref[e]
        glu = jnp.minimum(hid[:, :D_FF], SWIGLU_LIMIT)
        lin = jnp.clip(hid[:, D_FF:], -SWIGLU_LIMIT, SWIGLU_LIMIT)
        act = glu * _sigmoid(SWIGLU_ALPHA * glu) * (lin + 1.0)
        ys = jnp.dot(act.astype(BF16), w2buf[slot], preferred_element_type=F32) + b2_ref[e]
        for g in range(NG):
            yst[pl.ds(g, RB, stride=8), :] = ys[:, g * 128:(g + 1) * 128]

    def scatter(b, yst):
        r0 = r0_ref[j, b]
        valid = valid_ref[j, b]
        for g0 in range(0, RB, 8):
            dsts, vals = [], []
            for r in range(g0, g0 + 8):
                t8 = jnp.where(r < valid, tok_ref[0, 0, r0 + r], TT * 8)
                dst = pl.ds(pl.multiple_of(t8, 8), 8)
                dsts.append(dst)
                vals.append(yacc[dst, :] + wt_ref[0, 0, r0 + r] * yst[r * 8:(r + 1) * 8, :])
            for dst, val in zip(dsts, vals):
                yacc[dst, :] = val

    def step(b, xs_cur, yst_cur, xs_next, yst_prev):
        @pl.when(b < nbj)
        def _():
            e = blk_e_ref[j, b]
            slot = slot_ref[j, b]

            @pl.when(first_ref[j, b] == 1)
            def _():
                for cp in w_copies(e, slot):
                    cp.wait()
                nx = nxt_ref[j, b]

                @pl.when(nx >= 0)
                def _():
                    for cp in w_copies(nx, 1 - slot):
                        cp.start()

            for s in range(2):
                @pl.when(slot == s)
                def _(s=s):
                    gather(b + 1, xs_next)
                    mlp(xs_cur, yst_cur, e, s)
                    scatter(jnp.maximum(b - 1, 0), yst_prev)

    gather(0, xs_a)
    x2_copy.wait()

    def pair_body(pp, c):
        step(2 * pp, xs_a, yst_a, xs_b, yst_b)
        step(2 * pp + 1, xs_b, yst_b, xs_a, yst_a)
        return c

    lax.fori_loop(0, (nbj + 1) // 2, pair_body, 0)

    @pl.when(nbj % 2 == 1)
    def _():
        scatter(nbj - 1, yst_a)

    @pl.when(nbj % 2 == 0)
    def _():
        scatter(nbj - 1, yst_b)

    RC = 256

    def fin(ci, c):
        base = pl.multiple_of(ci * RC * 8, 8)
        yv = jnp.concatenate([yacc[pl.ds(base + g, RC, stride=8), :] for g in range(NG)], axis=1)
        out_ref[pl.ds(pl.multiple_of(ci * RC, 8), RC), :] = _rms(yv, gf_ref[...])
        return c

    lax.fori_loop(0, TT // RC, fin, 0)


def _moe_nb_max():
    return TOP_K * TT_MOE // RB_MOE + N_EXPERTS


def _moe(lists, x2t, hmt, w1, b1, w2, b2, g_final):
    blk_e, first, nxt, slot, r0, valid, nb, tok, wt = lists
    T = x2t.shape[0] // 8
    TT = TT_MOE
    nt = T // TT
    LP = tok.shape[-1]
    smem_row = pl.BlockSpec((1, 1, LP), lambda j, *_: (j, 0, 0), memory_space=pltpu.SMEM)
    whole = lambda a: pl.BlockSpec(a.shape, lambda j, *_: (0,) * a.ndim)
    grid_spec = pltpu.PrefetchScalarGridSpec(
        num_scalar_prefetch=7,
        grid=(nt,),
        in_specs=[
            smem_row, smem_row,
            pl.BlockSpec(memory_space=pl.ANY),
            pl.BlockSpec((TT * 8, 128), lambda j, *_: (j, 0)),
            pl.BlockSpec(memory_space=pl.ANY), pl.BlockSpec(memory_space=pl.ANY),
            whole(b1), whole(b2), whole(g_final),
        ],
        out_specs=pl.BlockSpec((TT, D_MODEL), lambda j, *_: (j, 0), pipeline_mode=pl.Buffered(1)),
        scratch_shapes=[
            pltpu.VMEM((TT * 8 + 8, 128), F32),
            pltpu.VMEM((RB_MOE * 8, 128), F32), pltpu.VMEM((RB_MOE * 8, 128), F32),
            pltpu.VMEM((RB_MOE * 8, 128), F32), pltpu.VMEM((RB_MOE * 8, 128), F32),
            pltpu.VMEM((2, D_MODEL, 2 * D_FF), BF16),
            pltpu.VMEM((2, D_FF, D_MODEL), BF16),
            pltpu.SemaphoreType.DMA((2, 2, MOE_W_CHUNKS)),
            pltpu.SemaphoreType.DMA((1,)),
        ],
    )
    return pl.pallas_call(
        _moe_kernel,
        grid_spec=grid_spec,
        out_shape=jax.ShapeDtypeStruct((T, D_MODEL), F32),
        compiler_params=pltpu.CompilerParams(
            dimension_semantics=("arbitrary",), vmem_limit_bytes=_vmem_limit(60 << 20)),
        name="moe",
    )(blk_e, first, nxt, slot, r0, valid, nb, tok, wt, x2t, hmt, w1, w2, b1, b2, g_final)


def _moe_lists(idx, wts):
    T = idx.shape[1]
    TT, RB = TT_MOE, RB_MOE
    nt = T // TT
    NB = _moe_nb_max()
    A = TOP_K * TT
    e_tile = idx.reshape(TOP_K, nt, TT).transpose(1, 0, 2).reshape(nt, A)
    w_tile = wts.reshape(TOP_K, nt, TT).transpose(1, 0, 2).reshape(nt, A)
    tok_local = jnp.broadcast_to(jnp.tile(jnp.arange(TT, dtype=I32), TOP_K), e_tile.shape)
    _, stok, swt = lax.sort((e_tile, tok_local, w_tile), dimension=1, is_stable=True, num_keys=1)

    experts = jnp.arange(N_EXPERTS, dtype=I32)
    counts = jnp.sum((e_tile[:, :, None] == experts[None, None, :]).astype(I32), axis=1)
    start = jnp.cumsum(counts, axis=1) - counts
    nblk = (counts + RB - 1) // RB
    blk_end = jnp.cumsum(nblk, axis=1)
    blk_start = blk_end - nblk
    nb = blk_end[:, -1]

    b = jnp.arange(NB + 1, dtype=I32)
    blk_e = jnp.minimum(jnp.sum((blk_end[:, :, None] <= b[None, None, :]).astype(I32), axis=1), N_EXPERTS - 1)
    onehot = (blk_e[:, :, None] == experts[None, None, :]).astype(I32)
    per_block = lambda v: jnp.sum(onehot * v[:, None, :], axis=2)
    present = (nblk > 0).astype(I32)
    ordinal = jnp.cumsum(present, axis=1) - present
    later = (experts[None, :] > experts[:, None])[None] & (present[:, None, :] > 0)
    nxt_e = jnp.min(jnp.where(later, experts[None, None, :], N_EXPERTS), axis=2)
    nxt_e = jnp.where(nxt_e >= N_EXPERTS, -1, nxt_e)
    bs_b = per_block(blk_start)
    first = ((b[None, :] == bs_b) & (b[None, :] < nb[:, None])).astype(I32)
    slot = per_block(ordinal) % 2
    nxt = per_block(nxt_e)

    live = b[None, :] < nb[:, None]
    off = (b[None, :] - bs_b) * RB
    r0 = jnp.where(live, per_block(start) + off, A)
    valid = jnp.where(live, jnp.clip(per_block(counts) - off, 0, RB), 0)
    pad = ((0, 0), (0, RB))
    return blk_e, first, nxt, slot, r0, valid, nb, jnp.pad(stok * 8, pad)[:, None, :], jnp.pad(swt, pad)[:, None, :]


def kernel(x, mem, g_mix, w_in, rel_bias, lambda_q1, lambda_k1, lambda_q2, lambda_k2, g_subln, conv_w, a_log,
           dt_bias, g_out_b, w_up_a, w_up_b, w_out, g_xattn, g_mem, w_q_x, w_kv_x, w_o_x, g_moe, w_router,
           b_router, w_mlp1, b_mlp1, w_mlp2, b_mlp2, g_final):
    B_, S_, _ = x.shape
    assert B_ == 1 and S_ % TT_MOE == 0 and x.dtype == F32
    l = 0
    x2d = x.reshape(S_, D_MODEL)
    row = lambda v: v.reshape(1, -1).astype(F32)

    wi = w_in[l]
    ab_cols = jnp.pad(wi[:, 3584:3592], ((0, 0), (0, 120)))
    w_main = jnp.concatenate(
        [wi[:, 0:1024], wi[:, 1536:3072], wi[:, 3072:3584], wi[:, 3592:5640], ab_cols], axis=1).astype(BF16)
    w_vt = wi[:, 1024:1536].T.astype(BF16)
    w_abt = jnp.pad(wi[:, 3584:3592].T, ((0, 8), (0, 0))).astype(BF16)

    q, k, vt, qkvb, zb, gates, ab, abt = _inproj(x2d, row(g_mix[l]), w_main, w_vt, w_abt)

    lam = (jnp.exp(jnp.sum(lambda_q1[l].astype(F32) * lambda_k1[l].astype(F32)))
           - jnp.exp(jnp.sum(lambda_q2[l].astype(F32) * lambda_k2[l].astype(F32))) + LAMBDA_INIT).reshape(1)
    bias_d, bias_p = _attn_bias_tables(rel_bias)
    oa = _attention(lam, q, k, vt, bias_d, bias_p, row(g_subln[l]))

    lane_pad = lambda v: jnp.pad(v.astype(F32), (0, 128 - v.shape[0])).reshape(1, 128)
    row_bcast = lambda v: jnp.broadcast_to(jnp.pad(v.astype(F32), (0, 16 - v.shape[0]))[:, None], (16, TC_GDN))
    ob = _gdn(qkvb, zb, ab, abt, jnp.pad(conv_w[l].astype(F32), ((0, 4), (0, 0))),
              lane_pad(a_log[l]), lane_pad(dt_bias[l]), row_bcast(a_log[l]), row_bcast(dt_bias[l]),
              row(g_out_b[l]))

    kx, vx = _memkv(mem.reshape(-1, D_MODEL), row(g_mem[l]), w_kv_x[l].astype(BF16))
    br = jnp.broadcast_to(b_router[l].astype(F32)[:, None], (N_EXPERTS, 128))
    x2t, hmt, idx, wts = _mix(
        x2d, oa, ob, gates, w_up_a[l].astype(BF16), w_up_b[l].astype(BF16), w_out[l].astype(BF16),
        row(g_xattn[l]), w_q_x[l].astype(BF16), kx, vx, w_o_x[l].astype(BF16), row(g_moe[l]),
        w_router[l].T.astype(F32), br)

    out = _moe(_moe_lists(idx, wts), x2t, hmt, w_mlp1[l].astype(BF16), b_mlp1[l].astype(F32)[:, None, :],
               w_mlp2[l].astype(BF16), b_mlp2[l].astype(F32)[:, None, :], row(g_final))
    return out.reshape(B_, S_, D_MODEL)
```

```python
import functools
import math

import jax
import jax.numpy as jnp
from jax import lax
from jax.experimental import pallas as pl
from jax.experimental.pallas import tpu as pltpu

F32 = jnp.float32
BF16 = jnp.bfloat16
I32 = jnp.int32

D_MODEL = 1024
CHUNK = 64
EPS = 1e-6
A_HEADS = 4
A_QK_DIM = 64
A_V_DIM = 128
REL_BUCKETS = 32
REL_MAX_DIST = 128
B_HEADS = 4
B_DIM = 128
CONV_WIDTH = 4
X_HEADS = 4
X_HEAD_DIM = 256
N_EXPERTS = 32
TOP_K = 4
D_FF = 1024
SWIGLU_LIMIT = 7.0
SWIGLU_ALPHA = 1.702
LAMBDA_INIT = 0.8 - 0.6 * math.exp(-0.3 * 0)

LOG2E = 1.4426950408889634
NEG_BIG = -1e30

V7X_LANES = 128
V7X_SUBLANES = 8
V7X_VMEM_BYTES = 64 * 1024 * 1024

TM_PROJ = 512
TQ = 512
TK = 512
TC_GDN = 256
TM_MIX = 512
TT_MOE = 2048
RB_MOE = 128
MOE_W_CHUNKS = 1

C_QA, C_KA, C_QKVB, C_ZB, C_GATE, C_AB, C_END = 0, 512, 1024, 2560, 3072, 5120, 5248

_NT = (((1,), (1,)), ((), ()))
_TN = (((0,), (0,)), ((), ()))


def _rms(x, g):
    return x * lax.rsqrt(jnp.mean(x * x, axis=-1, keepdims=True) + EPS) * g


def _sigmoid(x):
    return 1.0 / (1.0 + jnp.exp(-x))


def _vmem_limit(nbytes):
    return int(min(nbytes, V7X_VMEM_BYTES - 4 * 1024 * 1024))


def _inproj_kernel(x_ref, g_ref, w_ref, wvt_ref, wabt_ref,
                   q_ref, k_ref, vt_ref, qkvb_ref, zb_ref, gates_ref, ab_ref, abt_ref):
    h = _rms(x_ref[...], g_ref[...]).astype(BF16)

    def mm(c0, c1):
        return jnp.dot(h, w_ref[:, c0:c1], preferred_element_type=F32)

    nb = TM_PROJ // TK
    qa = mm(C_QA, C_KA) * (A_QK_DIM ** -0.5 * LOG2E)
    ka = mm(C_KA, C_QKVB)
    vt = lax.dot_general(wvt_ref[...], h, _NT, preferred_element_type=F32)
    for hh in range(A_HEADS):
        cs = slice(hh * 128, (hh + 1) * 128)
        q_ref[hh] = qa[:, cs].astype(BF16)
        for b in range(nb):
            rs = slice(b * TK, (b + 1) * TK)
            k_ref[hh, b] = ka[rs, cs].astype(BF16)
            vt_ref[hh, b] = vt[cs, rs].astype(BF16)
    for j in range(3):
        qkvb_ref[:, j * 512:(j + 1) * 512] = mm(C_QKVB + j * 512, C_QKVB + (j + 1) * 512)
    zb_ref[...] = mm(C_ZB, C_GATE)
    for j in range(4):
        gates_ref[:, j * 512:(j + 1) * 512] = mm(C_GATE + j * 512, C_GATE + (j + 1) * 512)
    ab_ref[...] = mm(C_AB, C_END)
    abt_ref[...] = lax.dot_general(wabt_ref[...], h, _NT, preferred_element_type=F32)


def _inproj(x2d, g_mix, w_main, w_vt, w_abt):
    T = x2d.shape[0]
    n = T // TM_PROJ
    nkb = T // TK
    nb = TM_PROJ // TK
    full = lambda shape: pl.BlockSpec(shape, lambda i: (0,) * len(shape))
    out_shape = (
        jax.ShapeDtypeStruct((A_HEADS, T, 128), BF16),
        jax.ShapeDtypeStruct((A_HEADS, nkb, TK, 128), BF16),
        jax.ShapeDtypeStruct((A_HEADS, nkb, 128, TK), BF16),
        jax.ShapeDtypeStruct((T, 1536), F32),
        jax.ShapeDtypeStruct((T, 512), F32),
        jax.ShapeDtypeStruct((T, 2048), F32),
        jax.ShapeDtypeStruct((T, 128), F32),
        jax.ShapeDtypeStruct((16, T), F32),
    )
    out_specs = (
        pl.BlockSpec((A_HEADS, TM_PROJ, 128), lambda i: (0, i, 0)),
        pl.BlockSpec((A_HEADS, nb, TK, 128), lambda i: (0, i, 0, 0)),
        pl.BlockSpec((A_HEADS, nb, 128, TK), lambda i: (0, i, 0, 0)),
        pl.BlockSpec((TM_PROJ, 1536), lambda i: (i, 0)),
        pl.BlockSpec((TM_PROJ, 512), lambda i: (i, 0)),
        pl.BlockSpec((TM_PROJ, 2048), lambda i: (i, 0)),
        pl.BlockSpec((TM_PROJ, 128), lambda i: (i, 0)),
        pl.BlockSpec((16, TM_PROJ), lambda i: (0, i)),
    )
    return pl.pallas_call(
        _inproj_kernel,
        grid=(n,),
        in_specs=[
            pl.BlockSpec((TM_PROJ, D_MODEL), lambda i: (i, 0)),
            full((1, D_MODEL)),
            full(w_main.shape),
            full(w_vt.shape),
            full(w_abt.shape),
        ],
        out_specs=out_specs,
        out_shape=out_shape,
        compiler_params=pltpu.CompilerParams(
            dimension_semantics=("arbitrary",), vmem_limit_bytes=_vmem_limit(56 << 20)),
        name="inproj",
    )(x2d, g_mix, w_main, w_vt, w_abt)


def _attn_kernel(lam_ref, q_ref, k_ref, vt_ref, bd_ref, bp_ref, gs_ref, o_ref, m_sc, l_sc, acc_sc, s_buf, pm_buf):
    qi = pl.program_id(1)
    q = q_ref[0]
    lane = lax.broadcasted_iota(I32, q.shape, 1)
    zero = jnp.zeros_like(q)
    qm = (jnp.where(lane < A_QK_DIM, q, zero), jnp.where(lane >= A_QK_DIM, q, zero))
    m_sc[...] = jnp.full(m_sc.shape, NEG_BIG, F32)
    l_sc[...] = jnp.zeros(l_sc.shape, F32)
    acc_sc[...] = jnp.zeros(acc_sc.shape, F32)

    nkb = k_ref.shape[1]

    def stage_a(slot, kb0):
        for j in range(2):
            kblk = k_ref[0, jnp.minimum(kb0 + j, nkb - 1)]
            for m in range(2):
                s = lax.dot_general(kblk, qm[m], _NT, preferred_element_type=F32)
                s_buf[slot, m, j] = s
                pm_buf[slot, m, j] = jnp.max(s, axis=0, keepdims=True)

    def stage_b(slot, kb0, biases):
        far = all(b is None for b in biases)
        for m in range(2):
            m_old = m_sc[m]
            m_new = m_old
            ss = []
            for j, bias in enumerate(biases):
                if far:
                    m_new = jnp.maximum(m_new, pm_buf[slot, m, j])
                else:
                    s = s_buf[slot, m, j]
                    s = s if bias is None else s + bias[0]
                    ss.append(s)
                    m_new = jnp.maximum(m_new, jnp.max(s, axis=0, keepdims=True))
            alpha = jnp.exp2(m_old - m_new)
            l_new = alpha * l_sc[m]
            acc = alpha * acc_sc[m]
            for j in range(len(biases)):
                s = s_buf[slot, m, j] if far else ss[j]
                p = jnp.exp2(s - m_new)
                l_new = l_new + jnp.sum(p, axis=0, keepdims=True)
                acc = acc + jnp.dot(vt_ref[0, kb0 + j], p.astype(BF16), preferred_element_type=F32)
            l_sc[m] = l_new
            acc_sc[m] = acc
            m_sc[m] = m_new

    def far_pair(slot_a, kb_a, slot_b, kb_b):
        def a_piece(j, m):
            kblk = k_ref[0, jnp.minimum(kb_a + j, nkb - 1)]
            s = lax.dot_general(kblk, qm[m], _NT, preferred_element_type=F32)
            s_buf[slot_a, m, j] = s
            pm_buf[slot_a, m, j] = jnp.max(s, axis=0, keepdims=True)

        for m in range(2):
            a_piece(0, m)
            m_old = m_sc[m]
            m_new = jnp.maximum(jnp.maximum(m_old, pm_buf[slot_b, m, 0]), pm_buf[slot_b, m, 1])
            alpha = jnp.exp2(m_old - m_new)
            l_new = alpha * l_sc[m]
            acc = alpha * acc_sc[m]
            for j in range(2):
                if j == 1:
                    a_piece(1, m)
                p = jnp.exp2(s_buf[slot_b, m, j] - m_new)
                l_new = l_new + jnp.sum(p, axis=0, keepdims=True)
                acc = acc + jnp.dot(vt_ref[0, kb_b + j], p.astype(BF16), preferred_element_type=F32)
            l_sc[m] = l_new
            acc_sc[m] = acc
            m_sc[m] = m_new

    far2 = [None, None]
    nfu = jnp.maximum(qi - 1, 0) // 2
    peel = nfu % 2

    @pl.when(peel == 1)
    def _():
        stage_a(0, 0)
        stage_b(0, 0, far2)

    stage_a(0, 2 * peel)

    def far_body(v, c):
        u = peel + 2 * v
        far_pair(1, 2 * u + 2, 0, 2 * u)
        far_pair(0, 2 * u + 4, 1, 2 * u + 2)
        return c

    lax.fori_loop(0, (nfu - peel) // 2, far_body, 0)
    kb0 = 2 * nfu

    @pl.when(qi % 2 == 1)
    def _():
        stage_b(0, kb0, [bp_ref, bd_ref])

    @pl.when((qi % 2 == 0) & (qi >= 2))
    def _():
        stage_a(1, kb0 + 2)
        stage_b(0, kb0, [None, bp_ref])
        stage_b(1, kb0 + 2, [bd_ref])

    @pl.when(qi == 0)
    def _():
        stage_b(0, kb0, [bd_ref])

    o = acc_sc[0] / l_sc[0] - lam_ref[0] * (acc_sc[1] / l_sc[1])
    ot = o.T
    o_ref[...] = (_rms(ot, gs_ref[...]) * (1.0 - LAMBDA_INIT)).astype(BF16)


def _attention(lam, q, k, vt, bias_d, bias_p, g_subln):
    T = q.shape[1]
    nq = T // TQ
    nkb = T // TK
    return pl.pallas_call(
        _attn_kernel,
        grid=(A_HEADS, nq),
        in_specs=[
            pl.BlockSpec(memory_space=pltpu.SMEM),
            pl.BlockSpec((1, TQ, 128), lambda h, i: (h, i, 0)),
            pl.BlockSpec((1, nkb, TK, 128), lambda h, i: (h, 0, 0, 0)),
            pl.BlockSpec((1, nkb, 128, TK), lambda h, i: (h, 0, 0, 0)),
            pl.BlockSpec((1, TK, TQ), lambda h, i: (h, 0, 0)),
            pl.BlockSpec((1, TK, TQ), lambda h, i: (h, 0, 0)),
            pl.BlockSpec((1, 128), lambda h, i: (0, 0)),
        ],
        out_specs=pl.BlockSpec((TQ, 128), lambda h, i: (i, h)),
        out_shape=jax.ShapeDtypeStruct((T, A_HEADS * A_V_DIM), BF16),
        scratch_shapes=[
            pltpu.VMEM((2, 1, TQ), F32),
            pltpu.VMEM((2, 1, TQ), F32),
            pltpu.VMEM((2, 128, TQ), F32),
            pltpu.VMEM((2, 2, 2, TK, TQ), F32),
            pltpu.VMEM((2, 2, 2, 1, TQ), F32),
        ],
        compiler_params=pltpu.CompilerParams(
            dimension_semantics=("arbitrary", "arbitrary"), vmem_limit_bytes=_vmem_limit(48 << 20)),
        name="attn",
    )(lam, q, k, vt, bias_d, bias_p, g_subln)


def _t5_bucket(rel):
    half = REL_BUCKETS // 2
    max_exact = half // 2
    ret = jnp.where(rel > 0, half, 0)
    n = jnp.abs(rel)
    large = max_exact + (jnp.log(jnp.maximum(n, 1).astype(F32) / max_exact)
                         / math.log(REL_MAX_DIST / max_exact) * (half - max_exact)).astype(I32)
    large = jnp.minimum(large, half - 1)
    return ret + jnp.where(n < max_exact, n, large)


def _attn_bias_tables(rel_bias):
    assert TK >= REL_MAX_DIST and TK == TQ and TK % CHUNK == 0
    kk = jnp.arange(TK, dtype=I32)[:, None]
    qq = jnp.arange(TQ, dtype=I32)[None, :]
    rb = rel_bias.astype(F32)
    far = rb[_t5_bucket(jnp.full((1,), -REL_MAX_DIST, I32))[0]]
    table = ((rb - far[None, :]) * LOG2E).T

    def lookup(rel):
        onehot = (_t5_bucket(rel).reshape(1, -1) == jnp.arange(REL_BUCKETS, dtype=I32)[:, None]).astype(F32)
        return jnp.dot(table, onehot, precision=lax.Precision.HIGHEST).reshape(A_HEADS, TK, TQ)

    bd = jnp.where((kk // CHUNK <= qq // CHUNK)[None], lookup(kk - qq), NEG_BIG)
    bp = lookup(kk - TK - qq)
    return bd, bp


def _gdn_kernel(x_ref, zb_ref, ab_ref, abt_ref, cw_ref, alc_ref, dtc_ref, alr_ref, dtr_ref, gob_ref,
                o_ref, xbuf, s_sc):
    i = pl.program_id(0)
    TC = TC_GDN
    NCH = TC // CHUNK
    HB = B_HEADS * CHUNK

    @pl.when(i == 0)
    def _():
        xbuf[0:8, :] = jnp.zeros((8, 1536), F32)
        s_sc[...] = jnp.zeros(s_sc.shape, F32)

    xbuf[8:8 + TC, :] = x_ref[...]
    y = cw_ref[3:4, :] * xbuf[8:8 + TC, :]
    for d in range(1, CONV_WIDTH):
        y = y + cw_ref[3 - d:4 - d, :] * xbuf[8 - d:8 - d + TC, :]
    xbuf[0:8, :] = x_ref[TC - 8:TC, :]
    a = y * _sigmoid(y)

    def l2n(t):
        return t * lax.rsqrt(jnp.sum(t * t, axis=-1, keepdims=True) + EPS)

    qh = [l2n(a[:, h * 128:(h + 1) * 128]) * (B_DIM ** -0.5) for h in range(B_HEADS)]
    kh = [l2n(a[:, 512 + h * 128:512 + (h + 1) * 128]) for h in range(B_HEADS)]
    vh = [a[:, 1024 + h * 128:1024 + (h + 1) * 128] for h in range(B_HEADS)]

    def softplus(t):
        return jnp.maximum(t, 0.0) + jnp.log(1.0 + jnp.exp(-jnp.abs(t)))

    ab = ab_ref[...]
    g_col = -jnp.exp(alc_ref[...]) * softplus(ab + dtc_ref[...])
    beta_col = _sigmoid(ab)
    g_row = -jnp.exp(alr_ref[...]) * softplus(abt_ref[...] + dtr_ref[...])

    rt = lax.broadcasted_iota(I32, (TC, TC), 0)
    ct = lax.broadcasted_iota(I32, (TC, TC), 1)
    same_chunk = (rt // CHUNK) == (ct // CHUNK)
    tril = jnp.where(same_chunk & (rt >= ct), 1.0, 0.0).astype(F32)
    triu = jnp.where(same_chunk & (rt <= ct), 1.0, 0.0).astype(F32)
    gc_col = jnp.dot(tril, g_col, precision=lax.Precision.HIGHEST, preferred_element_type=F32)
    gc_row = jnp.dot(g_row, triu, precision=lax.Precision.HIGHEST, preferred_element_type=F32)

    ri = lax.broadcasted_iota(I32, (HB, HB), 0)
    ci = lax.broadcasted_iota(I32, (HB, HB), 1)
    same_head = (ri // CHUNK) == (ci // CHUNK)
    incl = same_head & (ri >= ci)
    strict = same_head & (ri > ci)
    eye = jnp.where(ri == ci, 1.0, 0.0).astype(F32)

    def level_mask(s):
        return ((ri // (2 * s)) == (ci // (2 * s))) & (((ri // s) % 2) == 1) & (((ci // s) % 2) == 0)

    cat0 = lambda parts: jnp.concatenate(parts, axis=0)
    heads = range(B_HEADS)
    chunks = range(NCH)

    Kc, Qc, glc, gccc, Lc, QKc, invc, rhsc = [], [], [], [], [], [], [], []
    for c in chunks:
        rs = slice(c * CHUNK, (c + 1) * CHUNK)
        last = slice(c * CHUNK + CHUNK - 1, (c + 1) * CHUNK)
        K = cat0([kh[h][rs] for h in heads])
        Q = cat0([qh[h][rs] for h in heads])
        V = cat0([vh[h][rs] for h in heads])
        beta = cat0([beta_col[rs, 4 + h:5 + h] for h in heads])
        gcc = cat0([gc_col[rs, h:h + 1] for h in heads])
        gl = cat0([jnp.broadcast_to(gc_col[last, h:h + 1], (CHUNK, 1)) for h in heads])
        gcr = jnp.concatenate([gc_row[h:h + 1, rs] for h in heads], axis=1)
        Kb = K * beta
        e = jnp.exp(jnp.where(incl, gcc - gcr, 0.0))
        L = lax.dot_general(Kb, K, _NT, preferred_element_type=F32) * jnp.where(strict, e, 0.0)
        QKc.append(lax.dot_general(Q, K, _NT, preferred_element_type=F32) * jnp.where(incl, e, 0.0))
        rhsc.append(jnp.concatenate([V * beta, Kb * jnp.exp(gcc)], axis=1))
        invc.append(eye - jnp.where(level_mask(1), L, 0.0))
        Kc.append(K); Qc.append(Q); glc.append(gl); gccc.append(gcc); Lc.append(L)

    for s in (2, 4, 8, 16, 32):
        msk = level_mask(s)
        ts = [jnp.dot(invc[c], jnp.where(msk, Lc[c], 0.0), preferred_element_type=F32) for c in chunks]
        invc = [invc[c] - jnp.dot(ts[c], invc[c], preferred_element_type=F32) for c in chunks]
    solc = [jnp.dot(invc[c], rhsc[c], preferred_element_type=F32) for c in chunks]

    for c in chunks:
        rs = slice(c * CHUNK, (c + 1) * CHUNK)
        U = solc[c][:, :B_DIM]
        W = solc[c][:, B_DIM:]
        Qd = Qc[c] * jnp.exp(gccc[c])
        Kd = Kc[c] * jnp.exp(glc[c] - gccc[c])
        vnew = []
        ost = []
        for h in heads:
            hs = slice(h * CHUNK, (h + 1) * CHUNK)
            S = s_sc[h]
            vn = U[hs] - jnp.dot(W[hs], S, preferred_element_type=F32)
            ost.append(jnp.dot(Qd[hs], S, preferred_element_type=F32))
            s_sc[h] = S * jnp.exp(glc[c][h * CHUNK:h * CHUNK + 1, :]) + lax.dot_general(
                Kd[hs], vn, _TN, preferred_element_type=F32)
            vnew.append(vn)
        O = cat0(ost) + jnp.dot(QKc[c], cat0(vnew), preferred_element_type=F32)
        for h in heads:
            oh = _rms(O[h * CHUNK:(h + 1) * CHUNK], gob_ref[...])
            z = zb_ref[rs, h * 128:(h + 1) * 128]
            o_ref[rs, h * 128:(h + 1) * 128] = (oh * (z * _sigmoid(z))).astype(BF16)


def _gdn(qkvb, zb, ab, abt, conv_w8, alc, dtc, alr, dtr, gob):
    T = qkvb.shape[0]
    TC = TC_GDN
    full = lambda shape: pl.BlockSpec(shape, lambda i: (0,) * len(shape))
    return pl.pallas_call(
        _gdn_kernel,
        grid=(T // TC,),
        in_specs=[
            pl.BlockSpec((TC, 1536), lambda i: (i, 0)),
            pl.BlockSpec((TC, 512), lambda i: (i, 0)),
            pl.BlockSpec((TC, 128), lambda i: (i, 0)),
            pl.BlockSpec((16, TC), lambda i: (0, i)),
            full((8, 1536)), full((1, 128)), full((1, 128)), full((16, TC)), full((16, TC)), full((1, 128)),
        ],
        out_specs=pl.BlockSpec((TC, 512), lambda i: (i, 0)),
        out_shape=jax.ShapeDtypeStruct((T, 512), BF16),
        scratch_shapes=[pltpu.VMEM((TC + 8, 1536), F32), pltpu.VMEM((B_HEADS, B_DIM, B_DIM), F32)],
        compiler_params=pltpu.CompilerParams(
            dimension_semantics=("arbitrary",), vmem_limit_bytes=_vmem_limit(48 << 20)),
        name="gdn",
    )(qkvb, zb, ab, abt, conv_w8, alc, dtc, alr, dtr, gob)


def _memkv_kernel(m_ref, g_ref, w_ref, k_ref, v_ref):
    hm = _rms(m_ref[...], g_ref[...]).astype(BF16)
    kv = jnp.dot(hm, w_ref[...], preferred_element_type=F32)
    k_ref[...] = kv[:, :D_MODEL].astype(BF16)
    v_ref[...] = kv[:, D_MODEL:].astype(BF16)


def _memkv(mem2d, g_mem, w_kv):
    n = mem2d.shape[0]
    return pl.pallas_call(
        _memkv_kernel,
        out_shape=(jax.ShapeDtypeStruct((n, D_MODEL), BF16), jax.ShapeDtypeStruct((n, D_MODEL), BF16)),
        compiler_params=pltpu.CompilerParams(vmem_limit_bytes=_vmem_limit(32 << 20)),
        name="memkv",
    )(mem2d, g_mem, w_kv)


def _mix_kernel(x_ref, oa_ref, ob_ref, gates_ref, wua_ref, wub_ref, wout_ref, gx_ref, wq_ref, kx_ref, vx_ref,
                wo_ref, gm_ref, wr_ref, br_ref, x2t_ref, hmt_ref, idx_ref, wts_ref, cnt_ref):
    NSUB = 2
    TS = TM_MIX // NSUB
    subs = range(NSUB)
    rows = [slice(sub * TS, (sub + 1) * TS) for sub in subs]
    dotf = functools.partial(jnp.dot, preferred_element_type=F32)

    ma = [dotf(oa_ref[rows[g], :], wua_ref[...]) for g in subs]
    mb = [dotf(ob_ref[rows[g], :], wub_ref[...]) for g in subs]
    merged = [(_sigmoid(gates_ref[rows[g], :D_MODEL]) * ma[g]
               + _sigmoid(gates_ref[rows[g], D_MODEL:]) * mb[g]).astype(BF16) for g in subs]
    x1 = [x_ref[rows[g], :] + dotf(merged[g], wout_ref[...]) for g in subs]
    hx = [_rms(x1[g], gx_ref[...]).astype(BF16) for g in subs]

    heads = [[] for _ in subs]
    for h in range(X_HEADS):
        cs = slice(h * X_HEAD_DIM, (h + 1) * X_HEAD_DIM)
        qh = [dotf(hx[g], wq_ref[:, cs]).astype(BF16) for g in subs]
        s = [lax.dot_general(qh[g], kx_ref[:, cs], _NT, preferred_element_type=F32) * (X_HEAD_DIM ** -0.5)
             for g in subs]
        for g in subs:
            sg = s[g] - jnp.max(s[g], axis=-1, keepdims=True)
            p = jnp.exp(sg)
            p = p / jnp.sum(p, axis=-1, keepdims=True)
            heads[g].append(dotf(p.astype(BF16), vx_ref[:, cs]).astype(BF16))
    x2 = [x1[g] + dotf(jnp.concatenate(heads[g], axis=1), wo_ref[...]) for g in subs]
    hm = [_rms(x2[g], gm_ref[...]) for g in subs]

    for g in subs:
        for c in range(D_MODEL // 128):
            x2t_ref[pl.ds(g * TS * 8 + c, TS, stride=8), :] = x2[g][:, c * 128:(c + 1) * 128]
            hmt_ref[pl.ds(g * TS * 8 + c, TS, stride=8), :] = hm[g][:, c * 128:(c + 1) * 128]

        logits = lax.dot_general(wr_ref[...], hm[g], _NT, precision=lax.Precision.HIGHEST,
                                 preferred_element_type=F32) + br_ref[:, 0:1]
        eidx = lax.broadcasted_iota(I32, logits.shape, 0)
        vals, idxs = [], []
        cur = logits
        for _ in range(TOP_K):
            mx = jnp.max(cur, axis=0, keepdims=True)
            ix = jnp.min(jnp.where(cur == mx, eidx, N_EXPERTS), axis=0, keepdims=True)
            vals.append(mx)
            idxs.append(ix)
            cur = jnp.where(eidx == ix, -jnp.inf, cur)
        ex = [jnp.exp(v - vals[0]) for v in vals]
        den = ex[0] + ex[1] + ex[2] + ex[3]
        idx_ref[:, rows[g]] = jnp.concatenate(idxs, axis=0)
        wts_ref[:, rows[g]] = jnp.concatenate([e / den for e in ex], axis=0)
        hits = sum(jnp.where(eidx == ix, 1.0, 0.0) for ix in idxs)
        cnt = jnp.broadcast_to(jnp.sum(hits, axis=1, keepdims=True), (N_EXPERTS, 128))
        cnt_ref[0] = cnt if g == 0 else cnt_ref[0] + cnt


def _mix(x2d, oa, ob, gates, wua, wub, wout, gx, wq, kx, vx, wo, gm, wr_t, br):
    T = x2d.shape[0]
    TM = TM_MIX
    full = lambda a: pl.BlockSpec(a.shape, lambda i: (0,) * a.ndim)
    out_shape = (
        jax.ShapeDtypeStruct((T * 8, 128), F32),
        jax.ShapeDtypeStruct((T * 8, 128), F32),
        jax.ShapeDtypeStruct((TOP_K, T), I32),
        jax.ShapeDtypeStruct((TOP_K, T), F32),
        jax.ShapeDtypeStruct((T // TM, N_EXPERTS, 128), F32),
    )
    return pl.pallas_call(
        _mix_kernel,
        grid=(T // TM,),
        in_specs=[
            pl.BlockSpec((TM, D_MODEL), lambda i: (i, 0)),
            pl.BlockSpec((TM, 512), lambda i: (i, 0)),
            pl.BlockSpec((TM, 512), lambda i: (i, 0)),
            pl.BlockSpec((TM, 2048), lambda i: (i, 0)),
            full(wua), full(wub), full(wout), full(gx), full(wq), full(kx), full(vx), full(wo), full(gm),
            full(wr_t), full(br),
        ],
        out_specs=(
            pl.BlockSpec((TM * 8, 128), lambda i: (i, 0)),
            pl.BlockSpec((TM * 8, 128), lambda i: (i, 0)),
            pl.BlockSpec((TOP_K, TM), lambda i: (0, i)),
            pl.BlockSpec((TOP_K, TM), lambda i: (0, i)),
            pl.BlockSpec((1, N_EXPERTS, 128), lambda i: (i, 0, 0)),
        ),
        out_shape=out_shape,
        compiler_params=pltpu.CompilerParams(
            dimension_semantics=("arbitrary",), vmem_limit_bytes=_vmem_limit(56 << 20)),
        name="mix",
    )(x2d, oa, ob, gates, wua, wub, wout, gx, wq, kx, vx, wo, gm, wr_t, br)


def _moe_kernel(blk_e_ref, first_ref, nxt_ref, slot_ref, r0_ref, valid_ref, nb_ref,
                tok_ref, wt_ref, x2t_hbm, hmt_ref, w1_hbm, w2_hbm, b1_ref, b2_ref, gf_ref,
                out_ref, yacc, xs_a, xs_b, yst_a, yst_b, w1buf, w2buf, wsem, xsem):
    j = pl.program_id(0)
    TT = TT_MOE
    RB = RB_MOE
    NG = D_MODEL // 128
    nbj = nb_ref[j]

    x2_copy = pltpu.make_async_copy(
        x2t_hbm.at[pl.ds(pl.multiple_of(j * TT * 8, 8), TT * 8), :], yacc.at[pl.ds(0, TT * 8), :], xsem.at[0])
    x2_copy.start()

    def w_copies(e, slot):
        cps = []
        for c in range(MOE_W_CHUNKS):
            for k, (src, dst) in enumerate(((w1_hbm, w1buf), (w2_hbm, w2buf))):
                rows = pl.ds(c * (src.shape[1] // MOE_W_CHUNKS), src.shape[1] // MOE_W_CHUNKS)
                cps.append(pltpu.make_async_copy(src.at[e, rows], dst.at[slot, rows], wsem.at[slot, k, c]))
        return cps

    for cp in w_copies(blk_e_ref[j, 0], slot_ref[j, 0]):
        cp.start()

    yacc[TT * 8:TT * 8 + 8, :] = jnp.zeros((8, 128), F32)
    yst_b[...] = jnp.zeros(yst_b.shape, F32)

    def gather(b, xs):
        r0 = r0_ref[j, b]
        for r in range(RB):
            xs[r * 8:(r + 1) * 8, :] = hmt_ref[pl.ds(pl.multiple_of(tok_ref[0, 0, r0 + r], 8), 8), :]

    def mlp(xs, yst, e, slot):
        xb = jnp.concatenate([xs[pl.ds(g, RB, stride=8), :] for g in range(NG)], axis=1).astype(BF16)
        hid = jnp.dot(xb, w1buf[slot], preferred_element_type=F32) + b1_ref[e]
        glu = jnp.minimum(hid[:, :D_FF], SWIGLU_LIMIT)
        lin = jnp.clip(hid[:, D_FF:], -SWIGLU_LIMIT, SWIGLU_LIMIT)
        act = glu * _sigmoid(SWIGLU_ALPHA * glu) * (lin + 1.0)
        ys = jnp.dot(act.astype(BF16), w2buf[slot], preferred_element_type=F32) + b2_ref[e]
        for g in range(NG):
            yst[pl.ds(g, RB, stride=8), :] = ys[:, g * 128:(g + 1) * 128]

    def scatter(b, yst):
        r0 = r0_ref[j, b]
        valid = valid_ref[j, b]
        for g0 in range(0, RB, 8):
            dsts, vals = [], []
            for r in range(g0, g0 + 8):
                t8 = jnp.where(r < valid, tok_ref[0, 0, r0 + r], TT * 8)
                dst = pl.ds(pl.multiple_of(t8, 8), 8)
                dsts.append(dst)
                vals.append(yacc[dst, :] + wt_ref[0, 0, r0 + r] * yst[r * 8:(r + 1) * 8, :])
            for dst, val in zip(dsts, vals):
                yacc[dst, :] = val

    def step(b, xs_cur, yst_cur, xs_next, yst_prev):
        @pl.when(b < nbj)
        def _():
            e = blk_e_ref[j, b]
            slot = slot_ref[j, b]

            @pl.when(first_ref[j, b] == 1)
            def _():
                for cp in w_copies(e, slot):
                    cp.wait()
                nx = nxt_ref[j, b]

                @pl.when(nx >= 0)
                def _():
                    for cp in w_copies(nx, 1 - slot):
                        cp.start()

            for s in range(2):
                @pl.when(slot == s)
                def _(s=s):
                    gather(b + 1, xs_next)
                    mlp(xs_cur, yst_cur, e, s)
                    scatter(jnp.maximum(b - 1, 0), yst_prev)

    gather(0, xs_a)
    x2_copy.wait()

    def pair_body(pp, c):
        step(2 * pp, xs_a, yst_a, xs_b, yst_b)
        step(2 * pp + 1, xs_b, yst_b, xs_a, yst_a)
        return c

    lax.fori_loop(0, (nbj + 1) // 2, pair_body, 0)

    @pl.when(nbj % 2 == 1)
    def _():
        scatter(nbj - 1, yst_a)

    @pl.when(nbj % 2 == 0)
    def _():
        scatter(nbj - 1, yst_b)

    RC = 256

    def fin(ci, c):
        base = pl.multiple_of(ci * RC * 8, 8)
        yv = jnp.concatenate([yacc[pl.ds(base + g, RC, stride=8), :] for g in range(NG)], axis=1)
        out_ref[pl.ds(pl.multiple_of(ci * RC, 8), RC), :] = _rms(yv, gf_ref[...])
        return c

    lax.fori_loop(0, TT // RC, fin, 0)


def _moe_nb_max():
    return TOP_K * TT_MOE // RB_MOE + N_EXPERTS


def _moe(lists, x2t, hmt, w1, b1, w2, b2, g_final):
    blk_e, first, nxt, slot, r0, valid, nb, tok, wt = lists
    T = x2t.shape[0] // 8
    TT = TT_MOE
    nt = T // TT
    LP = tok.shape[-1]
    smem_row = pl.BlockSpec((1, 1, LP), lambda j, *_: (j, 0, 0), memory_space=pltpu.SMEM)
    whole = lambda a: pl.BlockSpec(a.shape, lambda j, *_: (0,) * a.ndim)
    grid_spec = pltpu.PrefetchScalarGridSpec(
        num_scalar_prefetch=7,
        grid=(nt,),
        in_specs=[
            smem_row, smem_row,
            pl.BlockSpec(memory_space=pl.ANY),
            pl.BlockSpec((TT * 8, 128), lambda j, *_: (j, 0)),
            pl.BlockSpec(memory_space=pl.ANY), pl.BlockSpec(memory_space=pl.ANY),
            whole(b1), whole(b2), whole(g_final),
        ],
        out_specs=pl.BlockSpec((TT, D_MODEL), lambda j, *_: (j, 0), pipeline_mode=pl.Buffered(1)),
        scratch_shapes=[
            pltpu.VMEM((TT * 8 + 8, 128), F32),
            pltpu.VMEM((RB_MOE * 8, 128), F32), pltpu.VMEM((RB_MOE * 8, 128), F32),
            pltpu.VMEM((RB_MOE * 8, 128), F32), pltpu.VMEM((RB_MOE * 8, 128), F32),
            pltpu.VMEM((2, D_MODEL, 2 * D_FF), BF16),
            pltpu.VMEM((2, D_FF, D_MODEL), BF16),
            pltpu.SemaphoreType.DMA((2, 2, MOE_W_CHUNKS)),
            pltpu.SemaphoreType.DMA((1,)),
        ],
    )
    return pl.pallas_call(
        _moe_kernel,
        grid_spec=grid_spec,
        out_shape=jax.ShapeDtypeStruct((T, D_MODEL), F32),
        compiler_params=pltpu.CompilerParams(
            dimension_semantics=("arbitrary",), vmem_limit_bytes=_vmem_limit(60 << 20)),
        name="moe",
    )(blk_e, first, nxt, slot, r0, valid, nb, tok, wt, x2t, hmt, w1, w2, b1, b2, g_final)


def _moe_lists(idx, wts, step_counts):
    T = idx.shape[1]
    TT, RB = TT_MOE, RB_MOE
    nt = T // TT
    NB = _moe_nb_max()
    A = TOP_K * TT
    e_tile = idx.reshape(TOP_K, nt, TT).transpose(1, 0, 2).reshape(nt, A)
    w_tile = wts.reshape(TOP_K, nt, TT).transpose(1, 0, 2).reshape(nt, A)
    tok_local = jnp.broadcast_to(jnp.tile(jnp.arange(TT, dtype=I32), TOP_K), e_tile.shape)
    _, stok, swt = lax.sort((e_tile, tok_local, w_tile), dimension=1, is_stable=True, num_keys=1)

    experts = jnp.arange(N_EXPERTS, dtype=I32)
    counts = jnp.sum(step_counts[:, :, 0].reshape(nt, -1, N_EXPERTS), axis=1).astype(I32)
    start = jnp.cumsum(counts, axis=1) - counts
    nblk = (counts + RB - 1) // RB
    blk_end = jnp.cumsum(nblk, axis=1)
    blk_start = blk_end - nblk
    nb = blk_end[:, -1]

    b = jnp.arange(NB + 1, dtype=I32)
    blk_e = jnp.minimum(jnp.sum((blk_end[:, :, None] <= b[None, None, :]).astype(I32), axis=1), N_EXPERTS - 1)
    onehot = (blk_e[:, :, None] == experts[None, None, :]).astype(I32)
    per_block = lambda v: jnp.sum(onehot * v[:, None, :], axis=2)
    present = (nblk > 0).astype(I32)
    ordinal = jnp.cumsum(present, axis=1) - present
    later = (experts[None, :] > experts[:, None])[None] & (present[:, None, :] > 0)
    nxt_e = jnp.min(jnp.where(later, experts[None, None, :], N_EXPERTS), axis=2)
    nxt_e = jnp.where(nxt_e >= N_EXPERTS, -1, nxt_e)
    bs_b = per_block(blk_start)
    first = ((b[None, :] == bs_b) & (b[None, :] < nb[:, None])).astype(I32)
    slot = per_block(ordinal) % 2
    nxt = per_block(nxt_e)

    live = b[None, :] < nb[:, None]
    off = (b[None, :] - bs_b) * RB
    r0 = jnp.where(live, per_block(start) + off, A)
    valid = jnp.where(live, jnp.clip(per_block(counts) - off, 0, RB), 0)
    pad = ((0, 0), (0, RB))
    return blk_e, first, nxt, slot, r0, valid, nb, jnp.pad(stok * 8, pad)[:, None, :], jnp.pad(swt, pad)[:, None, :]


def kernel(x, mem, g_mix, w_in, rel_bias, lambda_q1, lambda_k1, lambda_q2, lambda_k2, g_subln, conv_w, a_log,
           dt_bias, g_out_b, w_up_a, w_up_b, w_out, g_xattn, g_mem, w_q_x, w_kv_x, w_o_x, g_moe, w_router,
           b_router, w_mlp1, b_mlp1, w_mlp2, b_mlp2, g_final):
    B_, S_, _ = x.shape
    assert B_ == 1 and S_ % TT_MOE == 0 and x.dtype == F32
    l = 0
    x2d = x.reshape(S_, D_MODEL)
    row = lambda v: v.reshape(1, -1).astype(F32)

    wi = w_in[l]
    ab_cols = jnp.pad(wi[:, 3584:3592], ((0, 0), (0, 120)))
    w_main = jnp.concatenate(
        [wi[:, 0:1024], wi[:, 1536:3072], wi[:, 3072:3584], wi[:, 3592:5640], ab_cols], axis=1).astype(BF16)
    w_vt = wi[:, 1024:1536].T.astype(BF16)
    w_abt = jnp.pad(wi[:, 3584:3592].T, ((0, 8), (0, 0))).astype(BF16)

    q, k, vt, qkvb, zb, gates, ab, abt = _inproj(x2d, row(g_mix[l]), w_main, w_vt, w_abt)

    lam = (jnp.exp(jnp.sum(lambda_q1[l].astype(F32) * lambda_k1[l].astype(F32)))
           - jnp.exp(jnp.sum(lambda_q2[l].astype(F32) * lambda_k2[l].astype(F32))) + LAMBDA_INIT).reshape(1)
    bias_d, bias_p = _attn_bias_tables(rel_bias)
    oa = _attention(lam, q, k, vt, bias_d, bias_p, row(g_subln[l]))

    lane_pad = lambda v: jnp.pad(v.astype(F32), (0, 128 - v.shape[0])).reshape(1, 128)
    row_bcast = lambda v: jnp.broadcast_to(jnp.pad(v.astype(F32), (0, 16 - v.shape[0]))[:, None], (16, TC_GDN))
    ob = _gdn(qkvb, zb, ab, abt, jnp.pad(conv_w[l].astype(F32), ((0, 4), (0, 0))),
              lane_pad(a_log[l]), lane_pad(dt_bias[l]), row_bcast(a_log[l]), row_bcast(dt_bias[l]),
              row(g_out_b[l]))

    kx, vx = _memkv(mem.reshape(-1, D_MODEL), row(g_mem[l]), w_kv_x[l].astype(BF16))
    br = jnp.broadcast_to(b_router[l].astype(F32)[:, None], (N_EXPERTS, 128))
    x2t, hmt, idx, wts, step_counts = _mix(
        x2d, oa, ob, gates, w_up_a[l].astype(BF16), w_up_b[l].astype(BF16), w_out[l].astype(BF16),
        row(g_xattn[l]), w_q_x[l].astype(BF16), kx, vx, w_o_x[l].astype(BF16), row(g_moe[l]),
        w_router[l].T.astype(F32), br)

    out = _moe(_moe_lists(idx, wts, step_counts), x2t, hmt, w_mlp1[l].astype(BF16),
               b_mlp1[l].astype(F32)[:, None, :],
               w_mlp2[l].astype(BF16), b_mlp2[l].astype(F32)[:, None, :], row(g_final))
    return out.reshape(B_, S_, D_MODEL)
```

```python
import functools
import math

import jax
import jax.numpy as jnp
from jax import lax
from jax.experimental import pallas as pl
from jax.experimental.pallas import tpu as pltpu

F32 = jnp.float32
BF16 = jnp.bfloat16
I32 = jnp.int32

D_MODEL = 1024
CHUNK = 64
EPS = 1e-6
A_HEADS = 4
A_QK_DIM = 64
A_V_DIM = 128
REL_BUCKETS = 32
REL_MAX_DIST = 128
B_HEADS = 4
B_DIM = 128
CONV_WIDTH = 4
X_HEADS = 4
X_HEAD_DIM = 256
N_EXPERTS = 32
TOP_K = 4
D_FF = 1024
SWIGLU_LIMIT = 7.0
SWIGLU_ALPHA = 1.702
LAMBDA_INIT = 0.8 - 0.6 * math.exp(-0.3 * 0)

LOG2E = 1.4426950408889634
NEG_BIG = -1e30

V7X_LANES = 128
V7X_SUBLANES = 8
V7X_VMEM_BYTES = 64 * 1024 * 1024

TM_PROJ = 512
TQ = 512
TK = 512
TC_GDN = 256
TM_MIX = 512
TT_MOE = 2048
RB_MOE = 128
MOE_W_CHUNKS = 1

C_QA, C_KA, C_QKVB, C_ZB, C_GATE, C_AB, C_END = 0, 512, 1024, 2560, 3072, 5120, 5248

_NT = (((1,), (1,)), ((), ()))
_TN = (((0,), (0,)), ((), ()))


def _rms(x, g):
    return x * lax.rsqrt(jnp.mean(x * x, axis=-1, keepdims=True) + EPS) * g


def _sigmoid(x):
    return 1.0 / (1.0 + jnp.exp(-x))


def _vmem_limit(nbytes):
    return int(min(nbytes, V7X_VMEM_BYTES - 4 * 1024 * 1024))


def _inproj_kernel(x_ref, g_ref, w_ref, wvt_ref, wabt_ref,
                   q_ref, k_ref, vt_ref, qkvb_ref, zb_ref, gates_ref, ab_ref, abt_ref):
    h = _rms(x_ref[...], g_ref[...]).astype(BF16)

    def mm(c0, c1):
        return jnp.dot(h, w_ref[:, c0:c1], preferred_element_type=F32)

    nb = TM_PROJ // TK
    qa = mm(C_QA, C_KA) * (A_QK_DIM ** -0.5 * LOG2E)
    ka = mm(C_KA, C_QKVB)
    vt = lax.dot_general(wvt_ref[...], h, _NT, preferred_element_type=F32)
    for hh in range(A_HEADS):
        cs = slice(hh * 128, (hh + 1) * 128)
        q_ref[hh] = qa[:, cs].astype(BF16)
        for b in range(nb):
            rs = slice(b * TK, (b + 1) * TK)
            k_ref[hh, b] = ka[rs, cs].astype(BF16)
            vt_ref[hh, b] = vt[cs, rs].astype(BF16)
    for j in range(3):
        qkvb_ref[:, j * 512:(j + 1) * 512] = mm(C_QKVB + j * 512, C_QKVB + (j + 1) * 512)
    zb_ref[...] = mm(C_ZB, C_GATE)
    for j in range(4):
        gates_ref[:, j * 512:(j + 1) * 512] = mm(C_GATE + j * 512, C_GATE + (j + 1) * 512)
    ab_ref[...] = mm(C_AB, C_END)
    abt_ref[...] = lax.dot_general(wabt_ref[...], h, _NT, preferred_element_type=F32)


def _inproj(x2d, g_mix, w_main, w_vt, w_abt):
    T = x2d.shape[0]
    n = T // TM_PROJ
    nkb = T // TK
    nb = TM_PROJ // TK
    full = lambda shape: pl.BlockSpec(shape, lambda i: (0,) * len(shape))
    out_shape = (
        jax.ShapeDtypeStruct((A_HEADS, T, 128), BF16),
        jax.ShapeDtypeStruct((A_HEADS, nkb, TK, 128), BF16),
        jax.ShapeDtypeStruct((A_HEADS, nkb, 128, TK), BF16),
        jax.ShapeDtypeStruct((T, 1536), F32),
        jax.ShapeDtypeStruct((T, 512), F32),
        jax.ShapeDtypeStruct((T, 2048), F32),
        jax.ShapeDtypeStruct((T, 128), F32),
        jax.ShapeDtypeStruct((16, T), F32),
    )
    out_specs = (
        pl.BlockSpec((A_HEADS, TM_PROJ, 128), lambda i: (0, i, 0)),
        pl.BlockSpec((A_HEADS, nb, TK, 128), lambda i: (0, i, 0, 0)),
        pl.BlockSpec((A_HEADS, nb, 128, TK), lambda i: (0, i, 0, 0)),
        pl.BlockSpec((TM_PROJ, 1536), lambda i: (i, 0)),
        pl.BlockSpec((TM_PROJ, 512), lambda i: (i, 0)),
        pl.BlockSpec((TM_PROJ, 2048), lambda i: (i, 0)),
        pl.BlockSpec((TM_PROJ, 128), lambda i: (i, 0)),
        pl.BlockSpec((16, TM_PROJ), lambda i: (0, i)),
    )
    return pl.pallas_call(
        _inproj_kernel,
        grid=(n,),
        in_specs=[
            pl.BlockSpec((TM_PROJ, D_MODEL), lambda i: (i, 0)),
            full((1, D_MODEL)),
            full(w_main.shape),
            full(w_vt.shape),
            full(w_abt.shape),
        ],
        out_specs=out_specs,
        out_shape=out_shape,
        compiler_params=pltpu.CompilerParams(
            dimension_semantics=("arbitrary",), vmem_limit_bytes=_vmem_limit(56 << 20)),
        name="inproj",
    )(x2d, g_mix, w_main, w_vt, w_abt)


def _attn_kernel(lam_ref, q_ref, k_ref, vt_ref, bd_ref, bp_ref, gs_ref, o_ref, m_sc, l_sc, acc_sc, s_buf, pm_buf):
    qi = pl.program_id(1)
    q = q_ref[0]
    lane = lax.broadcasted_iota(I32, q.shape, 1)
    zero = jnp.zeros_like(q)
    qm = (jnp.where(lane < A_QK_DIM, q, zero), jnp.where(lane >= A_QK_DIM, q, zero))
    m_sc[...] = jnp.full(m_sc.shape, NEG_BIG, F32)
    l_sc[...] = jnp.zeros(l_sc.shape, F32)
    acc_sc[...] = jnp.zeros(acc_sc.shape, F32)

    nkb = k_ref.shape[1]

    def stage_a(slot, kb0):
        for j in range(2):
            kblk = k_ref[0, jnp.minimum(kb0 + j, nkb - 1)]
            for m in range(2):
                s = lax.dot_general(kblk, qm[m], _NT, preferred_element_type=F32)
                s_buf[slot, m, j] = s
                pm_buf[slot, m, j] = jnp.max(s, axis=0, keepdims=True)

    def stage_b(slot, kb0, biases):
        far = all(b is None for b in biases)
        for m in range(2):
            m_old = m_sc[m]
            m_new = m_old
            ss = []
            for j, bias in enumerate(biases):
                if far:
                    m_new = jnp.maximum(m_new, pm_buf[slot, m, j])
                else:
                    s = s_buf[slot, m, j]
                    s = s if bias is None else s + bias[0]
                    ss.append(s)
                    m_new = jnp.maximum(m_new, jnp.max(s, axis=0, keepdims=True))
            alpha = jnp.exp2(m_old - m_new)
            l_new = alpha * l_sc[m]
            acc = alpha * acc_sc[m]
            for j in range(len(biases)):
                s = s_buf[slot, m, j] if far else ss[j]
                p = jnp.exp2(s - m_new)
                l_new = l_new + jnp.sum(p, axis=0, keepdims=True)
                acc = acc + jnp.dot(vt_ref[0, kb0 + j], p.astype(BF16), preferred_element_type=F32)
            l_sc[m] = l_new
            acc_sc[m] = acc
            m_sc[m] = m_new

    def far_pair(slot_a, kb_a, slot_b, kb_b):
        def a_piece(j, m):
            kblk = k_ref[0, jnp.minimum(kb_a + j, nkb - 1)]
            s = lax.dot_general(kblk, qm[m], _NT, preferred_element_type=F32)
            s_buf[slot_a, m, j] = s
            pm_buf[slot_a, m, j] = jnp.max(s, axis=0, keepdims=True)

        for m in range(2):
            a_piece(0, m)
            m_old = m_sc[m]
            m_new = jnp.maximum(jnp.maximum(m_old, pm_buf[slot_b, m, 0]), pm_buf[slot_b, m, 1])
            alpha = jnp.exp2(m_old - m_new)
            l_new = alpha * l_sc[m]
            acc = alpha * acc_sc[m]
            for j in range(2):
                if j == 1:
                    a_piece(1, m)
                p = jnp.exp2(s_buf[slot_b, m, j] - m_new)
                l_new = l_new + jnp.sum(p, axis=0, keepdims=True)
                acc = acc + jnp.dot(vt_ref[0, kb_b + j], p.astype(BF16), preferred_element_type=F32)
            l_sc[m] = l_new
            acc_sc[m] = acc
            m_sc[m] = m_new

    far2 = [None, None]
    nfu = jnp.maximum(qi - 1, 0) // 2
    peel = nfu % 2

    @pl.when(peel == 1)
    def _():
        stage_a(0, 0)
        stage_b(0, 0, far2)

    stage_a(0, 2 * peel)

    def far_body(v, c):
        u = peel + 2 * v
        far_pair(1, 2 * u + 2, 0, 2 * u)
        far_pair(0, 2 * u + 4, 1, 2 * u + 2)
        return c

    lax.fori_loop(0, (nfu - peel) // 2, far_body, 0)
    kb0 = 2 * nfu

    @pl.when(qi % 2 == 1)
    def _():
        stage_b(0, kb0, [bp_ref, bd_ref])

    @pl.when((qi % 2 == 0) & (qi >= 2))
    def _():
        stage_a(1, kb0 + 2)
        stage_b(0, kb0, [None, bp_ref])
        stage_b(1, kb0 + 2, [bd_ref])

    @pl.when(qi == 0)
    def _():
        stage_b(0, kb0, [bd_ref])

    o = acc_sc[0] / l_sc[0] - lam_ref[0] * (acc_sc[1] / l_sc[1])
    ot = o.T
    o_ref[...] = (_rms(ot, gs_ref[...]) * (1.0 - LAMBDA_INIT)).astype(BF16)


def _attention(lam, q, k, vt, bias_d, bias_p, g_subln):
    T = q.shape[1]
    nq = T // TQ
    nkb = T // TK
    return pl.pallas_call(
        _attn_kernel,
        grid=(A_HEADS, nq),
        in_specs=[
            pl.BlockSpec(memory_space=pltpu.SMEM),
            pl.BlockSpec((1, TQ, 128), lambda h, i: (h, i, 0)),
            pl.BlockSpec((1, nkb, TK, 128), lambda h, i: (h, 0, 0, 0)),
            pl.BlockSpec((1, nkb, 128, TK), lambda h, i: (h, 0, 0, 0)),
            pl.BlockSpec((1, TK, TQ), lambda h, i: (h, 0, 0)),
            pl.BlockSpec((1, TK, TQ), lambda h, i: (h, 0, 0)),
            pl.BlockSpec((1, 128), lambda h, i: (0, 0)),
        ],
        out_specs=pl.BlockSpec((TQ, 128), lambda h, i: (i, h)),
        out_shape=jax.ShapeDtypeStruct((T, A_HEADS * A_V_DIM), BF16),
        scratch_shapes=[
            pltpu.VMEM((2, 1, TQ), F32),
            pltpu.VMEM((2, 1, TQ), F32),
            pltpu.VMEM((2, 128, TQ), F32),
            pltpu.VMEM((2, 2, 2, TK, TQ), F32),
            pltpu.VMEM((2, 2, 2, 1, TQ), F32),
        ],
        compiler_params=pltpu.CompilerParams(
            dimension_semantics=("arbitrary", "arbitrary"), vmem_limit_bytes=_vmem_limit(48 << 20)),
        name="attn",
    )(lam, q, k, vt, bias_d, bias_p, g_subln)


def _t5_bucket(rel):
    half = REL_BUCKETS // 2
    max_exact = half // 2
    ret = jnp.where(rel > 0, half, 0)
    n = jnp.abs(rel)
    large = max_exact + (jnp.log(jnp.maximum(n, 1).astype(F32) / max_exact)
                         / math.log(REL_MAX_DIST / max_exact) * (half - max_exact)).astype(I32)
    large = jnp.minimum(large, half - 1)
    return ret + jnp.where(n < max_exact, n, large)


def _attn_bias_tables(rel_bias):
    assert TK >= REL_MAX_DIST and TK == TQ and TK % CHUNK == 0
    kk = jnp.arange(TK, dtype=I32)[:, None]
    qq = jnp.arange(TQ, dtype=I32)[None, :]
    rb = rel_bias.astype(F32)
    far = rb[_t5_bucket(jnp.full((1,), -REL_MAX_DIST, I32))[0]]
    table = ((rb - far[None, :]) * LOG2E).T

    L = 2 * (TK + TQ)
    rel = jnp.arange(L, dtype=I32) - (TK + TQ)
    onehot = (_t5_bucket(rel)[None, :] == jnp.arange(REL_BUCKETS, dtype=I32)[:, None]).astype(F32)
    line = jnp.dot(table, onehot, precision=lax.Precision.HIGHEST)
    skew = jnp.tile(line, (1, TQ))[:, :TQ * (L - 1)].reshape(A_HEADS, TQ, L - 1)

    def tile_for(first_key):
        c0 = first_key + TK + TQ
        return jnp.swapaxes(skew[:, :, c0:c0 + TK], 1, 2)

    bd = jnp.where((kk // CHUNK <= qq // CHUNK)[None], tile_for(0), NEG_BIG)
    bp = tile_for(-TK)
    return bd, bp


def _gdn_kernel(x_ref, zb_ref, ab_ref, abt_ref, cw_ref, alc_ref, dtc_ref, alr_ref, dtr_ref, gob_ref,
                o_ref, xbuf, s_sc):
    i = pl.program_id(0)
    TC = TC_GDN
    NCH = TC // CHUNK
    HB = B_HEADS * CHUNK

    @pl.when(i == 0)
    def _():
        xbuf[0:8, :] = jnp.zeros((8, 1536), F32)
        s_sc[...] = jnp.zeros(s_sc.shape, F32)

    xbuf[8:8 + TC, :] = x_ref[...]
    y = cw_ref[3:4, :] * xbuf[8:8 + TC, :]
    for d in range(1, CONV_WIDTH):
        y = y + cw_ref[3 - d:4 - d, :] * xbuf[8 - d:8 - d + TC, :]
    xbuf[0:8, :] = x_ref[TC - 8:TC, :]
    a = y * _sigmoid(y)

    def l2n(t):
        return t * lax.rsqrt(jnp.sum(t * t, axis=-1, keepdims=True) + EPS)

    qh = [l2n(a[:, h * 128:(h + 1) * 128]) * (B_DIM ** -0.5) for h in range(B_HEADS)]
    kh = [l2n(a[:, 512 + h * 128:512 + (h + 1) * 128]) for h in range(B_HEADS)]
    vh = [a[:, 1024 + h * 128:1024 + (h + 1) * 128] for h in range(B_HEADS)]

    def softplus(t):
        return jnp.maximum(t, 0.0) + jnp.log(1.0 + jnp.exp(-jnp.abs(t)))

    ab = ab_ref[...]
    g_col = -jnp.exp(alc_ref[...]) * softplus(ab + dtc_ref[...])
    beta_col = _sigmoid(ab)
    g_row = -jnp.exp(alr_ref[...]) * softplus(abt_ref[...] + dtr_ref[...])

    rt = lax.broadcasted_iota(I32, (TC, TC), 0)
    ct = lax.broadcasted_iota(I32, (TC, TC), 1)
    same_chunk = (rt // CHUNK) == (ct // CHUNK)
    tril = jnp.where(same_chunk & (rt >= ct), 1.0, 0.0).astype(F32)
    triu = jnp.where(same_chunk & (rt <= ct), 1.0, 0.0).astype(F32)
    gc_col = jnp.dot(tril, g_col, precision=lax.Precision.HIGHEST, preferred_element_type=F32)
    gc_row = jnp.dot(g_row, triu, precision=lax.Precision.HIGHEST, preferred_element_type=F32)

    ri = lax.broadcasted_iota(I32, (HB, HB), 0)
    ci = lax.broadcasted_iota(I32, (HB, HB), 1)
    same_head = (ri // CHUNK) == (ci // CHUNK)
    incl = same_head & (ri >= ci)
    strict = same_head & (ri > ci)
    eye = jnp.where(ri == ci, 1.0, 0.0).astype(F32)

    def level_mask(s):
        return ((ri // (2 * s)) == (ci // (2 * s))) & (((ri // s) % 2) == 1) & (((ci // s) % 2) == 0)

    cat0 = lambda parts: jnp.concatenate(parts, axis=0)
    heads = range(B_HEADS)
    chunks = range(NCH)

    Kc, Qc, glc, gccc, Lc, QKc, invc, rhsc = [], [], [], [], [], [], [], []
    for c in chunks:
        rs = slice(c * CHUNK, (c + 1) * CHUNK)
        last = slice(c * CHUNK + CHUNK - 1, (c + 1) * CHUNK)
        K = cat0([kh[h][rs] for h in heads])
        Q = cat0([qh[h][rs] for h in heads])
        V = cat0([vh[h][rs] for h in heads])
        beta = cat0([beta_col[rs, 4 + h:5 + h] for h in heads])
        gcc = cat0([gc_col[rs, h:h + 1] for h in heads])
        gl = cat0([jnp.broadcast_to(gc_col[last, h:h + 1], (CHUNK, 1)) for h in heads])
        gcr = jnp.concatenate([gc_row[h:h + 1, rs] for h in heads], axis=1)
        Kb = K * beta
        e = jnp.exp(jnp.where(incl, gcc - gcr, 0.0))
        L = lax.dot_general(Kb, K, _NT, preferred_element_type=F32) * jnp.where(strict, e, 0.0)
        QKc.append(lax.dot_general(Q, K, _NT, preferred_element_type=F32) * jnp.where(incl, e, 0.0))
        rhsc.append(jnp.concatenate([V * beta, Kb * jnp.exp(gcc)], axis=1))
        invc.append(eye - jnp.where(level_mask(1), L, 0.0))
        Kc.append(K); Qc.append(Q); glc.append(gl); gccc.append(gcc); Lc.append(L)

    for s in (2, 4, 8, 16, 32):
        msk = level_mask(s)
        ts = [jnp.dot(invc[c], jnp.where(msk, Lc[c], 0.0), preferred_element_type=F32) for c in chunks]
        invc = [invc[c] - jnp.dot(ts[c], invc[c], preferred_element_type=F32) for c in chunks]
    solc = [jnp.dot(invc[c], rhsc[c], preferred_element_type=F32) for c in chunks]

    for c in chunks:
        rs = slice(c * CHUNK, (c + 1) * CHUNK)
        U = solc[c][:, :B_DIM]
        W = solc[c][:, B_DIM:]
        Qd = Qc[c] * jnp.exp(gccc[c])
        Kd = Kc[c] * jnp.exp(glc[c] - gccc[c])
        vnew = []
        ost = []
        for h in heads:
            hs = slice(h * CHUNK, (h + 1) * CHUNK)
            S = s_sc[h]
            vn = U[hs] - jnp.dot(W[hs], S, preferred_element_type=F32)
            ost.append(jnp.dot(Qd[hs], S, preferred_element_type=F32))
            s_sc[h] = S * jnp.exp(glc[c][h * CHUNK:h * CHUNK + 1, :]) + lax.dot_general(
                Kd[hs], vn, _TN, preferred_element_type=F32)
            vnew.append(vn)
        O = cat0(ost) + jnp.dot(QKc[c], cat0(vnew), preferred_element_type=F32)
        for h in heads:
            oh = _rms(O[h * CHUNK:(h + 1) * CHUNK], gob_ref[...])
            z = zb_ref[rs, h * 128:(h + 1) * 128]
            o_ref[rs, h * 128:(h + 1) * 128] = (oh * (z * _sigmoid(z))).astype(BF16)


def _gdn(qkvb, zb, ab, abt, conv_w8, alc, dtc, alr, dtr, gob):
    T = qkvb.shape[0]
    TC = TC_GDN
    full = lambda shape: pl.BlockSpec(shape, lambda i: (0,) * len(shape))
    return pl.pallas_call(
        _gdn_kernel,
        grid=(T // TC,),
        in_specs=[
            pl.BlockSpec((TC, 1536), lambda i: (i, 0)),
            pl.BlockSpec((TC, 512), lambda i: (i, 0)),
            pl.BlockSpec((TC, 128), lambda i: (i, 0)),
            pl.BlockSpec((16, TC), lambda i: (0, i)),
            full((8, 1536)), full((1, 128)), full((1, 128)), full((16, TC)), full((16, TC)), full((1, 128)),
        ],
        out_specs=pl.BlockSpec((TC, 512), lambda i: (i, 0)),
        out_shape=jax.ShapeDtypeStruct((T, 512), BF16),
        scratch_shapes=[pltpu.VMEM((TC + 8, 1536), F32), pltpu.VMEM((B_HEADS, B_DIM, B_DIM), F32)],
        compiler_params=pltpu.CompilerParams(
            dimension_semantics=("arbitrary",), vmem_limit_bytes=_vmem_limit(48 << 20)),
        name="gdn",
    )(qkvb, zb, ab, abt, conv_w8, alc, dtc, alr, dtr, gob)


def _memkv_kernel(m_ref, g_ref, w_ref, k_ref, v_ref):
    hm = _rms(m_ref[...], g_ref[...]).astype(BF16)
    kv = jnp.dot(hm, w_ref[...], preferred_element_type=F32)
    k_ref[...] = kv[:, :D_MODEL].astype(BF16)
    v_ref[...] = kv[:, D_MODEL:].astype(BF16)


def _memkv(mem2d, g_mem, w_kv):
    n = mem2d.shape[0]
    return pl.pallas_call(
        _memkv_kernel,
        out_shape=(jax.ShapeDtypeStruct((n, D_MODEL), BF16), jax.ShapeDtypeStruct((n, D_MODEL), BF16)),
        compiler_params=pltpu.CompilerParams(vmem_limit_bytes=_vmem_limit(32 << 20)),
        name="memkv",
    )(mem2d, g_mem, w_kv)


def _mix_kernel(x_ref, oa_ref, ob_ref, gates_ref, wua_ref, wub_ref, wout_ref, gx_ref, wq_ref, kx_ref, vx_ref,
                wo_ref, gm_ref, wr_ref, br_ref, x2t_ref, hmt_ref, idx_ref, wts_ref, cnt_ref):
    NSUB = 2
    TS = TM_MIX // NSUB
    subs = range(NSUB)
    rows = [slice(sub * TS, (sub + 1) * TS) for sub in subs]
    dotf = functools.partial(jnp.dot, preferred_element_type=F32)

    ma = [dotf(oa_ref[rows[g], :], wua_ref[...]) for g in subs]
    mb = [dotf(ob_ref[rows[g], :], wub_ref[...]) for g in subs]
    merged = [(_sigmoid(gates_ref[rows[g], :D_MODEL]) * ma[g]
               + _sigmoid(gates_ref[rows[g], D_MODEL:]) * mb[g]).astype(BF16) for g in subs]
    x1 = [x_ref[rows[g], :] + dotf(merged[g], wout_ref[...]) for g in subs]
    hx = [_rms(x1[g], gx_ref[...]).astype(BF16) for g in subs]

    heads = [[] for _ in subs]
    for h in range(X_HEADS):
        cs = slice(h * X_HEAD_DIM, (h + 1) * X_HEAD_DIM)
        qh = [dotf(hx[g], wq_ref[:, cs]).astype(BF16) for g in subs]
        s = [lax.dot_general(qh[g], kx_ref[:, cs], _NT, preferred_element_type=F32) * (X_HEAD_DIM ** -0.5)
             for g in subs]
        for g in subs:
            sg = s[g] - jnp.max(s[g], axis=-1, keepdims=True)
            p = jnp.exp(sg)
            p = p / jnp.sum(p, axis=-1, keepdims=True)
            heads[g].append(dotf(p.astype(BF16), vx_ref[:, cs]).astype(BF16))
    x2 = [x1[g] + dotf(jnp.concatenate(heads[g], axis=1), wo_ref[...]) for g in subs]
    hm = [_rms(x2[g], gm_ref[...]) for g in subs]

    for g in subs:
        for c in range(D_MODEL // 128):
            x2t_ref[pl.ds(g * TS * 8 + c, TS, stride=8), :] = x2[g][:, c * 128:(c + 1) * 128]
            hmt_ref[pl.ds(g * TS * 8 + c, TS, stride=8), :] = hm[g][:, c * 128:(c + 1) * 128]

        logits = lax.dot_general(wr_ref[...], hm[g], _NT, precision=lax.Precision.HIGHEST,
                                 preferred_element_type=F32) + br_ref[:, 0:1]
        eidx = lax.broadcasted_iota(I32, logits.shape, 0)
        vals, idxs = [], []
        cur = logits
        for _ in range(TOP_K):
            mx = jnp.max(cur, axis=0, keepdims=True)
            ix = jnp.min(jnp.where(cur == mx, eidx, N_EXPERTS), axis=0, keepdims=True)
            vals.append(mx)
            idxs.append(ix)
            cur = jnp.where(eidx == ix, -jnp.inf, cur)
        ex = [jnp.exp(v - vals[0]) for v in vals]
        den = ex[0] + ex[1] + ex[2] + ex[3]
        idx_ref[:, rows[g]] = jnp.concatenate(idxs, axis=0)
        wts_ref[:, rows[g]] = jnp.concatenate([e / den for e in ex], axis=0)
        hits = sum(jnp.where(eidx == ix, 1.0, 0.0) for ix in idxs)
        cnt = jnp.broadcast_to(jnp.sum(hits, axis=1, keepdims=True), (N_EXPERTS, 128))
        cnt_ref[0] = cnt if g == 0 else cnt_ref[0] + cnt


def _mix(x2d, oa, ob, gates, wua, wub, wout, gx, wq, kx, vx, wo, gm, wr_t, br):
    T = x2d.shape[0]
    TM = TM_MIX
    full = lambda a: pl.BlockSpec(a.shape, lambda i: (0,) * a.ndim)
    out_shape = (
        jax.ShapeDtypeStruct((T * 8, 128), F32),
        jax.ShapeDtypeStruct((T * 8, 128), F32),
        jax.ShapeDtypeStruct((TOP_K, T), I32),
        jax.ShapeDtypeStruct((TOP_K, T), F32),
        jax.ShapeDtypeStruct((T // TM, N_EXPERTS, 128), F32),
    )
    return pl.pallas_call(
        _mix_kernel,
        grid=(T // TM,),
        in_specs=[
            pl.BlockSpec((TM, D_MODEL), lambda i: (i, 0)),
            pl.BlockSpec((TM, 512), lambda i: (i, 0)),
            pl.BlockSpec((TM, 512), lambda i: (i, 0)),
            pl.BlockSpec((TM, 2048), lambda i: (i, 0)),
            full(wua), full(wub), full(wout), full(gx), full(wq), full(kx), full(vx), full(wo), full(gm),
            full(wr_t), full(br),
        ],
        out_specs=(
            pl.BlockSpec((TM * 8, 128), lambda i: (i, 0)),
            pl.BlockSpec((TM * 8, 128), lambda i: (i, 0)),
            pl.BlockSpec((TOP_K, TM), lambda i: (0, i)),
            pl.BlockSpec((TOP_K, TM), lambda i: (0, i)),
            pl.BlockSpec((1, N_EXPERTS, 128), lambda i: (i, 0, 0)),
        ),
        out_shape=out_shape,
        compiler_params=pltpu.CompilerParams(
            dimension_semantics=("arbitrary",), vmem_limit_bytes=_vmem_limit(56 << 20)),
        name="mix",
    )(x2d, oa, ob, gates, wua, wub, wout, gx, wq, kx, vx, wo, gm, wr_t, br)


def _moe_kernel(blk_e_ref, first_ref, nxt_ref, slot_ref, r0_ref, valid_ref, nb_ref,
                tok_ref, wt_ref, x2t_hbm, hmt_ref, w1_hbm, w2_hbm, b1_ref, b2_ref, gf_ref,
                out_ref, yacc, xs_a, xs_b, yst_a, yst_b, w1buf, w2buf, wsem, xsem):
    j = pl.program_id(0)
    TT = TT_MOE
    RB = RB_MOE
    NG = D_MODEL // 128
    nbj = nb_ref[j]

    x2_copy = pltpu.make_async_copy(
        x2t_hbm.at[pl.ds(pl.multiple_of(j * TT * 8, 8), TT * 8), :], yacc.at[pl.ds(0, TT * 8), :], xsem.at[0])
    x2_copy.start()

    def w_copies(e, slot):
        cps = []
        for c in range(MOE_W_CHUNKS):
            for k, (src, dst) in enumerate(((w1_hbm, w1buf), (w2_hbm, w2buf))):
                rows = pl.ds(c * (src.shape[1] // MOE_W_CHUNKS), src.shape[1] // MOE_W_CHUNKS)
                cps.append(pltpu.make_async_copy(src.at[e, rows], dst.at[slot, rows], wsem.at[slot, k, c]))
        return cps

    for cp in w_copies(blk_e_ref[j, 0], slot_ref[j, 0]):
        cp.start()

    yacc[TT * 8:TT * 8 + 8, :] = jnp.zeros((8, 128), F32)
    yst_b[...] = jnp.zeros(yst_b.shape, F32)

    def gather(b, xs):
        r0 = r0_ref[j, b]
        for r in range(RB):
            xs[r * 8:(r + 1) * 8, :] = hmt_ref[pl.ds(pl.multiple_of(tok_ref[0, 0, r0 + r], 8), 8), :]

    def mlp(xs, yst, e, slot):
        xb = jnp.concatenate([xs[pl.ds(g, RB, stride=8), :] for g in range(NG)], axis=1).astype(BF16)
        hid = jnp.dot(xb, w1buf[slot], preferred_element_type=F32) + b1_ref[e]
        glu = jnp.minimum(hid[:, :D_FF], SWIGLU_LIMIT)
        lin = jnp.clip(hid[:, D_FF:], -SWIGLU_LIMIT, SWIGLU_LIMIT)
        act = glu * _sigmoid(SWIGLU_ALPHA * glu) * (lin + 1.0)
        ys = jnp.dot(act.astype(BF16), w2buf[slot], preferred_element_type=F32) + b2_ref[e]
        for g in range(NG):
            yst[pl.ds(g, RB, stride=8), :] = ys[:, g * 128:(g + 1) * 128]

    def scatter(b, yst):
        r0 = r0_ref[j, b]
        valid = valid_ref[j, b]
        for g0 in range(0, RB, 8):
            dsts, vals = [], []
            for r in range(g0, g0 + 8):
                t8 = jnp.where(r < valid, tok_ref[0, 0, r0 + r], TT * 8)
                dst = pl.ds(pl.multiple_of(t8, 8), 8)
                dsts.append(dst)
                vals.append(yacc[dst, :] + wt_ref[0, 0, r0 + r] * yst[r * 8:(r + 1) * 8, :])
            for dst, val in zip(dsts, vals):
                yacc[dst, :] = val

    def step(b, xs_cur, yst_cur, xs_next, yst_prev):
        @pl.when(b < nbj)
        def _():
            e = blk_e_ref[j, b]
            slot = slot_ref[j, b]

            @pl.when(first_ref[j, b] == 1)
            def _():
                for cp in w_copies(e, slot):
                    cp.wait()
                nx = nxt_ref[j, b]

                @pl.when(nx >= 0)
                def _():
                    for cp in w_copies(nx, 1 - slot):
                        cp.start()

            for s in range(2):
                @pl.when(slot == s)
                def _(s=s):
                    gather(b + 1, xs_next)
                    mlp(xs_cur, yst_cur, e, s)
                    scatter(jnp.maximum(b - 1, 0), yst_prev)

    gather(0, xs_a)
    x2_copy.wait()

    def pair_body(pp, c):
        step(2 * pp, xs_a, yst_a, xs_b, yst_b)
        step(2 * pp + 1, xs_b, yst_b, xs_a, yst_a)
        return c

    lax.fori_loop(0, (nbj + 1) // 2, pair_body, 0)

    @pl.when(nbj % 2 == 1)
    def _():
        scatter(nbj - 1, yst_a)

    @pl.when(nbj % 2 == 0)
    def _():
        scatter(nbj - 1, yst_b)

    RC = 256

    def fin(ci, c):
        base = pl.multiple_of(ci * RC * 8, 8)
        yv = jnp.concatenate([yacc[pl.ds(base + g, RC, stride=8), :] for g in range(NG)], axis=1)
        out_ref[pl.ds(pl.multiple_of(ci * RC, 8), RC), :] = _rms(yv, gf_ref[...])
        return c

    lax.fori_loop(0, TT // RC, fin, 0)


def _moe_nb_max():
    return TOP_K * TT_MOE // RB_MOE + N_EXPERTS


def _moe(lists, x2t, hmt, w1, b1, w2, b2, g_final):
    blk_e, first, nxt, slot, r0, valid, nb, tok, wt = lists
    T = x2t.shape[0] // 8
    TT = TT_MOE
    nt = T // TT
    LP = tok.shape[-1]
    smem_row = pl.BlockSpec((1, 1, LP), lambda j, *_: (j, 0, 0), memory_space=pltpu.SMEM)
    whole = lambda a: pl.BlockSpec(a.shape, lambda j, *_: (0,) * a.ndim)
    grid_spec = pltpu.PrefetchScalarGridSpec(
        num_scalar_prefetch=7,
        grid=(nt,),
        in_specs=[
            smem_row, smem_row,
            pl.BlockSpec(memory_space=pl.ANY),
            pl.BlockSpec((TT * 8, 128), lambda j, *_: (j, 0)),
            pl.BlockSpec(memory_space=pl.ANY), pl.BlockSpec(memory_space=pl.ANY),
            whole(b1), whole(b2), whole(g_final),
        ],
        out_specs=pl.BlockSpec((TT, D_MODEL), lambda j, *_: (j, 0), pipeline_mode=pl.Buffered(1)),
        scratch_shapes=[
            pltpu.VMEM((TT * 8 + 8, 128), F32),
            pltpu.VMEM((RB_MOE * 8, 128), F32), pltpu.VMEM((RB_MOE * 8, 128), F32),
            pltpu.VMEM((RB_MOE * 8, 128), F32), pltpu.VMEM((RB_MOE * 8, 128), F32),
            pltpu.VMEM((2, D_MODEL, 2 * D_FF), BF16),
            pltpu.VMEM((2, D_FF, D_MODEL), BF16),
            pltpu.SemaphoreType.DMA((2, 2, MOE_W_CHUNKS)),
            pltpu.SemaphoreType.DMA((1,)),
        ],
    )
    return pl.pallas_call(
        _moe_kernel,
        grid_spec=grid_spec,
        out_shape=jax.ShapeDtypeStruct((T, D_MODEL), F32),
        compiler_params=pltpu.CompilerParams(
            dimension_semantics=("arbitrary",), vmem_limit_bytes=_vmem_limit(60 << 20)),
        name="moe",
    )(blk_e, first, nxt, slot, r0, valid, nb, tok, wt, x2t, hmt, w1, w2, b1, b2, g_final)


def _moe_lists(idx, wts, step_counts):
    T = idx.shape[1]
    TT, RB = TT_MOE, RB_MOE
    nt = T // TT
    NB = _moe_nb_max()
    A = TOP_K * TT
    e_tile = idx.reshape(TOP_K, nt, TT).transpose(1, 0, 2).reshape(nt, A)
    w_tile = wts.reshape(TOP_K, nt, TT).transpose(1, 0, 2).reshape(nt, A)
    tok_local = jnp.broadcast_to(jnp.tile(jnp.arange(TT, dtype=I32), TOP_K), e_tile.shape)
    _, stok, swt = lax.sort((e_tile, tok_local, w_tile), dimension=1, is_stable=True, num_keys=1)

    experts = jnp.arange(N_EXPERTS, dtype=I32)
    counts = jnp.sum(step_counts[:, :, 0].reshape(nt, -1, N_EXPERTS), axis=1).astype(I32)
    start = jnp.cumsum(counts, axis=1) - counts
    nblk = (counts + RB - 1) // RB
    blk_end = jnp.cumsum(nblk, axis=1)
    blk_start = blk_end - nblk
    nb = blk_end[:, -1]

    b = jnp.arange(NB + 1, dtype=I32)
    blk_e = jnp.minimum(jnp.sum((blk_end[:, :, None] <= b[None, None, :]).astype(I32), axis=1), N_EXPERTS - 1)
    onehot = (blk_e[:, :, None] == experts[None, None, :]).astype(I32)
    per_block = lambda v: jnp.sum(onehot * v[:, None, :], axis=2)
    present = (nblk > 0).astype(I32)
    ordinal = jnp.cumsum(present, axis=1) - present
    later = (experts[None, :] > experts[:, None])[None] & (present[:, None, :] > 0)
    nxt_e = jnp.min(jnp.where(later, experts[None, None, :], N_EXPERTS), axis=2)
    nxt_e = jnp.where(nxt_e >= N_EXPERTS, -1, nxt_e)
    bs_b = per_block(blk_start)
    first = ((b[None, :] == bs_b) & (b[None, :] < nb[:, None])).astype(I32)
    slot = per_block(ordinal) % 2
    nxt = per_block(nxt_e)

    live = b[None, :] < nb[:, None]
    off = (b[None, :] - bs_b) * RB
    r0 = jnp.where(live, per_block(start) + off, A)
    valid = jnp.where(live, jnp.clip(per_block(counts) - off, 0, RB), 0)
    pad = ((0, 0), (0, RB))
    return blk_e, first, nxt, slot, r0, valid, nb, jnp.pad(stok * 8, pad)[:, None, :], jnp.pad(swt, pad)[:, None, :]


def kernel(x, mem, g_mix, w_in, rel_bias, lambda_q1, lambda_k1, lambda_q2, lambda_k2, g_subln, conv_w, a_log,
           dt_bias, g_out_b, w_up_a, w_up_b, w_out, g_xattn, g_mem, w_q_x, w_kv_x, w_o_x, g_moe, w_router,
           b_router, w_mlp1, b_mlp1, w_mlp2, b_mlp2, g_final):
    B_, S_, _ = x.shape
    assert B_ == 1 and S_ % TT_MOE == 0 and x.dtype == F32
    l = 0
    x2d = x.reshape(S_, D_MODEL)
    row = lambda v: v.reshape(1, -1).astype(F32)

    wi = w_in[l]
    ab_cols = jnp.pad(wi[:, 3584:3592], ((0, 0), (0, 120)))
    w_main = jnp.concatenate(
        [wi[:, 0:1024], wi[:, 1536:3072], wi[:, 3072:3584], wi[:, 3592:5640], ab_cols], axis=1).astype(BF16)
    w_vt = wi[:, 1024:1536].T.astype(BF16)
    w_abt = jnp.pad(wi[:, 3584:3592].T, ((0, 8), (0, 0))).astype(BF16)

    q, k, vt, qkvb, zb, gates, ab, abt = _inproj(x2d, row(g_mix[l]), w_main, w_vt, w_abt)

    lam = (jnp.exp(jnp.sum(lambda_q1[l].astype(F32) * lambda_k1[l].astype(F32)))
           - jnp.exp(jnp.sum(lambda_q2[l].astype(F32) * lambda_k2[l].astype(F32))) + LAMBDA_INIT).reshape(1)
    bias_d, bias_p = _attn_bias_tables(rel_bias)
    oa = _attention(lam, q, k, vt, bias_d, bias_p, row(g_subln[l]))

    lane_pad = lambda v: jnp.pad(v.astype(F32), (0, 128 - v.shape[0])).reshape(1, 128)
    row_bcast = lambda v: jnp.broadcast_to(jnp.pad(v.astype(F32), (0, 16 - v.shape[0]))[:, None], (16, TC_GDN))
    ob = _gdn(qkvb, zb, ab, abt, jnp.pad(conv_w[l].astype(F32), ((0, 4), (0, 0))),
              lane_pad(a_log[l]), lane_pad(dt_bias[l]), row_bcast(a_log[l]), row_bcast(dt_bias[l]),
              row(g_out_b[l]))

    kx, vx = _memkv(mem.reshape(-1, D_MODEL), row(g_mem[l]), w_kv_x[l].astype(BF16))
    br = jnp.broadcast_to(b_router[l].astype(F32)[:, None], (N_EXPERTS, 128))
    x2t, hmt, idx, wts, step_counts = _mix(
        x2d, oa, ob, gates, w_up_a[l].astype(BF16), w_up_b[l].astype(BF16), w_out[l].astype(BF16),
        row(g_xattn[l]), w_q_x[l].astype(BF16), kx, vx, w_o_x[l].astype(BF16), row(g_moe[l]),
        w_router[l].T.astype(F32), br)

    out = _moe(_moe_lists(idx, wts, step_counts), x2t, hmt, w_mlp1[l].astype(BF16),
               b_mlp1[l].astype(F32)[:, None, :],
               w_mlp2[l].astype(BF16), b_mlp2[l].astype(F32)[:, None, :], row(g_final))
    return out.reshape(B_, S_, D_MODEL)
```

```python
import functools
import math

import jax
import jax.numpy as jnp
from jax import lax
from jax.experimental import pallas as pl
from jax.experimental.pallas import tpu as pltpu

F32 = jnp.float32
BF16 = jnp.bfloat16
I32 = jnp.int32

D_MODEL = 1024
CHUNK = 64
EPS = 1e-6
A_HEADS = 4
A_QK_DIM = 64
A_V_DIM = 128
REL_BUCKETS = 32
REL_MAX_DIST = 128
B_HEADS = 4
B_DIM = 128
CONV_WIDTH = 4
X_HEADS = 4
X_HEAD_DIM = 256
N_EXPERTS = 32
TOP_K = 4
D_FF = 1024
SWIGLU_LIMIT = 7.0
SWIGLU_ALPHA = 1.702
LAMBDA_INIT = 0.8 - 0.6 * math.exp(-0.3 * 0)

LOG2E = 1.4426950408889634
NEG_BIG = -1e30

V7X_LANES = 128
V7X_SUBLANES = 8
V7X_VMEM_BYTES = 64 * 1024 * 1024

TM_PROJ = 512
TQ = 512
TK = 512
TC_GDN = 256
TM_MIX = 512
TT_MOE = 2048
RB_MOE = 128
MOE_W_CHUNKS = 1

W_IN_QK, W_IN_VA, W_IN_B, W_IN_AB, W_IN_GATES = (0, 1024), (1024, 1536), (1536, 3584), (3584, 3592), (3592, 5640)

_NT = (((1,), (1,)), ((), ()))
_TN = (((0,), (0,)), ((), ()))


def _rms(x, g):
    return x * lax.rsqrt(jnp.mean(x * x, axis=-1, keepdims=True) + EPS) * g


def _sigmoid(x):
    return 1.0 / (1.0 + jnp.exp(-x))


def _vmem_limit(nbytes):
    return int(min(nbytes, V7X_VMEM_BYTES - 4 * 1024 * 1024))


def _inproj_kernel(x_ref, g_ref, wqk_ref, wb_ref, wg_ref, wab_ref, wvt_ref, wabt_ref,
                   q_ref, k_ref, vt_ref, qkvb_ref, zb_ref, gates_ref, ab_ref, abt_ref):
    h = _rms(x_ref[...], g_ref[...]).astype(BF16)

    def mm(w_ref, c0, c1):
        return jnp.dot(h, w_ref[:, c0:c1], preferred_element_type=F32)

    nb = TM_PROJ // TK
    qa = mm(wqk_ref, 0, 512) * (A_QK_DIM ** -0.5 * LOG2E)
    ka = mm(wqk_ref, 512, 1024)
    vt = lax.dot_general(wvt_ref[...], h, _NT, preferred_element_type=F32)
    for hh in range(A_HEADS):
        cs = slice(hh * 128, (hh + 1) * 128)
        q_ref[hh] = qa[:, cs].astype(BF16)
        for b in range(nb):
            rs = slice(b * TK, (b + 1) * TK)
            k_ref[hh, b] = ka[rs, cs].astype(BF16)
            vt_ref[hh, b] = vt[cs, rs].astype(BF16)
    for j in range(3):
        qkvb_ref[:, j * 512:(j + 1) * 512] = mm(wb_ref, j * 512, (j + 1) * 512)
    zb_ref[...] = mm(wb_ref, 1536, 2048)
    for j in range(4):
        gates_ref[:, j * 512:(j + 1) * 512] = mm(wg_ref, j * 512, (j + 1) * 512)
    ab_ref[...] = mm(wab_ref, 0, 128)
    abt_ref[...] = lax.dot_general(wabt_ref[...], h, _NT, preferred_element_type=F32)


def _inproj(x2d, g_mix, w_qk, w_b, w_g, w_ab, w_vt, w_abt):
    T = x2d.shape[0]
    n = T // TM_PROJ
    nkb = T // TK
    nb = TM_PROJ // TK
    full = lambda shape: pl.BlockSpec(shape, lambda i: (0,) * len(shape))
    out_shape = (
        jax.ShapeDtypeStruct((A_HEADS, T, 128), BF16),
        jax.ShapeDtypeStruct((A_HEADS, nkb, TK, 128), BF16),
        jax.ShapeDtypeStruct((A_HEADS, nkb, 128, TK), BF16),
        jax.ShapeDtypeStruct((T, 1536), F32),
        jax.ShapeDtypeStruct((T, 512), F32),
        jax.ShapeDtypeStruct((T, 2048), F32),
        jax.ShapeDtypeStruct((T, 128), F32),
        jax.ShapeDtypeStruct((16, T), F32),
    )
    out_specs = (
        pl.BlockSpec((A_HEADS, TM_PROJ, 128), lambda i: (0, i, 0)),
        pl.BlockSpec((A_HEADS, nb, TK, 128), lambda i: (0, i, 0, 0)),
        pl.BlockSpec((A_HEADS, nb, 128, TK), lambda i: (0, i, 0, 0)),
        pl.BlockSpec((TM_PROJ, 1536), lambda i: (i, 0)),
        pl.BlockSpec((TM_PROJ, 512), lambda i: (i, 0)),
        pl.BlockSpec((TM_PROJ, 2048), lambda i: (i, 0)),
        pl.BlockSpec((TM_PROJ, 128), lambda i: (i, 0)),
        pl.BlockSpec((16, TM_PROJ), lambda i: (0, i)),
    )
    return pl.pallas_call(
        _inproj_kernel,
        grid=(n,),
        in_specs=[
            pl.BlockSpec((TM_PROJ, D_MODEL), lambda i: (i, 0)),
            full((1, D_MODEL)),
            full(w_qk.shape), full(w_b.shape), full(w_g.shape), full(w_ab.shape),
            full(w_vt.shape),
            full(w_abt.shape),
        ],
        out_specs=out_specs,
        out_shape=out_shape,
        compiler_params=pltpu.CompilerParams(
            dimension_semantics=("arbitrary",), vmem_limit_bytes=_vmem_limit(56 << 20)),
        name="inproj",
    )(x2d, g_mix, w_qk, w_b, w_g, w_ab, w_vt, w_abt)


def _attn_kernel(lam_ref, q_ref, k_ref, vt_ref, bd_ref, bp_ref, gs_ref, o_ref, m_sc, l_sc, acc_sc, s_buf, pm_buf):
    qi = pl.program_id(1)
    q = q_ref[0]
    lane = lax.broadcasted_iota(I32, q.shape, 1)
    zero = jnp.zeros_like(q)
    qm = (jnp.where(lane < A_QK_DIM, q, zero), jnp.where(lane >= A_QK_DIM, q, zero))
    m_sc[...] = jnp.full(m_sc.shape, NEG_BIG, F32)
    l_sc[...] = jnp.zeros(l_sc.shape, F32)
    acc_sc[...] = jnp.zeros(acc_sc.shape, F32)

    nkb = k_ref.shape[1]

    def stage_a(slot, kb0):
        for j in range(2):
            kblk = k_ref[0, jnp.minimum(kb0 + j, nkb - 1)]
            for m in range(2):
                s = lax.dot_general(kblk, qm[m], _NT, preferred_element_type=F32)
                s_buf[slot, m, j] = s
                pm_buf[slot, m, j] = jnp.max(s, axis=0, keepdims=True)

    def stage_b(slot, kb0, biases):
        far = all(b is None for b in biases)
        for m in range(2):
            m_old = m_sc[m]
            m_new = m_old
            ss = []
            for j, bias in enumerate(biases):
                if far:
                    m_new = jnp.maximum(m_new, pm_buf[slot, m, j])
                else:
                    s = s_buf[slot, m, j]
                    s = s if bias is None else s + bias[0]
                    ss.append(s)
                    m_new = jnp.maximum(m_new, jnp.max(s, axis=0, keepdims=True))
            alpha = jnp.exp2(m_old - m_new)
            l_new = alpha * l_sc[m]
            acc = alpha * acc_sc[m]
            for j in range(len(biases)):
                s = s_buf[slot, m, j] if far else ss[j]
                p = jnp.exp2(s - m_new)
                l_new = l_new + jnp.sum(p, axis=0, keepdims=True)
                acc = acc + jnp.dot(vt_ref[0, kb0 + j], p.astype(BF16), preferred_element_type=F32)
            l_sc[m] = l_new
            acc_sc[m] = acc
            m_sc[m] = m_new

    def far_pair(slot_a, kb_a, slot_b, kb_b):
        def a_piece(j, m):
            kblk = k_ref[0, jnp.minimum(kb_a + j, nkb - 1)]
            s = lax.dot_general(kblk, qm[m], _NT, preferred_element_type=F32)
            s_buf[slot_a, m, j] = s
            pm_buf[slot_a, m, j] = jnp.max(s, axis=0, keepdims=True)

        for m in range(2):
            a_piece(0, m)
            m_old = m_sc[m]
            m_new = jnp.maximum(jnp.maximum(m_old, pm_buf[slot_b, m, 0]), pm_buf[slot_b, m, 1])
            alpha = jnp.exp2(m_old - m_new)
            l_new = alpha * l_sc[m]
            acc = alpha * acc_sc[m]
            for j in range(2):
                if j == 1:
                    a_piece(1, m)
                p = jnp.exp2(s_buf[slot_b, m, j] - m_new)
                l_new = l_new + jnp.sum(p, axis=0, keepdims=True)
                acc = acc + jnp.dot(vt_ref[0, kb_b + j], p.astype(BF16), preferred_element_type=F32)
            l_sc[m] = l_new
            acc_sc[m] = acc
            m_sc[m] = m_new

    far2 = [None, None]
    nfu = jnp.maximum(qi - 1, 0) // 2
    peel = nfu % 2

    @pl.when(peel == 1)
    def _():
        stage_a(0, 0)
        stage_b(0, 0, far2)

    stage_a(0, 2 * peel)

    def far_body(v, c):
        u = peel + 2 * v
        far_pair(1, 2 * u + 2, 0, 2 * u)
        far_pair(0, 2 * u + 4, 1, 2 * u + 2)
        return c

    lax.fori_loop(0, (nfu - peel) // 2, far_body, 0)
    kb0 = 2 * nfu

    @pl.when(qi % 2 == 1)
    def _():
        stage_b(0, kb0, [bp_ref, bd_ref])

    @pl.when((qi % 2 == 0) & (qi >= 2))
    def _():
        stage_a(1, kb0 + 2)
        stage_b(0, kb0, [None, bp_ref])
        stage_b(1, kb0 + 2, [bd_ref])

    @pl.when(qi == 0)
    def _():
        stage_b(0, kb0, [bd_ref])

    o = acc_sc[0] / l_sc[0] - lam_ref[0] * (acc_sc[1] / l_sc[1])
    ot = o.T
    o_ref[...] = (_rms(ot, gs_ref[...]) * (1.0 - LAMBDA_INIT)).astype(BF16)


def _attention(lam, q, k, vt, bias_d, bias_p, g_subln):
    T = q.shape[1]
    nq = T // TQ
    nkb = T // TK
    return pl.pallas_call(
        _attn_kernel,
        grid=(A_HEADS, nq),
        in_specs=[
            pl.BlockSpec(memory_space=pltpu.SMEM),
            pl.BlockSpec((1, TQ, 128), lambda h, i: (h, i, 0)),
            pl.BlockSpec((1, nkb, TK, 128), lambda h, i: (h, 0, 0, 0)),
            pl.BlockSpec((1, nkb, 128, TK), lambda h, i: (h, 0, 0, 0)),
            pl.BlockSpec((1, TK, TQ), lambda h, i: (h, 0, 0)),
            pl.BlockSpec((1, TK, TQ), lambda h, i: (h, 0, 0)),
            pl.BlockSpec((1, 128), lambda h, i: (0, 0)),
        ],
        out_specs=pl.BlockSpec((TQ, 128), lambda h, i: (i, h)),
        out_shape=jax.ShapeDtypeStruct((T, A_HEADS * A_V_DIM), BF16),
        scratch_shapes=[
            pltpu.VMEM((2, 1, TQ), F32),
            pltpu.VMEM((2, 1, TQ), F32),
            pltpu.VMEM((2, 128, TQ), F32),
            pltpu.VMEM((2, 2, 2, TK, TQ), F32),
            pltpu.VMEM((2, 2, 2, 1, TQ), F32),
        ],
        compiler_params=pltpu.CompilerParams(
            dimension_semantics=("arbitrary", "arbitrary"), vmem_limit_bytes=_vmem_limit(48 << 20)),
        name="attn",
    )(lam, q, k, vt, bias_d, bias_p, g_subln)


def _t5_bucket(rel):
    half = REL_BUCKETS // 2
    max_exact = half // 2
    ret = jnp.where(rel > 0, half, 0)
    n = jnp.abs(rel)
    large = max_exact + (jnp.log(jnp.maximum(n, 1).astype(F32) / max_exact)
                         / math.log(REL_MAX_DIST / max_exact) * (half - max_exact)).astype(I32)
    large = jnp.minimum(large, half - 1)
    return ret + jnp.where(n < max_exact, n, large)


def _attn_bias_tables(rel_bias):
    assert TK >= REL_MAX_DIST and TK == TQ and TK % CHUNK == 0
    kk = jnp.arange(TK, dtype=I32)[:, None]
    qq = jnp.arange(TQ, dtype=I32)[None, :]
    rb = rel_bias.astype(F32)
    far = rb[_t5_bucket(jnp.full((1,), -REL_MAX_DIST, I32))[0]]
    table = ((rb - far[None, :]) * LOG2E).T

    def lookup(rel):
        onehot = (_t5_bucket(rel).reshape(1, -1) == jnp.arange(REL_BUCKETS, dtype=I32)[:, None]).astype(F32)
        return jnp.dot(table, onehot, precision=lax.Precision.HIGHEST).reshape(A_HEADS, TK, TQ)

    bd = jnp.where((kk // CHUNK <= qq // CHUNK)[None], lookup(kk - qq), NEG_BIG)
    bp = lookup(kk - TK - qq)
    return bd, bp


def _gdn_kernel(x_ref, zb_ref, ab_ref, abt_ref, cw_ref, alc_ref, dtc_ref, alr_ref, dtr_ref, gob_ref,
                o_ref, xbuf, s_sc):
    i = pl.program_id(0)
    TC = TC_GDN
    NCH = TC // CHUNK
    HB = B_HEADS * CHUNK

    @pl.when(i == 0)
    def _():
        xbuf[0:8, :] = jnp.zeros((8, 1536), F32)
        s_sc[...] = jnp.zeros(s_sc.shape, F32)

    xbuf[8:8 + TC, :] = x_ref[...]
    y = cw_ref[3:4, :] * xbuf[8:8 + TC, :]
    for d in range(1, CONV_WIDTH):
        y = y + cw_ref[3 - d:4 - d, :] * xbuf[8 - d:8 - d + TC, :]
    xbuf[0:8, :] = x_ref[TC - 8:TC, :]
    a = y * _sigmoid(y)

    def l2n(t):
        return t * lax.rsqrt(jnp.sum(t * t, axis=-1, keepdims=True) + EPS)

    qh = [l2n(a[:, h * 128:(h + 1) * 128]) * (B_DIM ** -0.5) for h in range(B_HEADS)]
    kh = [l2n(a[:, 512 + h * 128:512 + (h + 1) * 128]) for h in range(B_HEADS)]
    vh = [a[:, 1024 + h * 128:1024 + (h + 1) * 128] for h in range(B_HEADS)]

    def softplus(t):
        return jnp.maximum(t, 0.0) + jnp.log(1.0 + jnp.exp(-jnp.abs(t)))

    ab = ab_ref[...]
    g_col = -jnp.exp(alc_ref[...]) * softplus(ab + dtc_ref[...])
    beta_col = _sigmoid(ab)
    g_row = -jnp.exp(alr_ref[...]) * softplus(abt_ref[...] + dtr_ref[...])

    rt = lax.broadcasted_iota(I32, (TC, TC), 0)
    ct = lax.broadcasted_iota(I32, (TC, TC), 1)
    same_chunk = (rt // CHUNK) == (ct // CHUNK)
    tril = jnp.where(same_chunk & (rt >= ct), 1.0, 0.0).astype(F32)
    triu = jnp.where(same_chunk & (rt <= ct), 1.0, 0.0).astype(F32)
    gc_col = jnp.dot(tril, g_col, precision=lax.Precision.HIGHEST, preferred_element_type=F32)
    gc_row = jnp.dot(g_row, triu, precision=lax.Precision.HIGHEST, preferred_element_type=F32)

    ri = lax.broadcasted_iota(I32, (HB, HB), 0)
    ci = lax.broadcasted_iota(I32, (HB, HB), 1)
    same_head = (ri // CHUNK) == (ci // CHUNK)
    incl = same_head & (ri >= ci)
    strict = same_head & (ri > ci)
    eye = jnp.where(ri == ci, 1.0, 0.0).astype(F32)

    def level_mask(s):
        return ((ri // (2 * s)) == (ci // (2 * s))) & (((ri // s) % 2) == 1) & (((ci // s) % 2) == 0)

    cat0 = lambda parts: jnp.concatenate(parts, axis=0)
    heads = range(B_HEADS)
    chunks = range(NCH)

    Kc, Qc, glc, gccc, Lc, QKc, invc, rhsc = [], [], [], [], [], [], [], []
    for c in chunks:
        rs = slice(c * CHUNK, (c + 1) * CHUNK)
        last = slice(c * CHUNK + CHUNK - 1, (c + 1) * CHUNK)
        K = cat0([kh[h][rs] for h in heads])
        Q = cat0([qh[h][rs] for h in heads])
        V = cat0([vh[h][rs] for h in heads])
        beta = cat0([beta_col[rs, 4 + h:5 + h] for h in heads])
        gcc = cat0([gc_col[rs, h:h + 1] for h in heads])
        gl = cat0([jnp.broadcast_to(gc_col[last, h:h + 1], (CHUNK, 1)) for h in heads])
        gcr = jnp.concatenate([gc_row[h:h + 1, rs] for h in heads], axis=1)
        Kb = K * beta
        e = jnp.exp(jnp.where(incl, gcc - gcr, 0.0))
        L = lax.dot_general(Kb, K, _NT, preferred_element_type=F32) * jnp.where(strict, e, 0.0)
        QKc.append(lax.dot_general(Q, K, _NT, preferred_element_type=F32) * jnp.where(incl, e, 0.0))
        rhsc.append(jnp.concatenate([V * beta, Kb * jnp.exp(gcc)], axis=1))
        invc.append(eye - jnp.where(level_mask(1), L, 0.0))
        Kc.append(K); Qc.append(Q); glc.append(gl); gccc.append(gcc); Lc.append(L)

    for s in (2, 4, 8, 16, 32):
        msk = level_mask(s)
        ts = [jnp.dot(invc[c], jnp.where(msk, Lc[c], 0.0), preferred_element_type=F32) for c in chunks]
        invc = [invc[c] - jnp.dot(ts[c], invc[c], preferred_element_type=F32) for c in chunks]
    solc = [jnp.dot(invc[c], rhsc[c], preferred_element_type=F32) for c in chunks]

    for c in chunks:
        rs = slice(c * CHUNK, (c + 1) * CHUNK)
        U = solc[c][:, :B_DIM]
        W = solc[c][:, B_DIM:]
        Qd = Qc[c] * jnp.exp(gccc[c])
        Kd = Kc[c] * jnp.exp(glc[c] - gccc[c])
        vnew = []
        ost = []
        for h in heads:
            hs = slice(h * CHUNK, (h + 1) * CHUNK)
            S = s_sc[h]
            vn = U[hs] - jnp.dot(W[hs], S, preferred_element_type=F32)
            ost.append(jnp.dot(Qd[hs], S, preferred_element_type=F32))
            s_sc[h] = S * jnp.exp(glc[c][h * CHUNK:h * CHUNK + 1, :]) + lax.dot_general(
                Kd[hs], vn, _TN, preferred_element_type=F32)
            vnew.append(vn)
        O = cat0(ost) + jnp.dot(QKc[c], cat0(vnew), preferred_element_type=F32)
        for h in heads:
            oh = _rms(O[h * CHUNK:(h + 1) * CHUNK], gob_ref[...])
            z = zb_ref[rs, h * 128:(h + 1) * 128]
            o_ref[rs, h * 128:(h + 1) * 128] = (oh * (z * _sigmoid(z))).astype(BF16)


def _gdn(qkvb, zb, ab, abt, conv_w8, alc, dtc, alr, dtr, gob):
    T = qkvb.shape[0]
    TC = TC_GDN
    full = lambda shape: pl.BlockSpec(shape, lambda i: (0,) * len(shape))
    return pl.pallas_call(
        _gdn_kernel,
        grid=(T // TC,),
        in_specs=[
            pl.BlockSpec((TC, 1536), lambda i: (i, 0)),
            pl.BlockSpec((TC, 512), lambda i: (i, 0)),
            pl.BlockSpec((TC, 128), lambda i: (i, 0)),
            pl.BlockSpec((16, TC), lambda i: (0, i)),
            full((8, 1536)), full((1, 128)), full((1, 128)), full((16, TC)), full((16, TC)), full((1, 128)),
        ],
        out_specs=pl.BlockSpec((TC, 512), lambda i: (i, 0)),
        out_shape=jax.ShapeDtypeStruct((T, 512), BF16),
        scratch_shapes=[pltpu.VMEM((TC + 8, 1536), F32), pltpu.VMEM((B_HEADS, B_DIM, B_DIM), F32)],
        compiler_params=pltpu.CompilerParams(
            dimension_semantics=("arbitrary",), vmem_limit_bytes=_vmem_limit(48 << 20)),
        name="gdn",
    )(qkvb, zb, ab, abt, conv_w8, alc, dtc, alr, dtr, gob)


def _memkv_kernel(m_ref, g_ref, w_ref, k_ref, v_ref):
    hm = _rms(m_ref[...], g_ref[...]).astype(BF16)
    kv = jnp.dot(hm, w_ref[...], preferred_element_type=F32)
    k_ref[...] = kv[:, :D_MODEL].astype(BF16)
    v_ref[...] = kv[:, D_MODEL:].astype(BF16)


def _memkv(mem2d, g_mem, w_kv):
    n = mem2d.shape[0]
    return pl.pallas_call(
        _memkv_kernel,
        out_shape=(jax.ShapeDtypeStruct((n, D_MODEL), BF16), jax.ShapeDtypeStruct((n, D_MODEL), BF16)),
        compiler_params=pltpu.CompilerParams(vmem_limit_bytes=_vmem_limit(32 << 20)),
        name="memkv",
    )(mem2d, g_mem, w_kv)


def _mix_kernel(x_ref, oa_ref, ob_ref, gates_ref, wua_ref, wub_ref, wout_ref, gx_ref, wq_ref, kx_ref, vx_ref,
                wo_ref, gm_ref, wr_ref, br_ref, x2t_ref, hmt_ref, idx_ref, wts_ref, cnt_ref):
    NSUB = 2
    TS = TM_MIX // NSUB
    subs = range(NSUB)
    rows = [slice(sub * TS, (sub + 1) * TS) for sub in subs]
    dotf = functools.partial(jnp.dot, preferred_element_type=F32)

    ma = [dotf(oa_ref[rows[g], :], wua_ref[...]) for g in subs]
    mb = [dotf(ob_ref[rows[g], :], wub_ref[...]) for g in subs]
    merged = [(_sigmoid(gates_ref[rows[g], :D_MODEL]) * ma[g]
               + _sigmoid(gates_ref[rows[g], D_MODEL:]) * mb[g]).astype(BF16) for g in subs]
    x1 = [x_ref[rows[g], :] + dotf(merged[g], wout_ref[...]) for g in subs]
    hx = [_rms(x1[g], gx_ref[...]).astype(BF16) for g in subs]

    heads = [[] for _ in subs]
    for h in range(X_HEADS):
        cs = slice(h * X_HEAD_DIM, (h + 1) * X_HEAD_DIM)
        qh = [dotf(hx[g], wq_ref[:, cs]).astype(BF16) for g in subs]
        s = [lax.dot_general(qh[g], kx_ref[:, cs], _NT, preferred_element_type=F32) * (X_HEAD_DIM ** -0.5)
             for g in subs]
        for g in subs:
            sg = s[g] - jnp.max(s[g], axis=-1, keepdims=True)
            p = jnp.exp(sg)
            p = p / jnp.sum(p, axis=-1, keepdims=True)
            heads[g].append(dotf(p.astype(BF16), vx_ref[:, cs]).astype(BF16))
    x2 = [x1[g] + dotf(jnp.concatenate(heads[g], axis=1), wo_ref[...]) for g in subs]
    hm = [_rms(x2[g], gm_ref[...]) for g in subs]

    for g in subs:
        for c in range(D_MODEL // 128):
            x2t_ref[pl.ds(g * TS * 8 + c, TS, stride=8), :] = x2[g][:, c * 128:(c + 1) * 128]
            hmt_ref[pl.ds(g * TS * 8 + c, TS, stride=8), :] = hm[g][:, c * 128:(c + 1) * 128]

        logits = lax.dot_general(wr_ref[...], hm[g], _NT, precision=lax.Precision.HIGHEST,
                                 preferred_element_type=F32) + br_ref[:, 0:1]
        eidx = lax.broadcasted_iota(I32, logits.shape, 0)
        vals, idxs = [], []
        cur = logits
        for _ in range(TOP_K):
            mx = jnp.max(cur, axis=0, keepdims=True)
            ix = jnp.min(jnp.where(cur == mx, eidx, N_EXPERTS), axis=0, keepdims=True)
            vals.append(mx)
            idxs.append(ix)
            cur = jnp.where(eidx == ix, -jnp.inf, cur)
        ex = [jnp.exp(v - vals[0]) for v in vals]
        den = ex[0] + ex[1] + ex[2] + ex[3]
        idx_ref[:, rows[g]] = jnp.concatenate(idxs, axis=0)
        wts_ref[:, rows[g]] = jnp.concatenate([e / den for e in ex], axis=0)
        hits = sum(jnp.where(eidx == ix, 1.0, 0.0) for ix in idxs)
        cnt = jnp.broadcast_to(jnp.sum(hits, axis=1, keepdims=True), (N_EXPERTS, 128))
        cnt_ref[0] = cnt if g == 0 else cnt_ref[0] + cnt


def _mix(x2d, oa, ob, gates, wua, wub, wout, gx, wq, kx, vx, wo, gm, wr_t, br):
    T = x2d.shape[0]
    TM = TM_MIX
    full = lambda a: pl.BlockSpec(a.shape, lambda i: (0,) * a.ndim)
    out_shape = (
        jax.ShapeDtypeStruct((T * 8, 128), F32),
        jax.ShapeDtypeStruct((T * 8, 128), F32),
        jax.ShapeDtypeStruct((TOP_K, T), I32),
        jax.ShapeDtypeStruct((TOP_K, T), F32),
        jax.ShapeDtypeStruct((T // TM, N_EXPERTS, 128), F32),
    )
    return pl.pallas_call(
        _mix_kernel,
        grid=(T // TM,),
        in_specs=[
            pl.BlockSpec((TM, D_MODEL), lambda i: (i, 0)),
            pl.BlockSpec((TM, 512), lambda i: (i, 0)),
            pl.BlockSpec((TM, 512), lambda i: (i, 0)),
            pl.BlockSpec((TM, 2048), lambda i: (i, 0)),
            full(wua), full(wub), full(wout), full(gx), full(wq), full(kx), full(vx), full(wo), full(gm),
            full(wr_t), full(br),
        ],
        out_specs=(
            pl.BlockSpec((TM * 8, 128), lambda i: (i, 0)),
            pl.BlockSpec((TM * 8, 128), lambda i: (i, 0)),
            pl.BlockSpec((TOP_K, TM), lambda i: (0, i)),
            pl.BlockSpec((TOP_K, TM), lambda i: (0, i)),
            pl.BlockSpec((1, N_EXPERTS, 128), lambda i: (i, 0, 0)),
        ),
        out_shape=out_shape,
        compiler_params=pltpu.CompilerParams(
            dimension_semantics=("arbitrary",), vmem_limit_bytes=_vmem_limit(56 << 20)),
        name="mix",
    )(x2d, oa, ob, gates, wua, wub, wout, gx, wq, kx, vx, wo, gm, wr_t, br)


def _moe_kernel(blk_e_ref, first_ref, nxt_ref, slot_ref, r0_ref, valid_ref, nb_ref,
                tok_ref, wt_ref, x2t_hbm, hmt_ref, w1_hbm, w2_hbm, b1_ref, b2_ref, gf_ref,
                out_ref, yacc, xs_a, xs_b, yst_a, yst_b, w1buf, w2buf, wsem, xsem):
    j = pl.program_id(0)
    TT = TT_MOE
    RB = RB_MOE
    NG = D_MODEL // 128
    nbj = nb_ref[j]

    x2_copy = pltpu.make_async_copy(
        x2t_hbm.at[pl.ds(pl.multiple_of(j * TT * 8, 8), TT * 8), :], yacc.at[pl.ds(0, TT * 8), :], xsem.at[0])
    x2_copy.start()

    def w_copies(e, slot):
        cps = []
        for c in range(MOE_W_CHUNKS):
            for k, (src, dst) in enumerate(((w1_hbm, w1buf), (w2_hbm, w2buf))):
                rows = pl.ds(c * (src.shape[1] // MOE_W_CHUNKS), src.shape[1] // MOE_W_CHUNKS)
                cps.append(pltpu.make_async_copy(src.at[e, rows], dst.at[slot, rows], wsem.at[slot, k, c]))
        return cps

    for cp in w_copies(blk_e_ref[j, 0], slot_ref[j, 0]):
        cp.start()

    yacc[TT * 8:TT * 8 + 8, :] = jnp.zeros((8, 128), F32)
    yst_b[...] = jnp.zeros(yst_b.shape, F32)

    def gather(b, xs):
        r0 = r0_ref[j, b]
        for r in range(RB):
            xs[r * 8:(r + 1) * 8, :] = hmt_ref[pl.ds(pl.multiple_of(tok_ref[0, 0, r0 + r], 8), 8), :]

    def mlp(xs, yst, e, slot):
        xb = jnp.concatenate([xs[pl.ds(g, RB, stride=8), :] for g in range(NG)], axis=1).astype(BF16)
        hid = jnp.dot(xb, w1buf[slot], preferred_element_type=F32) + b1_ref[e]
        glu = jnp.minimum(hid[:, :D_FF], SWIGLU_LIMIT)
        lin = jnp.clip(hid[:, D_FF:], -SWIGLU_LIMIT, SWIGLU_LIMIT)
        act = glu * _sigmoid(SWIGLU_ALPHA * glu) * (lin + 1.0)
        ys = jnp.dot(act.astype(BF16), w2buf[slot], preferred_element_type=F32) + b2_ref[e]
        for g in range(NG):
            yst[pl.ds(g, RB, stride=8), :] = ys[:, g * 128:(g + 1) * 128]

    def scatter(b, yst):
        r0 = r0_ref[j, b]
        valid = valid_ref[j, b]
        for g0 in range(0, RB, 8):
            dsts, vals = [], []
            for r in range(g0, g0 + 8):
                t8 = jnp.where(r < valid, tok_ref[0, 0, r0 + r], TT * 8)
                dst = pl.ds(pl.multiple_of(t8, 8), 8)
                dsts.append(dst)
                vals.append(yacc[dst, :] + wt_ref[0, 0, r0 + r] * yst[r * 8:(r + 1) * 8, :])
            for dst, val in zip(dsts, vals):
                yacc[dst, :] = val

    def step(b, xs_cur, yst_cur, xs_next, yst_prev):
        @pl.when(b < nbj)
        def _():
            e = blk_e_ref[j, b]
            slot = slot_ref[j, b]

            @pl.when(first_ref[j, b] == 1)
            def _():
                for cp in w_copies(e, slot):
                    cp.wait()
                nx = nxt_ref[j, b]

                @pl.when(nx >= 0)
                def _():
                    for cp in w_copies(nx, 1 - slot):
                        cp.start()

            for s in range(2):
                @pl.when(slot == s)
                def _(s=s):
                    gather(b + 1, xs_next)
                    mlp(xs_cur, yst_cur, e, s)
                    scatter(jnp.maximum(b - 1, 0), yst_prev)

    gather(0, xs_a)
    x2_copy.wait()

    def pair_body(pp, c):
        step(2 * pp, xs_a, yst_a, xs_b, yst_b)
        step(2 * pp + 1, xs_b, yst_b, xs_a, yst_a)
        return c

    lax.fori_loop(0, (nbj + 1) // 2, pair_body, 0)

    @pl.when(nbj % 2 == 1)
    def _():
        scatter(nbj - 1, yst_a)

    @pl.when(nbj % 2 == 0)
    def _():
        scatter(nbj - 1, yst_b)

    RC = 256

    def fin(ci, c):
        base = pl.multiple_of(ci * RC * 8, 8)
        yv = jnp.concatenate([yacc[pl.ds(base + g, RC, stride=8), :] for g in range(NG)], axis=1)
        out_ref[pl.ds(pl.multiple_of(ci * RC, 8), RC), :] = _rms(yv, gf_ref[...])
        return c

    lax.fori_loop(0, TT // RC, fin, 0)


def _moe_nb_max():
    return TOP_K * TT_MOE // RB_MOE + N_EXPERTS


def _moe(lists, x2t, hmt, w1, b1, w2, b2, g_final):
    blk_e, first, nxt, slot, r0, valid, nb, tok, wt = lists
    T = x2t.shape[0] // 8
    TT = TT_MOE
    nt = T // TT
    LP = tok.shape[-1]
    smem_row = pl.BlockSpec((1, 1, LP), lambda j, *_: (j, 0, 0), memory_space=pltpu.SMEM)
    whole = lambda a: pl.BlockSpec(a.shape, lambda j, *_: (0,) * a.ndim)
    grid_spec = pltpu.PrefetchScalarGridSpec(
        num_scalar_prefetch=7,
        grid=(nt,),
        in_specs=[
            smem_row, smem_row,
            pl.BlockSpec(memory_space=pl.ANY),
            pl.BlockSpec((TT * 8, 128), lambda j, *_: (j, 0)),
            pl.BlockSpec(memory_space=pl.ANY), pl.BlockSpec(memory_space=pl.ANY),
            whole(b1), whole(b2), whole(g_final),
        ],
        out_specs=pl.BlockSpec((TT, D_MODEL), lambda j, *_: (j, 0), pipeline_mode=pl.Buffered(1)),
        scratch_shapes=[
            pltpu.VMEM((TT * 8 + 8, 128), F32),
            pltpu.VMEM((RB_MOE * 8, 128), F32), pltpu.VMEM((RB_MOE * 8, 128), F32),
            pltpu.VMEM((RB_MOE * 8, 128), F32), pltpu.VMEM((RB_MOE * 8, 128), F32),
            pltpu.VMEM((2, D_MODEL, 2 * D_FF), BF16),
            pltpu.VMEM((2, D_FF, D_MODEL), BF16),
            pltpu.SemaphoreType.DMA((2, 2, MOE_W_CHUNKS)),
            pltpu.SemaphoreType.DMA((1,)),
        ],
    )
    return pl.pallas_call(
        _moe_kernel,
        grid_spec=grid_spec,
        out_shape=jax.ShapeDtypeStruct((T, D_MODEL), F32),
        compiler_params=pltpu.CompilerParams(
            dimension_semantics=("arbitrary",), vmem_limit_bytes=_vmem_limit(60 << 20)),
        name="moe",
    )(blk_e, first, nxt, slot, r0, valid, nb, tok, wt, x2t, hmt, w1, w2, b1, b2, g_final)


def _moe_lists(idx, wts, step_counts):
    T = idx.shape[1]
    TT, RB = TT_MOE, RB_MOE
    nt = T // TT
    NB = _moe_nb_max()
    A = TOP_K * TT
    e_tile = idx.reshape(TOP_K, nt, TT).transpose(1, 0, 2).reshape(nt, A)
    w_tile = wts.reshape(TOP_K, nt, TT).transpose(1, 0, 2).reshape(nt, A)
    tok_local = jnp.broadcast_to(jnp.tile(jnp.arange(TT, dtype=I32), TOP_K), e_tile.shape)
    _, stok, swt = lax.sort((e_tile, tok_local, w_tile), dimension=1, is_stable=True, num_keys=1)

    experts = jnp.arange(N_EXPERTS, dtype=I32)
    counts = jnp.sum(step_counts[:, :, 0].reshape(nt, -1, N_EXPERTS), axis=1).astype(I32)
    start = jnp.cumsum(counts, axis=1) - counts
    nblk = (counts + RB - 1) // RB
    blk_end = jnp.cumsum(nblk, axis=1)
    blk_start = blk_end - nblk
    nb = blk_end[:, -1]

    b = jnp.arange(NB + 1, dtype=I32)
    blk_e = jnp.minimum(jnp.sum((blk_end[:, :, None] <= b[None, None, :]).astype(I32), axis=1), N_EXPERTS - 1)
    onehot = (blk_e[:, :, None] == experts[None, None, :]).astype(I32)
    per_block = lambda v: jnp.sum(onehot * v[:, None, :], axis=2)
    present = (nblk > 0).astype(I32)
    ordinal = jnp.cumsum(present, axis=1) - present
    later = (experts[None, :] > experts[:, None])[None] & (present[:, None, :] > 0)
    nxt_e = jnp.min(jnp.where(later, experts[None, None, :], N_EXPERTS), axis=2)
    nxt_e = jnp.where(nxt_e >= N_EXPERTS, -1, nxt_e)
    bs_b = per_block(blk_start)
    first = ((b[None, :] == bs_b) & (b[None, :] < nb[:, None])).astype(I32)
    slot = per_block(ordinal) % 2
    nxt = per_block(nxt_e)

    live = b[None, :] < nb[:, None]
    off = (b[None, :] - bs_b) * RB
    r0 = jnp.where(live, per_block(start) + off, A)
    valid = jnp.where(live, jnp.clip(per_block(counts) - off, 0, RB), 0)
    pad = ((0, 0), (0, RB))
    return blk_e, first, nxt, slot, r0, valid, nb, jnp.pad(stok * 8, pad)[:, None, :], jnp.pad(swt, pad)[:, None, :]


def kernel(x, mem, g_mix, w_in, rel_bias, lambda_q1, lambda_k1, lambda_q2, lambda_k2, g_subln, conv_w, a_log,
           dt_bias, g_out_b, w_up_a, w_up_b, w_out, g_xattn, g_mem, w_q_x, w_kv_x, w_o_x, g_moe, w_router,
           b_router, w_mlp1, b_mlp1, w_mlp2, b_mlp2, g_final):
    B_, S_, _ = x.shape
    assert B_ == 1 and S_ % TT_MOE == 0 and x.dtype == F32
    l = 0
    x2d = x.reshape(S_, D_MODEL)
    row = lambda v: v.reshape(1, -1).astype(F32)

    wi = w_in[l]
    cols = lambda rng: wi[:, rng[0]:rng[1]]
    w_ab = jnp.pad(cols(W_IN_AB), ((0, 0), (0, 128 - 2 * B_HEADS))).astype(BF16)
    w_vt = cols(W_IN_VA).T.astype(BF16)
    w_abt = jnp.pad(cols(W_IN_AB).T, ((0, 16 - 2 * B_HEADS), (0, 0))).astype(BF16)

    q, k, vt, qkvb, zb, gates, ab, abt = _inproj(
        x2d, row(g_mix[l]), cols(W_IN_QK).astype(BF16), cols(W_IN_B).astype(BF16),
        cols(W_IN_GATES).astype(BF16), w_ab, w_vt, w_abt)

    lam = (jnp.exp(jnp.sum(lambda_q1[l].astype(F32) * lambda_k1[l].astype(F32)))
           - jnp.exp(jnp.sum(lambda_q2[l].astype(F32) * lambda_k2[l].astype(F32))) + LAMBDA_INIT).reshape(1)
    bias_d, bias_p = _attn_bias_tables(rel_bias)
    oa = _attention(lam, q, k, vt, bias_d, bias_p, row(g_subln[l]))

    lane_pad = lambda v: jnp.pad(v.astype(F32), (0, 128 - v.shape[0])).reshape(1, 128)
    row_bcast = lambda v: jnp.broadcast_to(jnp.pad(v.astype(F32), (0, 16 - v.shape[0]))[:, None], (16, TC_GDN))
    ob = _gdn(qkvb, zb, ab, abt, jnp.pad(conv_w[l].astype(F32), ((0, 4), (0, 0))),
              lane_pad(a_log[l]), lane_pad(dt_bias[l]), row_bcast(a_log[l]), row_bcast(dt_bias[l]),
              row(g_out_b[l]))

    kx, vx = _memkv(mem.reshape(-1, D_MODEL), row(g_mem[l]), w_kv_x[l].astype(BF16))
    br = jnp.broadcast_to(b_router[l].astype(F32)[:, None], (N_EXPERTS, 128))
    x2t, hmt, idx, wts, step_counts = _mix(
        x2d, oa, ob, gates, w_up_a[l].astype(BF16), w_up_b[l].astype(BF16), w_out[l].astype(BF16),
        row(g_xattn[l]), w_q_x[l].astype(BF16), kx, vx, w_o_x[l].astype(BF16), row(g_moe[l]),
        w_router[l].T.astype(F32), br)

    out = _moe(_moe_lists(idx, wts, step_counts), x2t, hmt, w_mlp1[l].astype(BF16),
               b_mlp1[l].astype(F32)[:, None, :],
               w_mlp2[l].astype(BF16), b_mlp2[l].astype(F32)[:, None, :], row(g_final))
    return out.reshape(B_, S_, D_MODEL)
```

```python
import functools
import math

import jax
import jax.numpy as jnp
from jax import lax
from jax.experimental import pallas as pl
from jax.experimental.pallas import tpu as pltpu

F32 = jnp.float32
BF16 = jnp.bfloat16
I32 = jnp.int32

D_MODEL = 1024
CHUNK = 64
EPS = 1e-6
A_HEADS = 4
A_QK_DIM = 64
A_V_DIM = 128
REL_BUCKETS = 32
REL_MAX_DIST = 128
B_HEADS = 4
B_DIM = 128
CONV_WIDTH = 4
X_HEADS = 4
X_HEAD_DIM = 256
N_EXPERTS = 32
TOP_K = 4
D_FF = 1024
SWIGLU_LIMIT = 7.0
SWIGLU_ALPHA = 1.702
LAMBDA_INIT = 0.8 - 0.6 * math.exp(-0.3 * 0)

LOG2E = 1.4426950408889634
NEG_BIG = -1e30

V7X_LANES = 128
V7X_SUBLANES = 8
V7X_VMEM_BYTES = 64 * 1024 * 1024

TM_PROJ = 512
TQ = 512
TK = 512
TC_GDN = 256
TM_MIX = 512
TT_MOE = 2048
RB_MOE = 128
MOE_W_CHUNKS = 1

W_IN_QK, W_IN_VA, W_IN_B, W_IN_AB, W_IN_GATES = (0, 1024), (1024, 1536), (1536, 3584), (3584, 3592), (3592, 5640)

_NT = (((1,), (1,)), ((), ()))
_TN = (((0,), (0,)), ((), ()))


def _rms(x, g):
    return x * lax.rsqrt(jnp.mean(x * x, axis=-1, keepdims=True) + EPS) * g


def _sigmoid(x):
    return 1.0 / (1.0 + jnp.exp(-x))


def _vmem_limit(nbytes):
    return int(min(nbytes, V7X_VMEM_BYTES - 4 * 1024 * 1024))


def _inproj_kernel(x_ref, g_ref, wqk_ref, wb_ref, wg_ref, wab_ref, wvt_ref, wabt_ref,
                   q_ref, k_ref, vt_ref, qkvb_ref, zb_ref, gates_ref, ab_ref, abt_ref):
    h = _rms(x_ref[...], g_ref[...]).astype(BF16)

    def mm(w_ref, c0, c1):
        return jnp.dot(h, w_ref[:, c0:c1], preferred_element_type=F32)

    nb = TM_PROJ // TK
    qa = mm(wqk_ref, 0, 512) * (A_QK_DIM ** -0.5 * LOG2E)
    ka = mm(wqk_ref, 512, 1024)
    vt = lax.dot_general(wvt_ref[...], h, _NT, preferred_element_type=F32)
    for hh in range(A_HEADS):
        cs = slice(hh * 128, (hh + 1) * 128)
        q_ref[hh] = qa[:, cs].astype(BF16)
        for b in range(nb):
            rs = slice(b * TK, (b + 1) * TK)
            k_ref[hh, b] = ka[rs, cs].astype(BF16)
            vt_ref[hh, b] = vt[cs, rs].astype(BF16)
    for j in range(3):
        qkvb_ref[:, j * 512:(j + 1) * 512] = mm(wb_ref, j * 512, (j + 1) * 512)
    zb_ref[...] = mm(wb_ref, 1536, 2048)
    for j in range(4):
        gates_ref[:, j * 512:(j + 1) * 512] = mm(wg_ref, j * 512, (j + 1) * 512)
    ab_ref[...] = mm(wab_ref, 0, 128)
    abt_ref[...] = lax.dot_general(wabt_ref[...], h, _NT, preferred_element_type=F32)


def _inproj(x2d, g_mix, w_qk, w_b, w_g, w_ab, w_vt, w_abt):
    T = x2d.shape[0]
    n = T // TM_PROJ
    nkb = T // TK
    nb = TM_PROJ // TK
    full = lambda shape: pl.BlockSpec(shape, lambda i: (0,) * len(shape))
    out_shape = (
        jax.ShapeDtypeStruct((A_HEADS, T, 128), BF16),
        jax.ShapeDtypeStruct((A_HEADS, nkb, TK, 128), BF16),
        jax.ShapeDtypeStruct((A_HEADS, nkb, 128, TK), BF16),
        jax.ShapeDtypeStruct((T, 1536), F32),
        jax.ShapeDtypeStruct((T, 512), F32),
        jax.ShapeDtypeStruct((T, 2048), F32),
        jax.ShapeDtypeStruct((T, 128), F32),
        jax.ShapeDtypeStruct((16, T), F32),
    )
    out_specs = (
        pl.BlockSpec((A_HEADS, TM_PROJ, 128), lambda i: (0, i, 0)),
        pl.BlockSpec((A_HEADS, nb, TK, 128), lambda i: (0, i, 0, 0)),
        pl.BlockSpec((A_HEADS, nb, 128, TK), lambda i: (0, i, 0, 0)),
        pl.BlockSpec((TM_PROJ, 1536), lambda i: (i, 0)),
        pl.BlockSpec((TM_PROJ, 512), lambda i: (i, 0)),
        pl.BlockSpec((TM_PROJ, 2048), lambda i: (i, 0)),
        pl.BlockSpec((TM_PROJ, 128), lambda i: (i, 0)),
        pl.BlockSpec((16, TM_PROJ), lambda i: (0, i)),
    )
    return pl.pallas_call(
        _inproj_kernel,
        grid=(n,),
        in_specs=[
            pl.BlockSpec((TM_PROJ, D_MODEL), lambda i: (i, 0)),
            full((1, D_MODEL)),
            full(w_qk.shape), full(w_b.shape), full(w_g.shape), full(w_ab.shape),
            full(w_vt.shape),
            full(w_abt.shape),
        ],
        out_specs=out_specs,
        out_shape=out_shape,
        compiler_params=pltpu.CompilerParams(
            dimension_semantics=("arbitrary",), vmem_limit_bytes=_vmem_limit(56 << 20)),
        name="inproj",
    )(x2d, g_mix, w_qk, w_b, w_g, w_ab, w_vt, w_abt)


def _attn_kernel(lam_ref, q_ref, k_ref, vt_ref, bd_ref, bp_ref, gs_ref, o_ref, m_sc, l_sc, acc_sc, s_buf, pm_buf):
    qi = pl.program_id(1)
    q = q_ref[0]
    lane = lax.broadcasted_iota(I32, q.shape, 1)
    zero = jnp.zeros_like(q)
    qm = (jnp.where(lane < A_QK_DIM, q, zero), jnp.where(lane >= A_QK_DIM, q, zero))
    m_sc[...] = jnp.full(m_sc.shape, NEG_BIG, F32)
    l_sc[...] = jnp.zeros(l_sc.shape, F32)
    acc_sc[...] = jnp.zeros(acc_sc.shape, F32)

    nkb = k_ref.shape[1]

    def stage_a(slot, kb0):
        for j in range(2):
            kblk = k_ref[0, jnp.minimum(kb0 + j, nkb - 1)]
            for m in range(2):
                s = lax.dot_general(kblk, qm[m], _NT, preferred_element_type=F32)
                s_buf[slot, m, j] = s
                pm_buf[slot, m, j] = jnp.max(s, axis=0, keepdims=True)

    def stage_b(slot, kb0, biases):
        for m in range(2):
            m_old = m_sc[m]
            m_new = m_old
            ss = []
            for j, bias in enumerate(biases):
                s = s_buf[slot, m, j]
                s = s if bias is None else s + bias[0]
                ss.append(s)
                m_new = jnp.maximum(m_new, jnp.max(s, axis=0, keepdims=True))
            alpha = jnp.exp2(m_old - m_new)
            l_new = alpha * l_sc[m]
            acc = alpha * acc_sc[m]
            for j in range(len(biases)):
                s = ss[j]
                p = jnp.exp2(s - m_new)
                l_new = l_new + jnp.sum(p, axis=0, keepdims=True)
                acc = acc + jnp.dot(vt_ref[0, kb0 + j], p.astype(BF16), preferred_element_type=F32)
            l_sc[m] = l_new
            acc_sc[m] = acc
            m_sc[m] = m_new

    def far_pair(slot_a, kb_a, slot_b, kb_b):
        def a_piece(j, m):
            kblk = k_ref[0, jnp.minimum(kb_a + j, nkb - 1)]
            s = lax.dot_general(kblk, qm[m], _NT, preferred_element_type=F32)
            s_buf[slot_a, m, j] = s
            pm_buf[slot_a, m, j] = jnp.max(s, axis=0, keepdims=True)

        for m in range(2):
            a_piece(0, m)
            m_old = m_sc[m]
            m_new = jnp.maximum(jnp.maximum(m_old, pm_buf[slot_b, m, 0]), pm_buf[slot_b, m, 1])
            alpha = jnp.exp2(m_old - m_new)
            l_new = alpha * l_sc[m]
            acc = alpha * acc_sc[m]
            for j in range(2):
                if j == 1:
                    a_piece(1, m)
                p = jnp.exp2(s_buf[slot_b, m, j] - m_new)
                l_new = l_new + jnp.sum(p, axis=0, keepdims=True)
                acc = acc + jnp.dot(vt_ref[0, kb_b + j], p.astype(BF16), preferred_element_type=F32)
            l_sc[m] = l_new
            acc_sc[m] = acc
            m_sc[m] = m_new

    nfu = jnp.maximum(qi - 1, 0) // 2
    stage_a(0, 0)

    def far_body(v, c):
        u = 2 * v
        far_pair(1, 2 * u + 2, 0, 2 * u)
        far_pair(0, 2 * u + 4, 1, 2 * u + 2)
        return c

    lax.fori_loop(0, nfu // 2, far_body, 0)
    kb0 = 2 * nfu

    def near(slot):
        @pl.when(qi % 2 == 1)
        def _():
            stage_b(slot, kb0, [bp_ref, bd_ref])

        @pl.when((qi % 2 == 0) & (qi >= 2))
        def _():
            stage_a(1 - slot, kb0 + 2)
            stage_b(slot, kb0, [None, bp_ref])
            stage_b(1 - slot, kb0 + 2, [bd_ref])

        @pl.when(qi == 0)
        def _():
            stage_b(slot, kb0, [bd_ref])

    @pl.when(nfu % 2 == 1)
    def _():
        far_pair(1, kb0, 0, kb0 - 2)
        near(1)

    @pl.when(nfu % 2 == 0)
    def _():
        near(0)

    o = acc_sc[0] / l_sc[0] - lam_ref[0] * (acc_sc[1] / l_sc[1])
    ot = o.T
    o_ref[...] = (_rms(ot, gs_ref[...]) * (1.0 - LAMBDA_INIT)).astype(BF16)


def _attention(lam, q, k, vt, bias_d, bias_p, g_subln):
    T = q.shape[1]
    nq = T // TQ
    nkb = T // TK
    return pl.pallas_call(
        _attn_kernel,
        grid=(A_HEADS, nq),
        in_specs=[
            pl.BlockSpec(memory_space=pltpu.SMEM),
            pl.BlockSpec((1, TQ, 128), lambda h, i: (h, i, 0)),
            pl.BlockSpec((1, nkb, TK, 128), lambda h, i: (h, 0, 0, 0)),
            pl.BlockSpec((1, nkb, 128, TK), lambda h, i: (h, 0, 0, 0)),
            pl.BlockSpec((1, TK, TQ), lambda h, i: (h, 0, 0)),
            pl.BlockSpec((1, TK, TQ), lambda h, i: (h, 0, 0)),
            pl.BlockSpec((1, 128), lambda h, i: (0, 0)),
        ],
        out_specs=pl.BlockSpec((TQ, 128), lambda h, i: (i, h)),
        out_shape=jax.ShapeDtypeStruct((T, A_HEADS * A_V_DIM), BF16),
        scratch_shapes=[
            pltpu.VMEM((2, 1, TQ), F32),
            pltpu.VMEM((2, 1, TQ), F32),
            pltpu.VMEM((2, 128, TQ), F32),
            pltpu.VMEM((2, 2, 2, TK, TQ), F32),
            pltpu.VMEM((2, 2, 2, 1, TQ), F32),
        ],
        compiler_params=pltpu.CompilerParams(
            dimension_semantics=("arbitrary", "arbitrary"), vmem_limit_bytes=_vmem_limit(48 << 20)),
        name="attn",
    )(lam, q, k, vt, bias_d, bias_p, g_subln)


def _t5_bucket(rel):
    half = REL_BUCKETS // 2
    max_exact = half // 2
    ret = jnp.where(rel > 0, half, 0)
    n = jnp.abs(rel)
    large = max_exact + (jnp.log(jnp.maximum(n, 1).astype(F32) / max_exact)
                         / math.log(REL_MAX_DIST / max_exact) * (half - max_exact)).astype(I32)
    large = jnp.minimum(large, half - 1)
    return ret + jnp.where(n < max_exact, n, large)


def _attn_bias_tables(rel_bias):
    assert TK >= REL_MAX_DIST and TK == TQ and TK % CHUNK == 0
    kk = jnp.arange(TK, dtype=I32)[:, None]
    qq = jnp.arange(TQ, dtype=I32)[None, :]
    rb = rel_bias.astype(F32)
    far = rb[_t5_bucket(jnp.full((1,), -REL_MAX_DIST, I32))[0]]
    table = ((rb - far[None, :]) * LOG2E).T

    def lookup(rel):
        onehot = (_t5_bucket(rel).reshape(1, -1) == jnp.arange(REL_BUCKETS, dtype=I32)[:, None]).astype(F32)
        return jnp.dot(table, onehot, precision=lax.Precision.HIGHEST).reshape(A_HEADS, TK, TQ)

    bd = jnp.where((kk // CHUNK <= qq // CHUNK)[None], lookup(kk - qq), NEG_BIG)
    bp = lookup(kk - TK - qq)
    return bd, bp


def _gdn_kernel(x_ref, zb_ref, ab_ref, abt_ref, cw_ref, alc_ref, dtc_ref, alr_ref, dtr_ref, gob_ref,
                o_ref, xbuf, s_sc):
    i = pl.program_id(0)
    TC = TC_GDN
    NCH = TC // CHUNK
    HB = B_HEADS * CHUNK

    @pl.when(i == 0)
    def _():
        xbuf[0:8, :] = jnp.zeros((8, 1536), F32)
        s_sc[...] = jnp.zeros(s_sc.shape, F32)

    xbuf[8:8 + TC, :] = x_ref[...]
    y = cw_ref[3:4, :] * xbuf[8:8 + TC, :]
    for d in range(1, CONV_WIDTH):
        y = y + cw_ref[3 - d:4 - d, :] * xbuf[8 - d:8 - d + TC, :]
    xbuf[0:8, :] = x_ref[TC - 8:TC, :]
    a = y * _sigmoid(y)

    def l2n(t):
        return t * lax.rsqrt(jnp.sum(t * t, axis=-1, keepdims=True) + EPS)

    qh = [l2n(a[:, h * 128:(h + 1) * 128]) * (B_DIM ** -0.5) for h in range(B_HEADS)]
    kh = [l2n(a[:, 512 + h * 128:512 + (h + 1) * 128]) for h in range(B_HEADS)]
    vh = [a[:, 1024 + h * 128:1024 + (h + 1) * 128] for h in range(B_HEADS)]

    def softplus(t):
        return jnp.maximum(t, 0.0) + jnp.log(1.0 + jnp.exp(-jnp.abs(t)))

    ab = ab_ref[...]
    g_col = -jnp.exp(alc_ref[...]) * softplus(ab + dtc_ref[...])
    beta_col = _sigmoid(ab)
    g_row = -jnp.exp(alr_ref[...]) * softplus(abt_ref[...] + dtr_ref[...])

    rt = lax.broadcasted_iota(I32, (TC, TC), 0)
    ct = lax.broadcasted_iota(I32, (TC, TC), 1)
    same_chunk = (rt // CHUNK) == (ct // CHUNK)
    tril = jnp.where(same_chunk & (rt >= ct), 1.0, 0.0).astype(F32)
    triu = jnp.where(same_chunk & (rt <= ct), 1.0, 0.0).astype(F32)
    gc_col = jnp.dot(tril, g_col, precision=lax.Precision.HIGHEST, preferred_element_type=F32)
    gc_row = jnp.dot(g_row, triu, precision=lax.Precision.HIGHEST, preferred_element_type=F32)

    ri = lax.broadcasted_iota(I32, (HB, HB), 0)
    ci = lax.broadcasted_iota(I32, (HB, HB), 1)
    same_head = (ri // CHUNK) == (ci // CHUNK)
    incl = same_head & (ri >= ci)
    strict = same_head & (ri > ci)
    eye = jnp.where(ri == ci, 1.0, 0.0).astype(F32)

    def level_mask(s):
        return ((ri // (2 * s)) == (ci // (2 * s))) & (((ri // s) % 2) == 1) & (((ci // s) % 2) == 0)

    cat0 = lambda parts: jnp.concatenate(parts, axis=0)
    heads = range(B_HEADS)
    chunks = range(NCH)

    Kc, Qc, glc, gccc, Lc, QKc, invc, rhsc = [], [], [], [], [], [], [], []
    for c in chunks:
        rs = slice(c * CHUNK, (c + 1) * CHUNK)
        last = slice(c * CHUNK + CHUNK - 1, (c + 1) * CHUNK)
        K = cat0([kh[h][rs] for h in heads])
        Q = cat0([qh[h][rs] for h in heads])
        V = cat0([vh[h][rs] for h in heads])
        beta = cat0([beta_col[rs, 4 + h:5 + h] for h in heads])
        gcc = cat0([gc_col[rs, h:h + 1] for h in heads])
        gl = cat0([jnp.broadcast_to(gc_col[last, h:h + 1], (CHUNK, 1)) for h in heads])
        gcr = jnp.concatenate([gc_row[h:h + 1, rs] for h in heads], axis=1)
        Kb = K * beta
        e = jnp.exp(jnp.where(incl, gcc - gcr, 0.0))
        L = lax.dot_general(Kb, K, _NT, preferred_element_type=F32) * jnp.where(strict, e, 0.0)
        QKc.append(lax.dot_general(Q, K, _NT, preferred_element_type=F32) * jnp.where(incl, e, 0.0))
        rhsc.append(jnp.concatenate([V * beta, Kb * jnp.exp(gcc)], axis=1))
        invc.append(eye - jnp.where(level_mask(1), L, 0.0))
        Kc.append(K); Qc.append(Q); glc.append(gl); gccc.append(gcc); Lc.append(L)

    for s in (2, 4, 8, 16, 32):
        msk = level_mask(s)
        ts = [jnp.dot(invc[c], jnp.where(msk, Lc[c], 0.0), preferred_element_type=F32) for c in chunks]
        invc = [invc[c] - jnp.dot(ts[c], invc[c], preferred_element_type=F32) for c in chunks]
    solc = [jnp.dot(invc[c], rhsc[c], preferred_element_type=F32) for c in chunks]

    for c in chunks:
        rs = slice(c * CHUNK, (c + 1) * CHUNK)
        U = solc[c][:, :B_DIM]
        W = solc[c][:, B_DIM:]
        Qd = Qc[c] * jnp.exp(gccc[c])
        Kd = Kc[c] * jnp.exp(glc[c] - gccc[c])
        vnew = []
        ost = []
        for h in heads:
            hs = slice(h * CHUNK, (h + 1) * CHUNK)
            S = s_sc[h]
            vn = U[hs] - jnp.dot(W[hs], S, preferred_element_type=F32)
            ost.append(jnp.dot(Qd[hs], S, preferred_element_type=F32))
            s_sc[h] = S * jnp.exp(glc[c][h * CHUNK:h * CHUNK + 1, :]) + lax.dot_general(
                Kd[hs], vn, _TN, preferred_element_type=F32)
            vnew.append(vn)
        O = cat0(ost) + jnp.dot(QKc[c], cat0(vnew), preferred_element_type=F32)
        for h in heads:
            oh = _rms(O[h * CHUNK:(h + 1) * CHUNK], gob_ref[...])
            z = zb_ref[rs, h * 128:(h + 1) * 128]
            o_ref[rs, h * 128:(h + 1) * 128] = (oh * (z * _sigmoid(z))).astype(BF16)


def _gdn(qkvb, zb, ab, abt, conv_w8, alc, dtc, alr, dtr, gob):
    T = qkvb.shape[0]
    TC = TC_GDN
    full = lambda shape: pl.BlockSpec(shape, lambda i: (0,) * len(shape))
    return pl.pallas_call(
        _gdn_kernel,
        grid=(T // TC,),
        in_specs=[
            pl.BlockSpec((TC, 1536), lambda i: (i, 0)),
            pl.BlockSpec((TC, 512), lambda i: (i, 0)),
            pl.BlockSpec((TC, 128), lambda i: (i, 0)),
            pl.BlockSpec((16, TC), lambda i: (0, i)),
            full((8, 1536)), full((1, 128)), full((1, 128)), full((16, TC)), full((16, TC)), full((1, 128)),
        ],
        out_specs=pl.BlockSpec((TC, 512), lambda i: (i, 0)),
        out_shape=jax.ShapeDtypeStruct((T, 512), BF16),
        scratch_shapes=[pltpu.VMEM((TC + 8, 1536), F32), pltpu.VMEM((B_HEADS, B_DIM, B_DIM), F32)],
        compiler_params=pltpu.CompilerParams(
            dimension_semantics=("arbitrary",), vmem_limit_bytes=_vmem_limit(48 << 20)),
        name="gdn",
    )(qkvb, zb, ab, abt, conv_w8, alc, dtc, alr, dtr, gob)


def _memkv_kernel(m_ref, g_ref, w_ref, k_ref, v_ref):
    hm = _rms(m_ref[...], g_ref[...]).astype(BF16)
    kv = jnp.dot(hm, w_ref[...], preferred_element_type=F32)
    k_ref[...] = kv[:, :D_MODEL].astype(BF16)
    v_ref[...] = kv[:, D_MODEL:].astype(BF16)


def _memkv(mem2d, g_mem, w_kv):
    n = mem2d.shape[0]
    return pl.pallas_call(
        _memkv_kernel,
        out_shape=(jax.ShapeDtypeStruct((n, D_MODEL), BF16), jax.ShapeDtypeStruct((n, D_MODEL), BF16)),
        compiler_params=pltpu.CompilerParams(vmem_limit_bytes=_vmem_limit(32 << 20)),
        name="memkv",
    )(mem2d, g_mem, w_kv)


def _mix_kernel(x_ref, oa_ref, ob_ref, gates_ref, wua_ref, wub_ref, wout_ref, gx_ref, wq_ref, kx_ref, vx_ref,
                wo_ref, gm_ref, wr_ref, br_ref, x2t_ref, hmt_ref, idx_ref, wts_ref, cnt_ref):
    NSUB = 2
    TS = TM_MIX // NSUB
    subs = range(NSUB)
    rows = [slice(sub * TS, (sub + 1) * TS) for sub in subs]
    dotf = functools.partial(jnp.dot, preferred_element_type=F32)

    ma = [dotf(oa_ref[rows[g], :], wua_ref[...]) for g in subs]
    mb = [dotf(ob_ref[rows[g], :], wub_ref[...]) for g in subs]
    merged = [(_sigmoid(gates_ref[rows[g], :D_MODEL]) * ma[g]
               + _sigmoid(gates_ref[rows[g], D_MODEL:]) * mb[g]).astype(BF16) for g in subs]
    x1 = [x_ref[rows[g], :] + dotf(merged[g], wout_ref[...]) for g in subs]
    hx = [_rms(x1[g], gx_ref[...]).astype(BF16) for g in subs]

    heads = [[] for _ in subs]
    for h in range(X_HEADS):
        cs = slice(h * X_HEAD_DIM, (h + 1) * X_HEAD_DIM)
        qh = [dotf(hx[g], wq_ref[:, cs]).astype(BF16) for g in subs]
        s = [lax.dot_general(qh[g], kx_ref[:, cs], _NT, preferred_element_type=F32) * (X_HEAD_DIM ** -0.5)
             for g in subs]
        for g in subs:
            sg = s[g] - jnp.max(s[g], axis=-1, keepdims=True)
            p = jnp.exp(sg)
            p = p / jnp.sum(p, axis=-1, keepdims=True)
            heads[g].append(dotf(p.astype(BF16), vx_ref[:, cs]).astype(BF16))
    x2 = [x1[g] + dotf(jnp.concatenate(heads[g], axis=1), wo_ref[...]) for g in subs]
    hm = [_rms(x2[g], gm_ref[...]) for g in subs]

    for g in subs:
        for c in range(D_MODEL // 128):
            x2t_ref[pl.ds(g * TS * 8 + c, TS, stride=8), :] = x2[g][:, c * 128:(c + 1) * 128]
            hmt_ref[pl.ds(g * TS * 8 + c, TS, stride=8), :] = hm[g][:, c * 128:(c + 1) * 128]

        logits = lax.dot_general(wr_ref[...], hm[g], _NT, precision=lax.Precision.HIGHEST,
                                 preferred_element_type=F32) + br_ref[:, 0:1]
        eidx = lax.broadcasted_iota(I32, logits.shape, 0)
        vals, idxs = [], []
        cur = logits
        for _ in range(TOP_K):
            mx = jnp.max(cur, axis=0, keepdims=True)
            ix = jnp.min(jnp.where(cur == mx, eidx, N_EXPERTS), axis=0, keepdims=True)
            vals.append(mx)
            idxs.append(ix)
            cur = jnp.where(eidx == ix, -jnp.inf, cur)
        ex = [jnp.exp(v - vals[0]) for v in vals]
        den = ex[0] + ex[1] + ex[2] + ex[3]
        idx_ref[:, rows[g]] = jnp.concatenate(idxs, axis=0)
        wts_ref[:, rows[g]] = jnp.concatenate([e / den for e in ex], axis=0)
        hits = sum(jnp.where(eidx == ix, 1.0, 0.0) for ix in idxs)
        cnt = jnp.broadcast_to(jnp.sum(hits, axis=1, keepdims=True), (N_EXPERTS, 128))
        cnt_ref[0] = cnt if g == 0 else cnt_ref[0] + cnt


def _mix(x2d, oa, ob, gates, wua, wub, wout, gx, wq, kx, vx, wo, gm, wr_t, br):
    T = x2d.shape[0]
    TM = TM_MIX
    full = lambda a: pl.BlockSpec(a.shape, lambda i: (0,) * a.ndim)
    out_shape = (
        jax.ShapeDtypeStruct((T * 8, 128), F32),
        jax.ShapeDtypeStruct((T * 8, 128), F32),
        jax.ShapeDtypeStruct((TOP_K, T), I32),
        jax.ShapeDtypeStruct((TOP_K, T), F32),
        jax.ShapeDtypeStruct((T // TM, N_EXPERTS, 128), F32),
    )
    return pl.pallas_call(
        _mix_kernel,
        grid=(T // TM,),
        in_specs=[
            pl.BlockSpec((TM, D_MODEL), lambda i: (i, 0)),
            pl.BlockSpec((TM, 512), lambda i: (i, 0)),
            pl.BlockSpec((TM, 512), lambda i: (i, 0)),
            pl.BlockSpec((TM, 2048), lambda i: (i, 0)),
            full(wua), full(wub), full(wout), full(gx), full(wq), full(kx), full(vx), full(wo), full(gm),
            full(wr_t), full(br),
        ],
        out_specs=(
            pl.BlockSpec((TM * 8, 128), lambda i: (i, 0)),
            pl.BlockSpec((TM * 8, 128), lambda i: (i, 0)),
            pl.BlockSpec((TOP_K, TM), lambda i: (0, i)),
            pl.BlockSpec((TOP_K, TM), lambda i: (0, i)),
            pl.BlockSpec((1, N_EXPERTS, 128), lambda i: (i, 0, 0)),
        ),
        out_shape=out_shape,
        compiler_params=pltpu.CompilerParams(
            dimension_semantics=("arbitrary",), vmem_limit_bytes=_vmem_limit(56 << 20)),
        name="mix",
    )(x2d, oa, ob, gates, wua, wub, wout, gx, wq, kx, vx, wo, gm, wr_t, br)


def _moe_kernel(blk_e_ref, first_ref, nxt_ref, slot_ref, r0_ref, valid_ref, nb_ref,
                tok_ref, wt_ref, x2t_hbm, hmt_ref, w1_hbm, w2_hbm, b1_ref, b2_ref, gf_ref,
                out_ref, yacc, xs_a, xs_b, yst_a, yst_b, w1buf, w2buf, wsem, xsem):
    j = pl.program_id(0)
    TT = TT_MOE
    RB = RB_MOE
    NG = D_MODEL // 128
    nbj = nb_ref[j]

    x2_copy = pltpu.make_async_copy(
        x2t_hbm.at[pl.ds(pl.multiple_of(j * TT * 8, 8), TT * 8), :], yacc.at[pl.ds(0, TT * 8), :], xsem.at[0])
    x2_copy.start()

    def w_copies(e, slot):
        cps = []
        for c in range(MOE_W_CHUNKS):
            for k, (src, dst) in enumerate(((w1_hbm, w1buf), (w2_hbm, w2buf))):
                rows = pl.ds(c * (src.shape[1] // MOE_W_CHUNKS), src.shape[1] // MOE_W_CHUNKS)
                cps.append(pltpu.make_async_copy(src.at[e, rows], dst.at[slot, rows], wsem.at[slot, k, c]))
        return cps

    for cp in w_copies(blk_e_ref[j, 0], slot_ref[j, 0]):
        cp.start()

    yacc[TT * 8:TT * 8 + 8, :] = jnp.zeros((8, 128), F32)
    yst_b[...] = jnp.zeros(yst_b.shape, F32)

    def gather(b, xs):
        r0 = r0_ref[j, b]
        for r in range(RB):
            xs[r * 8:(r + 1) * 8, :] = hmt_ref[pl.ds(pl.multiple_of(tok_ref[0, 0, r0 + r], 8), 8), :]

    def mlp(xs, yst, e, slot):
        xb = jnp.concatenate([xs[pl.ds(g, RB, stride=8), :] for g in range(NG)], axis=1).astype(BF16)
        hid = jnp.dot(xb, w1buf[slot], preferred_element_type=F32) + b1_ref[e]
        glu = jnp.minimum(hid[:, :D_FF], SWIGLU_LIMIT)
        lin = jnp.clip(hid[:, D_FF:], -SWIGLU_LIMIT, SWIGLU_LIMIT)
        act = glu * _sigmoid(SWIGLU_ALPHA * glu) * (lin + 1.0)
        ys = jnp.dot(act.astype(BF16), w2buf[slot], preferred_element_type=F32) + b2_ref[e]
        for g in range(NG):
            yst[pl.ds(g, RB, stride=8), :] = ys[:, g * 128:(g + 1) * 128]

    def scatter(b, yst):
        r0 = r0_ref[j, b]
        valid = valid_ref[j, b]
        for g0 in range(0, RB, 8):
            dsts, vals = [], []
            for r in range(g0, g0 + 8):
                t8 = jnp.where(r < valid, tok_ref[0, 0, r0 + r], TT * 8)
                dst = pl.ds(pl.multiple_of(t8, 8), 8)
                dsts.append(dst)
                vals.append(yacc[dst, :] + wt_ref[0, 0, r0 + r] * yst[r * 8:(r + 1) * 8, :])
            for dst, val in zip(dsts, vals):
                yacc[dst, :] = val

    def step(b, xs_cur, yst_cur, xs_next, yst_prev):
        @pl.when(b < nbj)
        def _():
            e = blk_e_ref[j, b]
            slot = slot_ref[j, b]

            @pl.when(first_ref[j, b] == 1)
            def _():
                for cp in w_copies(e, slot):
                    cp.wait()
                nx = nxt_ref[j, b]

                @pl.when(nx >= 0)
                def _():
                    for cp in w_copies(nx, 1 - slot):
                        cp.start()

            for s in range(2):
                @pl.when(slot == s)
                def _(s=s):
                    gather(b + 1, xs_next)
                    mlp(xs_cur, yst_cur, e, s)
                    scatter(jnp.maximum(b - 1, 0), yst_prev)

    gather(0, xs_a)
    x2_copy.wait()

    def pair_body(pp, c):
        step(2 * pp, xs_a, yst_a, xs_b, yst_b)
        step(2 * pp + 1, xs_b, yst_b, xs_a, yst_a)
        return c

    lax.fori_loop(0, (nbj + 1) // 2, pair_body, 0)

    @pl.when(nbj % 2 == 1)
    def _():
        scatter(nbj - 1, yst_a)

    @pl.when(nbj % 2 == 0)
    def _():
        scatter(nbj - 1, yst_b)

    RC = 256

    def fin(ci, c):
        base = pl.multiple_of(ci * RC * 8, 8)
        yv = jnp.concatenate([yacc[pl.ds(base + g, RC, stride=8), :] for g in range(NG)], axis=1)
        out_ref[pl.ds(pl.multiple_of(ci * RC, 8), RC), :] = _rms(yv, gf_ref[...])
        return c

    lax.fori_loop(0, TT // RC, fin, 0)


def _moe_nb_max():
    return TOP_K * TT_MOE // RB_MOE + N_EXPERTS


def _moe(lists, x2t, hmt, w1, b1, w2, b2, g_final):
    blk_e, first, nxt, slot, r0, valid, nb, tok, wt = lists
    T = x2t.shape[0] // 8
    TT = TT_MOE
    nt = T // TT
    LP = tok.shape[-1]
    smem_row = pl.BlockSpec((1, 1, LP), lambda j, *_: (j, 0, 0), memory_space=pltpu.SMEM)
    whole = lambda a: pl.BlockSpec(a.shape, lambda j, *_: (0,) * a.ndim)
    grid_spec = pltpu.PrefetchScalarGridSpec(
        num_scalar_prefetch=7,
        grid=(nt,),
        in_specs=[
            smem_row, smem_row,
            pl.BlockSpec(memory_space=pl.ANY),
            pl.BlockSpec((TT * 8, 128), lambda j, *_: (j, 0)),
            pl.BlockSpec(memory_space=pl.ANY), pl.BlockSpec(memory_space=pl.ANY),
            whole(b1), whole(b2), whole(g_final),
        ],
        out_specs=pl.BlockSpec((TT, D_MODEL), lambda j, *_: (j, 0), pipeline_mode=pl.Buffered(1)),
        scratch_shapes=[
            pltpu.VMEM((TT * 8 + 8, 128), F32),
            pltpu.VMEM((RB_MOE * 8, 128), F32), pltpu.VMEM((RB_MOE * 8, 128), F32),
            pltpu.VMEM((RB_MOE * 8, 128), F32), pltpu.VMEM((RB_MOE * 8, 128), F32),
            pltpu.VMEM((2, D_MODEL, 2 * D_FF), BF16),
            pltpu.VMEM((2, D_FF, D_MODEL), BF16),
            pltpu.SemaphoreType.DMA((2, 2, MOE_W_CHUNKS)),
            pltpu.SemaphoreType.DMA((1,)),
        ],
    )
    return pl.pallas_call(
        _moe_kernel,
        grid_spec=grid_spec,
        out_shape=jax.ShapeDtypeStruct((T, D_MODEL), F32),
        compiler_params=pltpu.CompilerParams(
            dimension_semantics=("arbitrary",), vmem_limit_bytes=_vmem_limit(60 << 20)),
        name="moe",
    )(blk_e, first, nxt, slot, r0, valid, nb, tok, wt, x2t, hmt, w1, w2, b1, b2, g_final)


def _moe_lists(idx, wts, step_counts):
    T = idx.shape[1]
    TT, RB = TT_MOE, RB_MOE
    nt = T // TT
    NB = _moe_nb_max()
    A = TOP_K * TT
    e_tile = idx.reshape(TOP_K, nt, TT).transpose(1, 0, 2).reshape(nt, A)
    w_tile = wts.reshape(TOP_K, nt, TT).transpose(1, 0, 2).reshape(nt, A)
    tok_local = jnp.broadcast_to(jnp.tile(jnp.arange(TT, dtype=I32), TOP_K), e_tile.shape)
    _, stok, swt = lax.sort((e_tile, tok_local, w_tile), dimension=1, is_stable=True, num_keys=1)

    experts = jnp.arange(N_EXPERTS, dtype=I32)
    counts = jnp.sum(step_counts[:, :, 0].reshape(nt, -1, N_EXPERTS), axis=1).astype(I32)
    start = jnp.cumsum(counts, axis=1) - counts
    nblk = (counts + RB - 1) // RB
    blk_end = jnp.cumsum(nblk, axis=1)
    blk_start = blk_end - nblk
    nb = blk_end[:, -1]

    b = jnp.arange(NB + 1, dtype=I32)
    blk_e = jnp.minimum(jnp.sum((blk_end[:, :, None] <= b[None, None, :]).astype(I32), axis=1), N_EXPERTS - 1)
    onehot = (blk_e[:, :, None] == experts[None, None, :]).astype(I32)
    per_block = lambda v: jnp.sum(onehot * v[:, None, :], axis=2)
    present = (nblk > 0).astype(I32)
    ordinal = jnp.cumsum(present, axis=1) - present
    later = (experts[None, :] > experts[:, None])[None] & (present[:, None, :] > 0)
    nxt_e = jnp.min(jnp.where(later, experts[None, None, :], N_EXPERTS), axis=2)
    nxt_e = jnp.where(nxt_e >= N_EXPERTS, -1, nxt_e)
    bs_b = per_block(blk_start)
    first = ((b[None, :] == bs_b) & (b[None, :] < nb[:, None])).astype(I32)
    slot = per_block(ordinal) % 2
    nxt = per_block(nxt_e)

    live = b[None, :] < nb[:, None]
    off = (b[None, :] - bs_b) * RB
    r0 = jnp.where(live, per_block(start) + off, A)
    valid = jnp.where(live, jnp.clip(per_block(counts) - off, 0, RB), 0)
    pad = ((0, 0), (0, RB))
    return blk_e, first, nxt, slot, r0, valid, nb, jnp.pad(stok * 8, pad)[:, None, :], jnp.pad(swt, pad)[:, None, :]


def kernel(x, mem, g_mix, w_in, rel_bias, lambda_q1, lambda_k1, lambda_q2, lambda_k2, g_subln, conv_w, a_log,
           dt_bias, g_out_b, w_up_a, w_up_b, w_out, g_xattn, g_mem, w_q_x, w_kv_x, w_o_x, g_moe, w_router,
           b_router, w_mlp1, b_mlp1, w_mlp2, b_mlp2, g_final):
    B_, S_, _ = x.shape
    assert B_ == 1 and S_ % TT_MOE == 0 and x.dtype == F32
    l = 0
    x2d = x.reshape(S_, D_MODEL)
    row = lambda v: v.reshape(1, -1).astype(F32)

    wi = w_in[l]
    cols = lambda rng: wi[:, rng[0]:rng[1]]
    w_ab = jnp.pad(cols(W_IN_AB), ((0, 0), (0, 128 - 2 * B_HEADS))).astype(BF16)
    w_vt = cols(W_IN_VA).T.astype(BF16)
    w_abt = jnp.pad(cols(W_IN_AB).T, ((0, 16 - 2 * B_HEADS), (0, 0))).astype(BF16)

    q, k, vt, qkvb, zb, gates, ab, abt = _inproj(
        x2d, row(g_mix[l]), cols(W_IN_QK).astype(BF16), cols(W_IN_B).astype(BF16),
        cols(W_IN_GATES).astype(BF16), w_ab, w_vt, w_abt)

    lam = (jnp.exp(jnp.sum(lambda_q1[l].astype(F32) * lambda_k1[l].astype(F32)))
           - jnp.exp(jnp.sum(lambda_q2[l].astype(F32) * lambda_k2[l].astype(F32))) + LAMBDA_INIT).reshape(1)
    bias_d, bias_p = _attn_bias_tables(rel_bias)
    oa = _attention(lam, q, k, vt, bias_d, bias_p, row(g_subln[l]))

    lane_pad = lambda v: jnp.pad(v.astype(F32), (0, 128 - v.shape[0])).reshape(1, 128)
    row_bcast = lambda v: jnp.broadcast_to(jnp.pad(v.astype(F32), (0, 16 - v.shape[0]))[:, None], (16, TC_GDN))
    ob = _gdn(qkvb, zb, ab, abt, jnp.pad(conv_w[l].astype(F32), ((0, 4), (0, 0))),
              lane_pad(a_log[l]), lane_pad(dt_bias[l]), row_bcast(a_log[l]), row_bcast(dt_bias[l]),
              row(g_out_b[l]))

    kx, vx = _memkv(mem.reshape(-1, D_MODEL), row(g_mem[l]), w_kv_x[l].astype(BF16))
    br = jnp.broadcast_to(b_router[l].astype(F32)[:, None], (N_EXPERTS, 128))
    x2t, hmt, idx, wts, step_counts = _mix(
        x2d, oa, ob, gates, w_up_a[l].astype(BF16), w_up_b[l].astype(BF16), w_out[l].astype(BF16),
        row(g_xattn[l]), w_q_x[l].astype(BF16), kx, vx, w_o_x[l].astype(BF16), row(g_moe[l]),
        w_router[l].T.astype(F32), br)

    out = _moe(_moe_lists(idx, wts, step_counts), x2t, hmt, w_mlp1[l].astype(BF16),
               b_mlp1[l].astype(F32)[:, None, :],
               w_mlp2[l].astype(BF16), b_mlp2[l].astype(F32)[:, None, :], row(g_final))
    return out.reshape(B_, S_, D_MODEL)
```

```python
import functools
import math

import jax
import jax.numpy as jnp
from jax import lax
from jax.experimental import pallas as pl
from jax.experimental.pallas import tpu as pltpu

F32 = jnp.float32
BF16 = jnp.bfloat16
I32 = jnp.int32

D_MODEL = 1024
CHUNK = 64
EPS = 1e-6
A_HEADS = 4
A_QK_DIM = 64
A_V_DIM = 128
REL_BUCKETS = 32
REL_MAX_DIST = 128
B_HEADS = 4
B_DIM = 128
CONV_WIDTH = 4
X_HEADS = 4
X_HEAD_DIM = 256
N_EXPERTS = 32
TOP_K = 4
D_FF = 1024
SWIGLU_LIMIT = 7.0
SWIGLU_ALPHA = 1.702
LAMBDA_INIT = 0.8 - 0.6 * math.exp(-0.3 * 0)

LOG2E = 1.4426950408889634
NEG_BIG = -1e30

V7X_LANES = 128
V7X_SUBLANES = 8
V7X_VMEM_BYTES = 64 * 1024 * 1024

TM_PROJ = 512
TQ = 512
TK = 512
TC_GDN = 256
TM_MIX = 512
TT_MOE = 2048
RB_MOE = 128
MOE_W_CHUNKS = 1

W_IN_QK, W_IN_VA, W_IN_B, W_IN_AB, W_IN_GATES = (0, 1024), (1024, 1536), (1536, 3584), (3584, 3592), (3592, 5640)

_NT = (((1,), (1,)), ((), ()))
_TN = (((0,), (0,)), ((), ()))


def _rms(x, g):
    return x * lax.rsqrt(jnp.mean(x * x, axis=-1, keepdims=True) + EPS) * g


def _sigmoid(x):
    return 1.0 / (1.0 + jnp.exp(-x))


def _vmem_limit(nbytes):
    return int(min(nbytes, V7X_VMEM_BYTES - 4 * 1024 * 1024))


def _inproj_kernel(x_ref, g_ref, wqk_ref, wb_ref, wg_ref, wab_ref, wvt_ref, wabt_ref,
                   q_ref, k_ref, vt_ref, qkvb_ref, zb_ref, gates_ref, ab_ref, abt_ref):
    h = _rms(x_ref[...], g_ref[...]).astype(BF16)

    def mm(w_ref, c0, c1):
        return jnp.dot(h, w_ref[:, c0:c1], preferred_element_type=F32)

    nb = TM_PROJ // TK
    qa = mm(wqk_ref, 0, 512) * (A_QK_DIM ** -0.5 * LOG2E)
    ka = mm(wqk_ref, 512, 1024)
    vt = lax.dot_general(wvt_ref[...], h, _NT, preferred_element_type=F32)
    for hh in range(A_HEADS):
        cs = slice(hh * 128, (hh + 1) * 128)
        q_ref[hh] = qa[:, cs].astype(BF16)
        for b in range(nb):
            rs = slice(b * TK, (b + 1) * TK)
            k_ref[hh, b] = ka[rs, cs].astype(BF16)
            vt_ref[hh, b] = vt[cs, rs].astype(BF16)
    for j in range(3):
        qkvb_ref[:, j * 512:(j + 1) * 512] = mm(wb_ref, j * 512, (j + 1) * 512)
    zb_ref[...] = mm(wb_ref, 1536, 2048)
    for j in range(4):
        gates_ref[:, j * 512:(j + 1) * 512] = mm(wg_ref, j * 512, (j + 1) * 512)
    ab_ref[...] = mm(wab_ref, 0, 128)
    abt_ref[...] = lax.dot_general(wabt_ref[...], h, _NT, preferred_element_type=F32)


def _inproj(x2d, g_mix, w_qk, w_b, w_g, w_ab, w_vt, w_abt):
    T = x2d.shape[0]
    n = T // TM_PROJ
    nkb = T // TK
    nb = TM_PROJ // TK
    full = lambda shape: pl.BlockSpec(shape, lambda i: (0,) * len(shape))
    out_shape = (
        jax.ShapeDtypeStruct((A_HEADS, T, 128), BF16),
        jax.ShapeDtypeStruct((A_HEADS, nkb, TK, 128), BF16),
        jax.ShapeDtypeStruct((A_HEADS, nkb, 128, TK), BF16),
        jax.ShapeDtypeStruct((T, 1536), F32),
        jax.ShapeDtypeStruct((T, 512), F32),
        jax.ShapeDtypeStruct((T, 2048), F32),
        jax.ShapeDtypeStruct((T, 128), F32),
        jax.ShapeDtypeStruct((16, T), F32),
    )
    out_specs = (
        pl.BlockSpec((A_HEADS, TM_PROJ, 128), lambda i: (0, i, 0)),
        pl.BlockSpec((A_HEADS, nb, TK, 128), lambda i: (0, i, 0, 0)),
        pl.BlockSpec((A_HEADS, nb, 128, TK), lambda i: (0, i, 0, 0)),
        pl.BlockSpec((TM_PROJ, 1536), lambda i: (i, 0)),
        pl.BlockSpec((TM_PROJ, 512), lambda i: (i, 0)),
        pl.BlockSpec((TM_PROJ, 2048), lambda i: (i, 0)),
        pl.BlockSpec((TM_PROJ, 128), lambda i: (i, 0)),
        pl.BlockSpec((16, TM_PROJ), lambda i: (0, i)),
    )
    return pl.pallas_call(
        _inproj_kernel,
        grid=(n,),
        in_specs=[
            pl.BlockSpec((TM_PROJ, D_MODEL), lambda i: (i, 0)),
            full((1, D_MODEL)),
            full(w_qk.shape), full(w_b.shape), full(w_g.shape), full(w_ab.shape),
            full(w_vt.shape),
            full(w_abt.shape),
        ],
        out_specs=out_specs,
        out_shape=out_shape,
        compiler_params=pltpu.CompilerParams(
            dimension_semantics=("arbitrary",), vmem_limit_bytes=_vmem_limit(56 << 20)),
        name="inproj",
    )(x2d, g_mix, w_qk, w_b, w_g, w_ab, w_vt, w_abt)


def _attn_kernel(lam_ref, q_ref, k_ref, vt_ref, bd_ref, bp_ref, gs_ref, o_ref, m_sc, l_sc, acc_sc, s_buf, pm_buf):
    qi = pl.program_id(1)
    q = q_ref[0]
    lane = lax.broadcasted_iota(I32, q.shape, 1)
    zero = jnp.zeros_like(q)
    qm = (jnp.where(lane < A_QK_DIM, q, zero), jnp.where(lane >= A_QK_DIM, q, zero))
    m_sc[...] = jnp.full(m_sc.shape, NEG_BIG, F32)
    l_sc[...] = jnp.zeros(l_sc.shape, F32)
    acc_sc[...] = jnp.zeros(acc_sc.shape, F32)

    nkb = k_ref.shape[1]

    def stage_a(slot, kb0):
        for j in range(2):
            kblk = k_ref[0, jnp.minimum(kb0 + j, nkb - 1)]
            for m in range(2):
                s = lax.dot_general(kblk, qm[m], _NT, preferred_element_type=F32)
                s_buf[slot, m, j] = s
                pm_buf[slot, m, j] = jnp.max(s, axis=0, keepdims=True)

    def stage_b(slot, kb0, biases):
        for m in range(2):
            m_old = m_sc[m]
            m_new = m_old
            ss = []
            for j, bias in enumerate(biases):
                s = s_buf[slot, m, j]
                s = s if bias is None else s + bias[0]
                ss.append(s)
                m_new = jnp.maximum(m_new, jnp.max(s, axis=0, keepdims=True))
            alpha = jnp.exp2(m_old - m_new)
            l_new = alpha * l_sc[m]
            acc = alpha * acc_sc[m]
            for j in range(len(biases)):
                s = ss[j]
                p = jnp.exp2(s - m_new)
                l_new = l_new + jnp.sum(p, axis=0, keepdims=True)
                acc = acc + jnp.dot(vt_ref[0, kb0 + j], p.astype(BF16), preferred_element_type=F32)
            l_sc[m] = l_new
            acc_sc[m] = acc
            m_sc[m] = m_new

    def far_pair(slot_a, kb_a, slot_b, kb_b):
        def a_piece(j, m):
            kblk = k_ref[0, jnp.minimum(kb_a + j, nkb - 1)]
            s = lax.dot_general(kblk, qm[m], _NT, preferred_element_type=F32)
            s_buf[slot_a, m, j] = s
            pm_buf[slot_a, m, j] = jnp.max(s, axis=0, keepdims=True)

        halves = [slice(c * (TQ // 2), (c + 1) * (TQ // 2)) for c in range(2)]
        for m in range(2):
            a_piece(0, m)
            for c, cols in enumerate(halves):
                m_old = m_sc[m, :, cols]
                m_new = jnp.maximum(jnp.maximum(m_old, pm_buf[slot_b, m, 0, :, cols]), pm_buf[slot_b, m, 1, :, cols])
                alpha = jnp.exp2(m_old - m_new)
                l_new = alpha * l_sc[m, :, cols]
                acc = alpha * acc_sc[m, :, cols]
                for j in range(2):
                    if j == 1 and c == 0:
                        a_piece(1, m)
                    p = jnp.exp2(s_buf[slot_b, m, j, :, cols] - m_new)
                    l_new = l_new + jnp.sum(p, axis=0, keepdims=True)
                    acc = acc + jnp.dot(vt_ref[0, kb_b + j], p.astype(BF16), preferred_element_type=F32)
                l_sc[m, :, cols] = l_new
                acc_sc[m, :, cols] = acc
                m_sc[m, :, cols] = m_new

    nfu = jnp.maximum(qi - 1, 0) // 2
    stage_a(0, 0)

    def far_body(v, c):
        u = 2 * v
        far_pair(1, 2 * u + 2, 0, 2 * u)
        far_pair(0, 2 * u + 4, 1, 2 * u + 2)
        return c

    lax.fori_loop(0, nfu // 2, far_body, 0)
    kb0 = 2 * nfu

    def near(slot):
        @pl.when(qi % 2 == 1)
        def _():
            stage_b(slot, kb0, [bp_ref, bd_ref])

        @pl.when((qi % 2 == 0) & (qi >= 2))
        def _():
            stage_a(1 - slot, kb0 + 2)
            stage_b(slot, kb0, [None, bp_ref])
            stage_b(1 - slot, kb0 + 2, [bd_ref])

        @pl.when(qi == 0)
        def _():
            stage_b(slot, kb0, [bd_ref])

    @pl.when(nfu % 2 == 1)
    def _():
        far_pair(1, kb0, 0, kb0 - 2)
        near(1)

    @pl.when(nfu % 2 == 0)
    def _():
        near(0)

    o = acc_sc[0] / l_sc[0] - lam_ref[0] * (acc_sc[1] / l_sc[1])
    ot = o.T
    o_ref[...] = (_rms(ot, gs_ref[...]) * (1.0 - LAMBDA_INIT)).astype(BF16)


def _attention(lam, q, k, vt, bias_d, bias_p, g_subln):
    T = q.shape[1]
    nq = T // TQ
    nkb = T // TK
    return pl.pallas_call(
        _attn_kernel,
        grid=(A_HEADS, nq),
        in_specs=[
            pl.BlockSpec(memory_space=pltpu.SMEM),
            pl.BlockSpec((1, TQ, 128), lambda h, i: (h, i, 0)),
            pl.BlockSpec((1, nkb, TK, 128), lambda h, i: (h, 0, 0, 0)),
            pl.BlockSpec((1, nkb, 128, TK), lambda h, i: (h, 0, 0, 0)),
            pl.BlockSpec((1, TK, TQ), lambda h, i: (h, 0, 0)),
            pl.BlockSpec((1, TK, TQ), lambda h, i: (h, 0, 0)),
            pl.BlockSpec((1, 128), lambda h, i: (0, 0)),
        ],
        out_specs=pl.BlockSpec((TQ, 128), lambda h, i: (i, h)),
        out_shape=jax.ShapeDtypeStruct((T, A_HEADS * A_V_DIM), BF16),
        scratch_shapes=[
            pltpu.VMEM((2, 1, TQ), F32),
            pltpu.VMEM((2, 1, TQ), F32),
            pltpu.VMEM((2, 128, TQ), F32),
            pltpu.VMEM((2, 2, 2, TK, TQ), F32),
            pltpu.VMEM((2, 2, 2, 1, TQ), F32),
        ],
        compiler_params=pltpu.CompilerParams(
            dimension_semantics=("arbitrary", "arbitrary"), vmem_limit_bytes=_vmem_limit(48 << 20)),
        name="attn",
    )(lam, q, k, vt, bias_d, bias_p, g_subln)


def _t5_bucket(rel):
    half = REL_BUCKETS // 2
    max_exact = half // 2
    ret = jnp.where(rel > 0, half, 0)
    n = jnp.abs(rel)
    large = max_exact + (jnp.log(jnp.maximum(n, 1).astype(F32) / max_exact)
                         / math.log(REL_MAX_DIST / max_exact) * (half - max_exact)).astype(I32)
    large = jnp.minimum(large, half - 1)
    return ret + jnp.where(n < max_exact, n, large)


def _attn_bias_tables(rel_bias):
    assert TK >= REL_MAX_DIST and TK == TQ and TK % CHUNK == 0
    kk = jnp.arange(TK, dtype=I32)[:, None]
    qq = jnp.arange(TQ, dtype=I32)[None, :]
    rb = rel_bias.astype(F32)
    far = rb[_t5_bucket(jnp.full((1,), -REL_MAX_DIST, I32))[0]]
    table = ((rb - far[None, :]) * LOG2E).T

    def lookup(rel):
        onehot = (_t5_bucket(rel).reshape(1, -1) == jnp.arange(REL_BUCKETS, dtype=I32)[:, None]).astype(F32)
        return jnp.dot(table, onehot, precision=lax.Precision.HIGHEST).reshape(A_HEADS, TK, TQ)

    bd = jnp.where((kk // CHUNK <= qq // CHUNK)[None], lookup(kk - qq), NEG_BIG)
    bp = lookup(kk - TK - qq)
    return bd, bp


def _gdn_kernel(x_ref, zb_ref, ab_ref, abt_ref, cw_ref, alc_ref, dtc_ref, alr_ref, dtr_ref, gob_ref,
                o_ref, xbuf, s_sc):
    i = pl.program_id(0)
    TC = TC_GDN
    NCH = TC // CHUNK
    HB = B_HEADS * CHUNK

    @pl.when(i == 0)
    def _():
        xbuf[0:8, :] = jnp.zeros((8, 1536), F32)
        s_sc[...] = jnp.zeros(s_sc.shape, F32)

    xbuf[8:8 + TC, :] = x_ref[...]
    y = cw_ref[3:4, :] * xbuf[8:8 + TC, :]
    for d in range(1, CONV_WIDTH):
        y = y + cw_ref[3 - d:4 - d, :] * xbuf[8 - d:8 - d + TC, :]
    xbuf[0:8, :] = x_ref[TC - 8:TC, :]
    a = y * _sigmoid(y)

    def l2n(t):
        return t * lax.rsqrt(jnp.sum(t * t, axis=-1, keepdims=True) + EPS)

    qh = [l2n(a[:, h * 128:(h + 1) * 128]) * (B_DIM ** -0.5) for h in range(B_HEADS)]
    kh = [l2n(a[:, 512 + h * 128:512 + (h + 1) * 128]) for h in range(B_HEADS)]
    vh = [a[:, 1024 + h * 128:1024 + (h + 1) * 128] for h in range(B_HEADS)]

    def softplus(t):
        return jnp.maximum(t, 0.0) + jnp.log(1.0 + jnp.exp(-jnp.abs(t)))

    ab = ab_ref[...]
    g_col = -jnp.exp(alc_ref[...]) * softplus(ab + dtc_ref[...])
    beta_col = _sigmoid(ab)
    g_row = -jnp.exp(alr_ref[...]) * softplus(abt_ref[...] + dtr_ref[...])

    rt = lax.broadcasted_iota(I32, (TC, TC), 0)
    ct = lax.broadcasted_iota(I32, (TC, TC), 1)
    same_chunk = (rt // CHUNK) == (ct // CHUNK)
    tril = jnp.where(same_chunk & (rt >= ct), 1.0, 0.0).astype(F32)
    triu = jnp.where(same_chunk & (rt <= ct), 1.0, 0.0).astype(F32)
    gc_col = jnp.dot(tril, g_col, precision=lax.Precision.HIGHEST, preferred_element_type=F32)
    gc_row = jnp.dot(g_row, triu, precision=lax.Precision.HIGHEST, preferred_element_type=F32)

    ri = lax.broadcasted_iota(I32, (HB, HB), 0)
    ci = lax.broadcasted_iota(I32, (HB, HB), 1)
    same_head = (ri // CHUNK) == (ci // CHUNK)
    incl = same_head & (ri >= ci)
    strict = same_head & (ri > ci)
    eye = jnp.where(ri == ci, 1.0, 0.0).astype(F32)

    def level_mask(s):
        return ((ri // (2 * s)) == (ci // (2 * s))) & (((ri // s) % 2) == 1) & (((ci // s) % 2) == 0)

    cat0 = lambda parts: jnp.concatenate(parts, axis=0)
    heads = range(B_HEADS)
    chunks = range(NCH)

    Kc, Qc, glc, gccc, Lc, QKc, invc, rhsc = [], [], [], [], [], [], [], []
    for c in chunks:
        rs = slice(c * CHUNK, (c + 1) * CHUNK)
        last = slice(c * CHUNK + CHUNK - 1, (c + 1) * CHUNK)
        K = cat0([kh[h][rs] for h in heads])
        Q = cat0([qh[h][rs] for h in heads])
        V = cat0([vh[h][rs] for h in heads])
        beta = cat0([beta_col[rs, 4 + h:5 + h] for h in heads])
        gcc = cat0([gc_col[rs, h:h + 1] for h in heads])
        gl = cat0([jnp.broadcast_to(gc_col[last, h:h + 1], (CHUNK, 1)) for h in heads])
        gcr = jnp.concatenate([gc_row[h:h + 1, rs] for h in heads], axis=1)
        Kb = K * beta
        e = jnp.exp(jnp.where(incl, gcc - gcr, 0.0))
        L = lax.dot_general(Kb, K, _NT, preferred_element_type=F32) * jnp.where(strict, e, 0.0)
        QKc.append(lax.dot_general(Q, K, _NT, preferred_element_type=F32) * jnp.where(incl, e, 0.0))
        rhsc.append(jnp.concatenate([V * beta, Kb * jnp.exp(gcc)], axis=1))
        invc.append(eye - jnp.where(level_mask(1), L, 0.0))
        Kc.append(K); Qc.append(Q); glc.append(gl); gccc.append(gcc); Lc.append(L)

    for s in (2, 4, 8, 16, 32):
        msk = level_mask(s)
        ts = [jnp.dot(invc[c], jnp.where(msk, Lc[c], 0.0), preferred_element_type=F32) for c in chunks]
        invc = [invc[c] - jnp.dot(ts[c], invc[c], preferred_element_type=F32) for c in chunks]
    solc = [jnp.dot(invc[c], rhsc[c], preferred_element_type=F32) for c in chunks]

    for c in chunks:
        rs = slice(c * CHUNK, (c + 1) * CHUNK)
        U = solc[c][:, :B_DIM]
        W = solc[c][:, B_DIM:]
        Qd = Qc[c] * jnp.exp(gccc[c])
        Kd = Kc[c] * jnp.exp(glc[c] - gccc[c])
        vnew = []
        ost = []
        for h in heads:
            hs = slice(h * CHUNK, (h + 1) * CHUNK)
            S = s_sc[h]
            vn = U[hs] - jnp.dot(W[hs], S, preferred_element_type=F32)
            ost.append(jnp.dot(Qd[hs], S, preferred_element_type=F32))
            s_sc[h] = S * jnp.exp(glc[c][h * CHUNK:h * CHUNK + 1, :]) + lax.dot_general(
                Kd[hs], vn, _TN, preferred_element_type=F32)
            vnew.append(vn)
        O = cat0(ost) + jnp.dot(QKc[c], cat0(vnew), preferred_element_type=F32)
        for h in heads:
            oh = _rms(O[h * CHUNK:(h + 1) * CHUNK], gob_ref[...])
            z = zb_ref[rs, h * 128:(h + 1) * 128]
            o_ref[rs, h * 128:(h + 1) * 128] = (oh * (z * _sigmoid(z))).astype(BF16)


def _gdn(qkvb, zb, ab, abt, conv_w8, alc, dtc, alr, dtr, gob):
    T = qkvb.shape[0]
    TC = TC_GDN
    full = lambda shape: pl.BlockSpec(shape, lambda i: (0,) * len(shape))
    return pl.pallas_call(
        _gdn_kernel,
        grid=(T // TC,),
        in_specs=[
            pl.BlockSpec((TC, 1536), lambda i: (i, 0)),
            pl.BlockSpec((TC, 512), lambda i: (i, 0)),
            pl.BlockSpec((TC, 128), lambda i: (i, 0)),
            pl.BlockSpec((16, TC), lambda i: (0, i)),
            full((8, 1536)), full((1, 128)), full((1, 128)), full((16, TC)), full((16, TC)), full((1, 128)),
        ],
        out_specs=pl.BlockSpec((TC, 512), lambda i: (i, 0)),
        out_shape=jax.ShapeDtypeStruct((T, 512), BF16),
        scratch_shapes=[pltpu.VMEM((TC + 8, 1536), F32), pltpu.VMEM((B_HEADS, B_DIM, B_DIM), F32)],
        compiler_params=pltpu.CompilerParams(
            dimension_semantics=("arbitrary",), vmem_limit_bytes=_vmem_limit(48 << 20)),
        name="gdn",
    )(qkvb, zb, ab, abt, conv_w8, alc, dtc, alr, dtr, gob)


def _memkv_kernel(m_ref, g_ref, w_ref, k_ref, v_ref):
    hm = _rms(m_ref[...], g_ref[...]).astype(BF16)
    kv = jnp.dot(hm, w_ref[...], preferred_element_type=F32)
    k_ref[...] = kv[:, :D_MODEL].astype(BF16)
    v_ref[...] = kv[:, D_MODEL:].astype(BF16)


def _memkv(mem2d, g_mem, w_kv):
    n = mem2d.shape[0]
    return pl.pallas_call(
        _memkv_kernel,
        out_shape=(jax.ShapeDtypeStruct((n, D_MODEL), BF16), jax.ShapeDtypeStruct((n, D_MODEL), BF16)),
        compiler_params=pltpu.CompilerParams(vmem_limit_bytes=_vmem_limit(32 << 20)),
        name="memkv",
    )(mem2d, g_mem, w_kv)


def _mix_kernel(x_ref, oa_ref, ob_ref, gates_ref, wua_ref, wub_ref, wout_ref, gx_ref, wq_ref, kx_ref, vx_ref,
                wo_ref, gm_ref, wr_ref, br_ref, x2t_ref, hmt_ref, idx_ref, wts_ref, cnt_ref):
    NSUB = 2
    TS = TM_MIX // NSUB
    subs = range(NSUB)
    rows = [slice(sub * TS, (sub + 1) * TS) for sub in subs]
    dotf = functools.partial(jnp.dot, preferred_element_type=F32)

    ma = [dotf(oa_ref[rows[g], :], wua_ref[...]) for g in subs]
    mb = [dotf(ob_ref[rows[g], :], wub_ref[...]) for g in subs]
    merged = [(_sigmoid(gates_ref[rows[g], :D_MODEL]) * ma[g]
               + _sigmoid(gates_ref[rows[g], D_MODEL:]) * mb[g]).astype(BF16) for g in subs]
    x1 = [x_ref[rows[g], :] + dotf(merged[g], wout_ref[...]) for g in subs]
    hx = [_rms(x1[g], gx_ref[...]).astype(BF16) for g in subs]

    heads = [[] for _ in subs]
    for h in range(X_HEADS):
        cs = slice(h * X_HEAD_DIM, (h + 1) * X_HEAD_DIM)
        qh = [dotf(hx[g], wq_ref[:, cs]).astype(BF16) for g in subs]
        s = [lax.dot_general(qh[g], kx_ref[:, cs], _NT, preferred_element_type=F32) * (X_HEAD_DIM ** -0.5)
             for g in subs]
        for g in subs:
            sg = s[g] - jnp.max(s[g], axis=-1, keepdims=True)
            p = jnp.exp(sg)
            p = p / jnp.sum(p, axis=-1, keepdims=True)
            heads[g].append(dotf(p.astype(BF16), vx_ref[:, cs]).astype(BF16))
    x2 = [x1[g] + dotf(jnp.concatenate(heads[g], axis=1), wo_ref[...]) for g in subs]
    hm = [_rms(x2[g], gm_ref[...]) for g in subs]

    for g in subs:
        for c in range(D_MODEL // 128):
            x2t_ref[pl.ds(g * TS * 8 + c, TS, stride=8), :] = x2[g][:, c * 128:(c + 1) * 128]
            hmt_ref[pl.ds(g * TS * 8 + c, TS, stride=8), :] = hm[g][:, c * 128:(c + 1) * 128]

        logits = lax.dot_general(wr_ref[...], hm[g], _NT, precision=lax.Precision.HIGHEST,
                                 preferred_element_type=F32) + br_ref[:, 0:1]
        eidx = lax.broadcasted_iota(I32, logits.shape, 0)
        vals, idxs = [], []
        cur = logits
        for _ in range(TOP_K):
            mx = jnp.max(cur, axis=0, keepdims=True)
            ix = jnp.min(jnp.where(cur == mx, eidx, N_EXPERTS), axis=0, keepdims=True)
            vals.append(mx)
            idxs.append(ix)
            cur = jnp.where(eidx == ix, -jnp.inf, cur)
        ex = [jnp.exp(v - vals[0]) for v in vals]
        den = ex[0] + ex[1] + ex[2] + ex[3]
        idx_ref[:, rows[g]] = jnp.concatenate(idxs, axis=0)
        wts_ref[:, rows[g]] = jnp.concatenate([e / den for e in ex], axis=0)
        hits = sum(jnp.where(eidx == ix, 1.0, 0.0) for ix in idxs)
        cnt = jnp.broadcast_to(jnp.sum(hits, axis=1, keepdims=True), (N_EXPERTS, 128))
        cnt_ref[0] = cnt if g == 0 else cnt_ref[0] + cnt


def _mix(x2d, oa, ob, gates, wua, wub, wout, gx, wq, kx, vx, wo, gm, wr_t, br):
    T = x2d.shape[0]
    TM = TM_MIX
    full = lambda a: pl.BlockSpec(a.shape, lambda i: (0,) * a.ndim)
    out_shape = (
        jax.ShapeDtypeStruct((T * 8, 128), F32),
        jax.ShapeDtypeStruct((T * 8, 128), F32),
        jax.ShapeDtypeStruct((TOP_K, T), I32),
        jax.ShapeDtypeStruct((TOP_K, T), F32),
        jax.ShapeDtypeStruct((T // TM, N_EXPERTS, 128), F32),
    )
    return pl.pallas_call(
        _mix_kernel,
        grid=(T // TM,),
        in_specs=[
            pl.BlockSpec((TM, D_MODEL), lambda i: (i, 0)),
            pl.BlockSpec((TM, 512), lambda i: (i, 0)),
            pl.BlockSpec((TM, 512), lambda i: (i, 0)),
            pl.BlockSpec((TM, 2048), lambda i: (i, 0)),
            full(wua), full(wub), full(wout), full(gx), full(wq), full(kx), full(vx), full(wo), full(gm),
            full(wr_t), full(br),
        ],
        out_specs=(
            pl.BlockSpec((TM * 8, 128), lambda i: (i, 0)),
            pl.BlockSpec((TM * 8, 128), lambda i: (i, 0)),
            pl.BlockSpec((TOP_K, TM), lambda i: (0, i)),
            pl.BlockSpec((TOP_K, TM), lambda i: (0, i)),
            pl.BlockSpec((1, N_EXPERTS, 128), lambda i: (i, 0, 0)),
        ),
        out_shape=out_shape,
        compiler_params=pltpu.CompilerParams(
            dimension_semantics=("arbitrary",), vmem_limit_bytes=_vmem_limit(56 << 20)),
        name="mix",
    )(x2d, oa, ob, gates, wua, wub, wout, gx, wq, kx, vx, wo, gm, wr_t, br)


def _moe_kernel(blk_e_ref, first_ref, nxt_ref, slot_ref, r0_ref, valid_ref, nb_ref,
                tok_ref, wt_ref, x2t_hbm, hmt_ref, w1_hbm, w2_hbm, b1_ref, b2_ref, gf_ref,
                out_ref, yacc, xs_a, xs_b, yst_a, yst_b, w1buf, w2buf, wsem, xsem):
    j = pl.program_id(0)
    TT = TT_MOE
    RB = RB_MOE
    NG = D_MODEL // 128
    nbj = nb_ref[j]

    x2_copy = pltpu.make_async_copy(
        x2t_hbm.at[pl.ds(pl.multiple_of(j * TT * 8, 8), TT * 8), :], yacc.at[pl.ds(0, TT * 8), :], xsem.at[0])
    x2_copy.start()

    def w_copies(e, slot):
        cps = []
        for c in range(MOE_W_CHUNKS):
            for k, (src, dst) in enumerate(((w1_hbm, w1buf), (w2_hbm, w2buf))):
                rows = pl.ds(c * (src.shape[1] // MOE_W_CHUNKS), src.shape[1] // MOE_W_CHUNKS)
                cps.append(pltpu.make_async_copy(src.at[e, rows], dst.at[slot, rows], wsem.at[slot, k, c]))
        return cps

    for cp in w_copies(blk_e_ref[j, 0], slot_ref[j, 0]):
        cp.start()

    yacc[TT * 8:TT * 8 + 8, :] = jnp.zeros((8, 128), F32)
    yst_b[...] = jnp.zeros(yst_b.shape, F32)

    def gather(b, xs):
        r0 = r0_ref[j, b]
        for r in range(RB):
            xs[r * 8:(r + 1) * 8, :] = hmt_ref[pl.ds(pl.multiple_of(tok_ref[0, 0, r0 + r], 8), 8), :]

    def mlp(xs, yst, e, slot):
        xb = jnp.concatenate([xs[pl.ds(g, RB, stride=8), :] for g in range(NG)], axis=1).astype(BF16)
        hid = jnp.dot(xb, w1buf[slot], preferred_element_type=F32) + b1_ref[e]
        glu = jnp.minimum(hid[:, :D_FF], SWIGLU_LIMIT)
        lin = jnp.clip(hid[:, D_FF:], -SWIGLU_LIMIT, SWIGLU_LIMIT)
        act = glu * _sigmoid(SWIGLU_ALPHA * glu) * (lin + 1.0)
        ys = jnp.dot(act.astype(BF16), w2buf[slot], preferred_element_type=F32) + b2_ref[e]
        for g in range(NG):
            yst[pl.ds(g, RB, stride=8), :] = ys[:, g * 128:(g + 1) * 128]

    def scatter(b, yst):
        r0 = r0_ref[j, b]
        valid = valid_ref[j, b]
        for g0 in range(0, RB, 8):
            dsts, vals = [], []
            for r in range(g0, g0 + 8):
                t8 = jnp.where(r < valid, tok_ref[0, 0, r0 + r], TT * 8)
                dst = pl.ds(pl.multiple_of(t8, 8), 8)
                dsts.append(dst)
                vals.append(yacc[dst, :] + wt_ref[0, 0, r0 + r] * yst[r * 8:(r + 1) * 8, :])
            for dst, val in zip(dsts, vals):
                yacc[dst, :] = val

    def step(b, xs_cur, yst_cur, xs_next, yst_prev):
        @pl.when(b < nbj)
        def _():
            e = blk_e_ref[j, b]
            slot = slot_ref[j, b]

            @pl.when(first_ref[j, b] == 1)
            def _():
                for cp in w_copies(e, slot):
                    cp.wait()
                nx = nxt_ref[j, b]

                @pl.when(nx >= 0)
                def _():
                    for cp in w_copies(nx, 1 - slot):
                        cp.start()

            for s in range(2):
                @pl.when(slot == s)
                def _(s=s):
                    gather(b + 1, xs_next)
                    mlp(xs_cur, yst_cur, e, s)
                    scatter(jnp.maximum(b - 1, 0), yst_prev)

    gather(0, xs_a)
    x2_copy.wait()

    def pair_body(pp, c):
        step(2 * pp, xs_a, yst_a, xs_b, yst_b)
        step(2 * pp + 1, xs_b, yst_b, xs_a, yst_a)
        return c

    lax.fori_loop(0, (nbj + 1) // 2, pair_body, 0)

    @pl.when(nbj % 2 == 1)
    def _():
        scatter(nbj - 1, yst_a)

    @pl.when(nbj % 2 == 0)
    def _():
        scatter(nbj - 1, yst_b)

    RC = 256

    def fin(ci, c):
        base = pl.multiple_of(ci * RC * 8, 8)
        yv = jnp.concatenate([yacc[pl.ds(base + g, RC, stride=8), :] for g in range(NG)], axis=1)
        out_ref[pl.ds(pl.multiple_of(ci * RC, 8), RC), :] = _rms(yv, gf_ref[...])
        return c

    lax.fori_loop(0, TT // RC, fin, 0)


def _moe_nb_max():
    return TOP_K * TT_MOE // RB_MOE + N_EXPERTS


def _moe(lists, x2t, hmt, w1, b1, w2, b2, g_final):
    blk_e, first, nxt, slot, r0, valid, nb, tok, wt = lists
    T = x2t.shape[0] // 8
    TT = TT_MOE
    nt = T // TT
    LP = tok.shape[-1]
    smem_row = pl.BlockSpec((1, 1, LP), lambda j, *_: (j, 0, 0), memory_space=pltpu.SMEM)
    whole = lambda a: pl.BlockSpec(a.shape, lambda j, *_: (0,) * a.ndim)
    grid_spec = pltpu.PrefetchScalarGridSpec(
        num_scalar_prefetch=7,
        grid=(nt,),
        in_specs=[
            smem_row, smem_row,
            pl.BlockSpec(memory_space=pl.ANY),
            pl.BlockSpec((TT * 8, 128), lambda j, *_: (j, 0)),
            pl.BlockSpec(memory_space=pl.ANY), pl.BlockSpec(memory_space=pl.ANY),
            whole(b1), whole(b2), whole(g_final),
        ],
        out_specs=pl.BlockSpec((TT, D_MODEL), lambda j, *_: (j, 0), pipeline_mode=pl.Buffered(1)),
        scratch_shapes=[
            pltpu.VMEM((TT * 8 + 8, 128), F32),
            pltpu.VMEM((RB_MOE * 8, 128), F32), pltpu.VMEM((RB_MOE * 8, 128), F32),
            pltpu.VMEM((RB_MOE * 8, 128), F32), pltpu.VMEM((RB_MOE * 8, 128), F32),
            pltpu.VMEM((2, D_MODEL, 2 * D_FF), BF16),
            pltpu.VMEM((2, D_FF, D_MODEL), BF16),
            pltpu.SemaphoreType.DMA((2, 2, MOE_W_CHUNKS)),
            pltpu.SemaphoreType.DMA((1,)),
        ],
    )
    return pl.pallas_call(
        _moe_kernel,
        grid_spec=grid_spec,
        out_shape=jax.ShapeDtypeStruct((T, D_MODEL), F32),
        compiler_params=pltpu.CompilerParams(
            dimension_semantics=("arbitrary",), vmem_limit_bytes=_vmem_limit(60 << 20)),
        name="moe",
    )(blk_e, first, nxt, slot, r0, valid, nb, tok, wt, x2t, hmt, w1, w2, b1, b2, g_final)


def _moe_lists(idx, wts, step_counts):
    T = idx.shape[1]
    TT, RB = TT_MOE, RB_MOE
    nt = T // TT
    NB = _moe_nb_max()
    A = TOP_K * TT
    e_tile = idx.reshape(TOP_K, nt, TT).transpose(1, 0, 2).reshape(nt, A)
    w_tile = wts.reshape(TOP_K, nt, TT).transpose(1, 0, 2).reshape(nt, A)
    tok_local = jnp.broadcast_to(jnp.tile(jnp.arange(TT, dtype=I32), TOP_K), e_tile.shape)
    _, stok, swt = lax.sort((e_tile, tok_local, w_tile), dimension=1, is_stable=True, num_keys=1)

    experts = jnp.arange(N_EXPERTS, dtype=I32)
    counts = jnp.sum(step_counts[:, :, 0].reshape(nt, -1, N_EXPERTS), axis=1).astype(I32)
    start = jnp.cumsum(counts, axis=1) - counts
    nblk = (counts + RB - 1) // RB
    blk_end = jnp.cumsum(nblk, axis=1)
    blk_start = blk_end - nblk
    nb = blk_end[:, -1]

    b = jnp.arange(NB + 1, dtype=I32)
    blk_e = jnp.minimum(jnp.sum((blk_end[:, :, None] <= b[None, None, :]).astype(I32), axis=1), N_EXPERTS - 1)
    onehot = (blk_e[:, :, None] == experts[None, None, :]).astype(I32)
    per_block = lambda v: jnp.sum(onehot * v[:, None, :], axis=2)
    present = (nblk > 0).astype(I32)
    ordinal = jnp.cumsum(present, axis=1) - present
    later = (experts[None, :] > experts[:, None])[None] & (present[:, None, :] > 0)
    nxt_e = jnp.min(jnp.where(later, experts[None, None, :], N_EXPERTS), axis=2)
    nxt_e = jnp.where(nxt_e >= N_EXPERTS, -1, nxt_e)
    bs_b = per_block(blk_start)
    first = ((b[None, :] == bs_b) & (b[None, :] < nb[:, None])).astype(I32)
    slot = per_block(ordinal) % 2
    nxt = per_block(nxt_e)

    live = b[None, :] < nb[:, None]
    off = (b[None, :] - bs_b) * RB
    r0 = jnp.where(live, per_block(start) + off, A)
    valid = jnp.where(live, jnp.clip(per_block(counts) - off, 0, RB), 0)
    pad = ((0, 0), (0, RB))
    return blk_e, first, nxt, slot, r0, valid, nb, jnp.pad(stok * 8, pad)[:, None, :], jnp.pad(swt, pad)[:, None, :]


def kernel(x, mem, g_mix, w_in, rel_bias, lambda_q1, lambda_k1, lambda_q2, lambda_k2, g_subln, conv_w, a_log,
           dt_bias, g_out_b, w_up_a, w_up_b, w_out, g_xattn, g_mem, w_q_x, w_kv_x, w_o_x, g_moe, w_router,
           b_router, w_mlp1, b_mlp1, w_mlp2, b_mlp2, g_final):
    B_, S_, _ = x.shape
    assert B_ == 1 and S_ % TT_MOE == 0 and x.dtype == F32
    l = 0
    x2d = x.reshape(S_, D_MODEL)
    row = lambda v: v.reshape(1, -1).astype(F32)

    wi = w_in[l]
    cols = lambda rng: wi[:, rng[0]:rng[1]]
    w_ab = jnp.pad(cols(W_IN_AB), ((0, 0), (0, 128 - 2 * B_HEADS))).astype(BF16)
    w_vt = cols(W_IN_VA).T.astype(BF16)
    w_abt = jnp.pad(cols(W_IN_AB).T, ((0, 16 - 2 * B_HEADS), (0, 0))).astype(BF16)

    q, k, vt, qkvb, zb, gates, ab, abt = _inproj(
        x2d, row(g_mix[l]), cols(W_IN_QK).astype(BF16), cols(W_IN_B).astype(BF16),
        cols(W_IN_GATES).astype(BF16), w_ab, w_vt, w_abt)

    lam = (jnp.exp(jnp.sum(lambda_q1[l].astype(F32) * lambda_k1[l].astype(F32)))
           - jnp.exp(jnp.sum(lambda_q2[l].astype(F32) * lambda_k2[l].astype(F32))) + LAMBDA_INIT).reshape(1)
    bias_d, bias_p = _attn_bias_tables(rel_bias)
    oa = _attention(lam, q, k, vt, bias_d, bias_p, row(g_subln[l]))

    lane_pad = lambda v: jnp.pad(v.astype(F32), (0, 128 - v.shape[0])).reshape(1, 128)
    row_bcast = lambda v: jnp.broadcast_to(jnp.pad(v.astype(F32), (0, 16 - v.shape[0]))[:, None], (16, TC_GDN))
    ob = _gdn(qkvb, zb, ab, abt, jnp.pad(conv_w[l].astype(F32), ((0, 4), (0, 0))),
              lane_pad(a_log[l]), lane_pad(dt_bias[l]), row_bcast(a_log[l]), row_bcast(dt_bias[l]),
              row(g_out_b[l]))

    kx, vx = _memkv(mem.reshape(-1, D_MODEL), row(g_mem[l]), w_kv_x[l].astype(BF16))
    br = jnp.broadcast_to(b_router[l].astype(F32)[:, None], (N_EXPERTS, 128))
    x2t, hmt, idx, wts, step_counts = _mix(
        x2d, oa, ob, gates, w_up_a[l].astype(BF16), w_up_b[l].astype(BF16), w_out[l].astype(BF16),
        row(g_xattn[l]), w_q_x[l].astype(BF16), kx, vx, w_o_x[l].astype(BF16), row(g_moe[l]),
        w_router[l].T.astype(F32), br)

    out = _moe(_moe_lists(idx, wts, step_counts), x2t, hmt, w_mlp1[l].astype(BF16),
               b_mlp1[l].astype(F32)[:, None, :],
               w_mlp2[l].astype(BF16), b_mlp2[l].astype(F32)[:, None, :], row(g_final))
    return out.reshape(B_, S_, D_MODEL)
```

```python
import functools
import math

import jax
import jax.numpy as jnp
from jax import lax
from jax.experimental import pallas as pl
from jax.experimental.pallas import tpu as pltpu

F32 = jnp.float32
BF16 = jnp.bfloat16
I32 = jnp.int32

D_MODEL = 1024
CHUNK = 64
EPS = 1e-6
A_HEADS = 4
A_QK_DIM = 64
A_V_DIM = 128
REL_BUCKETS = 32
REL_MAX_DIST = 128
B_HEADS = 4
B_DIM = 128
CONV_WIDTH = 4
X_HEADS = 4
X_HEAD_DIM = 256
N_EXPERTS = 32
TOP_K = 4
D_FF = 1024
SWIGLU_LIMIT = 7.0
SWIGLU_ALPHA = 1.702
LAMBDA_INIT = 0.8 - 0.6 * math.exp(-0.3 * 0)

LOG2E = 1.4426950408889634
NEG_BIG = -1e30

V7X_LANES = 128
V7X_SUBLANES = 8
V7X_VMEM_BYTES = 64 * 1024 * 1024

TM_PROJ = 512
TQ = 512
TK = 512
TC_GDN = 256
TM_MIX = 512
TT_MOE = 2048
RB_MOE = 128
MOE_W_CHUNKS = 1

W_IN_QK, W_IN_VA, W_IN_B, W_IN_AB, W_IN_GATES = (0, 1024), (1024, 1536), (1536, 3584), (3584, 3592), (3592, 5640)

_NT = (((1,), (1,)), ((), ()))
_TN = (((0,), (0,)), ((), ()))


def _rms(x, g):
    return x * lax.rsqrt(jnp.mean(x * x, axis=-1, keepdims=True) + EPS) * g


def _sigmoid(x):
    return 1.0 / (1.0 + jnp.exp(-x))


def _vmem_limit(nbytes):
    return int(min(nbytes, V7X_VMEM_BYTES - 4 * 1024 * 1024))


def _inproj_kernel(x_ref, g_ref, wqk_ref, wb_ref, wg_ref, wab_ref, wvt_ref, wabt_ref, cw_ref,
                   q_ref, k_ref, vt_ref, qkvb_ref, zb_ref, gates_ref, ab_ref, abt_ref, xbuf):
    h = _rms(x_ref[...], g_ref[...]).astype(BF16)

    def mm(w_ref, c0, c1):
        return jnp.dot(h, w_ref[:, c0:c1], preferred_element_type=F32)

    nb = TM_PROJ // TK
    qa = mm(wqk_ref, 0, 512) * (A_QK_DIM ** -0.5 * LOG2E)
    ka = mm(wqk_ref, 512, 1024)
    vt = lax.dot_general(wvt_ref[...], h, _NT, preferred_element_type=F32)
    for hh in range(A_HEADS):
        cs = slice(hh * 128, (hh + 1) * 128)
        q_ref[hh] = qa[:, cs].astype(BF16)
        for b in range(nb):
            rs = slice(b * TK, (b + 1) * TK)
            k_ref[hh, b] = ka[rs, cs].astype(BF16)
            vt_ref[hh, b] = vt[cs, rs].astype(BF16)
    TM = TM_PROJ

    @pl.when(pl.program_id(0) == 0)
    def _():
        xbuf[0:8, :] = jnp.zeros((8, 1536), F32)

    for j in range(3):
        xbuf[8:8 + TM, j * 512:(j + 1) * 512] = mm(wb_ref, j * 512, (j + 1) * 512)
    y = cw_ref[3:4, :] * xbuf[8:8 + TM, :]
    for d in range(1, CONV_WIDTH):
        y = y + cw_ref[3 - d:4 - d, :] * xbuf[8 - d:8 - d + TM, :]
    xbuf[0:8, :] = xbuf[TM:TM + 8, :]
    a = y * _sigmoid(y)

    def l2n(t):
        return t * lax.rsqrt(jnp.sum(t * t, axis=-1, keepdims=True) + EPS)

    for hh in range(B_HEADS):
        cs = slice(hh * 128, (hh + 1) * 128)
        qkvb_ref[:, cs] = l2n(a[:, cs]) * (B_DIM ** -0.5)
        qkvb_ref[:, 512 + hh * 128:512 + (hh + 1) * 128] = l2n(a[:, 512 + hh * 128:512 + (hh + 1) * 128])
    qkvb_ref[:, 1024:1536] = a[:, 1024:1536]
    zb_ref[...] = mm(wb_ref, 1536, 2048)
    for j in range(4):
        gates_ref[:, j * 512:(j + 1) * 512] = mm(wg_ref, j * 512, (j + 1) * 512)
    ab_ref[...] = mm(wab_ref, 0, 128)
    abt_ref[...] = lax.dot_general(wabt_ref[...], h, _NT, preferred_element_type=F32)


def _inproj(x2d, g_mix, w_qk, w_b, w_g, w_ab, w_vt, w_abt, conv_w8):
    T = x2d.shape[0]
    n = T // TM_PROJ
    nkb = T // TK
    nb = TM_PROJ // TK
    full = lambda shape: pl.BlockSpec(shape, lambda i: (0,) * len(shape))
    out_shape = (
        jax.ShapeDtypeStruct((A_HEADS, T, 128), BF16),
        jax.ShapeDtypeStruct((A_HEADS, nkb, TK, 128), BF16),
        jax.ShapeDtypeStruct((A_HEADS, nkb, 128, TK), BF16),
        jax.ShapeDtypeStruct((T, 1536), F32),
        jax.ShapeDtypeStruct((T, 512), F32),
        jax.ShapeDtypeStruct((T, 2048), F32),
        jax.ShapeDtypeStruct((T, 128), F32),
        jax.ShapeDtypeStruct((16, T), F32),
    )
    out_specs = (
        pl.BlockSpec((A_HEADS, TM_PROJ, 128), lambda i: (0, i, 0)),
        pl.BlockSpec((A_HEADS, nb, TK, 128), lambda i: (0, i, 0, 0)),
        pl.BlockSpec((A_HEADS, nb, 128, TK), lambda i: (0, i, 0, 0)),
        pl.BlockSpec((TM_PROJ, 1536), lambda i: (i, 0)),
        pl.BlockSpec((TM_PROJ, 512), lambda i: (i, 0)),
        pl.BlockSpec((TM_PROJ, 2048), lambda i: (i, 0)),
        pl.BlockSpec((TM_PROJ, 128), lambda i: (i, 0)),
        pl.BlockSpec((16, TM_PROJ), lambda i: (0, i)),
    )
    return pl.pallas_call(
        _inproj_kernel,
        grid=(n,),
        in_specs=[
            pl.BlockSpec((TM_PROJ, D_MODEL), lambda i: (i, 0)),
            full((1, D_MODEL)),
            full(w_qk.shape), full(w_b.shape), full(w_g.shape), full(w_ab.shape),
            full(w_vt.shape),
            full(w_abt.shape),
            full(conv_w8.shape),
        ],
        out_specs=out_specs,
        out_shape=out_shape,
        scratch_shapes=[pltpu.VMEM((TM_PROJ + 8, 1536), F32)],
        compiler_params=pltpu.CompilerParams(
            dimension_semantics=("arbitrary",), vmem_limit_bytes=_vmem_limit(56 << 20)),
        name="inproj",
    )(x2d, g_mix, w_qk, w_b, w_g, w_ab, w_vt, w_abt, conv_w8)


def _attn_kernel(lam_ref, q_ref, k_ref, vt_ref, bd_ref, bp_ref, gs_ref, o_ref, m_sc, l_sc, acc_sc, s_buf, pm_buf):
    qi = pl.program_id(1)
    q = q_ref[0]
    lane = lax.broadcasted_iota(I32, q.shape, 1)
    zero = jnp.zeros_like(q)
    qm = (jnp.where(lane < A_QK_DIM, q, zero), jnp.where(lane >= A_QK_DIM, q, zero))
    m_sc[...] = jnp.full(m_sc.shape, NEG_BIG, F32)
    l_sc[...] = jnp.zeros(l_sc.shape, F32)
    acc_sc[...] = jnp.zeros(acc_sc.shape, F32)

    nkb = k_ref.shape[1]

    def stage_a(slot, kb0):
        for j in range(2):
            kblk = k_ref[0, jnp.minimum(kb0 + j, nkb - 1)]
            for m in range(2):
                s = lax.dot_general(kblk, qm[m], _NT, preferred_element_type=F32)
                s_buf[slot, m, j] = s
                pm_buf[slot, m, j] = jnp.max(s, axis=0, keepdims=True)

    def stage_b(slot, kb0, biases):
        for m in range(2):
            m_old = m_sc[m]
            m_new = m_old
            ss = []
            for j, bias in enumerate(biases):
                s = s_buf[slot, m, j]
                s = s if bias is None else s + bias[0]
                ss.append(s)
                m_new = jnp.maximum(m_new, jnp.max(s, axis=0, keepdims=True))
            alpha = jnp.exp2(m_old - m_new)
            l_new = alpha * l_sc[m]
            acc = alpha * acc_sc[m]
            for j in range(len(biases)):
                s = ss[j]
                p = jnp.exp2(s - m_new)
                l_new = l_new + jnp.sum(p, axis=0, keepdims=True)
                acc = acc + jnp.dot(vt_ref[0, kb0 + j], p.astype(BF16), preferred_element_type=F32)
            l_sc[m] = l_new
            acc_sc[m] = acc
            m_sc[m] = m_new

    def far_pair(slot_a, kb_a, slot_b, kb_b):
        def a_piece(j, m):
            kblk = k_ref[0, jnp.minimum(kb_a + j, nkb - 1)]
            s = lax.dot_general(kblk, qm[m], _NT, preferred_element_type=F32)
            s_buf[slot_a, m, j] = s
            pm_buf[slot_a, m, j] = jnp.max(s, axis=0, keepdims=True)

        for m in range(2):
            a_piece(0, m)
            m_old = m_sc[m]
            m_new = jnp.maximum(jnp.maximum(m_old, pm_buf[slot_b, m, 0]), pm_buf[slot_b, m, 1])
            alpha = jnp.exp2(m_old - m_new)
            l_new = alpha * l_sc[m]
            acc = alpha * acc_sc[m]
            for j in range(2):
                if j == 1:
                    a_piece(1, m)
                p = jnp.exp2(s_buf[slot_b, m, j] - m_new)
                l_new = l_new + jnp.sum(p, axis=0, keepdims=True)
                acc = acc + jnp.dot(vt_ref[0, kb_b + j], p.astype(BF16), preferred_element_type=F32)
            l_sc[m] = l_new
            acc_sc[m] = acc
            m_sc[m] = m_new

    nfu = jnp.maximum(qi - 1, 0) // 2
    stage_a(0, 0)

    def far_body(v, c):
        u = 2 * v
        far_pair(1, 2 * u + 2, 0, 2 * u)
        far_pair(0, 2 * u + 4, 1, 2 * u + 2)
        return c

    lax.fori_loop(0, nfu // 2, far_body, 0)
    kb0 = 2 * nfu

    def near(slot):
        @pl.when(qi % 2 == 1)
        def _():
            stage_b(slot, kb0, [bp_ref, bd_ref])

        @pl.when((qi % 2 == 0) & (qi >= 2))
        def _():
            stage_a(1 - slot, kb0 + 2)
            stage_b(slot, kb0, [None, bp_ref])
            stage_b(1 - slot, kb0 + 2, [bd_ref])

        @pl.when(qi == 0)
        def _():
            stage_b(slot, kb0, [bd_ref])

    @pl.when(nfu % 2 == 1)
    def _():
        far_pair(1, kb0, 0, kb0 - 2)
        near(1)

    @pl.when(nfu % 2 == 0)
    def _():
        near(0)

    o = acc_sc[0] / l_sc[0] - lam_ref[0] * (acc_sc[1] / l_sc[1])
    ot = o.T
    o_ref[...] = (_rms(ot, gs_ref[...]) * (1.0 - LAMBDA_INIT)).astype(BF16)


def _attention(lam, q, k, vt, bias_d, bias_p, g_subln):
    T = q.shape[1]
    nq = T // TQ
    nkb = T // TK
    return pl.pallas_call(
        _attn_kernel,
        grid=(A_HEADS, nq),
        in_specs=[
            pl.BlockSpec(memory_space=pltpu.SMEM),
            pl.BlockSpec((1, TQ, 128), lambda h, i: (h, i, 0)),
            pl.BlockSpec((1, nkb, TK, 128), lambda h, i: (h, 0, 0, 0)),
            pl.BlockSpec((1, nkb, 128, TK), lambda h, i: (h, 0, 0, 0)),
            pl.BlockSpec((1, TK, TQ), lambda h, i: (h, 0, 0)),
            pl.BlockSpec((1, TK, TQ), lambda h, i: (h, 0, 0)),
            pl.BlockSpec((1, 128), lambda h, i: (0, 0)),
        ],
        out_specs=pl.BlockSpec((TQ, 128), lambda h, i: (i, h)),
        out_shape=jax.ShapeDtypeStruct((T, A_HEADS * A_V_DIM), BF16),
        scratch_shapes=[
            pltpu.VMEM((2, 1, TQ), F32),
            pltpu.VMEM((2, 1, TQ), F32),
            pltpu.VMEM((2, 128, TQ), F32),
            pltpu.VMEM((2, 2, 2, TK, TQ), F32),
            pltpu.VMEM((2, 2, 2, 1, TQ), F32),
        ],
        compiler_params=pltpu.CompilerParams(
            dimension_semantics=("arbitrary", "arbitrary"), vmem_limit_bytes=_vmem_limit(48 << 20)),
        name="attn",
    )(lam, q, k, vt, bias_d, bias_p, g_subln)


def _t5_bucket(rel):
    half = REL_BUCKETS // 2
    max_exact = half // 2
    ret = jnp.where(rel > 0, half, 0)
    n = jnp.abs(rel)
    large = max_exact + (jnp.log(jnp.maximum(n, 1).astype(F32) / max_exact)
                         / math.log(REL_MAX_DIST / max_exact) * (half - max_exact)).astype(I32)
    large = jnp.minimum(large, half - 1)
    return ret + jnp.where(n < max_exact, n, large)


def _attn_bias_tables(rel_bias):
    assert TK >= REL_MAX_DIST and TK == TQ and TK % CHUNK == 0
    kk = jnp.arange(TK, dtype=I32)[:, None]
    qq = jnp.arange(TQ, dtype=I32)[None, :]
    rb = rel_bias.astype(F32)
    far = rb[_t5_bucket(jnp.full((1,), -REL_MAX_DIST, I32))[0]]
    table = ((rb - far[None, :]) * LOG2E).T

    def lookup(rel):
        onehot = (_t5_bucket(rel).reshape(1, -1) == jnp.arange(REL_BUCKETS, dtype=I32)[:, None]).astype(F32)
        return jnp.dot(table, onehot, precision=lax.Precision.HIGHEST).reshape(A_HEADS, TK, TQ)

    bd = jnp.where((kk // CHUNK <= qq // CHUNK)[None], lookup(kk - qq), NEG_BIG)
    bp = lookup(kk - TK - qq)
    return bd, bp


def _gdn_kernel(x_ref, zb_ref, ab_ref, abt_ref, cw_ref, alc_ref, dtc_ref, alr_ref, dtr_ref, gob_ref,
                o_ref, xbuf, s_sc):
    i = pl.program_id(0)
    TC = TC_GDN
    NCH = TC // CHUNK
    HB = B_HEADS * CHUNK

    @pl.when(i == 0)
    def _():
        s_sc[...] = jnp.zeros(s_sc.shape, F32)

    qh = [x_ref[:, h * 128:(h + 1) * 128] for h in range(B_HEADS)]
    kh = [x_ref[:, 512 + h * 128:512 + (h + 1) * 128] for h in range(B_HEADS)]
    vh = [x_ref[:, 1024 + h * 128:1024 + (h + 1) * 128] for h in range(B_HEADS)]

    def softplus(t):
        return jnp.maximum(t, 0.0) + jnp.log(1.0 + jnp.exp(-jnp.abs(t)))

    ab = ab_ref[...]
    g_col = -jnp.exp(alc_ref[...]) * softplus(ab + dtc_ref[...])
    beta_col = _sigmoid(ab)
    g_row = -jnp.exp(alr_ref[...]) * softplus(abt_ref[...] + dtr_ref[...])

    rt = lax.broadcasted_iota(I32, (TC, TC), 0)
    ct = lax.broadcasted_iota(I32, (TC, TC), 1)
    same_chunk = (rt // CHUNK) == (ct // CHUNK)
    tril = jnp.where(same_chunk & (rt >= ct), 1.0, 0.0).astype(F32)
    triu = jnp.where(same_chunk & (rt <= ct), 1.0, 0.0).astype(F32)
    gc_col = jnp.dot(tril, g_col, precision=lax.Precision.HIGHEST, preferred_element_type=F32)
    gc_row = jnp.dot(g_row, triu, precision=lax.Precision.HIGHEST, preferred_element_type=F32)

    ri = lax.broadcasted_iota(I32, (HB, HB), 0)
    ci = lax.broadcasted_iota(I32, (HB, HB), 1)
    same_head = (ri // CHUNK) == (ci // CHUNK)
    incl = same_head & (ri >= ci)
    strict = same_head & (ri > ci)
    eye = jnp.where(ri == ci, 1.0, 0.0).astype(F32)

    def level_mask(s):
        return ((ri // (2 * s)) == (ci // (2 * s))) & (((ri // s) % 2) == 1) & (((ci // s) % 2) == 0)

    cat0 = lambda parts: jnp.concatenate(parts, axis=0)
    heads = range(B_HEADS)
    chunks = range(NCH)

    Kc, Qc, glc, gccc, Lc, QKc, invc, rhsc = [], [], [], [], [], [], [], []
    for c in chunks:
        rs = slice(c * CHUNK, (c + 1) * CHUNK)
        last = slice(c * CHUNK + CHUNK - 1, (c + 1) * CHUNK)
        K = cat0([kh[h][rs] for h in heads])
        Q = cat0([qh[h][rs] for h in heads])
        V = cat0([vh[h][rs] for h in heads])
        beta = cat0([beta_col[rs, 4 + h:5 + h] for h in heads])
        gcc = cat0([gc_col[rs, h:h + 1] for h in heads])
        gl = cat0([jnp.broadcast_to(gc_col[last, h:h + 1], (CHUNK, 1)) for h in heads])
        gcr = jnp.concatenate([gc_row[h:h + 1, rs] for h in heads], axis=1)
        Kb = K * beta
        e = jnp.exp(jnp.where(incl, gcc - gcr, 0.0))
        L = lax.dot_general(Kb, K, _NT, preferred_element_type=F32) * jnp.where(strict, e, 0.0)
        QKc.append(lax.dot_general(Q, K, _NT, preferred_element_type=F32) * jnp.where(incl, e, 0.0))
        rhsc.append(jnp.concatenate([V * beta, Kb * jnp.exp(gcc)], axis=1))
        invc.append(eye - jnp.where(level_mask(1), L, 0.0))
        Kc.append(K); Qc.append(Q); glc.append(gl); gccc.append(gcc); Lc.append(L)

    for s in (2, 4, 8, 16, 32):
        msk = level_mask(s)
        ts = [jnp.dot(invc[c], jnp.where(msk, Lc[c], 0.0), preferred_element_type=F32) for c in chunks]
        invc = [invc[c] - jnp.dot(ts[c], invc[c], preferred_element_type=F32) for c in chunks]
    solc = [jnp.dot(invc[c], rhsc[c], preferred_element_type=F32) for c in chunks]

    for c in chunks:
        rs = slice(c * CHUNK, (c + 1) * CHUNK)
        U = solc[c][:, :B_DIM]
        W = solc[c][:, B_DIM:]
        Qd = Qc[c] * jnp.exp(gccc[c])
        Kd = Kc[c] * jnp.exp(glc[c] - gccc[c])
        vnew = []
        ost = []
        for h in heads:
            hs = slice(h * CHUNK, (h + 1) * CHUNK)
            S = s_sc[h]
            vn = U[hs] - jnp.dot(W[hs], S, preferred_element_type=F32)
            ost.append(jnp.dot(Qd[hs], S, preferred_element_type=F32))
            s_sc[h] = S * jnp.exp(glc[c][h * CHUNK:h * CHUNK + 1, :]) + lax.dot_general(
                Kd[hs], vn, _TN, preferred_element_type=F32)
            vnew.append(vn)
        O = cat0(ost) + jnp.dot(QKc[c], cat0(vnew), preferred_element_type=F32)
        for h in heads:
            oh = _rms(O[h * CHUNK:(h + 1) * CHUNK], gob_ref[...])
            z = zb_ref[rs, h * 128:(h + 1) * 128]
            o_ref[rs, h * 128:(h + 1) * 128] = (oh * (z * _sigmoid(z))).astype(BF16)


def _gdn(qkvb, zb, ab, abt, conv_w8, alc, dtc, alr, dtr, gob):
    T = qkvb.shape[0]
    TC = TC_GDN
    full = lambda shape: pl.BlockSpec(shape, lambda i: (0,) * len(shape))
    return pl.pallas_call(
        _gdn_kernel,
        grid=(T // TC,),
        in_specs=[
            pl.BlockSpec((TC, 1536), lambda i: (i, 0)),
            pl.BlockSpec((TC, 512), lambda i: (i, 0)),
            pl.BlockSpec((TC, 128), lambda i: (i, 0)),
            pl.BlockSpec((16, TC), lambda i: (0, i)),
            full((8, 1536)), full((1, 128)), full((1, 128)), full((16, TC)), full((16, TC)), full((1, 128)),
        ],
        out_specs=pl.BlockSpec((TC, 512), lambda i: (i, 0)),
        out_shape=jax.ShapeDtypeStruct((T, 512), BF16),
        scratch_shapes=[pltpu.VMEM((TC + 8, 1536), F32), pltpu.VMEM((B_HEADS, B_DIM, B_DIM), F32)],
        compiler_params=pltpu.CompilerParams(
            dimension_semantics=("arbitrary",), vmem_limit_bytes=_vmem_limit(48 << 20)),
        name="gdn",
    )(qkvb, zb, ab, abt, conv_w8, alc, dtc, alr, dtr, gob)


def _memkv_kernel(m_ref, g_ref, w_ref, k_ref, v_ref):
    hm = _rms(m_ref[...], g_ref[...]).astype(BF16)
    kv = jnp.dot(hm, w_ref[...], preferred_element_type=F32)
    k_ref[...] = kv[:, :D_MODEL].astype(BF16)
    v_ref[...] = kv[:, D_MODEL:].astype(BF16)


def _memkv(mem2d, g_mem, w_kv):
    n = mem2d.shape[0]
    return pl.pallas_call(
        _memkv_kernel,
        out_shape=(jax.ShapeDtypeStruct((n, D_MODEL), BF16), jax.ShapeDtypeStruct((n, D_MODEL), BF16)),
        compiler_params=pltpu.CompilerParams(vmem_limit_bytes=_vmem_limit(32 << 20)),
        name="memkv",
    )(mem2d, g_mem, w_kv)


def _mix_kernel(x_ref, oa_ref, ob_ref, gates_ref, wua_ref, wub_ref, wout_ref, gx_ref, wq_ref, kx_ref, vx_ref,
                wo_ref, gm_ref, wr_ref, br_ref, x2t_ref, hmt_ref, idx_ref, wts_ref, cnt_ref):
    NSUB = 2
    TS = TM_MIX // NSUB
    subs = range(NSUB)
    rows = [slice(sub * TS, (sub + 1) * TS) for sub in subs]
    dotf = functools.partial(jnp.dot, preferred_element_type=F32)

    ma = [dotf(oa_ref[rows[g], :], wua_ref[...]) for g in subs]
    mb = [dotf(ob_ref[rows[g], :], wub_ref[...]) for g in subs]
    merged = [(_sigmoid(gates_ref[rows[g], :D_MODEL]) * ma[g]
               + _sigmoid(gates_ref[rows[g], D_MODEL:]) * mb[g]).astype(BF16) for g in subs]
    x1 = [x_ref[rows[g], :] + dotf(merged[g], wout_ref[...]) for g in subs]
    hx = [_rms(x1[g], gx_ref[...]).astype(BF16) for g in subs]

    heads = [[] for _ in subs]
    for h in range(X_HEADS):
        cs = slice(h * X_HEAD_DIM, (h + 1) * X_HEAD_DIM)
        qh = [dotf(hx[g], wq_ref[:, cs]).astype(BF16) for g in subs]
        s = [lax.dot_general(qh[g], kx_ref[:, cs], _NT, preferred_element_type=F32) * (X_HEAD_DIM ** -0.5)
             for g in subs]
        for g in subs:
            sg = s[g] - jnp.max(s[g], axis=-1, keepdims=True)
            p = jnp.exp(sg)
            p = p / jnp.sum(p, axis=-1, keepdims=True)
            heads[g].append(dotf(p.astype(BF16), vx_ref[:, cs]).astype(BF16))
    x2 = [x1[g] + dotf(jnp.concatenate(heads[g], axis=1), wo_ref[...]) for g in subs]
    hm = [_rms(x2[g], gm_ref[...]) for g in subs]

    for g in subs:
        for c in range(D_MODEL // 128):
            x2t_ref[pl.ds(g * TS * 8 + c, TS, stride=8), :] = x2[g][:, c * 128:(c + 1) * 128]
            hmt_ref[pl.ds(g * TS * 8 + c, TS, stride=8), :] = hm[g][:, c * 128:(c + 1) * 128]

        logits = lax.dot_general(wr_ref[...], hm[g], _NT, precision=lax.Precision.HIGHEST,
                                 preferred_element_type=F32) + br_ref[:, 0:1]
        eidx = lax.broadcasted_iota(I32, logits.shape, 0)
        vals, idxs = [], []
        cur = logits
        for _ in range(TOP_K):
            mx = jnp.max(cur, axis=0, keepdims=True)
            ix = jnp.min(jnp.where(cur == mx, eidx, N_EXPERTS), axis=0, keepdims=True)
            vals.append(mx)
            idxs.append(ix)
            cur = jnp.where(eidx == ix, -jnp.inf, cur)
        ex = [jnp.exp(v - vals[0]) for v in vals]
        den = ex[0] + ex[1] + ex[2] + ex[3]
        idx_ref[:, rows[g]] = jnp.concatenate(idxs, axis=0)
        wts_ref[:, rows[g]] = jnp.concatenate([e / den for e in ex], axis=0)
        hits = sum(jnp.where(eidx == ix, 1.0, 0.0) for ix in idxs)
        cnt = jnp.broadcast_to(jnp.sum(hits, axis=1, keepdims=True), (N_EXPERTS, 128))
        cnt_ref[0] = cnt if g == 0 else cnt_ref[0] + cnt


def _mix(x2d, oa, ob, gates, wua, wub, wout, gx, wq, kx, vx, wo, gm, wr_t, br):
    T = x2d.shape[0]
    TM = TM_MIX
    full = lambda a: pl.BlockSpec(a.shape, lambda i: (0,) * a.ndim)
    out_shape = (
        jax.ShapeDtypeStruct((T * 8, 128), F32),
        jax.ShapeDtypeStruct((T * 8, 128), F32),
        jax.ShapeDtypeStruct((TOP_K, T), I32),
        jax.ShapeDtypeStruct((TOP_K, T), F32),
        jax.ShapeDtypeStruct((T // TM, N_EXPERTS, 128), F32),
    )
    return pl.pallas_call(
        _mix_kernel,
        grid=(T // TM,),
        in_specs=[
            pl.BlockSpec((TM, D_MODEL), lambda i: (i, 0)),
            pl.BlockSpec((TM, 512), lambda i: (i, 0)),
            pl.BlockSpec((TM, 512), lambda i: (i, 0)),
            pl.BlockSpec((TM, 2048), lambda i: (i, 0)),
            full(wua), full(wub), full(wout), full(gx), full(wq), full(kx), full(vx), full(wo), full(gm),
            full(wr_t), full(br),
        ],
        out_specs=(
            pl.BlockSpec((TM * 8, 128), lambda i: (i, 0)),
            pl.BlockSpec((TM * 8, 128), lambda i: (i, 0)),
            pl.BlockSpec((TOP_K, TM), lambda i: (0, i)),
            pl.BlockSpec((TOP_K, TM), lambda i: (0, i)),
            pl.BlockSpec((1, N_EXPERTS, 128), lambda i: (i, 0, 0)),
        ),
        out_shape=out_shape,
        compiler_params=pltpu.CompilerParams(
            dimension_semantics=("arbitrary",), vmem_limit_bytes=_vmem_limit(56 << 20)),
        name="mix",
    )(x2d, oa, ob, gates, wua, wub, wout, gx, wq, kx, vx, wo, gm, wr_t, br)


def _moe_kernel(blk_e_ref, first_ref, nxt_ref, slot_ref, r0_ref, valid_ref, nb_ref,
                tok_ref, wt_ref, x2t_hbm, hmt_ref, w1_hbm, w2_hbm, b1_ref, b2_ref, gf_ref,
                out_ref, yacc, xs_a, xs_b, yst_a, yst_b, w1buf, w2buf, wsem, xsem):
    j = pl.program_id(0)
    TT = TT_MOE
    RB = RB_MOE
    NG = D_MODEL // 128
    nbj = nb_ref[j]

    x2_copy = pltpu.make_async_copy(
        x2t_hbm.at[pl.ds(pl.multiple_of(j * TT * 8, 8), TT * 8), :], yacc.at[pl.ds(0, TT * 8), :], xsem.at[0])
    x2_copy.start()

    def w_copies(e, slot):
        cps = []
        for c in range(MOE_W_CHUNKS):
            for k, (src, dst) in enumerate(((w1_hbm, w1buf), (w2_hbm, w2buf))):
                rows = pl.ds(c * (src.shape[1] // MOE_W_CHUNKS), src.shape[1] // MOE_W_CHUNKS)
                cps.append(pltpu.make_async_copy(src.at[e, rows], dst.at[slot, rows], wsem.at[slot, k, c]))
        return cps

    for cp in w_copies(blk_e_ref[j, 0], slot_ref[j, 0]):
        cp.start()

    yacc[TT * 8:TT * 8 + 8, :] = jnp.zeros((8, 128), F32)
    yst_b[...] = jnp.zeros(yst_b.shape, F32)

    def gather(b, xs):
        r0 = r0_ref[j, b]
        for r in range(RB):
            xs[r * 8:(r + 1) * 8, :] = hmt_ref[pl.ds(pl.multiple_of(tok_ref[0, 0, r0 + r], 8), 8), :]

    def mlp(xs, yst, e, slot):
        xb = jnp.concatenate([xs[pl.ds(g, RB, stride=8), :] for g in range(NG)], axis=1).astype(BF16)
        hid = jnp.dot(xb, w1buf[slot], preferred_element_type=F32) + b1_ref[e]
        glu = jnp.minimum(hid[:, :D_FF], SWIGLU_LIMIT)
        lin = jnp.clip(hid[:, D_FF:], -SWIGLU_LIMIT, SWIGLU_LIMIT)
        act = glu * _sigmoid(SWIGLU_ALPHA * glu) * (lin + 1.0)
        ys = jnp.dot(act.astype(BF16), w2buf[slot], preferred_element_type=F32) + b2_ref[e]
        for g in range(NG):
            yst[pl.ds(g, RB, stride=8), :] = ys[:, g * 128:(g + 1) * 128]

    def scatter(b, yst):
        r0 = r0_ref[j, b]
        valid = valid_ref[j, b]
        for g0 in range(0, RB, 8):
            dsts, vals = [], []
            for r in range(g0, g0 + 8):
                t8 = jnp.where(r < valid, tok_ref[0, 0, r0 + r], TT * 8)
                dst = pl.ds(pl.multiple_of(t8, 8), 8)
                dsts.append(dst)
                vals.append(yacc[dst, :] + wt_ref[0, 0, r0 + r] * yst[r * 8:(r + 1) * 8, :])
            for dst, val in zip(dsts, vals):
                yacc[dst, :] = val

    def step(b, xs_cur, yst_cur, xs_next, yst_prev):
        @pl.when(b < nbj)
        def _():
            e = blk_e_ref[j, b]
            slot = slot_ref[j, b]

            @pl.when(first_ref[j, b] == 1)
            def _():
                for cp in w_copies(e, slot):
                    cp.wait()
                nx = nxt_ref[j, b]

                @pl.when(nx >= 0)
                def _():
                    for cp in w_copies(nx, 1 - slot):
                        cp.start()

            for s in range(2):
                @pl.when(slot == s)
                def _(s=s):
                    gather(b + 1, xs_next)
                    mlp(xs_cur, yst_cur, e, s)
                    scatter(jnp.maximum(b - 1, 0), yst_prev)

    gather(0, xs_a)
    x2_copy.wait()

    def pair_body(pp, c):
        step(2 * pp, xs_a, yst_a, xs_b, yst_b)
        step(2 * pp + 1, xs_b, yst_b, xs_a, yst_a)
        return c

    lax.fori_loop(0, (nbj + 1) // 2, pair_body, 0)

    @pl.when(nbj % 2 == 1)
    def _():
        scatter(nbj - 1, yst_a)

    @pl.when(nbj % 2 == 0)
    def _():
        scatter(nbj - 1, yst_b)

    RC = 256

    def fin(ci, c):
        base = pl.multiple_of(ci * RC * 8, 8)
        yv = jnp.concatenate([yacc[pl.ds(base + g, RC, stride=8), :] for g in range(NG)], axis=1)
        out_ref[pl.ds(pl.multiple_of(ci * RC, 8), RC), :] = _rms(yv, gf_ref[...])
        return c

    lax.fori_loop(0, TT // RC, fin, 0)


def _moe_nb_max():
    return TOP_K * TT_MOE // RB_MOE + N_EXPERTS


def _moe(lists, x2t, hmt, w1, b1, w2, b2, g_final):
    blk_e, first, nxt, slot, r0, valid, nb, tok, wt = lists
    T = x2t.shape[0] // 8
    TT = TT_MOE
    nt = T // TT
    LP = tok.shape[-1]
    smem_row = pl.BlockSpec((1, 1, LP), lambda j, *_: (j, 0, 0), memory_space=pltpu.SMEM)
    whole = lambda a: pl.BlockSpec(a.shape, lambda j, *_: (0,) * a.ndim)
    grid_spec = pltpu.PrefetchScalarGridSpec(
        num_scalar_prefetch=7,
        grid=(nt,),
        in_specs=[
            smem_row, smem_row,
            pl.BlockSpec(memory_space=pl.ANY),
            pl.BlockSpec((TT * 8, 128), lambda j, *_: (j, 0)),
            pl.BlockSpec(memory_space=pl.ANY), pl.BlockSpec(memory_space=pl.ANY),
            whole(b1), whole(b2), whole(g_final),
        ],
        out_specs=pl.BlockSpec((TT, D_MODEL), lambda j, *_: (j, 0), pipeline_mode=pl.Buffered(1)),
        scratch_shapes=[
            pltpu.VMEM((TT * 8 + 8, 128), F32),
            pltpu.VMEM((RB_MOE * 8, 128), F32), pltpu.VMEM((RB_MOE * 8, 128), F32),
            pltpu.VMEM((RB_MOE * 8, 128), F32), pltpu.VMEM((RB_MOE * 8, 128), F32),
            pltpu.VMEM((2, D_MODEL, 2 * D_FF), BF16),
            pltpu.VMEM((2, D_FF, D_MODEL), BF16),
            pltpu.SemaphoreType.DMA((2, 2, MOE_W_CHUNKS)),
            pltpu.SemaphoreType.DMA((1,)),
        ],
    )
    return pl.pallas_call(
        _moe_kernel,
        grid_spec=grid_spec,
        out_shape=jax.ShapeDtypeStruct((T, D_MODEL), F32),
        compiler_params=pltpu.CompilerParams(
            dimension_semantics=("arbitrary",), vmem_limit_bytes=_vmem_limit(60 << 20)),
        name="moe",
    )(blk_e, first, nxt, slot, r0, valid, nb, tok, wt, x2t, hmt, w1, w2, b1, b2, g_final)


def _moe_lists(idx, wts, step_counts):
    T = idx.shape[1]
    TT, RB = TT_MOE, RB_MOE
    nt = T // TT
    NB = _moe_nb_max()
    A = TOP_K * TT
    e_tile = idx.reshape(TOP_K, nt, TT).transpose(1, 0, 2).reshape(nt, A)
    w_tile = wts.reshape(TOP_K, nt, TT).transpose(1, 0, 2).reshape(nt, A)
    tok_local = jnp.broadcast_to(jnp.tile(jnp.arange(TT, dtype=I32), TOP_K), e_tile.shape)
    _, stok, swt = lax.sort((e_tile, tok_local, w_tile), dimension=1, is_stable=True, num_keys=1)

    experts = jnp.arange(N_EXPERTS, dtype=I32)
    counts = jnp.sum(step_counts[:, :, 0].reshape(nt, -1, N_EXPERTS), axis=1).astype(I32)
    start = jnp.cumsum(counts, axis=1) - counts
    nblk = (counts + RB - 1) // RB
    blk_end = jnp.cumsum(nblk, axis=1)
    blk_start = blk_end - nblk
    nb = blk_end[:, -1]

    b = jnp.arange(NB + 1, dtype=I32)
    blk_e = jnp.minimum(jnp.sum((blk_end[:, :, None] <= b[None, None, :]).astype(I32), axis=1), N_EXPERTS - 1)
    onehot = (blk_e[:, :, None] == experts[None, None, :]).astype(I32)
    per_block = lambda v: jnp.sum(onehot * v[:, None, :], axis=2)
    present = (nblk > 0).astype(I32)
    ordinal = jnp.cumsum(present, axis=1) - present
    later = (experts[None, :] > experts[:, None])[None] & (present[:, None, :] > 0)
    nxt_e = jnp.min(jnp.where(later, experts[None, None, :], N_EXPERTS), axis=2)
    nxt_e = jnp.where(nxt_e >= N_EXPERTS, -1, nxt_e)
    bs_b = per_block(blk_start)
    first = ((b[None, :] == bs_b) & (b[None, :] < nb[:, None])).astype(I32)
    slot = per_block(ordinal) % 2
    nxt = per_block(nxt_e)

    live = b[None, :] < nb[:, None]
    off = (b[None, :] - bs_b) * RB
    r0 = jnp.where(live, per_block(start) + off, A)
    valid = jnp.where(live, jnp.clip(per_block(counts) - off, 0, RB), 0)
    pad = ((0, 0), (0, RB))
    return blk_e, first, nxt, slot, r0, valid, nb, jnp.pad(stok * 8, pad)[:, None, :], jnp.pad(swt, pad)[:, None, :]


def kernel(x, mem, g_mix, w_in, rel_bias, lambda_q1, lambda_k1, lambda_q2, lambda_k2, g_subln, conv_w, a_log,
           dt_bias, g_out_b, w_up_a, w_up_b, w_out, g_xattn, g_mem, w_q_x, w_kv_x, w_o_x, g_moe, w_router,
           b_router, w_mlp1, b_mlp1, w_mlp2, b_mlp2, g_final):
    B_, S_, _ = x.shape
    assert B_ == 1 and S_ % TT_MOE == 0 and x.dtype == F32
    l = 0
    x2d = x.reshape(S_, D_MODEL)
    row = lambda v: v.reshape(1, -1).astype(F32)

    wi = w_in[l]
    cols = lambda rng: wi[:, rng[0]:rng[1]]
    w_ab = jnp.pad(cols(W_IN_AB), ((0, 0), (0, 128 - 2 * B_HEADS))).astype(BF16)
    w_vt = cols(W_IN_VA).T.astype(BF16)
    w_abt = jnp.pad(cols(W_IN_AB).T, ((0, 16 - 2 * B_HEADS), (0, 0))).astype(BF16)

    q, k, vt, qkvb, zb, gates, ab, abt = _inproj(
        x2d, row(g_mix[l]), cols(W_IN_QK).astype(BF16), cols(W_IN_B).astype(BF16),
        cols(W_IN_GATES).astype(BF16), w_ab, w_vt, w_abt,
        jnp.pad(conv_w[l].astype(F32), ((0, 4), (0, 0))))

    lam = (jnp.exp(jnp.sum(lambda_q1[l].astype(F32) * lambda_k1[l].astype(F32)))
           - jnp.exp(jnp.sum(lambda_q2[l].astype(F32) * lambda_k2[l].astype(F32))) + LAMBDA_INIT).reshape(1)
    bias_d, bias_p = _attn_bias_tables(rel_bias)
    oa = _attention(lam, q, k, vt, bias_d, bias_p, row(g_subln[l]))

    lane_pad = lambda v: jnp.pad(v.astype(F32), (0, 128 - v.shape[0])).reshape(1, 128)
    row_bcast = lambda v: jnp.broadcast_to(jnp.pad(v.astype(F32), (0, 16 - v.shape[0]))[:, None], (16, TC_GDN))
    ob = _gdn(qkvb, zb, ab, abt, jnp.pad(conv_w[l].astype(F32), ((0, 4), (0, 0))),
              lane_pad(a_log[l]), lane_pad(dt_bias[l]), row_bcast(a_log[l]), row_bcast(dt_bias[l]),
              row(g_out_b[l]))

    kx, vx = _memkv(mem.reshape(-1, D_MODEL), row(g_mem[l]), w_kv_x[l].astype(BF16))
    br = jnp.broadcast_to(b_router[l].astype(F32)[:, None], (N_EXPERTS, 128))
    x2t, hmt, idx, wts, step_counts = _mix(
        x2d, oa, ob, gates, w_up_a[l].astype(BF16), w_up_b[l].astype(BF16), w_out[l].astype(BF16),
        row(g_xattn[l]), w_q_x[l].astype(BF16), kx, vx, w_o_x[l].astype(BF16), row(g_moe[l]),
        w_router[l].T.astype(F32), br)

    out = _moe(_moe_lists(idx, wts, step_counts), x2t, hmt, w_mlp1[l].astype(BF16),
               b_mlp1[l].astype(F32)[:, None, :],
               w_mlp2[l].astype(BF16), b_mlp2[l].astype(F32)[:, None, :], row(g_final))
    return out.reshape(B_, S_, D_MODEL)
```
